```python
import math
import jax, jax.numpy as jnp
from jax import lax
import numpy as np

D_MODEL = 1024
BATCH = 16
SEQ = 4096
DEPTH = 1

CTX_LEN = 256
GRID_W = 64
D_SSM = D_MODEL // 2
SSM_GROUP = 16
N_SSM_GROUPS = D_SSM // SSM_GROUP
SSM_STATE = 64
D_CONV = D_MODEL // 2
N_EXPERTS = 32
TOP_K = 4
D_EXPERT = D_MODEL
SWIGLU_LIMIT = 7.0
SWIGLU_ALPHA = 1.702
MOE_BLOCK = 128
RMS_EPS = 1e-6
D_IN = D_SSM + 3 * D_CONV + 2 * D_MODEL
SPLITS = (D_SSM, D_SSM + D_CONV, D_SSM + 2 * D_CONV, D_SSM + 3 * D_CONV,
          D_SSM + 3 * D_CONV + D_MODEL)

kernel_name = "hybrid_s5_shortconv_moe_dit_block"


def rmsnorm(x, g):
    xf = x.astype(jnp.float32)
    y = xf * lax.rsqrt(jnp.mean(xf * xf, axis=-1, keepdims=True) + RMS_EPS)
    return (y * g.astype(jnp.float32)).astype(x.dtype)


def adaln(cond, w_mod, b_mod):
    m = jax.nn.silu(cond) @ w_mod + b_mod
    return jnp.split(m, 6, axis=-1)


def modulate(h, shift, scale):
    return h * (1 + scale) + shift


def s5_discretise(lam_re, lam_im, log_dt, b_re, b_im):
    f = jnp.float32
    lam_re, lam_im = lam_re.astype(f), lam_im.astype(f)
    b_re, b_im = b_re.astype(f), b_im.astype(f)
    dt = jnp.exp(log_dt.astype(f))[:, None]
    mag = jnp.exp(lam_re * dt)
    a_re, a_im = mag * jnp.cos(lam_im * dt), mag * jnp.sin(lam_im * dt)
    den = lam_re * lam_re + lam_im * lam_im
    q_re = ((a_re - 1) * lam_re + a_im * lam_im) / den
    q_im = (a_im * lam_re - (a_re - 1) * lam_im) / den
    bb_re = q_re[..., None] * b_re - q_im[..., None] * b_im
    bb_im = q_re[..., None] * b_im + q_im[..., None] * b_re
    return a_re, a_im, bb_re, bb_im


def _complex_affine_combine(e1, e2):
    a1r, a1i, b1r, b1i = e1
    a2r, a2i, b2r, b2i = e2
    return (a1r * a2r - a1i * a2i, a1r * a2i + a1i * a2r,
            a2r * b1r - a2i * b1i + b2r, a2r * b1i + a2i * b1r + b2i)


def s5_scan(u, disc, h0, reverse):
    a_re, a_im, bb_re, bb_im = disc
    bu_re = jnp.einsum("blgc,gpc->lbgp", u, bb_re)
    bu_im = jnp.einsum("blgc,gpc->lbgp", u, bb_im)
    if h0 is not None:
        h0_re, h0_im = h0
        entry = -1 if reverse else 0
        bu_re = bu_re.at[entry].add(a_re * h0_re - a_im * h0_im)
        bu_im = bu_im.at[entry].add(a_re * h0_im + a_im * h0_re)
    seq_len = u.shape[1]
    ar = jnp.broadcast_to(a_re[None, None], (seq_len, 1) + a_re.shape)
    ai = jnp.broadcast_to(a_im[None, None], (seq_len, 1) + a_im.shape)
    _, _, h_re, h_im = lax.associative_scan(
        _complex_affine_combine, (ar, ai, bu_re, bu_im), reverse=reverse, axis=0)
    return h_re, h_im


def s5_readout(states, c_re, c_im):
    h_re, h_im = states
    return (jnp.einsum("lbgp,gcp->blgc", h_re, c_re.astype(jnp.float32))
            - jnp.einsum("lbgp,gcp->blgc", h_im, c_im.astype(jnp.float32)))


def s5_output(u_flat, s_f, s_b, c_re, c_im, d_skip, w_glu):
    bsz, seq_len, _ = u_flat.shape
    uf = u_flat.astype(jnp.float32)
    y = s5_readout(s_f, c_re[0], c_im[0]) + s5_readout(s_b, c_re[1], c_im[1])
    y = y.reshape(bsz, seq_len, D_SSM) + d_skip.astype(jnp.float32) * uf
    y = jax.nn.gelu(y)
    y = y * jax.nn.sigmoid(y @ w_glu.astype(jnp.float32))
    return y.astype(u_flat.dtype)


def short_conv(v, gate_b, gate_c, conv_w, conv_b, rows, row_len):
    bsz, seq_len, ch = v.shape
    z = (gate_c * v).reshape(bsz, rows, row_len, ch)
    zp = jnp.pad(z, ((0, 0), (0, 0), (1, 1), (0, 0)))
    y = (zp[:, :, :-2] * conv_w[0] + zp[:, :, 1:-1] * conv_w[1]
         + zp[:, :, 2:] * conv_w[2] + conv_b)
    return gate_b * y.reshape(bsz, seq_len, ch)


def mixer_merge(parts, s_f, s_b, rows, row_len, c_re, c_im, d_skip, w_glu,
                conv_w, conv_b, w_ssm_br, w_conv_br, w_o):
    u, v, gb, gc, g_s, g_c = parts
    y_s = s5_output(u, s_f, s_b, c_re, c_im, d_skip, w_glu) @ w_ssm_br
    y_c = short_conv(v, gb, gc, conv_w, conv_b, rows, row_len) @ w_conv_br
    return (jax.nn.sigmoid(g_s) * y_s + jax.nn.sigmoid(g_c) * y_c) @ w_o


def group_channels(u):
    bsz, seq_len, _ = u.shape
    return u.astype(jnp.float32).reshape(bsz, seq_len, N_SSM_GROUPS, SSM_GROUP)


def moe_ffn(h, w_router, b_router, w_gu, b_gu, w_down, b_down):
    bsz, seq_len, d = h.shape
    t = h.reshape(-1, d)
    n_tok = t.shape[0]
    n_slot = n_tok * TOP_K
    logits = (t @ w_router + b_router).astype(jnp.float32)
    top_logit, top_expert = lax.top_k(logits, TOP_K)
    gate = jax.nn.softmax(top_logit, axis=-1)
    slot_expert = top_expert.reshape(-1)
    order = jnp.argsort(slot_expert)
    srt_expert = slot_expert[order]
    srt_token = order // TOP_K
    srt_gate = gate.reshape(-1)[order]
    counts = jnp.bincount(slot_expert, length=N_EXPERTS)
    padded = (counts + MOE_BLOCK - 1) // MOE_BLOCK * MOE_BLOCK
    pad_end = jnp.cumsum(padded)
    pad_start = pad_end - padded
    start = jnp.cumsum(counts) - counts
    dest = pad_start[srt_expert] + jnp.arange(n_slot) - start[srt_expert]
    n_blocks = -(-n_slot // MOE_BLOCK) + N_EXPERTS
    cap = n_blocks * MOE_BLOCK
    buf_token = jnp.full((cap,), n_tok, jnp.int32).at[dest].set(srt_token.astype(jnp.int32))
    buf_gate = jnp.zeros((cap,), jnp.float32).at[dest].set(srt_gate)
    block_expert = jnp.minimum(
        jnp.searchsorted(pad_end, jnp.arange(n_blocks) * MOE_BLOCK, side="right"),
        N_EXPERTS - 1)
    t_pad = jnp.concatenate([t, jnp.zeros((1, d), t.dtype)], axis=0)
    xb = t_pad[buf_token].reshape(n_blocks, MOE_BLOCK, d)

    def expert_block(args):
        xe, e = args
        gu = xe @ w_gu[e] + b_gu[e]
        g, up = gu[:, :D_EXPERT], gu[:, D_EXPERT:]
        g = jnp.minimum(g, SWIGLU_LIMIT)
        up = jnp.clip(up, -SWIGLU_LIMIT, SWIGLU_LIMIT)
        act = g * jax.nn.sigmoid(SWIGLU_ALPHA * g) * (up + 1)
        return act @ w_down[e] + b_down[e]

    yb = lax.map(expert_block, (xb, block_expert)).reshape(cap, d)
    yb = yb * buf_gate[:, None].astype(yb.dtype)
    out = jnp.zeros((n_tok + 1, d), yb.dtype).at[buf_token].add(yb)[:n_tok]
    return out.reshape(bsz, seq_len, d)


def setup_inputs(seed: int = 0) -> dict:
    key = jax.random.key(seed)
    ks = iter(jax.random.split(key, 40))
    f = jnp.float32

    def nrm(shape, scale):
        return scale * jax.random.normal(next(ks), shape, f)

    G, P, CH = N_SSM_GROUPS, SSM_STATE, SSM_GROUP
    n_idx = jnp.arange(P, dtype=f)
    lam_re = -0.5 + nrm((DEPTH, 2, G, P), 0.01)
    lam_im = math.pi * n_idx + nrm((DEPTH, 2, G, P), 0.01)
    log_dt = jax.random.uniform(next(ks), (DEPTH, 2, G), f, math.log(1e-3), math.log(1e-1))
    return {
        "x": nrm((BATCH, SEQ, D_MODEL), 1.0),
        "c": nrm((BATCH, D_MODEL), 1.0),
        "ctx": nrm((BATCH, CTX_LEN, D_MODEL), 1.0),
        "c_ctx": nrm((D_MODEL,), 1.0),
        "w_mod": nrm((DEPTH, D_MODEL, 6 * D_MODEL), D_MODEL ** -0.5),
        "b_mod": nrm((DEPTH, 6 * D_MODEL), 0.02),
        "g_mix": 1.0 + nrm((DEPTH, D_MODEL), 0.02),
        "w_in": nrm((DEPTH, D_MODEL, D_IN), D_MODEL ** -0.5),
        "lam_re": lam_re,
        "lam_im": lam_im,
        "log_dt": log_dt,
        "b_re": nrm((DEPTH, 2, G, P, CH), (2 * CH) ** -0.5),
        "b_im": nrm((DEPTH, 2, G, P, CH), (2 * CH) ** -0.5),
        "c_re": nrm((DEPTH, 2, G, CH, P), (2 * P) ** -0.5),
        "c_im": nrm((DEPTH, 2, G, CH, P), (2 * P) ** -0.5),
        "d_skip": nrm((DEPTH, D_SSM), 1.0),
        "w_glu": nrm((DEPTH, D_SSM, D_SSM), D_SSM ** -0.5),
        "conv_w": nrm((DEPTH, 3, D_CONV), 3 ** -0.5),
        "conv_b": nrm((DEPTH, D_CONV), 0.02),
        "w_ssm_br": nrm((DEPTH, D_SSM, D_MODEL), D_SSM ** -0.5),
        "w_conv_br": nrm((DEPTH, D_CONV, D_MODEL), D_CONV ** -0.5),
        "w_o": nrm((DEPTH, D_MODEL, D_MODEL), D_MODEL ** -0.5),
        "g_ffn": 1.0 + nrm((DEPTH, D_MODEL), 0.02),
        "w_router": nrm((DEPTH, D_MODEL, N_EXPERTS), D_MODEL ** -0.5),
        "b_router": nrm((DEPTH, N_EXPERTS), 0.01),
        "w_gu": nrm((DEPTH, N_EXPERTS, D_MODEL, 2 * D_EXPERT), D_MODEL ** -0.5),
        "b_gu": nrm((DEPTH, N_EXPERTS, 2 * D_EXPERT), 0.02),
        "w_down": nrm((DEPTH, N_EXPERTS, D_EXPERT, D_MODEL), D_EXPERT ** -0.5),
        "b_down": nrm((DEPTH, N_EXPERTS, D_MODEL), 0.02),
        "g_final": 1.0 + nrm((D_MODEL,), 0.02),
    }


def reference(x, c, ctx, c_ctx, w_mod, b_mod, g_mix, w_in, lam_re, lam_im, log_dt,
              b_re, b_im, c_re, c_im, d_skip, w_glu, conv_w, conv_b, w_ssm_br,
              w_conv_br, w_o, g_ffn, w_router, b_router, w_gu, b_gu, w_down, b_down,
              g_final):
    rows = x.shape[1] // GRID_W
    ctx_len = ctx.shape[1]
    for i in range(DEPTH):
        last = i == DEPTH - 1
        mx = adaln(c[:, None, :], w_mod[i], b_mod[i])
        mc = adaln(c_ctx[None, None, :], w_mod[i], b_mod[i])

        hx = modulate(rmsnorm(x, g_mix[i]), mx[0], mx[1])
        hc = modulate(rmsnorm(ctx, g_mix[i]), mc[0], mc[1])
        px = jnp.split(hx @ w_in[i], SPLITS, axis=-1)
        pc = jnp.split(hc @ w_in[i], SPLITS, axis=-1)
        disc_f = s5_discretise(lam_re[i, 0], lam_im[i, 0], log_dt[i, 0], b_re[i, 0], b_im[i, 0])
        disc_b = s5_discretise(lam_re[i, 1], lam_im[i, 1], log_dt[i, 1], b_re[i, 1], b_im[i, 1])
        uc = group_channels(pc[0])
        ux = group_channels(px[0])
        sc_f = s5_scan(uc, disc_f, None, False)
        sc_b = s5_scan(uc, disc_b, None, True)
        h0_f = (sc_f[0][-1], sc_f[1][-1])
        h0_b = (sc_b[0][0], sc_b[1][0])
        sx_f = s5_scan(ux, disc_f, h0_f, False)
        sx_b = s5_scan(ux, disc_b, h0_b, True)
        mix_params = (c_re[i], c_im[i], d_skip[i], w_glu[i], conv_w[i], conv_b[i],
                      w_ssm_br[i], w_conv_br[i], w_o[i])
        x = x + mx[2] * mixer_merge(px, sx_f, sx_b, rows, GRID_W, *mix_params)
        if not last:
            ctx = ctx + mc[2] * mixer_merge(pc, sc_f, sc_b, 1, ctx_len, *mix_params)

        moe_params = (w_router[i], b_router[i], w_gu[i], b_gu[i], w_down[i], b_down[i])
        x = x + mx[5] * moe_ffn(modulate(rmsnorm(x, g_ffn[i]), mx[3], mx[4]), *moe_params)
        if not last:
            ctx = ctx + mc[5] * moe_ffn(modulate(rmsnorm(ctx, g_ffn[i]), mc[3], mc[4]), *moe_params)
    return rmsnorm(x, g_final)
```

```python
import functools
import math

import jax
import jax.numpy as jnp
from jax import lax
from jax.experimental import pallas as pl
from jax.experimental.pallas import tpu as pltpu

F32 = jnp.float32
BF16 = jnp.bfloat16
HIGHEST = lax.Precision.HIGHEST

RMS_EPS = 1e-6
GRID_W = 64
SSM_GROUP = 16
SSM_STATE = 64
N_EXPERTS = 32
TOP_K = 4
SWIGLU_LIMIT = 7.0
SWIGLU_ALPHA = 1.702

CHUNK_T = 16
LANES = 128
V7X_VMEM_LIMIT_BYTES = 56 * 1024 * 1024

FRONT_TM = 1024
S5_TR = 1024
BACK_TM = 512
MOE_TB = 256
COMBINE_TM = 256


def _cparams(sem):
    return pltpu.CompilerParams(dimension_semantics=sem,
                                vmem_limit_bytes=V7X_VMEM_LIMIT_BYTES)


def _bdot(a, b):
    return jnp.dot(a, b, preferred_element_type=F32)


def _rmsnorm(xt, g):
    ms = jnp.mean(xt * xt, axis=-1, keepdims=True)
    return xt * lax.rsqrt(ms + RMS_EPS) * g


def _adaln_kernel(c_ref, w_ref, b_ref, o_ref):
    s = jax.nn.silu(c_ref[...])
    o_ref[...] = jnp.dot(s, w_ref[...], precision=HIGHEST,
                         preferred_element_type=F32) + b_ref[...]


def _adaln(cond, w_mod, b_mod):
    r, d = cond.shape
    n = w_mod.shape[1]
    tn = n // 4
    return pl.pallas_call(
        _adaln_kernel,
        out_shape=jax.ShapeDtypeStruct((r, n), F32),
        grid=(n // tn,),
        in_specs=[pl.BlockSpec((r, d), lambda j: (0, 0)),
                  pl.BlockSpec((d, tn), lambda j: (0, j)),
                  pl.BlockSpec((1, tn), lambda j: (0, j))],
        out_specs=pl.BlockSpec((r, tn), lambda j: (0, j)),
        compiler_params=_cparams(("arbitrary",)),
        name="adaln",
    )(cond, w_mod, b_mod.reshape(1, n))


def _front_kernel(x_ref, mod_ref, g_ref, w_ref, u_ref):
    xn = _rmsnorm(x_ref[0], g_ref[...])
    hx = xn * (1.0 + mod_ref[0, 1:2, :]) + mod_ref[0, 0:1, :]
    u_ref[0] = _bdot(hx.astype(BF16), w_ref[...]).astype(u_ref.dtype)


def _front(x, mod, g_mix, w_u):
    b, s, d = x.shape
    n = w_u.shape[1]
    tm = min(FRONT_TM, s)
    assert s % tm == 0
    shared = mod.shape[0] == 1
    return pl.pallas_call(
        _front_kernel,
        out_shape=jax.ShapeDtypeStruct((b, s, n), BF16),
        grid=(b, s // tm),
        in_specs=[pl.BlockSpec((1, tm, d), lambda i, j: (i, j, 0)),
                  pl.BlockSpec((1, 8, d), (lambda i, j: (0, 0, 0)) if shared
                               else (lambda i, j: (i, 0, 0))),
                  pl.BlockSpec((1, d), lambda i, j: (0, 0)),
                  pl.BlockSpec((d, n), lambda i, j: (0, 0))],
        out_specs=pl.BlockSpec((1, tm, n), lambda i, j: (i, j, 0)),
        compiler_params=_cparams(("arbitrary", "arbitrary")),
        name="front",
    )(x, mod, g_mix, w_u)


def _s5_discretise(lam_re, lam_im, log_dt, b_re, b_im):
    dt = jnp.exp(log_dt)[..., None]
    mag = jnp.exp(lam_re * dt)
    a_re, a_im = mag * jnp.cos(lam_im * dt), mag * jnp.sin(lam_im * dt)
    den = lam_re * lam_re + lam_im * lam_im
    q_re = ((a_re - 1) * lam_re + a_im * lam_im) / den
    q_im = (a_im * lam_re - (a_re - 1) * lam_im) / den
    bb_re = q_re[..., None] * b_re - q_im[..., None] * b_im
    bb_im = q_re[..., None] * b_im + q_im[..., None] * b_re
    return a_re, a_im, bb_re, bb_im


def _s5_matrices(lam_re, lam_im, log_dt, b_re, b_im, c_re, c_im):
    t = CHUNK_T
    a_re, a_im, bb_re, bb_im = _s5_discretise(lam_re, lam_im, log_dt, b_re, b_im)
    g, p = a_re.shape[1], a_re.shape[2]
    ch = bb_re.shape[-1]
    pw_re, pw_im = [jnp.ones_like(a_re)], [jnp.zeros_like(a_im)]
    for _ in range(t):
        r, i = pw_re[-1], pw_im[-1]
        pw_re.append(r * a_re - i * a_im)
        pw_im.append(r * a_im + i * a_re)
    pw_re, pw_im = jnp.stack(pw_re), jnp.stack(pw_im)
    cr = c_re[None] * pw_re[:, :, :, None, :] - c_im[None] * pw_im[:, :, :, None, :]
    ci = c_re[None] * pw_im[:, :, :, None, :] + c_im[None] * pw_re[:, :, :, None, :]
    kern = (jnp.einsum("dzgcp,zgpe->dzgce", cr, bb_re, precision=HIGHEST)
            - jnp.einsum("dzgcp,zgpe->dzgce", ci, bb_im, precision=HIGHEST))
    s_idx = jnp.arange(t)[:, None]
    t_idx = jnp.arange(t)[None, :]
    dlt = t_idx - s_idx
    kf = kern[jnp.clip(dlt, 0, t - 1), 0]
    kb = kern[jnp.clip(-dlt, 0, t - 1), 1]
    msk_f = (dlt >= 0)[:, :, None, None, None]
    msk_b = (dlt <= 0)[:, :, None, None, None]
    tsum = jnp.where(msk_f, kf, 0.0) + jnp.where(msk_b, kb, 0.0)
    toep = tsum.transpose(2, 0, 4, 1, 3).reshape(g, t * ch, t * ch)

    pf_re, pf_im = pw_re[t - 1 - jnp.arange(t), 0], pw_im[t - 1 - jnp.arange(t), 0]
    pb_re, pb_im = pw_re[jnp.arange(t), 1], pw_im[jnp.arange(t), 1]

    def in_to_state(q_re, q_im, z):
        bre = bb_re[z].transpose(0, 2, 1)[None]
        bim = bb_im[z].transpose(0, 2, 1)[None]
        s_re = q_re[:, :, None, :] * bre - q_im[:, :, None, :] * bim
        s_im = q_re[:, :, None, :] * bim + q_im[:, :, None, :] * bre
        to_rows = lambda m: m.transpose(1, 0, 2, 3).reshape(g, t * ch, p)
        return to_rows(s_re), to_rows(s_im)

    sf_re, sf_im = in_to_state(pf_re, pf_im, 0)
    sb_re, sb_im = in_to_state(pb_re, pb_im, 1)
    s_all = jnp.stack([sf_re, sf_im, sb_re, sb_im], axis=1)
    s_all = s_all.reshape(g // 2, 2, 4, t * ch, p)
    eye2 = jnp.eye(2, dtype=F32)
    smat = jnp.einsum("ajqrp,jk->ajrqkp", s_all, eye2).reshape(g // 2, 2 * t * ch, 4 * 2 * p)

    def state_to_out(idx, z):
        cre = cr[idx, z].transpose(1, 3, 0, 2).reshape(g, p, t * ch)
        cim = ci[idx, z].transpose(1, 3, 0, 2).reshape(g, p, t * ch)
        return cre, -cim

    cf_re, cf_im = state_to_out(jnp.arange(t) + 1, 0)
    cb_re, cb_im = state_to_out(t - jnp.arange(t), 1)
    c_all = jnp.stack([cf_re, cf_im, cb_re, cb_im], axis=1)
    c_all = c_all.reshape(g // 2, 2, 4, p, t * ch)
    cpow = jnp.einsum("ajqpn,jk->aqjpkn", c_all, eye2).reshape(g // 2, 4 * 2 * p, 2 * t * ch)

    a_t = jnp.stack([pw_re[t, 0], pw_im[t, 0], pw_re[t, 1], pw_im[t, 1]]).reshape(4, g * p)
    return smat.astype(BF16), toep.astype(BF16), cpow.astype(BF16), a_t


def _to_chunk_rows(u):
    b, s, n = u.shape
    g = n // SSM_GROUP
    k = s // CHUNK_T
    ut = u.reshape(b, k, CHUNK_T, g, SSM_GROUP).transpose(3, 1, 0, 2, 4)
    return ut.reshape(g, k * b, CHUNK_T * SSM_GROUP)


def _from_chunk_rows(y, b):
    g, rows, _ = y.shape
    k = rows // b
    yt = y.reshape(g, k, b, CHUNK_T, SSM_GROUP).transpose(2, 1, 3, 0, 4)
    return yt.reshape(b, k * CHUNK_T, g * SSM_GROUP)


def _s5_states_kernel(u_ref, m_ref, fre_ref, fim_ref, bre_ref, bim_ref):
    lhs = jnp.concatenate([u_ref[0], u_ref[1]], axis=1)
    res = _bdot(lhs, m_ref[0])
    fre_ref[...] = res[:, 0 * LANES:1 * LANES]
    fim_ref[...] = res[:, 1 * LANES:2 * LANES]
    bre_ref[...] = res[:, 2 * LANES:3 * LANES]
    bim_ref[...] = res[:, 3 * LANES:4 * LANES]


def _s5_states(ut, smat):
    g, rows, kdim = ut.shape
    tr = min(S5_TR, rows)
    assert rows % tr == 0
    n_state = g * SSM_STATE
    out = jax.ShapeDtypeStruct((rows, n_state), F32)
    ospec = pl.BlockSpec((tr, LANES), lambda a, r: (r, a))
    return pl.pallas_call(
        _s5_states_kernel,
        out_shape=(out, out, out, out),
        grid=(g // 2, rows // tr),
        in_specs=[pl.BlockSpec((2, tr, kdim), lambda a, r: (a, r, 0)),
                  pl.BlockSpec((1, 2 * kdim, 4 * LANES), lambda a, r: (a, 0, 0))],
        out_specs=(ospec, ospec, ospec, ospec),
        compiler_params=_cparams(("arbitrary", "arbitrary")),
        name="s5_states",
    )(ut, smat)


def _s5_scan_kernel(a_ref, cfr_ref, cfi_ref, cbr_ref, cbi_ref,
                    xfr_ref, xfi_ref, xbr_ref, xbi_ref,
                    hfr_ref, hfi_ref, hbr_ref, hbi_ref, *, batch, kc, kx):
    a_fr, a_fi = a_ref[0:1, :], a_ref[1:2, :]
    a_br, a_bi = a_ref[2:3, :], a_ref[3:4, :]

    def rows(k):
        return pl.ds(pl.multiple_of(k * batch, 8), batch)

    def step(h_re, h_im, a_re, a_im, s_re, s_im):
        return (a_re * h_re - a_im * h_im + s_re, a_re * h_im + a_im * h_re + s_im)

    zero = jnp.zeros((batch, LANES), F32)

    def ctx_body(i, c):
        fr, fi, br, bi = c
        kf, kb = i, kc - 1 - i
        fr, fi = step(fr, fi, a_fr, a_fi, cfr_ref[rows(kf), :], cfi_ref[rows(kf), :])
        br, bi = step(br, bi, a_br, a_bi, cbr_ref[rows(kb), :], cbi_ref[rows(kb), :])
        return fr, fi, br, bi

    carry = lax.fori_loop(0, kc, ctx_body, (zero, zero, zero, zero))

    def x_body(i, c):
        fr, fi, br, bi = c
        kf, kb = i, kx - 1 - i
        hfr_ref[rows(kf), :] = fr.astype(hfr_ref.dtype)
        hfi_ref[rows(kf), :] = fi.astype(hfi_ref.dtype)
        hbr_ref[rows(kb), :] = br.astype(hbr_ref.dtype)
        hbi_ref[rows(kb), :] = bi.astype(hbi_ref.dtype)
        fr, fi = step(fr, fi, a_fr, a_fi, xfr_ref[rows(kf), :], xfi_ref[rows(kf), :])
        br, bi = step(br, bi, a_br, a_bi, xbr_ref[rows(kb), :], xbi_ref[rows(kb), :])
        return fr, fi, br, bi

    lax.fori_loop(0, kx, x_body, carry)


def _s5_scan(a_t, s_ctx, s_x, batch):
    rows_c, n_state = s_ctx[0].shape
    rows_x = s_x[0].shape[0]
    kc, kx = rows_c // batch, rows_x // batch
    out = jax.ShapeDtypeStruct((rows_x, n_state), BF16)
    cspec = pl.BlockSpec((rows_c, LANES), lambda j: (0, j))
    xspec = pl.BlockSpec((rows_x, LANES), lambda j: (0, j))
    return pl.pallas_call(
        functools.partial(_s5_scan_kernel, batch=batch, kc=kc, kx=kx),
        out_shape=(out, out, out, out),
        grid=(n_state // LANES,),
        in_specs=[pl.BlockSpec((4, LANES), lambda j: (0, j))] + [cspec] * 4 + [xspec] * 4,
        out_specs=(xspec, xspec, xspec, xspec),
        compiler_params=_cparams(("arbitrary",)),
        name="s5_scan",
    )(a_t, *s_ctx, *s_x)


def _s5_apply_kernel(u_ref, t_ref, c_ref, hfr_ref, hfi_ref, hbr_ref, hbi_ref, y_ref):
    hcat = jnp.concatenate([hfr_ref[...], hfi_ref[...], hbr_ref[...], hbi_ref[...]], axis=1)
    yst = _bdot(hcat, c_ref[0])
    n = u_ref.shape[2]
    y_ref[0] = (_bdot(u_ref[0], t_ref[0]) + yst[:, :n]).astype(y_ref.dtype)
    y_ref[1] = (_bdot(u_ref[1], t_ref[1]) + yst[:, n:]).astype(y_ref.dtype)


def _s5_apply(ut, toep, cpow, h_in):
    g, rows, kdim = ut.shape
    tr = min(S5_TR, rows)
    hspec = pl.BlockSpec((tr, LANES), lambda a, r: (r, a))
    return pl.pallas_call(
        _s5_apply_kernel,
        out_shape=jax.ShapeDtypeStruct((g, rows, kdim), BF16),
        grid=(g // 2, rows // tr),
        in_specs=[pl.BlockSpec((2, tr, kdim), lambda a, r: (a, r, 0)),
                  pl.BlockSpec((2, kdim, kdim), lambda a, r: (a, 0, 0)),
                  pl.BlockSpec((1, 4 * LANES, 2 * kdim), lambda a, r: (a, 0, 0))] + [hspec] * 4,
        out_specs=pl.BlockSpec((2, tr, kdim), lambda a, r: (a, r, 0)),
        compiler_params=_cparams(("arbitrary", "arbitrary")),
        name="s5_apply",
    )(ut, toep, cpow, *h_in)


def _back_kernel(x_ref, u_ref, ys_ref, mod_ref, gmix_ref, win_ref, convw_ref, convb_ref,
                 dskip_ref, wglu_ref, wsbr_ref, wcbr_ref, wo_ref, gffn_ref, wr_ref, br_ref,
                 x1_ref, h_ref, ids_ref, gates_ref):
    xt = x_ref[0]
    tm = xt.shape[0]
    d_ssm = u_ref.shape[2]
    d_model = xt.shape[1]
    xn = _rmsnorm(xt, gmix_ref[...])
    hx = (xn * (1.0 + mod_ref[0, 1:2, :]) + mod_ref[0, 0:1, :]).astype(BF16)
    p = _bdot(hx, win_ref[...])
    v = p[:, 0:d_ssm]
    gate_b = p[:, d_ssm:2 * d_ssm]
    gate_c = p[:, 2 * d_ssm:3 * d_ssm]
    g_s = p[:, 3 * d_ssm:3 * d_ssm + d_model]
    g_c = p[:, 3 * d_ssm + d_model:]

    z = gate_c * v
    col = lax.broadcasted_iota(jnp.int32, z.shape, 0) % GRID_W
    z_prev = jnp.where(col == 0, 0.0, pltpu.roll(z, 1, 0))
    z_next = jnp.where(col == GRID_W - 1, 0.0, pltpu.roll(z, tm - 1, 0))
    conv = (z_prev * convw_ref[0:1, :] + z * convw_ref[1:2, :]
            + z_next * convw_ref[2:3, :] + convb_ref[...])
    y_conv = gate_b * conv

    ys = ys_ref[0].astype(F32) + dskip_ref[...] * u_ref[0].astype(F32)
    ys = jax.nn.gelu(ys)
    ys = ys * jax.nn.sigmoid(_bdot(ys.astype(BF16), wglu_ref[...]))

    y_s = _bdot(ys.astype(BF16), wsbr_ref[...])
    y_c = _bdot(y_conv.astype(BF16), wcbr_ref[...])
    merged = jax.nn.sigmoid(g_s) * y_s + jax.nn.sigmoid(g_c) * y_c
    x1 = xt + mod_ref[0, 2:3, :] * _bdot(merged.astype(BF16), wo_ref[...])
    x1_ref[0] = x1

    hn = _rmsnorm(x1, gffn_ref[...]) * (1.0 + mod_ref[0, 4:5, :]) + mod_ref[0, 3:4, :]
    h_ref[0] = hn

    logits = jnp.dot(hn, wr_ref[...], precision=HIGHEST, preferred_element_type=F32) + br_ref[...]
    lane = lax.broadcasted_iota(jnp.int32, logits.shape, 1)
    neg = jnp.float32(-jnp.inf)
    cur = jnp.where(lane < N_EXPERTS, logits, neg)
    vals, idxs = [], []
    for _ in range(TOP_K):
        mk = jnp.max(cur, axis=-1, keepdims=True)
        ik = jnp.min(jnp.where(cur == mk, lane, LANES), axis=-1, keepdims=True)
        vals.append(mk)
        idxs.append(ik)
        cur = jnp.where(lane == ik, neg, cur)
    exps = [jnp.exp(vk - vals[0]) for vk in vals]
    denom = exps[0] + exps[1] + exps[2] + exps[3]
    ids = jnp.zeros(logits.shape, jnp.int32)
    gates = jnp.zeros(logits.shape, F32)
    for k in range(TOP_K):
        ids = jnp.where(lane == k, idxs[k], ids)
        gates = jnp.where(lane == k, exps[k] / denom, gates)
    ids_ref[0] = ids
    gates_ref[0] = gates


def _back(x, u, y_ssm, mod, g_mix, w_rest, conv_w, conv_b, d_skip, w_glu, w_ssm_br,
          w_conv_br, w_o, g_ffn, w_router, b_router):
    b, s, d = x.shape
    d_ssm = u.shape[2]
    tm = min(BACK_TM, s)
    assert s % tm == 0 and tm % GRID_W == 0
    tok = lambda n: pl.BlockSpec((1, tm, n), lambda i, j: (i, j, 0))

    def const(arr):
        nd = arr.ndim
        return pl.BlockSpec(arr.shape, lambda i, j: (0,) * nd, pipeline_mode=pl.Buffered(1))

    params = (g_mix, w_rest, conv_w, conv_b, d_skip, w_glu, w_ssm_br, w_conv_br, w_o, g_ffn,
              w_router, b_router)
    return pl.pallas_call(
        _back_kernel,
        out_shape=(jax.ShapeDtypeStruct((b, s, d), F32), jax.ShapeDtypeStruct((b, s, d), F32),
                   jax.ShapeDtypeStruct((b, s, LANES), jnp.int32),
                   jax.ShapeDtypeStruct((b, s, LANES), F32)),
        grid=(b, s // tm),
        in_specs=[tok(d), tok(d_ssm), tok(d_ssm),
                  pl.BlockSpec((1, 8, d), lambda i, j: (i, 0, 0))] + [const(a) for a in params],
        out_specs=(tok(d), tok(d), tok(LANES), tok(LANES)),
        compiler_params=_cparams(("arbitrary", "arbitrary")),
        name="back",
    )(x, u, y_ssm, mod, *params)


def _routing(ids, n_tok):
    tb = MOE_TB
    n_slot = n_tok * TOP_K
    slot_expert = ids.reshape(-1)
    order = jnp.argsort(slot_expert, stable=True).astype(jnp.int32)
    inv = jnp.argsort(order).astype(jnp.int32)
    srt_expert = slot_expert[order]
    counts = jnp.sum(slot_expert[:, None] == jnp.arange(N_EXPERTS, dtype=jnp.int32)[None, :],
                     axis=0, dtype=jnp.int32)
    padded = (counts + tb - 1) // tb * tb
    pad_end = jnp.cumsum(padded)
    pad_start = pad_end - padded
    start = jnp.cumsum(counts) - counts
    dest_sorted = pad_start[srt_expert] + jnp.arange(n_slot, dtype=jnp.int32) - start[srt_expert]
    pos = dest_sorted[inv].reshape(n_tok, TOP_K)
    n_blocks = n_slot // tb + N_EXPERTS
    cap = n_blocks * tb
    block_expert = jnp.minimum(
        jnp.searchsorted(pad_end, jnp.arange(n_blocks, dtype=jnp.int32) * tb, side="right"),
        N_EXPERTS - 1).astype(jnp.int32)
    row = jnp.arange(cap, dtype=jnp.int32)
    row_expert = jnp.repeat(block_expert, tb)
    within = row - pad_start[row_expert]
    valid = within < counts[row_expert]
    src = jnp.clip(start[row_expert] + within, 0, n_slot - 1)
    buf_token = jnp.where(valid, order[src] // TOP_K, 0).astype(jnp.int32)
    n_used = (pad_end[-1] // tb).astype(jnp.int32).reshape(1)
    return buf_token, pos.astype(jnp.int32), block_expert, n_used


def _row_gather_start(idx_ref, src_hbm, dst, sem, n_rows):
    def body(r, carry):
        t = idx_ref[0, 0, r]
        pltpu.make_async_copy(src_hbm.at[pl.ds(t, 1), :], dst.at[pl.ds(r, 1), :], sem).start()
        return carry
    lax.fori_loop(0, n_rows, body, 0, unroll=8)


def _row_gather_wait(src_hbm, dst, sem, n_rows):
    pltpu.make_async_copy(src_hbm.at[pl.ds(0, n_rows), :], dst, sem).wait()


def _moe_kernel(be_ref, nu_ref, tokc_ref, tokn_ref, h_hbm, wgu_ref, bgu_ref, wd_ref, bd_ref,
                out_ref, xbuf, sem):
    del be_ref
    i = pl.program_id(0)
    n_used = nu_ref[0]
    slot = i % 2
    tb = xbuf.shape[1]
    f = wd_ref.shape[1]

    @pl.when(i == 0)
    def _():
        _row_gather_start(tokc_ref, h_hbm, xbuf.at[0], sem.at[0], tb)

    @pl.when(i + 1 < n_used)
    def _():
        _row_gather_start(tokn_ref, h_hbm, xbuf.at[1 - slot], sem.at[1 - slot], tb)

    @pl.when(i < n_used)
    def _():
        _row_gather_wait(h_hbm, xbuf.at[slot], sem.at[slot], tb)
        xe = xbuf[slot].astype(BF16)
        gu = _bdot(xe, wgu_ref[0]) + bgu_ref[0]
        gt = jnp.minimum(gu[:, :f], SWIGLU_LIMIT)
        up = jnp.clip(gu[:, f:], -SWIGLU_LIMIT, SWIGLU_LIMIT)
        act = gt * jax.nn.sigmoid(SWIGLU_ALPHA * gt) * (up + 1.0)
        out_ref[...] = _bdot(act.astype(BF16), wd_ref[0]) + bd_ref[0]

    @pl.when(i >= n_used)
    def _():
        out_ref[...] = jnp.zeros(out_ref.shape, out_ref.dtype)


def _moe(h2d, buf_token, block_expert, n_used, w_gu, b_gu, w_down, b_down):
    n_tok, d = h2d.shape
    e, _, f2 = w_gu.shape
    f = f2 // 2
    tb = MOE_TB
    cap = buf_token.shape[0]
    n_blocks = cap // tb
    tok3 = buf_token.reshape(n_blocks, 1, tb)
    smem_blk = lambda imap: pl.BlockSpec((1, 1, tb), imap, memory_space=pltpu.SMEM)
    grid_spec = pltpu.PrefetchScalarGridSpec(
        num_scalar_prefetch=2,
        grid=(n_blocks,),
        in_specs=[smem_blk(lambda i, be, nu: (i, 0, 0)),
                  smem_blk(lambda i, be, nu: (jnp.minimum(i + 1, n_blocks - 1), 0, 0)),
                  pl.BlockSpec(memory_space=pl.ANY),
                  pl.BlockSpec((1, d, f2), lambda i, be, nu: (be[i], 0, 0)),
                  pl.BlockSpec((1, 1, f2), lambda i, be, nu: (be[i], 0, 0)),
                  pl.BlockSpec((1, f, d), lambda i, be, nu: (be[i], 0, 0)),
                  pl.BlockSpec((1, 1, d), lambda i, be, nu: (be[i], 0, 0))],
        out_specs=pl.BlockSpec((tb, d), lambda i, be, nu: (i, 0)),
        scratch_shapes=[pltpu.VMEM((2, tb, d), F32), pltpu.SemaphoreType.DMA((2,))],
    )
    return pl.pallas_call(
        _moe_kernel,
        out_shape=jax.ShapeDtypeStruct((cap, d), F32),
        grid_spec=grid_spec,
        compiler_params=_cparams(("arbitrary",)),
        name="moe",
    )(block_expert, n_used, tok3, tok3, h2d, w_gu, b_gu.reshape(e, 1, f2),
      w_down, b_down.reshape(e, 1, d))


def _combine_kernel(posc_ref, posn_ref, yb_hbm, x1_ref, gates_ref, mod_ref, gfin_ref,
                    out_ref, buf, sem):
    i = pl.program_id(0)
    n = pl.num_programs(0)
    slot = i % 2
    n_rows = buf.shape[1]
    tm = n_rows // TOP_K

    @pl.when(i == 0)
    def _():
        _row_gather_start(posc_ref, yb_hbm, buf.at[0], sem.at[0], n_rows)

    @pl.when(i + 1 < n)
    def _():
        _row_gather_start(posn_ref, yb_hbm, buf.at[1 - slot], sem.at[1 - slot], n_rows)

    _row_gather_wait(yb_hbm, buf.at[slot], sem.at[slot], n_rows)
    g = gates_ref[...]
    acc = g[:, 0:1] * buf[slot, pl.ds(0, tm), :]
    for k in range(1, TOP_K):
        acc = acc + g[:, k:k + 1] * buf[slot, pl.ds(k * tm, tm), :]
    x2 = x1_ref[...] + mod_ref[0, 5:6, :] * acc
    out_ref[...] = _rmsnorm(x2, gfin_ref[...])


def _combine(yb, pos, x1_2d, gates_2d, mod, g_final, seq):
    n_tok, d = x1_2d.shape
    tm = min(COMBINE_TM, seq)
    assert seq % tm == 0
    n_tiles = n_tok // tm
    per_batch = seq // tm
    pos3 = pos.reshape(n_tiles, tm, TOP_K).transpose(0, 2, 1).reshape(n_tiles, 1, TOP_K * tm)
    smem_blk = lambda imap: pl.BlockSpec((1, 1, TOP_K * tm), imap, memory_space=pltpu.SMEM)
    return pl.pallas_call(
        _combine_kernel,
        out_shape=jax.ShapeDtypeStruct((n_tok, d), F32),
        grid=(n_tiles,),
        in_specs=[smem_blk(lambda i: (i, 0, 0)),
                  smem_blk(lambda i: (jnp.minimum(i + 1, n_tiles - 1), 0, 0)),
                  pl.BlockSpec(memory_space=pl.ANY),
                  pl.BlockSpec((tm, d), lambda i: (i, 0)),
                  pl.BlockSpec((tm, LANES), lambda i: (i, 0)),
                  pl.BlockSpec((1, 8, d), lambda i: (i // per_batch, 0, 0)),
                  pl.BlockSpec((1, d), lambda i: (0, 0))],
        out_specs=pl.BlockSpec((tm, d), lambda i: (i, 0)),
        scratch_shapes=[pltpu.VMEM((2, TOP_K * tm, d), F32), pltpu.SemaphoreType.DMA((2,))],
        compiler_params=_cparams(("arbitrary",)),
        name="combine",
    )(pos3, pos3, yb, x1_2d, gates_2d, mod, g_final)


def kernel(x, c, ctx, c_ctx, w_mod, b_mod, g_mix, w_in, lam_re, lam_im, log_dt, b_re, b_im,
           c_re, c_im, d_skip, w_glu, conv_w, conv_b, w_ssm_br, w_conv_br, w_o, g_ffn,
           w_router, b_router, w_gu, b_gu, w_down, b_down, g_final):
    depth = w_mod.shape[0]
    assert depth == 1, "single-layer trunk"
    bsz, seq, d = x.shape
    ctx_len = ctx.shape[1]
    d_ssm = d // 2
    assert bsz % 8 == 0 and seq % CHUNK_T == 0 and ctx_len % CHUNK_T == 0 and seq % GRID_W == 0

    n_cond = -(-(bsz + 1) // 8) * 8
    cond = jnp.zeros((n_cond, d), F32).at[:bsz].set(c).at[bsz].set(c_ctx)
    m = _adaln(cond, w_mod[0], b_mod[0])
    zeros2 = jnp.zeros((n_cond, 2, d), F32)
    mod_all = jnp.concatenate([m.reshape(n_cond, 6, d), zeros2], axis=1)
    mod_x, mod_c = mod_all[:bsz], mod_all[bsz:bsz + 1]

    w_in_bf = w_in[0].astype(BF16)
    w_u, w_rest = w_in_bf[:, :d_ssm], w_in_bf[:, d_ssm:]
    gm = g_mix[0].reshape(1, d)

    ux = _front(x, mod_x, gm, w_u)
    uc = _front(ctx, mod_c, gm, w_u)
    utx, utc = _to_chunk_rows(ux), _to_chunk_rows(uc)

    smat, toep, cpow, a_t = _s5_matrices(lam_re[0], lam_im[0], log_dt[0], b_re[0], b_im[0],
                                         c_re[0], c_im[0])
    s_x = _s5_states(utx, smat)
    s_c = _s5_states(utc, smat)
    h_in = _s5_scan(a_t, s_c, s_x, bsz)
    y_ssm = _from_chunk_rows(_s5_apply(utx, toep, cpow, h_in), bsz)

    pad_r = jnp.zeros((d, LANES - N_EXPERTS), F32)
    w_r = jnp.concatenate([w_router[0], pad_r], axis=1)
    b_r = jnp.concatenate([b_router[0], jnp.zeros((LANES - N_EXPERTS,), F32)]).reshape(1, LANES)
    x1, h, ids, gates = _back(
        x, ux, y_ssm, mod_x, gm, w_rest, conv_w[0], conv_b[0].reshape(1, d_ssm),
        d_skip[0].reshape(1, d_ssm), w_glu[0].astype(BF16), w_ssm_br[0].astype(BF16),
        w_conv_br[0].astype(BF16), w_o[0].astype(BF16), g_ffn[0].reshape(1, d), w_r, b_r)

    n_tok = bsz * seq
    buf_token, pos, block_expert, n_used = _routing(ids.reshape(n_tok, LANES)[:, :TOP_K], n_tok)
    yb = _moe(h.reshape(n_tok, d), buf_token, block_expert, n_used,
              w_gu[0].astype(BF16), b_gu[0], w_down[0].astype(BF16), b_down[0])
    out = _combine(yb, pos, x1.reshape(n_tok, d), gates.reshape(n_tok, LANES), mod_x,
                   g_final.reshape(1, d), seq)
    return out.reshape(bsz, seq, d)
```

```python
import functools
import math

import jax
import jax.numpy as jnp
from jax import lax
from jax.experimental import pallas as pl
from jax.experimental.pallas import tpu as pltpu

F32 = jnp.float32
BF16 = jnp.bfloat16
HIGHEST = lax.Precision.HIGHEST

RMS_EPS = 1e-6
GRID_W = 64
SSM_GROUP = 16
SSM_STATE = 64
N_EXPERTS = 32
TOP_K = 4
SWIGLU_LIMIT = 7.0
SWIGLU_ALPHA = 1.702

CHUNK_T = 16
LANES = 128
SUBLANES = 8
V7X_VMEM_LIMIT_BYTES = 56 * 1024 * 1024

FRONT_TM = 1024
S5_TR = 1024
BACK_TM = 512
MOE_TB = 256
COMBINE_TM = 256


def _cparams(sem):
    return pltpu.CompilerParams(dimension_semantics=sem,
                                vmem_limit_bytes=V7X_VMEM_LIMIT_BYTES)


def _bdot(a, b):
    return jnp.dot(a, b, preferred_element_type=F32)


def _rmsnorm(xt, g):
    ms = jnp.mean(xt * xt, axis=-1, keepdims=True)
    return xt * lax.rsqrt(ms + RMS_EPS) * g


def _adaln_kernel(c_ref, w_ref, b_ref, o_ref):
    s = jax.nn.silu(c_ref[...])
    o_ref[...] = jnp.dot(s, w_ref[...], precision=HIGHEST,
                         preferred_element_type=F32) + b_ref[...]


def _adaln(cond, w_mod, b_mod):
    r, d = cond.shape
    n = w_mod.shape[1]
    tn = n // 4
    return pl.pallas_call(
        _adaln_kernel,
        out_shape=jax.ShapeDtypeStruct((r, n), F32),
        grid=(n // tn,),
        in_specs=[pl.BlockSpec((r, d), lambda j: (0, 0)),
                  pl.BlockSpec((d, tn), lambda j: (0, j)),
                  pl.BlockSpec((1, tn), lambda j: (0, j))],
        out_specs=pl.BlockSpec((r, tn), lambda j: (0, j)),
        compiler_params=_cparams(("arbitrary",)),
        name="adaln",
    )(cond, w_mod, b_mod.reshape(1, n))


def _front_kernel(x_ref, mod_ref, g_ref, w_ref, u_ref):
    xn = _rmsnorm(x_ref[0], g_ref[...])
    hx = xn * (1.0 + mod_ref[0, 1:2, :]) + mod_ref[0, 0:1, :]
    u_ref[0] = _bdot(hx.astype(BF16), w_ref[...]).astype(u_ref.dtype)


def _front(x, mod, g_mix, w_u):
    b, s, d = x.shape
    n = w_u.shape[1]
    tm = min(FRONT_TM, s)
    assert s % tm == 0
    shared = mod.shape[0] == 1
    return pl.pallas_call(
        _front_kernel,
        out_shape=jax.ShapeDtypeStruct((b, s, n), BF16),
        grid=(b, s // tm),
        in_specs=[pl.BlockSpec((1, tm, d), lambda i, j: (i, j, 0)),
                  pl.BlockSpec((1, 8, d), (lambda i, j: (0, 0, 0)) if shared
                               else (lambda i, j: (i, 0, 0))),
                  pl.BlockSpec((1, d), lambda i, j: (0, 0)),
                  pl.BlockSpec((d, n), lambda i, j: (0, 0))],
        out_specs=pl.BlockSpec((1, tm, n), lambda i, j: (i, j, 0)),
        compiler_params=_cparams(("arbitrary", "arbitrary")),
        name="front",
    )(x, mod, g_mix, w_u)


def _s5_discretise(lam_re, lam_im, log_dt, b_re, b_im):
    dt = jnp.exp(log_dt)[..., None]
    mag = jnp.exp(lam_re * dt)
    a_re, a_im = mag * jnp.cos(lam_im * dt), mag * jnp.sin(lam_im * dt)
    den = lam_re * lam_re + lam_im * lam_im
    q_re = ((a_re - 1) * lam_re + a_im * lam_im) / den
    q_im = (a_im * lam_re - (a_re - 1) * lam_im) / den
    bb_re = q_re[..., None] * b_re - q_im[..., None] * b_im
    bb_im = q_re[..., None] * b_im + q_im[..., None] * b_re
    return a_re, a_im, bb_re, bb_im


def _s5_matrices(lam_re, lam_im, log_dt, b_re, b_im, c_re, c_im):
    t = CHUNK_T
    a_re, a_im, bb_re, bb_im = _s5_discretise(lam_re, lam_im, log_dt, b_re, b_im)
    g, p = a_re.shape[1], a_re.shape[2]
    ch = bb_re.shape[-1]
    pw_re, pw_im = [jnp.ones_like(a_re)], [jnp.zeros_like(a_im)]
    for _ in range(t):
        r, i = pw_re[-1], pw_im[-1]
        pw_re.append(r * a_re - i * a_im)
        pw_im.append(r * a_im + i * a_re)
    pw_re, pw_im = jnp.stack(pw_re), jnp.stack(pw_im)
    cr = c_re[None] * pw_re[:, :, :, None, :] - c_im[None] * pw_im[:, :, :, None, :]
    ci = c_re[None] * pw_im[:, :, :, None, :] + c_im[None] * pw_re[:, :, :, None, :]
    kern = (jnp.einsum("dzgcp,zgpe->dzgce", cr, bb_re, precision=HIGHEST)
            - jnp.einsum("dzgcp,zgpe->dzgce", ci, bb_im, precision=HIGHEST))
    s_idx = jnp.arange(t)[:, None]
    t_idx = jnp.arange(t)[None, :]
    dlt = t_idx - s_idx
    kf = kern[jnp.clip(dlt, 0, t - 1), 0]
    kb = kern[jnp.clip(-dlt, 0, t - 1), 1]
    msk_f = (dlt >= 0)[:, :, None, None, None]
    msk_b = (dlt <= 0)[:, :, None, None, None]
    tsum = jnp.where(msk_f, kf, 0.0) + jnp.where(msk_b, kb, 0.0)
    toep = tsum.transpose(2, 0, 4, 1, 3).reshape(g, t * ch, t * ch)

    pf_re, pf_im = pw_re[t - 1 - jnp.arange(t), 0], pw_im[t - 1 - jnp.arange(t), 0]
    pb_re, pb_im = pw_re[jnp.arange(t), 1], pw_im[jnp.arange(t), 1]

    def in_to_state(q_re, q_im, z):
        bre = bb_re[z].transpose(0, 2, 1)[None]
        bim = bb_im[z].transpose(0, 2, 1)[None]
        s_re = q_re[:, :, None, :] * bre - q_im[:, :, None, :] * bim
        s_im = q_re[:, :, None, :] * bim + q_im[:, :, None, :] * bre
        to_rows = lambda m: m.transpose(1, 0, 2, 3).reshape(g, t * ch, p)
        return to_rows(s_re), to_rows(s_im)

    sf_re, sf_im = in_to_state(pf_re, pf_im, 0)
    sb_re, sb_im = in_to_state(pb_re, pb_im, 1)
    s_all = jnp.stack([sf_re, sf_im, sb_re, sb_im], axis=1)
    s_all = s_all.reshape(g // 2, 2, 4, t * ch, p)
    eye2 = jnp.eye(2, dtype=F32)
    smat = jnp.einsum("ajqrp,jk->ajrqkp", s_all, eye2).reshape(g // 2, 2 * t * ch, 4 * 2 * p)

    def state_to_out(idx, z):
        cre = cr[idx, z].transpose(1, 3, 0, 2).reshape(g, p, t * ch)
        cim = ci[idx, z].transpose(1, 3, 0, 2).reshape(g, p, t * ch)
        return cre, -cim

    cf_re, cf_im = state_to_out(jnp.arange(t) + 1, 0)
    cb_re, cb_im = state_to_out(t - jnp.arange(t), 1)
    c_all = jnp.stack([cf_re, cf_im, cb_re, cb_im], axis=1)
    c_all = c_all.reshape(g // 2, 2, 4, p, t * ch)
    cpow = jnp.einsum("ajqpn,jk->aqjpkn", c_all, eye2).reshape(g // 2, 4 * 2 * p, 2 * t * ch)

    a_t = jnp.stack([pw_re[t, 0], pw_im[t, 0], pw_re[t, 1], pw_im[t, 1]]).reshape(4, g * p)
    return smat.astype(BF16), toep.astype(BF16), cpow.astype(BF16), a_t


def _to_chunk_rows(u):
    b, s, n = u.shape
    g = n // SSM_GROUP
    k = s // CHUNK_T
    ut = u.reshape(b, k, CHUNK_T, g, SSM_GROUP).transpose(3, 1, 0, 2, 4)
    return ut.reshape(g, k * b, CHUNK_T * SSM_GROUP)


def _from_chunk_rows(y, b):
    g, rows, _ = y.shape
    k = rows // b
    yt = y.reshape(g, k, b, CHUNK_T, SSM_GROUP).transpose(2, 1, 3, 0, 4)
    return yt.reshape(b, k * CHUNK_T, g * SSM_GROUP)


def _s5_states_kernel(u_ref, m_ref, fre_ref, fim_ref, bre_ref, bim_ref):
    lhs = jnp.concatenate([u_ref[0], u_ref[1]], axis=1)
    res = _bdot(lhs, m_ref[0])
    fre_ref[...] = res[:, 0 * LANES:1 * LANES]
    fim_ref[...] = res[:, 1 * LANES:2 * LANES]
    bre_ref[...] = res[:, 2 * LANES:3 * LANES]
    bim_ref[...] = res[:, 3 * LANES:4 * LANES]


def _s5_states(ut, smat):
    g, rows, kdim = ut.shape
    tr = min(S5_TR, rows)
    assert rows % tr == 0
    n_state = g * SSM_STATE
    out = jax.ShapeDtypeStruct((rows, n_state), F32)
    ospec = pl.BlockSpec((tr, LANES), lambda a, r: (r, a))
    return pl.pallas_call(
        _s5_states_kernel,
        out_shape=(out, out, out, out),
        grid=(g // 2, rows // tr),
        in_specs=[pl.BlockSpec((2, tr, kdim), lambda a, r: (a, r, 0)),
                  pl.BlockSpec((1, 2 * kdim, 4 * LANES), lambda a, r: (a, 0, 0))],
        out_specs=(ospec, ospec, ospec, ospec),
        compiler_params=_cparams(("arbitrary", "arbitrary")),
        name="s5_states",
    )(ut, smat)


def _s5_scan_kernel(a_ref, cfr_ref, cfi_ref, cbr_ref, cbi_ref,
                    xfr_ref, xfi_ref, xbr_ref, xbi_ref,
                    hfr_ref, hfi_ref, hbr_ref, hbi_ref, *, batch, kc, kx):
    a_fr, a_fi = a_ref[0:1, :], a_ref[1:2, :]
    a_br, a_bi = a_ref[2:3, :], a_ref[3:4, :]

    def rows(k):
        return pl.ds(pl.multiple_of(k * batch, 8), batch)

    def step(h_re, h_im, a_re, a_im, s_re, s_im):
        return (a_re * h_re - a_im * h_im + s_re, a_re * h_im + a_im * h_re + s_im)

    zero = jnp.zeros((batch, LANES), F32)

    def ctx_body(i, c):
        fr, fi, br, bi = c
        kf, kb = i, kc - 1 - i
        fr, fi = step(fr, fi, a_fr, a_fi, cfr_ref[rows(kf), :], cfi_ref[rows(kf), :])
        br, bi = step(br, bi, a_br, a_bi, cbr_ref[rows(kb), :], cbi_ref[rows(kb), :])
        return fr, fi, br, bi

    carry = lax.fori_loop(0, kc, ctx_body, (zero, zero, zero, zero))

    def x_body(i, c):
        fr, fi, br, bi = c
        kf, kb = i, kx - 1 - i
        hfr_ref[rows(kf), :] = fr.astype(hfr_ref.dtype)
        hfi_ref[rows(kf), :] = fi.astype(hfi_ref.dtype)
        hbr_ref[rows(kb), :] = br.astype(hbr_ref.dtype)
        hbi_ref[rows(kb), :] = bi.astype(hbi_ref.dtype)
        fr, fi = step(fr, fi, a_fr, a_fi, xfr_ref[rows(kf), :], xfi_ref[rows(kf), :])
        br, bi = step(br, bi, a_br, a_bi, xbr_ref[rows(kb), :], xbi_ref[rows(kb), :])
        return fr, fi, br, bi

    lax.fori_loop(0, kx, x_body, carry)


def _s5_scan(a_t, s_ctx, s_x, batch):
    rows_c, n_state = s_ctx[0].shape
    rows_x = s_x[0].shape[0]
    kc, kx = rows_c // batch, rows_x // batch
    out = jax.ShapeDtypeStruct((rows_x, n_state), BF16)
    cspec = pl.BlockSpec((rows_c, LANES), lambda j: (0, j))
    xspec = pl.BlockSpec((rows_x, LANES), lambda j: (0, j))
    return pl.pallas_call(
        functools.partial(_s5_scan_kernel, batch=batch, kc=kc, kx=kx),
        out_shape=(out, out, out, out),
        grid=(n_state // LANES,),
        in_specs=[pl.BlockSpec((4, LANES), lambda j: (0, j))] + [cspec] * 4 + [xspec] * 4,
        out_specs=(xspec, xspec, xspec, xspec),
        compiler_params=_cparams(("arbitrary",)),
        name="s5_scan",
    )(a_t, *s_ctx, *s_x)


def _s5_apply_kernel(u_ref, t_ref, c_ref, hfr_ref, hfi_ref, hbr_ref, hbi_ref, y_ref):
    hcat = jnp.concatenate([hfr_ref[...], hfi_ref[...], hbr_ref[...], hbi_ref[...]], axis=1)
    yst = _bdot(hcat, c_ref[0])
    n = u_ref.shape[2]
    y_ref[0] = (_bdot(u_ref[0], t_ref[0]) + yst[:, :n]).astype(y_ref.dtype)
    y_ref[1] = (_bdot(u_ref[1], t_ref[1]) + yst[:, n:]).astype(y_ref.dtype)


def _s5_apply(ut, toep, cpow, h_in):
    g, rows, kdim = ut.shape
    tr = min(S5_TR, rows)
    hspec = pl.BlockSpec((tr, LANES), lambda a, r: (r, a))
    return pl.pallas_call(
        _s5_apply_kernel,
        out_shape=jax.ShapeDtypeStruct((g, rows, kdim), BF16),
        grid=(g // 2, rows // tr),
        in_specs=[pl.BlockSpec((2, tr, kdim), lambda a, r: (a, r, 0)),
                  pl.BlockSpec((2, kdim, kdim), lambda a, r: (a, 0, 0)),
                  pl.BlockSpec((1, 4 * LANES, 2 * kdim), lambda a, r: (a, 0, 0))] + [hspec] * 4,
        out_specs=pl.BlockSpec((2, tr, kdim), lambda a, r: (a, r, 0)),
        compiler_params=_cparams(("arbitrary", "arbitrary")),
        name="s5_apply",
    )(ut, toep, cpow, *h_in)


def _back_kernel(x_ref, u_ref, ys_ref, mod_ref, gmix_ref, win_ref, convw_ref, convb_ref,
                 dskip_ref, wglu_ref, wsbr_ref, wcbr_ref, wo_ref, gffn_ref, wr_ref, br_ref,
                 x1_ref, h_ref, ids_ref, gates_ref):
    xt = x_ref[0]
    tm = xt.shape[0]
    d_ssm = u_ref.shape[2]
    d_model = xt.shape[1]
    xn = _rmsnorm(xt, gmix_ref[...])
    hx = (xn * (1.0 + mod_ref[0, 1:2, :]) + mod_ref[0, 0:1, :]).astype(BF16)
    p = _bdot(hx, win_ref[...])
    v = p[:, 0:d_ssm]
    gate_b = p[:, d_ssm:2 * d_ssm]
    gate_c = p[:, 2 * d_ssm:3 * d_ssm]
    g_s = p[:, 3 * d_ssm:3 * d_ssm + d_model]
    g_c = p[:, 3 * d_ssm + d_model:]

    z = gate_c * v
    col = lax.broadcasted_iota(jnp.int32, z.shape, 0) % GRID_W
    z_prev = jnp.where(col == 0, 0.0, pltpu.roll(z, 1, 0))
    z_next = jnp.where(col == GRID_W - 1, 0.0, pltpu.roll(z, tm - 1, 0))
    conv = (z_prev * convw_ref[0:1, :] + z * convw_ref[1:2, :]
            + z_next * convw_ref[2:3, :] + convb_ref[...])
    y_conv = gate_b * conv

    ys = ys_ref[0].astype(F32) + dskip_ref[...] * u_ref[0].astype(F32)
    ys = jax.nn.gelu(ys)
    ys = ys * jax.nn.sigmoid(_bdot(ys.astype(BF16), wglu_ref[...]))

    y_s = _bdot(ys.astype(BF16), wsbr_ref[...])
    y_c = _bdot(y_conv.astype(BF16), wcbr_ref[...])
    merged = jax.nn.sigmoid(g_s) * y_s + jax.nn.sigmoid(g_c) * y_c
    x1 = xt + mod_ref[0, 2:3, :] * _bdot(merged.astype(BF16), wo_ref[...])
    x1_ref[0] = x1

    hn = _rmsnorm(x1, gffn_ref[...]) * (1.0 + mod_ref[0, 4:5, :]) + mod_ref[0, 3:4, :]
    for s in range(SUBLANES):
        h_ref[pl.ds(s, tm, stride=SUBLANES), :] = hn[:, s * LANES:(s + 1) * LANES]

    logits = jnp.dot(hn, wr_ref[...], precision=HIGHEST, preferred_element_type=F32) + br_ref[...]
    lane = lax.broadcasted_iota(jnp.int32, logits.shape, 1)
    neg = jnp.float32(-jnp.inf)
    cur = jnp.where(lane < N_EXPERTS, logits, neg)
    vals, idxs = [], []
    for _ in range(TOP_K):
        mk = jnp.max(cur, axis=-1, keepdims=True)
        ik = jnp.min(jnp.where(cur == mk, lane, LANES), axis=-1, keepdims=True)
        vals.append(mk)
        idxs.append(ik)
        cur = jnp.where(lane == ik, neg, cur)
    exps = [jnp.exp(vk - vals[0]) for vk in vals]
    denom = exps[0] + exps[1] + exps[2] + exps[3]
    ids = jnp.zeros(logits.shape, jnp.int32)
    gates = jnp.zeros(logits.shape, F32)
    for k in range(TOP_K):
        ids = jnp.where(lane == k, idxs[k], ids)
        gates = jnp.where(lane == k, exps[k] / denom, gates)
    ids_ref[0] = ids
    gates_ref[0] = gates


def _back(x, u, y_ssm, mod, g_mix, w_rest, conv_w, conv_b, d_skip, w_glu, w_ssm_br,
          w_conv_br, w_o, g_ffn, w_router, b_router):
    b, s, d = x.shape
    d_ssm = u.shape[2]
    tm = min(BACK_TM, s)
    assert s % tm == 0 and tm % GRID_W == 0
    tok = lambda n: pl.BlockSpec((1, tm, n), lambda i, j: (i, j, 0))

    def const(arr):
        nd = arr.ndim
        return pl.BlockSpec(arr.shape, lambda i, j: (0,) * nd, pipeline_mode=pl.Buffered(1))

    params = (g_mix, w_rest, conv_w, conv_b, d_skip, w_glu, w_ssm_br, w_conv_br, w_o, g_ffn,
              w_router, b_router)
    return pl.pallas_call(
        _back_kernel,
        out_shape=(jax.ShapeDtypeStruct((b, s, d), F32),
                   jax.ShapeDtypeStruct((b * s * SUBLANES, LANES), F32),
                   jax.ShapeDtypeStruct((b, s, LANES), jnp.int32),
                   jax.ShapeDtypeStruct((b, s, LANES), F32)),
        grid=(b, s // tm),
        in_specs=[tok(d), tok(d_ssm), tok(d_ssm),
                  pl.BlockSpec((1, 8, d), lambda i, j: (i, 0, 0))] + [const(a) for a in params],
        out_specs=(tok(d), pl.BlockSpec((tm * SUBLANES, LANES),
                                        lambda i, j: (i * (s // tm) + j, 0)),
                   tok(LANES), tok(LANES)),
        compiler_params=_cparams(("arbitrary", "arbitrary")),
        name="back",
    )(x, u, y_ssm, mod, *params)


def _routing(ids, n_tok):
    tb = MOE_TB
    n_slot = n_tok * TOP_K
    slot_expert = ids.reshape(-1)
    order = jnp.argsort(slot_expert, stable=True).astype(jnp.int32)
    inv = jnp.argsort(order).astype(jnp.int32)
    srt_expert = slot_expert[order]
    counts = jnp.sum(slot_expert[:, None] == jnp.arange(N_EXPERTS, dtype=jnp.int32)[None, :],
                     axis=0, dtype=jnp.int32)
    padded = (counts + tb - 1) // tb * tb
    pad_end = jnp.cumsum(padded)
    pad_start = pad_end - padded
    start = jnp.cumsum(counts) - counts
    dest_sorted = pad_start[srt_expert] + jnp.arange(n_slot, dtype=jnp.int32) - start[srt_expert]
    pos = dest_sorted[inv].reshape(n_tok, TOP_K)
    n_blocks = n_slot // tb + N_EXPERTS
    cap = n_blocks * tb
    block_expert = jnp.minimum(
        jnp.searchsorted(pad_end, jnp.arange(n_blocks, dtype=jnp.int32) * tb, side="right"),
        N_EXPERTS - 1).astype(jnp.int32)
    row = jnp.arange(cap, dtype=jnp.int32)
    row_expert = jnp.repeat(block_expert, tb)
    within = row - pad_start[row_expert]
    valid = within < counts[row_expert]
    src = jnp.clip(start[row_expert] + within, 0, n_slot - 1)
    buf_token = jnp.where(valid, order[src] // TOP_K, 0).astype(jnp.int32)
    n_used = (pad_end[-1] // tb).astype(jnp.int32).reshape(1)
    return buf_token, pos.astype(jnp.int32), block_expert, n_used


def _to_tiles(ref, val):
    rows = val.shape[0]
    for s in range(SUBLANES):
        ref[pl.ds(s, rows, stride=SUBLANES), :] = val[:, s * LANES:(s + 1) * LANES]


def _tile_piece(ref, first_row, rows, s):
    return ref[pl.ds(first_row * SUBLANES + s, rows, stride=SUBLANES), :]


def _tile_gather_start(idx_ref, src_hbm, dst, sem, n_rows, unrolled, alternate_priority=False):
    def copy(r, t):
        return pltpu.make_async_copy(
            src_hbm.at[pl.ds(pl.multiple_of(t * SUBLANES, SUBLANES), SUBLANES), :],
            dst.at[pl.ds(pl.multiple_of(r * SUBLANES, SUBLANES), SUBLANES), :], sem)

    if unrolled:
        for r in range(n_rows):
            copy(r, idx_ref[0, 0, r]).start(priority=(r % 2) if alternate_priority else 0)
    else:
        def body(r, carry):
            copy(r, idx_ref[0, 0, r]).start()
            return carry
        lax.fori_loop(0, n_rows, body, 0, unroll=8)


def _tile_gather_wait(src_hbm, dst, sem):
    pltpu.make_async_copy(src_hbm.at[pl.ds(0, dst.shape[0]), :], dst, sem).wait()


def _moe_kernel(be_ref, nu_ref, tokc_ref, tokn_ref, h_hbm, wgu_ref, bgu_ref, wd_ref, bd_ref,
                out_ref, xbuf0, xbuf1, sem):
    del be_ref
    i = pl.program_id(0)
    n_used = nu_ref[0]
    tb = xbuf0.shape[0] // SUBLANES
    f = wd_ref.shape[1]

    @pl.when(i == 0)
    def _():
        _tile_gather_start(tokc_ref, h_hbm, xbuf0, sem.at[0], tb, unrolled=False)

    def block(xcur, xnext, scur, snext):
        _tile_gather_wait(h_hbm, xcur, scur)
        _tile_gather_start(tokn_ref, h_hbm, xnext, snext, tb, unrolled=True)
        xe = jnp.concatenate([_tile_piece(xcur, 0, tb, s) for s in range(SUBLANES)],
                             axis=1).astype(BF16)
        gu = _bdot(xe, wgu_ref[0]) + bgu_ref[0]
        gt = jnp.minimum(gu[:, :f], SWIGLU_LIMIT)
        up = jnp.clip(gu[:, f:], -SWIGLU_LIMIT, SWIGLU_LIMIT)
        act = gt * jax.nn.sigmoid(SWIGLU_ALPHA * gt) * (up + 1.0)
        _to_tiles(out_ref, _bdot(act.astype(BF16), wd_ref[0]) + bd_ref[0])

        @pl.when(i == n_used - 1)
        def _():
            _tile_gather_wait(h_hbm, xnext, snext)

    valid = i < n_used

    @pl.when(valid & (i % 2 == 0))
    def _():
        block(xbuf0, xbuf1, sem.at[0], sem.at[1])

    @pl.when(valid & (i % 2 == 1))
    def _():
        block(xbuf1, xbuf0, sem.at[1], sem.at[0])

    @pl.when(jnp.logical_not(valid))
    def _():
        out_ref[...] = jnp.zeros(out_ref.shape, out_ref.dtype)


def _moe(h_tiles, buf_token, block_expert, n_used, w_gu, b_gu, w_down, b_down):
    e, d, f2 = w_gu.shape
    f = f2 // 2
    tb = MOE_TB
    cap = buf_token.shape[0]
    n_blocks = cap // tb
    tok3 = buf_token.reshape(n_blocks, 1, tb)
    smem_blk = lambda imap: pl.BlockSpec((1, 1, tb), imap, memory_space=pltpu.SMEM)
    grid_spec = pltpu.PrefetchScalarGridSpec(
        num_scalar_prefetch=2,
        grid=(n_blocks,),
        in_specs=[smem_blk(lambda i, be, nu: (i, 0, 0)),
                  smem_blk(lambda i, be, nu: (jnp.minimum(i + 1, n_blocks - 1), 0, 0)),
                  pl.BlockSpec(memory_space=pl.ANY),
                  pl.BlockSpec((1, d, f2), lambda i, be, nu: (be[i], 0, 0)),
                  pl.BlockSpec((1, 1, f2), lambda i, be, nu: (be[i], 0, 0)),
                  pl.BlockSpec((1, f, d), lambda i, be, nu: (be[i], 0, 0)),
                  pl.BlockSpec((1, 1, d), lambda i, be, nu: (be[i], 0, 0))],
        out_specs=pl.BlockSpec((tb * SUBLANES, LANES), lambda i, be, nu: (i, 0)),
        scratch_shapes=[pltpu.VMEM((tb * SUBLANES, LANES), F32),
                        pltpu.VMEM((tb * SUBLANES, LANES), F32),
                        pltpu.SemaphoreType.DMA((2,))],
    )
    return pl.pallas_call(
        _moe_kernel,
        out_shape=jax.ShapeDtypeStruct((cap * SUBLANES, LANES), F32),
        grid_spec=grid_spec,
        compiler_params=_cparams(("arbitrary",)),
        name="moe",
    )(block_expert, n_used, tok3, tok3, h_tiles, w_gu, b_gu.reshape(e, 1, f2),
      w_down, b_down.reshape(e, 1, d))


def _combine_kernel(posc_ref, posn_ref, yb_hbm, x1_ref, gates_ref, mod_ref, gfin_ref,
                    out_ref, buf0, buf1, sem):
    i = pl.program_id(0)
    n = pl.num_programs(0)
    n_rows = buf0.shape[0] // SUBLANES
    tm = n_rows // TOP_K

    @pl.when(i == 0)
    def _():
        _tile_gather_start(posc_ref, yb_hbm, buf0, sem.at[0], n_rows, unrolled=False)

    def tile(cur, nxt, scur, snext):
        _tile_gather_wait(yb_hbm, cur, scur)
        _tile_gather_start(posn_ref, yb_hbm, nxt, snext, n_rows, unrolled=True,
                           alternate_priority=True)
        g = gates_ref[...]
        gk = [jnp.broadcast_to(g[:, k:k + 1], (tm, LANES)) for k in range(TOP_K)]
        pieces = []
        for s in range(SUBLANES):
            acc = gk[0] * _tile_piece(cur, 0, tm, s)
            for k in range(1, TOP_K):
                acc = acc + gk[k] * _tile_piece(cur, k * tm, tm, s)
            pieces.append(acc)
        x2 = x1_ref[...] + mod_ref[0, 5:6, :] * jnp.concatenate(pieces, axis=1)
        out_ref[...] = _rmsnorm(x2, gfin_ref[...])

        @pl.when(i == n - 1)
        def _():
            _tile_gather_wait(yb_hbm, nxt, snext)

    @pl.when(i % 2 == 0)
    def _():
        tile(buf0, buf1, sem.at[0], sem.at[1])

    @pl.when(i % 2 == 1)
    def _():
        tile(buf1, buf0, sem.at[1], sem.at[0])


def _combine(yb, pos, x1_2d, gates_2d, mod, g_final, seq):
    n_tok, d = x1_2d.shape
    tm = min(COMBINE_TM, seq)
    assert seq % tm == 0
    n_tiles = n_tok // tm
    per_batch = seq // tm
    pos3 = pos.reshape(n_tiles, tm, TOP_K).transpose(0, 2, 1).reshape(n_tiles, 1, TOP_K * tm)
    smem_blk = lambda imap: pl.BlockSpec((1, 1, TOP_K * tm), imap, memory_space=pltpu.SMEM)
    buf = pltpu.VMEM((TOP_K * tm * SUBLANES, LANES), F32)
    return pl.pallas_call(
        _combine_kernel,
        out_shape=jax.ShapeDtypeStruct((n_tok, d), F32),
        grid=(n_tiles,),
        in_specs=[smem_blk(lambda i: (i, 0, 0)),
                  smem_blk(lambda i: (jnp.minimum(i + 1, n_tiles - 1), 0, 0)),
                  pl.BlockSpec(memory_space=pl.ANY),
                  pl.BlockSpec((tm, d), lambda i: (i, 0)),
                  pl.BlockSpec((tm, LANES), lambda i: (i, 0)),
                  pl.BlockSpec((1, 8, d), lambda i: (i // per_batch, 0, 0)),
                  pl.BlockSpec((1, d), lambda i: (0, 0))],
        out_specs=pl.BlockSpec((tm, d), lambda i: (i, 0)),
        scratch_shapes=[buf, buf, pltpu.SemaphoreType.DMA((2,))],
        compiler_params=_cparams(("arbitrary",)),
        name="combine",
    )(pos3, pos3, yb, x1_2d, gates_2d, mod, g_final)


def kernel(x, c, ctx, c_ctx, w_mod, b_mod, g_mix, w_in, lam_re, lam_im, log_dt, b_re, b_im,
           c_re, c_im, d_skip, w_glu, conv_w, conv_b, w_ssm_br, w_conv_br, w_o, g_ffn,
           w_router, b_router, w_gu, b_gu, w_down, b_down, g_final):
    depth = w_mod.shape[0]
    assert depth == 1, "single-layer trunk"
    bsz, seq, d = x.shape
    ctx_len = ctx.shape[1]
    d_ssm = d // 2
    assert bsz % 8 == 0 and seq % CHUNK_T == 0 and ctx_len % CHUNK_T == 0 and seq % GRID_W == 0
    assert d == SUBLANES * LANES, "row gathers move one (8,128) f32 tile per token"

    n_cond = -(-(bsz + 1) // 8) * 8
    cond = jnp.zeros((n_cond, d), F32).at[:bsz].set(c).at[bsz].set(c_ctx)
    m = _adaln(cond, w_mod[0], b_mod[0])
    zeros2 = jnp.zeros((n_cond, 2, d), F32)
    mod_all = jnp.concatenate([m.reshape(n_cond, 6, d), zeros2], axis=1)
    mod_x, mod_c = mod_all[:bsz], mod_all[bsz:bsz + 1]

    w_in_bf = w_in[0].astype(BF16)
    w_u, w_rest = w_in_bf[:, :d_ssm], w_in_bf[:, d_ssm:]
    gm = g_mix[0].reshape(1, d)

    ux = _front(x, mod_x, gm, w_u)
    uc = _front(ctx, mod_c, gm, w_u)
    utx, utc = _to_chunk_rows(ux), _to_chunk_rows(uc)

    smat, toep, cpow, a_t = _s5_matrices(lam_re[0], lam_im[0], log_dt[0], b_re[0], b_im[0],
                                         c_re[0], c_im[0])
    s_x = _s5_states(utx, smat)
    s_c = _s5_states(utc, smat)
    h_in = _s5_scan(a_t, s_c, s_x, bsz)
    y_ssm = _from_chunk_rows(_s5_apply(utx, toep, cpow, h_in), bsz)

    pad_r = jnp.zeros((d, LANES - N_EXPERTS), F32)
    w_r = jnp.concatenate([w_router[0], pad_r], axis=1)
    b_r = jnp.concatenate([b_router[0], jnp.zeros((LANES - N_EXPERTS,), F32)]).reshape(1, LANES)
    x1, h, ids, gates = _back(
        x, ux, y_ssm, mod_x, gm, w_rest, conv_w[0], conv_b[0].reshape(1, d_ssm),
        d_skip[0].reshape(1, d_ssm), w_glu[0].astype(BF16), w_ssm_br[0].astype(BF16),
        w_conv_br[0].astype(BF16), w_o[0].astype(BF16), g_ffn[0].reshape(1, d), w_r, b_r)

    n_tok = bsz * seq
    buf_token, pos, block_expert, n_used = _routing(ids.reshape(n_tok, LANES)[:, :TOP_K], n_tok)
    yb = _moe(h, buf_token, block_expert, n_used,
              w_gu[0].astype(BF16), b_gu[0], w_down[0].astype(BF16), b_down[0])
    out = _combine(yb, pos, x1.reshape(n_tok, d), gates.reshape(n_tok, LANES), mod_x,
                   g_final.reshape(1, d), seq)
    return out.reshape(bsz, seq, d)
```

```python
import functools
import math

import jax
import jax.numpy as jnp
from jax import lax
from jax.experimental import pallas as pl
from jax.experimental.pallas import tpu as pltpu

F32 = jnp.float32
BF16 = jnp.bfloat16
HIGHEST = lax.Precision.HIGHEST

RMS_EPS = 1e-6
GRID_W = 64
SSM_GROUP = 16
SSM_STATE = 64
N_EXPERTS = 32
TOP_K = 4
SWIGLU_LIMIT = 7.0
SWIGLU_ALPHA = 1.702

CHUNK_T = 16
LANES = 128
SUBLANES = 8
V7X_VMEM_LIMIT_BYTES = 56 * 1024 * 1024

FRONT_TM = 1024
S5_TR = 1024
BACK_TM = 512
MOE_TB = 256
COMBINE_TM = 256


def _cparams(sem):
    return pltpu.CompilerParams(dimension_semantics=sem,
                                vmem_limit_bytes=V7X_VMEM_LIMIT_BYTES)


def _bdot(a, b):
    return jnp.dot(a, b, preferred_element_type=F32)


def _rmsnorm(xt, g):
    ms = jnp.mean(xt * xt, axis=-1, keepdims=True)
    return xt * lax.rsqrt(ms + RMS_EPS) * g


def _adaln_kernel(c_ref, w_ref, b_ref, o_ref):
    s = jax.nn.silu(c_ref[...])
    o_ref[...] = jnp.dot(s, w_ref[...], precision=HIGHEST,
                         preferred_element_type=F32) + b_ref[...]


def _adaln(cond, w_mod, b_mod):
    r, d = cond.shape
    n = w_mod.shape[1]
    tn = n // 4
    return pl.pallas_call(
        _adaln_kernel,
        out_shape=jax.ShapeDtypeStruct((r, n), F32),
        grid=(n // tn,),
        in_specs=[pl.BlockSpec((r, d), lambda j: (0, 0)),
                  pl.BlockSpec((d, tn), lambda j: (0, j)),
                  pl.BlockSpec((1, tn), lambda j: (0, j))],
        out_specs=pl.BlockSpec((r, tn), lambda j: (0, j)),
        compiler_params=_cparams(("arbitrary",)),
        name="adaln",
    )(cond, w_mod, b_mod.reshape(1, n))


def _front_kernel(x_ref, mod_ref, g_ref, w_ref, u_ref):
    xn = _rmsnorm(x_ref[0], g_ref[...])
    hx = xn * (1.0 + mod_ref[0, 1:2, :]) + mod_ref[0, 0:1, :]
    u_ref[0] = _bdot(hx.astype(BF16), w_ref[...]).astype(u_ref.dtype)


def _front(x, mod, g_mix, w_u):
    b, s, d = x.shape
    n = w_u.shape[1]
    tm = min(FRONT_TM, s)
    assert s % tm == 0
    shared = mod.shape[0] == 1
    return pl.pallas_call(
        _front_kernel,
        out_shape=jax.ShapeDtypeStruct((b, s, n), BF16),
        grid=(b, s // tm),
        in_specs=[pl.BlockSpec((1, tm, d), lambda i, j: (i, j, 0)),
                  pl.BlockSpec((1, 8, d), (lambda i, j: (0, 0, 0)) if shared
                               else (lambda i, j: (i, 0, 0))),
                  pl.BlockSpec((1, d), lambda i, j: (0, 0)),
                  pl.BlockSpec((d, n), lambda i, j: (0, 0))],
        out_specs=pl.BlockSpec((1, tm, n), lambda i, j: (i, j, 0)),
        compiler_params=_cparams(("arbitrary", "arbitrary")),
        name="front",
    )(x, mod, g_mix, w_u)


def _s5_discretise(lam_re, lam_im, log_dt, b_re, b_im):
    dt = jnp.exp(log_dt)[..., None]
    mag = jnp.exp(lam_re * dt)
    a_re, a_im = mag * jnp.cos(lam_im * dt), mag * jnp.sin(lam_im * dt)
    den = lam_re * lam_re + lam_im * lam_im
    q_re = ((a_re - 1) * lam_re + a_im * lam_im) / den
    q_im = (a_im * lam_re - (a_re - 1) * lam_im) / den
    bb_re = q_re[..., None] * b_re - q_im[..., None] * b_im
    bb_im = q_re[..., None] * b_im + q_im[..., None] * b_re
    return a_re, a_im, bb_re, bb_im


def _s5_matrices(lam_re, lam_im, log_dt, b_re, b_im, c_re, c_im):
    t = CHUNK_T
    a_re, a_im, bb_re, bb_im = _s5_discretise(lam_re, lam_im, log_dt, b_re, b_im)
    g, p = a_re.shape[1], a_re.shape[2]
    ch = bb_re.shape[-1]
    pw_re, pw_im = [jnp.ones_like(a_re)], [jnp.zeros_like(a_im)]
    for _ in range(t):
        r, i = pw_re[-1], pw_im[-1]
        pw_re.append(r * a_re - i * a_im)
        pw_im.append(r * a_im + i * a_re)
    pw_re, pw_im = jnp.stack(pw_re), jnp.stack(pw_im)
    cr = c_re[None] * pw_re[:, :, :, None, :] - c_im[None] * pw_im[:, :, :, None, :]
    ci = c_re[None] * pw_im[:, :, :, None, :] + c_im[None] * pw_re[:, :, :, None, :]
    kern = (jnp.einsum("dzgcp,zgpe->dzgce", cr, bb_re, precision=HIGHEST)
            - jnp.einsum("dzgcp,zgpe->dzgce", ci, bb_im, precision=HIGHEST))
    s_idx = jnp.arange(t)[:, None]
    t_idx = jnp.arange(t)[None, :]
    dlt = t_idx - s_idx
    kf = kern[jnp.clip(dlt, 0, t - 1), 0]
    kb = kern[jnp.clip(-dlt, 0, t - 1), 1]
    msk_f = (dlt >= 0)[:, :, None, None, None]
    msk_b = (dlt <= 0)[:, :, None, None, None]
    tsum = jnp.where(msk_f, kf, 0.0) + jnp.where(msk_b, kb, 0.0)
    toep = tsum.transpose(2, 0, 4, 1, 3).reshape(g, t * ch, t * ch)

    pf_re, pf_im = pw_re[t - 1 - jnp.arange(t), 0], pw_im[t - 1 - jnp.arange(t), 0]
    pb_re, pb_im = pw_re[jnp.arange(t), 1], pw_im[jnp.arange(t), 1]

    def in_to_state(q_re, q_im, z):
        bre = bb_re[z].transpose(0, 2, 1)[None]
        bim = bb_im[z].transpose(0, 2, 1)[None]
        s_re = q_re[:, :, None, :] * bre - q_im[:, :, None, :] * bim
        s_im = q_re[:, :, None, :] * bim + q_im[:, :, None, :] * bre
        to_rows = lambda m: m.transpose(1, 0, 2, 3).reshape(g, t * ch, p)
        return to_rows(s_re), to_rows(s_im)

    sf_re, sf_im = in_to_state(pf_re, pf_im, 0)
    sb_re, sb_im = in_to_state(pb_re, pb_im, 1)
    s_all = jnp.stack([sf_re, sf_im, sb_re, sb_im], axis=1)
    s_all = s_all.reshape(g // 2, 2, 4, t * ch, p)
    eye2 = jnp.eye(2, dtype=F32)
    smat = jnp.einsum("ajqrp,jk->ajrqkp", s_all, eye2).reshape(g // 2, 2 * t * ch, 4 * 2 * p)

    def state_to_out(idx, z):
        cre = cr[idx, z].transpose(1, 3, 0, 2).reshape(g, p, t * ch)
        cim = ci[idx, z].transpose(1, 3, 0, 2).reshape(g, p, t * ch)
        return cre, -cim

    cf_re, cf_im = state_to_out(jnp.arange(t) + 1, 0)
    cb_re, cb_im = state_to_out(t - jnp.arange(t), 1)
    c_all = jnp.stack([cf_re, cf_im, cb_re, cb_im], axis=1)
    c_all = c_all.reshape(g // 2, 2, 4, p, t * ch)
    cpow = jnp.einsum("ajqpn,jk->aqjpkn", c_all, eye2).reshape(g // 2, 4 * 2 * p, 2 * t * ch)

    a_t = jnp.stack([pw_re[t, 0], pw_im[t, 0], pw_re[t, 1], pw_im[t, 1]]).reshape(4, g * p)
    return smat.astype(BF16), toep.astype(BF16), cpow.astype(BF16), a_t


def _to_chunk_rows(u):
    b, s, n = u.shape
    g = n // SSM_GROUP
    k = s // CHUNK_T
    ut = u.reshape(b, k, CHUNK_T, g, SSM_GROUP).transpose(3, 1, 0, 2, 4)
    return ut.reshape(g, k * b, CHUNK_T * SSM_GROUP)


def _from_chunk_rows(y, b):
    g, rows, _ = y.shape
    k = rows // b
    yt = y.reshape(g, k, b, CHUNK_T, SSM_GROUP).transpose(2, 1, 3, 0, 4)
    return yt.reshape(b, k * CHUNK_T, g * SSM_GROUP)


def _s5_states_kernel(u_ref, m_ref, fre_ref, fim_ref, bre_ref, bim_ref):
    lhs = jnp.concatenate([u_ref[0], u_ref[1]], axis=1)
    res = _bdot(lhs, m_ref[0])
    fre_ref[...] = res[:, 0 * LANES:1 * LANES]
    fim_ref[...] = res[:, 1 * LANES:2 * LANES]
    bre_ref[...] = res[:, 2 * LANES:3 * LANES]
    bim_ref[...] = res[:, 3 * LANES:4 * LANES]


def _s5_states(ut, smat):
    g, rows, kdim = ut.shape
    tr = min(S5_TR, rows)
    assert rows % tr == 0
    n_state = g * SSM_STATE
    out = jax.ShapeDtypeStruct((rows, n_state), F32)
    ospec = pl.BlockSpec((tr, LANES), lambda a, r: (r, a))
    return pl.pallas_call(
        _s5_states_kernel,
        out_shape=(out, out, out, out),
        grid=(g // 2, rows // tr),
        in_specs=[pl.BlockSpec((2, tr, kdim), lambda a, r: (a, r, 0)),
                  pl.BlockSpec((1, 2 * kdim, 4 * LANES), lambda a, r: (a, 0, 0))],
        out_specs=(ospec, ospec, ospec, ospec),
        compiler_params=_cparams(("arbitrary", "arbitrary")),
        name="s5_states",
    )(ut, smat)


def _s5_scan_kernel(a_ref, cfr_ref, cfi_ref, cbr_ref, cbi_ref,
                    xfr_ref, xfi_ref, xbr_ref, xbi_ref,
                    hfr_ref, hfi_ref, hbr_ref, hbi_ref, *, batch, kc, kx):
    a_fr, a_fi = a_ref[0:1, :], a_ref[1:2, :]
    a_br, a_bi = a_ref[2:3, :], a_ref[3:4, :]

    def rows(k):
        return pl.ds(pl.multiple_of(k * batch, 8), batch)

    def step(h_re, h_im, a_re, a_im, s_re, s_im):
        return (a_re * h_re - a_im * h_im + s_re, a_re * h_im + a_im * h_re + s_im)

    zero = jnp.zeros((batch, LANES), F32)

    def ctx_body(i, c):
        fr, fi, br, bi = c
        kf, kb = i, kc - 1 - i
        fr, fi = step(fr, fi, a_fr, a_fi, cfr_ref[rows(kf), :], cfi_ref[rows(kf), :])
        br, bi = step(br, bi, a_br, a_bi, cbr_ref[rows(kb), :], cbi_ref[rows(kb), :])
        return fr, fi, br, bi

    carry = lax.fori_loop(0, kc, ctx_body, (zero, zero, zero, zero))

    def x_body(i, c):
        fr, fi, br, bi = c
        kf, kb = i, kx - 1 - i
        hfr_ref[rows(kf), :] = fr.astype(hfr_ref.dtype)
        hfi_ref[rows(kf), :] = fi.astype(hfi_ref.dtype)
        hbr_ref[rows(kb), :] = br.astype(hbr_ref.dtype)
        hbi_ref[rows(kb), :] = bi.astype(hbi_ref.dtype)
        fr, fi = step(fr, fi, a_fr, a_fi, xfr_ref[rows(kf), :], xfi_ref[rows(kf), :])
        br, bi = step(br, bi, a_br, a_bi, xbr_ref[rows(kb), :], xbi_ref[rows(kb), :])
        return fr, fi, br, bi

    lax.fori_loop(0, kx, x_body, carry)


def _s5_scan(a_t, s_ctx, s_x, batch):
    rows_c, n_state = s_ctx[0].shape
    rows_x = s_x[0].shape[0]
    kc, kx = rows_c // batch, rows_x // batch
    out = jax.ShapeDtypeStruct((rows_x, n_state), BF16)
    cspec = pl.BlockSpec((rows_c, LANES), lambda j: (0, j))
    xspec = pl.BlockSpec((rows_x, LANES), lambda j: (0, j))
    return pl.pallas_call(
        functools.partial(_s5_scan_kernel, batch=batch, kc=kc, kx=kx),
        out_shape=(out, out, out, out),
        grid=(n_state // LANES,),
        in_specs=[pl.BlockSpec((4, LANES), lambda j: (0, j))] + [cspec] * 4 + [xspec] * 4,
        out_specs=(xspec, xspec, xspec, xspec),
        compiler_params=_cparams(("arbitrary",)),
        name="s5_scan",
    )(a_t, *s_ctx, *s_x)


def _s5_apply_kernel(u_ref, t_ref, c_ref, hfr_ref, hfi_ref, hbr_ref, hbi_ref, y_ref):
    hcat = jnp.concatenate([hfr_ref[...], hfi_ref[...], hbr_ref[...], hbi_ref[...]], axis=1)
    yst = _bdot(hcat, c_ref[0])
    n = u_ref.shape[2]
    y_ref[0] = (_bdot(u_ref[0], t_ref[0]) + yst[:, :n]).astype(y_ref.dtype)
    y_ref[1] = (_bdot(u_ref[1], t_ref[1]) + yst[:, n:]).astype(y_ref.dtype)


def _s5_apply(ut, toep, cpow, h_in):
    g, rows, kdim = ut.shape
    tr = min(S5_TR, rows)
    hspec = pl.BlockSpec((tr, LANES), lambda a, r: (r, a))
    return pl.pallas_call(
        _s5_apply_kernel,
        out_shape=jax.ShapeDtypeStruct((g, rows, kdim), BF16),
        grid=(g // 2, rows // tr),
        in_specs=[pl.BlockSpec((2, tr, kdim), lambda a, r: (a, r, 0)),
                  pl.BlockSpec((2, kdim, kdim), lambda a, r: (a, 0, 0)),
                  pl.BlockSpec((1, 4 * LANES, 2 * kdim), lambda a, r: (a, 0, 0))] + [hspec] * 4,
        out_specs=pl.BlockSpec((2, tr, kdim), lambda a, r: (a, r, 0)),
        compiler_params=_cparams(("arbitrary", "arbitrary")),
        name="s5_apply",
    )(ut, toep, cpow, *h_in)


def _back_kernel(x_ref, u_ref, ys_ref, mod_ref, gmix_ref, win_ref, convw_ref, convb_ref,
                 dskip_ref, wglu_ref, wsbr_ref, wcbr_ref, wo_ref, gffn_ref, wr_ref, br_ref,
                 x1_ref, h_ref, ids_ref, gates_ref):
    xt = x_ref[0]
    tm = xt.shape[0]
    d_ssm = u_ref.shape[2]
    d_model = xt.shape[1]
    xn = _rmsnorm(xt, gmix_ref[...])
    hx = (xn * (1.0 + mod_ref[0, 1:2, :]) + mod_ref[0, 0:1, :]).astype(BF16)
    p = _bdot(hx, win_ref[...])
    v = p[:, 0:d_ssm]
    gate_b = p[:, d_ssm:2 * d_ssm]
    gate_c = p[:, 2 * d_ssm:3 * d_ssm]
    g_s = p[:, 3 * d_ssm:3 * d_ssm + d_model]
    g_c = p[:, 3 * d_ssm + d_model:]

    z = gate_c * v
    col = lax.broadcasted_iota(jnp.int32, z.shape, 0) % GRID_W
    z_prev = jnp.where(col == 0, 0.0, pltpu.roll(z, 1, 0))
    z_next = jnp.where(col == GRID_W - 1, 0.0, pltpu.roll(z, tm - 1, 0))
    conv = (z_prev * convw_ref[0:1, :] + z * convw_ref[1:2, :]
            + z_next * convw_ref[2:3, :] + convb_ref[...])
    y_conv = gate_b * conv

    ys = ys_ref[0].astype(F32) + dskip_ref[...] * u_ref[0].astype(F32)
    ys = jax.nn.gelu(ys)
    ys = ys * jax.nn.sigmoid(_bdot(ys.astype(BF16), wglu_ref[...]))

    y_s = _bdot(ys.astype(BF16), wsbr_ref[...])
    y_c = _bdot(y_conv.astype(BF16), wcbr_ref[...])
    merged = jax.nn.sigmoid(g_s) * y_s + jax.nn.sigmoid(g_c) * y_c
    x1 = xt + mod_ref[0, 2:3, :] * _bdot(merged.astype(BF16), wo_ref[...])
    x1_ref[0] = x1

    hn = _rmsnorm(x1, gffn_ref[...]) * (1.0 + mod_ref[0, 4:5, :]) + mod_ref[0, 3:4, :]
    for s in range(SUBLANES):
        h_ref[pl.ds(s, tm, stride=SUBLANES), :] = hn[:, s * LANES:(s + 1) * LANES]

    logits = jnp.dot(hn, wr_ref[...], precision=HIGHEST, preferred_element_type=F32) + br_ref[...]
    lane = lax.broadcasted_iota(jnp.int32, logits.shape, 1)
    neg = jnp.float32(-jnp.inf)
    cur = jnp.where(lane < N_EXPERTS, logits, neg)
    vals, idxs = [], []
    for _ in range(TOP_K):
        mk = jnp.max(cur, axis=-1, keepdims=True)
        ik = jnp.min(jnp.where(cur == mk, lane, LANES), axis=-1, keepdims=True)
        vals.append(mk)
        idxs.append(ik)
        cur = jnp.where(lane == ik, neg, cur)
    exps = [jnp.exp(vk - vals[0]) for vk in vals]
    denom = exps[0] + exps[1] + exps[2] + exps[3]
    ids = jnp.zeros(logits.shape, jnp.int32)
    gates = jnp.zeros(logits.shape, F32)
    for k in range(TOP_K):
        ids = jnp.where(lane == k, idxs[k], ids)
        gates = jnp.where(lane == k, exps[k] / denom, gates)
    ids_ref[0] = ids
    gates_ref[0] = gates


def _back(x, u, y_ssm, mod, g_mix, w_rest, conv_w, conv_b, d_skip, w_glu, w_ssm_br,
          w_conv_br, w_o, g_ffn, w_router, b_router):
    b, s, d = x.shape
    d_ssm = u.shape[2]
    tm = min(BACK_TM, s)
    assert s % tm == 0 and tm % GRID_W == 0
    tok = lambda n: pl.BlockSpec((1, tm, n), lambda i, j: (i, j, 0))

    def const(arr):
        nd = arr.ndim
        return pl.BlockSpec(arr.shape, lambda i, j: (0,) * nd, pipeline_mode=pl.Buffered(1))

    params = (g_mix, w_rest, conv_w, conv_b, d_skip, w_glu, w_ssm_br, w_conv_br, w_o, g_ffn,
              w_router, b_router)
    return pl.pallas_call(
        _back_kernel,
        out_shape=(jax.ShapeDtypeStruct((b, s, d), F32),
                   jax.ShapeDtypeStruct((b * s * SUBLANES, LANES), F32),
                   jax.ShapeDtypeStruct((b, s, LANES), jnp.int32),
                   jax.ShapeDtypeStruct((b, s, LANES), F32)),
        grid=(b, s // tm),
        in_specs=[tok(d), tok(d_ssm), tok(d_ssm),
                  pl.BlockSpec((1, 8, d), lambda i, j: (i, 0, 0))] + [const(a) for a in params],
        out_specs=(tok(d), pl.BlockSpec((tm * SUBLANES, LANES),
                                        lambda i, j: (i * (s // tm) + j, 0)),
                   tok(LANES), tok(LANES)),
        compiler_params=_cparams(("arbitrary", "arbitrary")),
        name="back",
    )(x, u, y_ssm, mod, *params)


def _routing(ids, n_tok):
    tb = MOE_TB
    n_slot = n_tok * TOP_K
    slot_expert = ids.reshape(-1)
    order = jnp.argsort(slot_expert, stable=True).astype(jnp.int32)
    inv = jnp.argsort(order).astype(jnp.int32)
    srt_expert = slot_expert[order]
    counts = jnp.sum(slot_expert[:, None] == jnp.arange(N_EXPERTS, dtype=jnp.int32)[None, :],
                     axis=0, dtype=jnp.int32)
    padded = (counts + tb - 1) // tb * tb
    pad_end = jnp.cumsum(padded)
    pad_start = pad_end - padded
    start = jnp.cumsum(counts) - counts
    dest_sorted = pad_start[srt_expert] + jnp.arange(n_slot, dtype=jnp.int32) - start[srt_expert]
    pos = dest_sorted[inv].reshape(n_tok, TOP_K)
    n_blocks = n_slot // tb + N_EXPERTS
    cap = n_blocks * tb
    block_first_row = jnp.arange(n_blocks, dtype=jnp.int32) * tb
    block_expert = jnp.minimum(
        jnp.sum(pad_end[None, :] <= block_first_row[:, None], axis=1, dtype=jnp.int32),
        N_EXPERTS - 1)
    row = jnp.arange(cap, dtype=jnp.int32)
    row_expert = jnp.repeat(block_expert, tb)
    within = row - pad_start[row_expert]
    valid = within < counts[row_expert]
    src = jnp.clip(start[row_expert] + within, 0, n_slot - 1)
    buf_token = jnp.where(valid, order[src] // TOP_K, 0).astype(jnp.int32)
    n_used = (pad_end[-1] // tb).astype(jnp.int32).reshape(1)
    return buf_token, pos.astype(jnp.int32), block_expert, n_used


def _to_tiles(ref, val):
    rows = val.shape[0]
    for s in range(SUBLANES):
        ref[pl.ds(s, rows, stride=SUBLANES), :] = val[:, s * LANES:(s + 1) * LANES]


def _tile_piece(ref, first_row, rows, s):
    return ref[pl.ds(first_row * SUBLANES + s, rows, stride=SUBLANES), :]


def _tile_gather_start(idx_ref, src_hbm, dst, sem, n_rows, unrolled, alternate_priority=False):
    def copy(r, t):
        return pltpu.make_async_copy(
            src_hbm.at[pl.ds(pl.multiple_of(t * SUBLANES, SUBLANES), SUBLANES), :],
            dst.at[pl.ds(pl.multiple_of(r * SUBLANES, SUBLANES), SUBLANES), :], sem)

    if unrolled:
        for r in range(n_rows):
            copy(r, idx_ref[0, 0, r]).start(priority=(r % 2) if alternate_priority else 0)
    else:
        def body(r, carry):
            copy(r, idx_ref[0, 0, r]).start()
            return carry
        lax.fori_loop(0, n_rows, body, 0, unroll=8)


def _tile_gather_wait(src_hbm, dst, sem):
    pltpu.make_async_copy(src_hbm.at[pl.ds(0, dst.shape[0]), :], dst, sem).wait()


def _ring_step(i, last, bufs, sem, wait, prefetch, compute):
    n = len(bufs)
    for p in range(n):
        def branch(p=p):
            wait(bufs[p], sem.at[p])
            q = (p + n - 1) % n
            prefetch(bufs[q], sem.at[q])
            compute(bufs[p])

            @pl.when(i == last)
            def _():
                for r in range(1, n):
                    wait(bufs[(p + r) % n], sem.at[(p + r) % n])

        pl.when((i <= last) & (lax.rem(i, n) == p))(branch)


def _moe_kernel(be_ref, nu_ref, tok0_ref, tok1_ref, tok2_ref, h_hbm, wgu_ref, bgu_ref, wd_ref,
                bd_ref, out_ref, xbuf0, xbuf1, xbuf2, sem):
    del be_ref
    i = pl.program_id(0)
    n_used = nu_ref[0]
    tb = xbuf0.shape[0] // SUBLANES
    f = wd_ref.shape[1]

    @pl.when(i == 0)
    def _():
        _tile_gather_start(tok0_ref, h_hbm, xbuf0, sem.at[0], tb, unrolled=False)
        _tile_gather_start(tok1_ref, h_hbm, xbuf1, sem.at[1], tb, unrolled=False)

    def compute(xcur):
        xe = jnp.concatenate([_tile_piece(xcur, 0, tb, s) for s in range(SUBLANES)],
                             axis=1).astype(BF16)
        gu = _bdot(xe, wgu_ref[0]) + bgu_ref[0]
        gt = jnp.minimum(gu[:, :f], SWIGLU_LIMIT)
        up = jnp.clip(gu[:, f:], -SWIGLU_LIMIT, SWIGLU_LIMIT)
        act = gt * jax.nn.sigmoid(SWIGLU_ALPHA * gt) * (up + 1.0)
        _to_tiles(out_ref, _bdot(act.astype(BF16), wd_ref[0]) + bd_ref[0])

    _ring_step(i, n_used - 1, (xbuf0, xbuf1, xbuf2), sem,
               wait=lambda buf, s: _tile_gather_wait(h_hbm, buf, s),
               prefetch=lambda buf, s: _tile_gather_start(tok2_ref, h_hbm, buf, s, tb,
                                                          unrolled=True),
               compute=compute)

    @pl.when(i >= n_used)
    def _():
        out_ref[...] = jnp.zeros(out_ref.shape, out_ref.dtype)


def _moe(h_tiles, buf_token, block_expert, n_used, w_gu, b_gu, w_down, b_down):
    e, d, f2 = w_gu.shape
    f = f2 // 2
    tb = MOE_TB
    cap = buf_token.shape[0]
    n_blocks = cap // tb
    tok3 = buf_token.reshape(n_blocks, 1, tb)
    smem_blk = lambda imap: pl.BlockSpec((1, 1, tb), imap, memory_space=pltpu.SMEM)
    grid_spec = pltpu.PrefetchScalarGridSpec(
        num_scalar_prefetch=2,
        grid=(n_blocks,),
        in_specs=[smem_blk(lambda i, be, nu: (i, 0, 0)),
                  smem_blk(lambda i, be, nu: (jnp.minimum(i + 1, n_blocks - 1), 0, 0)),
                  smem_blk(lambda i, be, nu: (jnp.minimum(i + 2, n_blocks - 1), 0, 0)),
                  pl.BlockSpec(memory_space=pl.ANY),
                  pl.BlockSpec((1, d, f2), lambda i, be, nu: (be[i], 0, 0)),
                  pl.BlockSpec((1, 1, f2), lambda i, be, nu: (be[i], 0, 0)),
                  pl.BlockSpec((1, f, d), lambda i, be, nu: (be[i], 0, 0)),
                  pl.BlockSpec((1, 1, d), lambda i, be, nu: (be[i], 0, 0))],
        out_specs=pl.BlockSpec((tb * SUBLANES, LANES), lambda i, be, nu: (i, 0)),
        scratch_shapes=[pltpu.VMEM((tb * SUBLANES, LANES), F32)] * 3
        + [pltpu.SemaphoreType.DMA((3,))],
    )
    return pl.pallas_call(
        _moe_kernel,
        out_shape=jax.ShapeDtypeStruct((cap * SUBLANES, LANES), F32),
        grid_spec=grid_spec,
        compiler_params=_cparams(("arbitrary",)),
        name="moe",
    )(block_expert, n_used, tok3, tok3, tok3, h_tiles, w_gu, b_gu.reshape(e, 1, f2),
      w_down, b_down.reshape(e, 1, d))


def _combine_kernel(pos0_ref, pos1_ref, pos2_ref, yb_hbm, x1_ref, gates_ref, mod_ref, gfin_ref,
                    out_ref, buf0, buf1, buf2, sem):
    i = pl.program_id(0)
    n = pl.num_programs(0)
    n_rows = buf0.shape[0] // SUBLANES
    tm = n_rows // TOP_K

    @pl.when(i == 0)
    def _():
        _tile_gather_start(pos0_ref, yb_hbm, buf0, sem.at[0], n_rows, unrolled=False)
        _tile_gather_start(pos1_ref, yb_hbm, buf1, sem.at[1], n_rows, unrolled=False)

    def compute(cur):
        g = gates_ref[...]
        gk = [jnp.broadcast_to(g[:, k:k + 1], (tm, LANES)) for k in range(TOP_K)]
        pieces = []
        for s in range(SUBLANES):
            acc = gk[0] * _tile_piece(cur, 0, tm, s)
            for k in range(1, TOP_K):
                acc = acc + gk[k] * _tile_piece(cur, k * tm, tm, s)
            pieces.append(acc)
        x2 = x1_ref[...] + mod_ref[0, 5:6, :] * jnp.concatenate(pieces, axis=1)
        out_ref[...] = _rmsnorm(x2, gfin_ref[...])

    _ring_step(i, n - 1, (buf0, buf1, buf2), sem,
               wait=lambda buf, s: _tile_gather_wait(yb_hbm, buf, s),
               prefetch=lambda buf, s: _tile_gather_start(pos2_ref, yb_hbm, buf, s, n_rows,
                                                          unrolled=True, alternate_priority=True),
               compute=compute)


def _combine(yb, pos, x1_2d, gates_2d, mod, g_final, seq):
    n_tok, d = x1_2d.shape
    tm = min(COMBINE_TM, seq)
    assert seq % tm == 0
    n_tiles = n_tok // tm
    per_batch = seq // tm
    pos3 = pos.reshape(n_tiles, tm, TOP_K).transpose(0, 2, 1).reshape(n_tiles, 1, TOP_K * tm)
    smem_blk = lambda imap: pl.BlockSpec((1, 1, TOP_K * tm), imap, memory_space=pltpu.SMEM)
    buf = pltpu.VMEM((TOP_K * tm * SUBLANES, LANES), F32)
    return pl.pallas_call(
        _combine_kernel,
        out_shape=jax.ShapeDtypeStruct((n_tok, d), F32),
        grid=(n_tiles,),
        in_specs=[smem_blk(lambda i: (i, 0, 0)),
                  smem_blk(lambda i: (jnp.minimum(i + 1, n_tiles - 1), 0, 0)),
                  smem_blk(lambda i: (jnp.minimum(i + 2, n_tiles - 1), 0, 0)),
                  pl.BlockSpec(memory_space=pl.ANY),
                  pl.BlockSpec((tm, d), lambda i: (i, 0)),
                  pl.BlockSpec((tm, LANES), lambda i: (i, 0)),
                  pl.BlockSpec((1, 8, d), lambda i: (i // per_batch, 0, 0)),
                  pl.BlockSpec((1, d), lambda i: (0, 0))],
        out_specs=pl.BlockSpec((tm, d), lambda i: (i, 0)),
        scratch_shapes=[buf, buf, buf, pltpu.SemaphoreType.DMA((3,))],
        compiler_params=_cparams(("arbitrary",)),
        name="combine",
    )(pos3, pos3, pos3, yb, x1_2d, gates_2d, mod, g_final)


def kernel(x, c, ctx, c_ctx, w_mod, b_mod, g_mix, w_in, lam_re, lam_im, log_dt, b_re, b_im,
           c_re, c_im, d_skip, w_glu, conv_w, conv_b, w_ssm_br, w_conv_br, w_o, g_ffn,
           w_router, b_router, w_gu, b_gu, w_down, b_down, g_final):
    depth = w_mod.shape[0]
    assert depth == 1, "single-layer trunk"
    bsz, seq, d = x.shape
    ctx_len = ctx.shape[1]
    d_ssm = d // 2
    assert bsz % 8 == 0 and seq % CHUNK_T == 0 and ctx_len % CHUNK_T == 0 and seq % GRID_W == 0
    assert d == SUBLANES * LANES, "row gathers move one (8,128) f32 tile per token"

    n_cond = -(-(bsz + 1) // 8) * 8
    cond = jnp.zeros((n_cond, d), F32).at[:bsz].set(c).at[bsz].set(c_ctx)
    m = _adaln(cond, w_mod[0], b_mod[0])
    zeros2 = jnp.zeros((n_cond, 2, d), F32)
    mod_all = jnp.concatenate([m.reshape(n_cond, 6, d), zeros2], axis=1)
    mod_x, mod_c = mod_all[:bsz], mod_all[bsz:bsz + 1]

    w_in_bf = w_in[0].astype(BF16)
    w_u, w_rest = w_in_bf[:, :d_ssm], w_in_bf[:, d_ssm:]
    gm = g_mix[0].reshape(1, d)

    ux = _front(x, mod_x, gm, w_u)
    uc = _front(ctx, mod_c, gm, w_u)
    utx, utc = _to_chunk_rows(ux), _to_chunk_rows(uc)

    smat, toep, cpow, a_t = _s5_matrices(lam_re[0], lam_im[0], log_dt[0], b_re[0], b_im[0],
                                         c_re[0], c_im[0])
    s_x = _s5_states(utx, smat)
    s_c = _s5_states(utc, smat)
    h_in = _s5_scan(a_t, s_c, s_x, bsz)
    y_ssm = _from_chunk_rows(_s5_apply(utx, toep, cpow, h_in), bsz)

    pad_r = jnp.zeros((d, LANES - N_EXPERTS), F32)
    w_r = jnp.concatenate([w_router[0], pad_r], axis=1)
    b_r = jnp.concatenate([b_router[0], jnp.zeros((LANES - N_EXPERTS,), F32)]).reshape(1, LANES)
    x1, h, ids, gates = _back(
        x, ux, y_ssm, mod_x, gm, w_rest, conv_w[0], conv_b[0].reshape(1, d_ssm),
        d_skip[0].reshape(1, d_ssm), w_glu[0].astype(BF16), w_ssm_br[0].astype(BF16),
        w_conv_br[0].astype(BF16), w_o[0].astype(BF16), g_ffn[0].reshape(1, d), w_r, b_r)

    n_tok = bsz * seq
    buf_token, pos, block_expert, n_used = _routing(ids.reshape(n_tok, LANES)[:, :TOP_K], n_tok)
    yb = _moe(h, buf_token, block_expert, n_used,
              w_gu[0].astype(BF16), b_gu[0], w_down[0].astype(BF16), b_down[0])
    out = _combine(yb, pos, x1.reshape(n_tok, d), gates.reshape(n_tok, LANES), mod_x,
                   g_final.reshape(1, d), seq)
    return out.reshape(bsz, seq, d)
```

```python
import functools
import math

import jax
import jax.numpy as jnp
from jax import lax
from jax.experimental import pallas as pl
from jax.experimental.pallas import tpu as pltpu

F32 = jnp.float32
BF16 = jnp.bfloat16
HIGHEST = lax.Precision.HIGHEST

RMS_EPS = 1e-6
GRID_W = 64
SSM_GROUP = 16
SSM_STATE = 64
N_EXPERTS = 32
TOP_K = 4
SWIGLU_LIMIT = 7.0
SWIGLU_ALPHA = 1.702

CHUNK_T = 16
LANES = 128
SUBLANES = 8
V7X_VMEM_LIMIT_BYTES = 56 * 1024 * 1024

S5_TR = 1024
MOE_TB = 256
COMBINE_TM = 256


def _cparams(sem):
    return pltpu.CompilerParams(dimension_semantics=sem,
                                vmem_limit_bytes=V7X_VMEM_LIMIT_BYTES)


def _bdot(a, b):
    return jnp.dot(a, b, preferred_element_type=F32)


def _rmsnorm(xt, g):
    ms = jnp.mean(xt * xt, axis=-1, keepdims=True)
    return xt * lax.rsqrt(ms + RMS_EPS) * g


def _adaln_kernel(c_ref, w_ref, b_ref, o_ref):
    s = jax.nn.silu(c_ref[...])
    o_ref[...] = jnp.dot(s, w_ref[...], precision=HIGHEST,
                         preferred_element_type=F32) + b_ref[...]


def _adaln(cond, w_mod, b_mod):
    r, d = cond.shape
    n = w_mod.shape[1]
    tn = n // 4
    return pl.pallas_call(
        _adaln_kernel,
        out_shape=jax.ShapeDtypeStruct((r, n), F32),
        grid=(n // tn,),
        in_specs=[pl.BlockSpec((r, d), lambda j: (0, 0)),
                  pl.BlockSpec((d, tn), lambda j: (0, j)),
                  pl.BlockSpec((1, tn), lambda j: (0, j))],
        out_specs=pl.BlockSpec((r, tn), lambda j: (0, j)),
        compiler_params=_cparams(("arbitrary",)),
        name="adaln",
    )(cond, w_mod, b_mod.reshape(1, n))


ROW_T = GRID_W
PITCH = ROW_T + 8
N_SLAB = 4
CHUNKS_PER_TILE = ROW_T // CHUNK_T
GRANULES = LANES // SSM_GROUP


def _granule_transpose_matrix():
    n = GRANULES * LANES
    idx = jnp.arange(n)
    j, m, c = idx // LANES, (idx % LANES) // SSM_GROUP, idx % SSM_GROUP
    dst = m * LANES + j * SSM_GROUP + c
    return jnp.zeros((n, n), F32).at[idx, dst].set(1.0).astype(BF16)


def _modulate_tile(xn, mod_ref, shift_row, scale_row):
    d = xn.shape[1]
    x3 = xn.reshape(8, ROW_T, d)
    x3 = x3 * (1.0 + mod_ref[:, scale_row:scale_row + 1, :]) + mod_ref[:, shift_row:shift_row + 1, :]
    return x3.reshape(8 * ROW_T, d)


def _front_kernel(x_ref, mod_ref, g_ref, w_ref, perm_ref, u_ref, us):
    d = x_ref.shape[2]
    xn = _rmsnorm(x_ref[...].reshape(8 * ROW_T, d), g_ref[...])
    hx = _modulate_tile(xn, mod_ref, 0, 1)
    u = _bdot(hx.astype(BF16), w_ref[...])
    for sl in range(N_SLAB):
        for b in range(8):
            us[sl, pl.ds(b * PITCH, ROW_T), :] = u[b * ROW_T:(b + 1) * ROW_T,
                                                  sl * LANES:(sl + 1) * LANES]
    blocks = []
    for sl in range(N_SLAB):
        for th in range(CHUNK_T // GRANULES):
            rows = []
            for kk in range(CHUNKS_PER_TILE):
                t0 = kk * CHUNK_T + th * GRANULES
                rows.append(jnp.concatenate(
                    [us[sl, pl.ds(t0 + j, 8, stride=PITCH), :] for j in range(GRANULES)], axis=1))
            blocks.append(jnp.concatenate(rows, axis=0))
    acat = jnp.concatenate(blocks, axis=0).astype(BF16)
    bmat = _bdot(acat, perm_ref[...])
    nrow = CHUNKS_PER_TILE * 8
    blk = 0
    for sl in range(N_SLAB):
        for th in range(CHUNK_T // GRANULES):
            for m in range(GRANULES):
                u_ref[sl * GRANULES + m, :, th * LANES:(th + 1) * LANES] = (
                    bmat[blk * nrow:(blk + 1) * nrow, m * LANES:(m + 1) * LANES].astype(u_ref.dtype))
            blk += 1


def _front(x, mod, g_mix, w_u, perm):
    b, s, d = x.shape
    n = w_u.shape[1]
    g = n // SSM_GROUP
    assert s % ROW_T == 0 and b % 8 == 0 and n == N_SLAB * LANES
    k = s // CHUNK_T
    tiles = s // ROW_T
    nrow = CHUNKS_PER_TILE * 8
    shared = mod.shape[0] == 1
    mod_spec = (pl.BlockSpec((1, 8, d), lambda i, j: (0, 0, 0)) if shared
                else pl.BlockSpec((8, 8, d), lambda i, j: (i, 0, 0)))
    return pl.pallas_call(
        _front_kernel,
        out_shape=jax.ShapeDtypeStruct((g, (b // 8) * k * 8, CHUNK_T * SSM_GROUP), BF16),
        grid=(b // 8, tiles),
        in_specs=[pl.BlockSpec((8, ROW_T, d), lambda i, j: (i, j, 0)),
                  mod_spec,
                  pl.BlockSpec((1, d), lambda i, j: (0, 0)),
                  pl.BlockSpec((d, n), lambda i, j: (0, 0)),
                  pl.BlockSpec(perm.shape, lambda i, j: (0, 0))],
        out_specs=pl.BlockSpec((g, nrow, CHUNK_T * SSM_GROUP), lambda i, j: (0, i * tiles + j, 0)),
        scratch_shapes=[pltpu.VMEM((N_SLAB, 8 * PITCH, LANES), F32)],
        compiler_params=_cparams(("arbitrary", "arbitrary")),
        name="front",
    )(x, mod, g_mix, w_u, perm)


def _s5_discretise(lam_re, lam_im, log_dt, b_re, b_im):
    dt = jnp.exp(log_dt)[..., None]
    mag = jnp.exp(lam_re * dt)
    a_re, a_im = mag * jnp.cos(lam_im * dt), mag * jnp.sin(lam_im * dt)
    den = lam_re * lam_re + lam_im * lam_im
    q_re = ((a_re - 1) * lam_re + a_im * lam_im) / den
    q_im = (a_im * lam_re - (a_re - 1) * lam_im) / den
    bb_re = q_re[..., None] * b_re - q_im[..., None] * b_im
    bb_im = q_re[..., None] * b_im + q_im[..., None] * b_re
    return a_re, a_im, bb_re, bb_im


def _s5_matrices(lam_re, lam_im, log_dt, b_re, b_im, c_re, c_im, d_skip):
    t = CHUNK_T
    a_re, a_im, bb_re, bb_im = _s5_discretise(lam_re, lam_im, log_dt, b_re, b_im)
    g, p = a_re.shape[1], a_re.shape[2]
    ch = bb_re.shape[-1]
    pw_re, pw_im = [jnp.ones_like(a_re)], [jnp.zeros_like(a_im)]
    for _ in range(t):
        r, i = pw_re[-1], pw_im[-1]
        pw_re.append(r * a_re - i * a_im)
        pw_im.append(r * a_im + i * a_re)
    pw_re, pw_im = jnp.stack(pw_re), jnp.stack(pw_im)
    cr = c_re[None] * pw_re[:, :, :, None, :] - c_im[None] * pw_im[:, :, :, None, :]
    ci = c_re[None] * pw_im[:, :, :, None, :] + c_im[None] * pw_re[:, :, :, None, :]
    kern = (jnp.einsum("dzgcp,zgpe->dzgce", cr, bb_re, precision=HIGHEST)
            - jnp.einsum("dzgcp,zgpe->dzgce", ci, bb_im, precision=HIGHEST))
    s_idx = jnp.arange(t)[:, None]
    t_idx = jnp.arange(t)[None, :]
    dlt = t_idx - s_idx
    kf = kern[jnp.clip(dlt, 0, t - 1), 0]
    kb = kern[jnp.clip(-dlt, 0, t - 1), 1]
    msk_f = (dlt >= 0)[:, :, None, None, None]
    msk_b = (dlt <= 0)[:, :, None, None, None]
    tsum = jnp.where(msk_f, kf, 0.0) + jnp.where(msk_b, kb, 0.0)
    toep = tsum.transpose(2, 0, 4, 1, 3).reshape(g, t * ch, t * ch)
    skip = jnp.tile(d_skip.reshape(g, 1, ch), (1, t, 1)).reshape(g, 1, t * ch)
    toep = toep + jnp.eye(t * ch, dtype=F32)[None] * skip

    pf_re, pf_im = pw_re[t - 1 - jnp.arange(t), 0], pw_im[t - 1 - jnp.arange(t), 0]
    pb_re, pb_im = pw_re[jnp.arange(t), 1], pw_im[jnp.arange(t), 1]

    def in_to_state(q_re, q_im, z):
        bre = bb_re[z].transpose(0, 2, 1)[None]
        bim = bb_im[z].transpose(0, 2, 1)[None]
        s_re = q_re[:, :, None, :] * bre - q_im[:, :, None, :] * bim
        s_im = q_re[:, :, None, :] * bim + q_im[:, :, None, :] * bre
        to_rows = lambda m: m.transpose(1, 0, 2, 3).reshape(g, t * ch, p)
        return to_rows(s_re), to_rows(s_im)

    sf_re, sf_im = in_to_state(pf_re, pf_im, 0)
    sb_re, sb_im = in_to_state(pb_re, pb_im, 1)
    s_all = jnp.stack([sf_re, sf_im, sb_re, sb_im], axis=1)
    s_all = s_all.reshape(g // 2, 2, 4, t * ch, p)
    eye2 = jnp.eye(2, dtype=F32)
    smat = jnp.einsum("ajqrp,jk->ajrqkp", s_all, eye2).reshape(g // 2, 2 * t * ch, 4 * 2 * p)

    def state_to_out(idx, z):
        cre = cr[idx, z].transpose(1, 3, 0, 2).reshape(g, p, t * ch)
        cim = ci[idx, z].transpose(1, 3, 0, 2).reshape(g, p, t * ch)
        return cre, -cim

    cf_re, cf_im = state_to_out(jnp.arange(t) + 1, 0)
    cb_re, cb_im = state_to_out(t - jnp.arange(t), 1)
    c_all = jnp.stack([cf_re, cf_im, cb_re, cb_im], axis=1)
    c_all = c_all.reshape(g // 2, 2, 4, p, t * ch)
    cpow = jnp.einsum("ajqpn,jk->aqjpkn", c_all, eye2).reshape(g // 2, 4 * 2 * p, 2 * t * ch)

    a_t = jnp.stack([pw_re[t, 0], pw_im[t, 0], pw_re[t, 1], pw_im[t, 1]]).reshape(4, g * p)
    return smat.astype(BF16), toep.astype(BF16), cpow.astype(BF16), a_t


def _s5_states_kernel(u_ref, m_ref, fre_ref, fim_ref, bre_ref, bim_ref):
    lhs = jnp.concatenate([u_ref[0], u_ref[1]], axis=1)
    res = _bdot(lhs, m_ref[0])
    fre_ref[...] = res[:, 0 * LANES:1 * LANES]
    fim_ref[...] = res[:, 1 * LANES:2 * LANES]
    bre_ref[...] = res[:, 2 * LANES:3 * LANES]
    bim_ref[...] = res[:, 3 * LANES:4 * LANES]


def _s5_states(ut, smat):
    g, rows, kdim = ut.shape
    tr = min(S5_TR, rows)
    assert rows % tr == 0
    n_state = g * SSM_STATE
    out = jax.ShapeDtypeStruct((rows, n_state), F32)
    ospec = pl.BlockSpec((tr, LANES), lambda a, r: (r, a))
    return pl.pallas_call(
        _s5_states_kernel,
        out_shape=(out, out, out, out),
        grid=(g // 2, rows // tr),
        in_specs=[pl.BlockSpec((2, tr, kdim), lambda a, r: (a, r, 0)),
                  pl.BlockSpec((1, 2 * kdim, 4 * LANES), lambda a, r: (a, 0, 0))],
        out_specs=(ospec, ospec, ospec, ospec),
        compiler_params=_cparams(("arbitrary", "arbitrary")),
        name="s5_states",
    )(ut, smat)


def _s5_scan_kernel(a_ref, cfr_ref, cfi_ref, cbr_ref, cbi_ref,
                    xfr_ref, xfi_ref, xbr_ref, xbi_ref,
                    hfr_ref, hfi_ref, hbr_ref, hbi_ref, *, halves, kc, kx):
    a_fr, a_fi = a_ref[0:1, :], a_ref[1:2, :]
    a_br, a_bi = a_ref[2:3, :], a_ref[3:4, :]

    def rows(half, k, n_chunks, count=1):
        return pl.ds(pl.multiple_of((half * n_chunks + k) * 8, 8 * count), 8 * count)

    def step(h_re, h_im, a_re, a_im, s_re, s_im):
        return (a_re * h_re - a_im * h_im + s_re, a_re * h_im + a_im * h_re + s_im)

    zero = jnp.zeros((8, LANES), F32)

    def ctx_body(i, carry):
        out = []
        for half in range(halves):
            fr, fi, br, bi = carry[half]
            kf, kb = i, kc - 1 - i
            fr, fi = step(fr, fi, a_fr, a_fi, cfr_ref[rows(half, kf, kc), :],
                          cfi_ref[rows(half, kf, kc), :])
            br, bi = step(br, bi, a_br, a_bi, cbr_ref[rows(half, kb, kc), :],
                          cbi_ref[rows(half, kb, kc), :])
            out.append((fr, fi, br, bi))
        return tuple(out)

    carry = lax.fori_loop(0, kc, ctx_body, ((zero,) * 4,) * halves)

    def x_body(i, carry):
        out = []
        for half in range(halves):
            fr, fi, br, bi = carry[half]
            kf = 2 * i
            kb = kx - 2 - 2 * i
            fr1, fi1 = step(fr, fi, a_fr, a_fi, xfr_ref[rows(half, kf, kx), :],
                            xfi_ref[rows(half, kf, kx), :])
            br1, bi1 = step(br, bi, a_br, a_bi, xbr_ref[rows(half, kb + 1, kx), :],
                            xbi_ref[rows(half, kb + 1, kx), :])
            hfr_ref[rows(half, kf, kx, 2), :] = jnp.concatenate([fr, fr1], 0).astype(hfr_ref.dtype)
            hfi_ref[rows(half, kf, kx, 2), :] = jnp.concatenate([fi, fi1], 0).astype(hfi_ref.dtype)
            hbr_ref[rows(half, kb, kx, 2), :] = jnp.concatenate([br1, br], 0).astype(hbr_ref.dtype)
            hbi_ref[rows(half, kb, kx, 2), :] = jnp.concatenate([bi1, bi], 0).astype(hbi_ref.dtype)
            fr, fi = step(fr1, fi1, a_fr, a_fi, xfr_ref[rows(half, kf + 1, kx), :],
                          xfi_ref[rows(half, kf + 1, kx), :])
            br, bi = step(br1, bi1, a_br, a_bi, xbr_ref[rows(half, kb, kx), :],
                          xbi_ref[rows(half, kb, kx), :])
            out.append((fr, fi, br, bi))
        return tuple(out)

    lax.fori_loop(0, kx // 2, x_body, carry)


def _s5_scan(a_t, s_ctx, s_x, batch):
    rows_c, n_state = s_ctx[0].shape
    rows_x = s_x[0].shape[0]
    kc, kx = rows_c // batch, rows_x // batch
    assert kx % 2 == 0
    out = jax.ShapeDtypeStruct((rows_x, n_state), BF16)
    cspec = pl.BlockSpec((rows_c, LANES), lambda j: (0, j))
    xspec = pl.BlockSpec((rows_x, LANES), lambda j: (0, j))
    return pl.pallas_call(
        functools.partial(_s5_scan_kernel, halves=batch // 8, kc=kc, kx=kx),
        out_shape=(out, out, out, out),
        grid=(n_state // LANES,),
        in_specs=[pl.BlockSpec((4, LANES), lambda j: (0, j))] + [cspec] * 4 + [xspec] * 4,
        out_specs=(xspec, xspec, xspec, xspec),
        compiler_params=_cparams(("arbitrary",)),
        name="s5_scan",
    )(a_t, *s_ctx, *s_x)


def _s5_apply_kernel(u_ref, t_ref, c_ref, hfr_ref, hfi_ref, hbr_ref, hbi_ref, y_ref):
    hcat = jnp.concatenate([hfr_ref[...], hfi_ref[...], hbr_ref[...], hbi_ref[...]], axis=1)
    yst = _bdot(hcat, c_ref[0])
    n = u_ref.shape[2]
    y_ref[0] = (_bdot(u_ref[0], t_ref[0]) + yst[:, :n]).astype(y_ref.dtype)
    y_ref[1] = (_bdot(u_ref[1], t_ref[1]) + yst[:, n:]).astype(y_ref.dtype)


def _s5_apply(ut, toep, cpow, h_in):
    g, rows, kdim = ut.shape
    tr = min(S5_TR, rows)
    hspec = pl.BlockSpec((tr, LANES), lambda a, r: (r, a))
    return pl.pallas_call(
        _s5_apply_kernel,
        out_shape=jax.ShapeDtypeStruct((g, rows, kdim), BF16),
        grid=(g // 2, rows // tr),
        in_specs=[pl.BlockSpec((2, tr, kdim), lambda a, r: (a, r, 0)),
                  pl.BlockSpec((2, kdim, kdim), lambda a, r: (a, 0, 0)),
                  pl.BlockSpec((1, 4 * LANES, 2 * kdim), lambda a, r: (a, 0, 0))] + [hspec] * 4,
        out_specs=pl.BlockSpec((2, tr, kdim), lambda a, r: (a, r, 0)),
        compiler_params=_cparams(("arbitrary", "arbitrary")),
        name="s5_apply",
    )(ut, toep, cpow, *h_in)


def _chunk_rows_to_tokens(y_ref, perm_ref, ys):
    nrow = CHUNKS_PER_TILE * 8
    blocks = []
    for sl in range(N_SLAB):
        for th in range(CHUNK_T // GRANULES):
            blocks.append(jnp.concatenate(
                [y_ref[sl * GRANULES + m, :, th * LANES:(th + 1) * LANES] for m in range(GRANULES)],
                axis=1))
    acat = _bdot(jnp.concatenate(blocks, axis=0), perm_ref[...])
    blk = 0
    for sl in range(N_SLAB):
        for th in range(CHUNK_T // GRANULES):
            for kk in range(CHUNKS_PER_TILE):
                for j in range(GRANULES):
                    t = kk * CHUNK_T + th * GRANULES + j
                    ys[sl, pl.ds(t, 8, stride=PITCH), :] = acat[blk * nrow + kk * 8:
                                                               blk * nrow + (kk + 1) * 8,
                                                               j * LANES:(j + 1) * LANES]
            blk += 1
    return jnp.concatenate(
        [jnp.concatenate([ys[sl, pl.ds(b * PITCH, ROW_T), :] for sl in range(N_SLAB)], axis=1)
         for b in range(8)], axis=0)


def _back_kernel(x_ref, y_ref, mod_ref, perm_ref, gmix_ref, win_ref, convw_ref, convb_ref,
                 wglu_ref, wsbr_ref, wcbr_ref, wo_ref, gffn_ref, wr_ref, br_ref,
                 x1_ref, h_ref, ids_ref, gates_ref, ys_scr):
    d_model = x_ref.shape[2]
    tm = 8 * ROW_T
    d_ssm = N_SLAB * LANES
    xt = x_ref[...].reshape(tm, d_model)
    xn = _rmsnorm(xt, gmix_ref[...])
    hx = _modulate_tile(xn, mod_ref, 0, 1).astype(BF16)
    p = _bdot(hx, win_ref[...])
    v = p[:, 0:d_ssm]
    gate_b = p[:, d_ssm:2 * d_ssm]
    gate_c = p[:, 2 * d_ssm:3 * d_ssm]
    g_s = p[:, 3 * d_ssm:3 * d_ssm + d_model]
    g_c = p[:, 3 * d_ssm + d_model:]

    z = gate_c * v
    col = lax.broadcasted_iota(jnp.int32, z.shape, 0) % GRID_W
    z_prev = jnp.where(col == 0, 0.0, pltpu.roll(z, 1, 0))
    z_next = jnp.where(col == GRID_W - 1, 0.0, pltpu.roll(z, tm - 1, 0))
    conv = (z_prev * convw_ref[0:1, :] + z * convw_ref[1:2, :]
            + z_next * convw_ref[2:3, :] + convb_ref[...])
    y_conv = gate_b * conv

    ys = jax.nn.gelu(_chunk_rows_to_tokens(y_ref, perm_ref, ys_scr))
    ys = ys * jax.nn.sigmoid(_bdot(ys.astype(BF16), wglu_ref[...]))

    y_s = _bdot(ys.astype(BF16), wsbr_ref[...])
    y_c = _bdot(y_conv.astype(BF16), wcbr_ref[...])
    merged = jax.nn.sigmoid(g_s) * y_s + jax.nn.sigmoid(g_c) * y_c
    mo = _bdot(merged.astype(BF16), wo_ref[...])
    x1 = xt + (mo.reshape(8, ROW_T, d_model) * mod_ref[:, 2:3, :]).reshape(tm, d_model)
    x1_ref[...] = x1.reshape(8, ROW_T, d_model)

    hn = _modulate_tile(_rmsnorm(x1, gffn_ref[...]), mod_ref, 3, 4)
    for b in range(8):
        for s in range(SUBLANES):
            h_ref[b, pl.ds(s, ROW_T, stride=SUBLANES), :] = hn[b * ROW_T:(b + 1) * ROW_T,
                                                              s * LANES:(s + 1) * LANES]

    logits = jnp.dot(hn, wr_ref[...], precision=HIGHEST, preferred_element_type=F32) + br_ref[...]
    lane = lax.broadcasted_iota(jnp.int32, logits.shape, 1)
    neg = jnp.float32(-jnp.inf)
    cur = jnp.where(lane < N_EXPERTS, logits, neg)
    vals, idxs = [], []
    for _ in range(TOP_K):
        mk = jnp.max(cur, axis=-1, keepdims=True)
        ik = jnp.min(jnp.where(cur == mk, lane, LANES), axis=-1, keepdims=True)
        vals.append(mk)
        idxs.append(ik)
        cur = jnp.where(lane == ik, neg, cur)
    exps = [jnp.exp(vk - vals[0]) for vk in vals]
    denom = exps[0] + exps[1] + exps[2] + exps[3]
    ids = jnp.zeros(logits.shape, jnp.int32)
    gates = jnp.zeros(logits.shape, F32)
    for k in range(TOP_K):
        ids = jnp.where(lane == k, idxs[k], ids)
        gates = jnp.where(lane == k, exps[k] / denom, gates)
    ids_ref[...] = ids.reshape(8, ROW_T, LANES)
    gates_ref[...] = gates.reshape(8, ROW_T, LANES)


def _back(x, y_chunks, mod, perm, g_mix, w_rest, conv_w, conv_b, w_glu, w_ssm_br,
          w_conv_br, w_o, g_ffn, w_router, b_router):
    b, s, d = x.shape
    g, _, kdim = y_chunks.shape
    tiles = s // ROW_T
    nrow = CHUNKS_PER_TILE * 8
    tok = lambda n: pl.BlockSpec((8, ROW_T, n), lambda i, j: (i, j, 0))

    def const(arr):
        nd = arr.ndim
        return pl.BlockSpec(arr.shape, lambda i, j: (0,) * nd, pipeline_mode=pl.Buffered(1))

    params = (perm, g_mix, w_rest, conv_w, conv_b, w_glu, w_ssm_br, w_conv_br, w_o, g_ffn,
              w_router, b_router)
    return pl.pallas_call(
        _back_kernel,
        out_shape=(jax.ShapeDtypeStruct((b, s, d), F32),
                   jax.ShapeDtypeStruct((b, s * SUBLANES, LANES), F32),
                   jax.ShapeDtypeStruct((b, s, LANES), jnp.int32),
                   jax.ShapeDtypeStruct((b, s, LANES), F32)),
        grid=(b // 8, tiles),
        in_specs=[tok(d),
                  pl.BlockSpec((g, nrow, kdim), lambda i, j: (0, i * tiles + j, 0)),
                  pl.BlockSpec((8, 8, d), lambda i, j: (i, 0, 0))] + [const(a) for a in params],
        out_specs=(tok(d), pl.BlockSpec((8, ROW_T * SUBLANES, LANES), lambda i, j: (i, j, 0)),
                   tok(LANES), tok(LANES)),
        scratch_shapes=[pltpu.VMEM((N_SLAB, 8 * PITCH, LANES), F32)],
        compiler_params=_cparams(("arbitrary", "arbitrary")),
        name="back",
    )(x, y_chunks, mod, *params)


def _routing(ids, n_tok):
    tb = MOE_TB
    n_slot = n_tok * TOP_K
    slot_expert = ids.reshape(-1)
    order = jnp.argsort(slot_expert, stable=True).astype(jnp.int32)
    inv = jnp.argsort(order).astype(jnp.int32)
    srt_expert = slot_expert[order]
    counts = jnp.sum(slot_expert[:, None] == jnp.arange(N_EXPERTS, dtype=jnp.int32)[None, :],
                     axis=0, dtype=jnp.int32)
    padded = (counts + tb - 1) // tb * tb
    pad_end = jnp.cumsum(padded)
    pad_start = pad_end - padded
    start = jnp.cumsum(counts) - counts
    dest_sorted = pad_start[srt_expert] + jnp.arange(n_slot, dtype=jnp.int32) - start[srt_expert]
    pos = dest_sorted[inv].reshape(n_tok, TOP_K)
    n_blocks = n_slot // tb + N_EXPERTS
    cap = n_blocks * tb
    block_first_row = jnp.arange(n_blocks, dtype=jnp.int32) * tb
    block_expert = jnp.minimum(
        jnp.sum(pad_end[None, :] <= block_first_row[:, None], axis=1, dtype=jnp.int32),
        N_EXPERTS - 1)
    row = jnp.arange(cap, dtype=jnp.int32)
    row_expert = jnp.repeat(block_expert, tb)
    within = row - pad_start[row_expert]
    valid = within < counts[row_expert]
    src = jnp.clip(start[row_expert] + within, 0, n_slot - 1)
    buf_token = jnp.where(valid, order[src] // TOP_K, 0).astype(jnp.int32)
    n_used = (pad_end[-1] // tb).astype(jnp.int32).reshape(1)
    return buf_token, pos.astype(jnp.int32), block_expert, n_used


def _to_tiles(ref, val):
    rows = val.shape[0]
    for s in range(SUBLANES):
        ref[pl.ds(s, rows, stride=SUBLANES), :] = val[:, s * LANES:(s + 1) * LANES]


def _tile_piece(ref, first_row, rows, s):
    return ref[pl.ds(first_row * SUBLANES + s, rows, stride=SUBLANES), :]


def _tile_gather_start(idx_ref, src_hbm, dst, sem, n_rows, unrolled, alternate_priority=False):
    def copy(r, t):
        return pltpu.make_async_copy(
            src_hbm.at[pl.ds(pl.multiple_of(t * SUBLANES, SUBLANES), SUBLANES), :],
            dst.at[pl.ds(pl.multiple_of(r * SUBLANES, SUBLANES), SUBLANES), :], sem)

    if unrolled:
        for r in range(n_rows):
            copy(r, idx_ref[0, 0, r]).start(priority=(r % 2) if alternate_priority else 0)
    else:
        def body(r, carry):
            copy(r, idx_ref[0, 0, r]).start()
            return carry
        lax.fori_loop(0, n_rows, body, 0, unroll=8)


def _tile_gather_wait(src_hbm, dst, sem):
    pltpu.make_async_copy(src_hbm.at[pl.ds(0, dst.shape[0]), :], dst, sem).wait()


def _ring_step(i, last, bufs, sem, wait, prefetch, compute):
    n = len(bufs)
    for p in range(n):
        def branch(p=p):
            wait(bufs[p], sem.at[p])
            q = (p + n - 1) % n
            prefetch(bufs[q], sem.at[q])
            compute(bufs[p])

            @pl.when(i == last)
            def _():
                for r in range(1, n):
                    wait(bufs[(p + r) % n], sem.at[(p + r) % n])

        pl.when((i <= last) & (lax.rem(i, n) == p))(branch)


def _moe_kernel(be_ref, nu_ref, tok0_ref, tok1_ref, tok2_ref, h_hbm, wgu_ref, bgu_ref, wd_ref,
                bd_ref, out_ref, xbuf0, xbuf1, xbuf2, sem):
    del be_ref
    i = pl.program_id(0)
    n_used = nu_ref[0]
    tb = xbuf0.shape[0] // SUBLANES
    f = wd_ref.shape[1]

    @pl.when(i == 0)
    def _():
        _tile_gather_start(tok0_ref, h_hbm, xbuf0, sem.at[0], tb, unrolled=False)
        _tile_gather_start(tok1_ref, h_hbm, xbuf1, sem.at[1], tb, unrolled=False)

    def compute(xcur):
        xe = jnp.concatenate([_tile_piece(xcur, 0, tb, s) for s in range(SUBLANES)],
                             axis=1).astype(BF16)
        gu = _bdot(xe, wgu_ref[0]) + bgu_ref[0]
        gt = jnp.minimum(gu[:, :f], SWIGLU_LIMIT)
        up = jnp.clip(gu[:, f:], -SWIGLU_LIMIT, SWIGLU_LIMIT)
        act = gt * jax.nn.sigmoid(SWIGLU_ALPHA * gt) * (up + 1.0)
        _to_tiles(out_ref, _bdot(act.astype(BF16), wd_ref[0]) + bd_ref[0])

    _ring_step(i, n_used - 1, (xbuf0, xbuf1, xbuf2), sem,
               wait=lambda buf, s: _tile_gather_wait(h_hbm, buf, s),
               prefetch=lambda buf, s: _tile_gather_start(tok2_ref, h_hbm, buf, s, tb,
                                                          unrolled=True),
               compute=compute)

    @pl.when(i >= n_used)
    def _():
        out_ref[...] = jnp.zeros(out_ref.shape, out_ref.dtype)


def _moe(h_tiles, buf_token, block_expert, n_used, w_gu, b_gu, w_down, b_down):
    e, d, f2 = w_gu.shape
    f = f2 // 2
    tb = MOE_TB
    cap = buf_token.shape[0]
    n_blocks = cap // tb
    tok3 = buf_token.reshape(n_blocks, 1, tb)
    smem_blk = lambda imap: pl.BlockSpec((1, 1, tb), imap, memory_space=pltpu.SMEM)
    grid_spec = pltpu.PrefetchScalarGridSpec(
        num_scalar_prefetch=2,
        grid=(n_blocks,),
        in_specs=[smem_blk(lambda i, be, nu: (i, 0, 0)),
                  smem_blk(lambda i, be, nu: (jnp.minimum(i + 1, n_blocks - 1), 0, 0)),
                  smem_blk(lambda i, be, nu: (jnp.minimum(i + 2, n_blocks - 1), 0, 0)),
                  pl.BlockSpec(memory_space=pl.ANY),
                  pl.BlockSpec((1, d, f2), lambda i, be, nu: (be[i], 0, 0)),
                  pl.BlockSpec((1, 1, f2), lambda i, be, nu: (be[i], 0, 0)),
                  pl.BlockSpec((1, f, d), lambda i, be, nu: (be[i], 0, 0)),
                  pl.BlockSpec((1, 1, d), lambda i, be, nu: (be[i], 0, 0))],
        out_specs=pl.BlockSpec((tb * SUBLANES, LANES), lambda i, be, nu: (i, 0)),
        scratch_shapes=[pltpu.VMEM((tb * SUBLANES, LANES), F32)] * 3
        + [pltpu.SemaphoreType.DMA((3,))],
    )
    return pl.pallas_call(
        _moe_kernel,
        out_shape=jax.ShapeDtypeStruct((cap * SUBLANES, LANES), F32),
        grid_spec=grid_spec,
        compiler_params=_cparams(("arbitrary",)),
        name="moe",
    )(block_expert, n_used, tok3, tok3, tok3, h_tiles, w_gu, b_gu.reshape(e, 1, f2),
      w_down, b_down.reshape(e, 1, d))


def _combine_kernel(pos0_ref, pos1_ref, pos2_ref, yb_hbm, x1_ref, gates_ref, mod_ref, gfin_ref,
                    out_ref, buf0, buf1, buf2, sem):
    i = pl.program_id(0)
    n = pl.num_programs(0)
    n_rows = buf0.shape[0] // SUBLANES
    tm = n_rows // TOP_K

    @pl.when(i == 0)
    def _():
        _tile_gather_start(pos0_ref, yb_hbm, buf0, sem.at[0], n_rows, unrolled=False)
        _tile_gather_start(pos1_ref, yb_hbm, buf1, sem.at[1], n_rows, unrolled=False)

    def compute(cur):
        g = gates_ref[...]
        gk = [jnp.broadcast_to(g[:, k:k + 1], (tm, LANES)) for k in range(TOP_K)]
        pieces = []
        for s in range(SUBLANES):
            acc = gk[0] * _tile_piece(cur, 0, tm, s)
            for k in range(1, TOP_K):
                acc = acc + gk[k] * _tile_piece(cur, k * tm, tm, s)
            pieces.append(acc)
        x2 = x1_ref[...] + mod_ref[0, 5:6, :] * jnp.concatenate(pieces, axis=1)
        out_ref[...] = _rmsnorm(x2, gfin_ref[...])

    _ring_step(i, n - 1, (buf0, buf1, buf2), sem,
               wait=lambda buf, s: _tile_gather_wait(yb_hbm, buf, s),
               prefetch=lambda buf, s: _tile_gather_start(pos2_ref, yb_hbm, buf, s, n_rows,
                                                          unrolled=True, alternate_priority=True),
               compute=compute)


def _combine(yb, pos, x1_2d, gates_2d, mod, g_final, seq):
    n_tok, d = x1_2d.shape
    tm = min(COMBINE_TM, seq)
    assert seq % tm == 0
    n_tiles = n_tok // tm
    per_batch = seq // tm
    pos3 = pos.reshape(n_tiles, tm, TOP_K).transpose(0, 2, 1).reshape(n_tiles, 1, TOP_K * tm)
    smem_blk = lambda imap: pl.BlockSpec((1, 1, TOP_K * tm), imap, memory_space=pltpu.SMEM)
    buf = pltpu.VMEM((TOP_K * tm * SUBLANES, LANES), F32)
    return pl.pallas_call(
        _combine_kernel,
        out_shape=jax.ShapeDtypeStruct((n_tok, d), F32),
        grid=(n_tiles,),
        in_specs=[smem_blk(lambda i: (i, 0, 0)),
                  smem_blk(lambda i: (jnp.minimum(i + 1, n_tiles - 1), 0, 0)),
                  smem_blk(lambda i: (jnp.minimum(i + 2, n_tiles - 1), 0, 0)),
                  pl.BlockSpec(memory_space=pl.ANY),
                  pl.BlockSpec((tm, d), lambda i: (i, 0)),
                  pl.BlockSpec((tm, LANES), lambda i: (i, 0)),
                  pl.BlockSpec((1, 8, d), lambda i: (i // per_batch, 0, 0)),
                  pl.BlockSpec((1, d), lambda i: (0, 0))],
        out_specs=pl.BlockSpec((tm, d), lambda i: (i, 0)),
        scratch_shapes=[buf, buf, buf, pltpu.SemaphoreType.DMA((3,))],
        compiler_params=_cparams(("arbitrary",)),
        name="combine",
    )(pos3, pos3, pos3, yb, x1_2d, gates_2d, mod, g_final)


def kernel(x, c, ctx, c_ctx, w_mod, b_mod, g_mix, w_in, lam_re, lam_im, log_dt, b_re, b_im,
           c_re, c_im, d_skip, w_glu, conv_w, conv_b, w_ssm_br, w_conv_br, w_o, g_ffn,
           w_router, b_router, w_gu, b_gu, w_down, b_down, g_final):
    depth = w_mod.shape[0]
    assert depth == 1, "single-layer trunk"
    bsz, seq, d = x.shape
    ctx_len = ctx.shape[1]
    d_ssm = d // 2
    assert bsz % 8 == 0 and seq % CHUNK_T == 0 and ctx_len % CHUNK_T == 0 and seq % GRID_W == 0
    assert d == SUBLANES * LANES, "row gathers move one (8,128) f32 tile per token"

    n_cond = -(-(bsz + 1) // 8) * 8
    cond = jnp.zeros((n_cond, d), F32).at[:bsz].set(c).at[bsz].set(c_ctx)
    m = _adaln(cond, w_mod[0], b_mod[0])
    zeros2 = jnp.zeros((n_cond, 2, d), F32)
    mod_all = jnp.concatenate([m.reshape(n_cond, 6, d), zeros2], axis=1)
    mod_x, mod_c = mod_all[:bsz], mod_all[bsz:bsz + 1]

    w_in_bf = w_in[0].astype(BF16)
    w_u, w_rest = w_in_bf[:, :d_ssm], w_in_bf[:, d_ssm:]
    gm = g_mix[0].reshape(1, d)

    perm = _granule_transpose_matrix()
    utx = _front(x, mod_x, gm, w_u, perm)
    utc = _front(ctx, mod_c, gm, w_u, perm)

    smat, toep, cpow, a_t = _s5_matrices(lam_re[0], lam_im[0], log_dt[0], b_re[0], b_im[0],
                                         c_re[0], c_im[0], d_skip[0])
    s_x = _s5_states(utx, smat)
    s_c = _s5_states(utc, smat)
    h_in = _s5_scan(a_t, s_c, s_x, bsz)
    y_chunks = _s5_apply(utx, toep, cpow, h_in)

    pad_r = jnp.zeros((d, LANES - N_EXPERTS), F32)
    w_r = jnp.concatenate([w_router[0], pad_r], axis=1)
    b_r = jnp.concatenate([b_router[0], jnp.zeros((LANES - N_EXPERTS,), F32)]).reshape(1, LANES)
    x1, h, ids, gates = _back(
        x, y_chunks, mod_x, perm, gm, w_rest, conv_w[0], conv_b[0].reshape(1, d_ssm),
        w_glu[0].astype(BF16), w_ssm_br[0].astype(BF16),
        w_conv_br[0].astype(BF16), w_o[0].astype(BF16), g_ffn[0].reshape(1, d), w_r, b_r)

    n_tok = bsz * seq
    buf_token, pos, block_expert, n_used = _routing(ids.reshape(n_tok, LANES)[:, :TOP_K], n_tok)
    yb = _moe(h.reshape(n_tok * SUBLANES, LANES), buf_token, block_expert, n_used,
              w_gu[0].astype(BF16), b_gu[0], w_down[0].astype(BF16), b_down[0])
    out = _combine(yb, pos, x1.reshape(n_tok, d), gates.reshape(n_tok, LANES), mod_x,
                   g_final.reshape(1, d), seq)
    return out.reshape(bsz, seq, d)
```

```python
import functools
import math

import jax
import jax.numpy as jnp
from jax import lax
from jax.experimental import pallas as pl
from jax.experimental.pallas import tpu as pltpu

F32 = jnp.float32
BF16 = jnp.bfloat16
HIGHEST = lax.Precision.HIGHEST

RMS_EPS = 1e-6
GRID_W = 64
SSM_GROUP = 16
SSM_STATE = 64
N_EXPERTS = 32
TOP_K = 4
SWIGLU_LIMIT = 7.0
SWIGLU_ALPHA = 1.702

CHUNK_T = 16
LANES = 128
SUBLANES = 8
V7X_VMEM_LIMIT_BYTES = 56 * 1024 * 1024

S5_TR = 1024
MOE_TB = 256
COMBINE_TM = 256


def _cparams(sem):
    return pltpu.CompilerParams(dimension_semantics=sem,
                                vmem_limit_bytes=V7X_VMEM_LIMIT_BYTES)


def _bdot(a, b):
    return jnp.dot(a, b, preferred_element_type=F32)


def _rmsnorm(xt, g):
    ms = jnp.mean(xt * xt, axis=-1, keepdims=True)
    return xt * lax.rsqrt(ms + RMS_EPS) * g


def _adaln_kernel(c_ref, w_ref, b_ref, o_ref):
    s = jax.nn.silu(c_ref[...])
    o_ref[...] = jnp.dot(s, w_ref[...], precision=HIGHEST,
                         preferred_element_type=F32) + b_ref[...]


def _adaln(cond, w_mod, b_mod):
    r, d = cond.shape
    n = w_mod.shape[1]
    tn = n // 4
    return pl.pallas_call(
        _adaln_kernel,
        out_shape=jax.ShapeDtypeStruct((r, n), F32),
        grid=(n // tn,),
        in_specs=[pl.BlockSpec((r, d), lambda j: (0, 0)),
                  pl.BlockSpec((d, tn), lambda j: (0, j)),
                  pl.BlockSpec((1, tn), lambda j: (0, j))],
        out_specs=pl.BlockSpec((r, tn), lambda j: (0, j)),
        compiler_params=_cparams(("arbitrary",)),
        name="adaln",
    )(cond, w_mod, b_mod.reshape(1, n))


ROW_T = GRID_W
PITCH = ROW_T + 8
N_SLAB = 4
CHUNKS_PER_TILE = ROW_T // CHUNK_T
GRANULES = LANES // SSM_GROUP


def _granule_transpose_matrix():
    n = GRANULES * LANES
    idx = jnp.arange(n)
    j, m, c = idx // LANES, (idx % LANES) // SSM_GROUP, idx % SSM_GROUP
    dst = m * LANES + j * SSM_GROUP + c
    return jnp.zeros((n, n), F32).at[idx, dst].set(1.0).astype(BF16)


def _modulate_tile(xn, mod_ref, shift_row, scale_row):
    d = xn.shape[1]
    x3 = xn.reshape(8, ROW_T, d)
    x3 = x3 * (1.0 + mod_ref[:, scale_row:scale_row + 1, :]) + mod_ref[:, shift_row:shift_row + 1, :]
    return x3.reshape(8 * ROW_T, d)


def _front_kernel(x_ref, mod_ref, g_ref, w_ref, perm_ref, u_ref, us):
    d = x_ref.shape[2]
    xn = _rmsnorm(x_ref[...].reshape(8 * ROW_T, d), g_ref[...])
    hx = _modulate_tile(xn, mod_ref, 0, 1)
    u = _bdot(hx.astype(BF16), w_ref[...])
    for sl in range(N_SLAB):
        for b in range(8):
            us[sl, pl.ds(b * PITCH, ROW_T), :] = u[b * ROW_T:(b + 1) * ROW_T,
                                                  sl * LANES:(sl + 1) * LANES]
    blocks = []
    for sl in range(N_SLAB):
        for th in range(CHUNK_T // GRANULES):
            rows = []
            for kk in range(CHUNKS_PER_TILE):
                t0 = kk * CHUNK_T + th * GRANULES
                rows.append(jnp.concatenate(
                    [us[sl, pl.ds(t0 + j, 8, stride=PITCH), :] for j in range(GRANULES)], axis=1))
            blocks.append(jnp.concatenate(rows, axis=0))
    acat = jnp.concatenate(blocks, axis=0).astype(BF16)
    bmat = _bdot(acat, perm_ref[...])
    nrow = CHUNKS_PER_TILE * 8
    blk = 0
    for sl in range(N_SLAB):
        for th in range(CHUNK_T // GRANULES):
            for m in range(GRANULES):
                u_ref[sl * GRANULES + m, :, th * LANES:(th + 1) * LANES] = (
                    bmat[blk * nrow:(blk + 1) * nrow, m * LANES:(m + 1) * LANES].astype(u_ref.dtype))
            blk += 1


def _front(x, mod, g_mix, w_u, perm):
    b, s, d = x.shape
    n = w_u.shape[1]
    g = n // SSM_GROUP
    assert s % ROW_T == 0 and b % 8 == 0 and n == N_SLAB * LANES
    k = s // CHUNK_T
    tiles = s // ROW_T
    nrow = CHUNKS_PER_TILE * 8
    shared = mod.shape[0] == 1
    mod_spec = (pl.BlockSpec((1, 8, d), lambda i, j: (0, 0, 0)) if shared
                else pl.BlockSpec((8, 8, d), lambda i, j: (i, 0, 0)))
    return pl.pallas_call(
        _front_kernel,
        out_shape=jax.ShapeDtypeStruct((g, (b // 8) * k * 8, CHUNK_T * SSM_GROUP), BF16),
        grid=(b // 8, tiles),
        in_specs=[pl.BlockSpec((8, ROW_T, d), lambda i, j: (i, j, 0)),
                  mod_spec,
                  pl.BlockSpec((1, d), lambda i, j: (0, 0)),
                  pl.BlockSpec((d, n), lambda i, j: (0, 0)),
                  pl.BlockSpec(perm.shape, lambda i, j: (0, 0))],
        out_specs=pl.BlockSpec((g, nrow, CHUNK_T * SSM_GROUP), lambda i, j: (0, i * tiles + j, 0)),
        scratch_shapes=[pltpu.VMEM((N_SLAB, 8 * PITCH, LANES), F32)],
        compiler_params=_cparams(("arbitrary", "arbitrary")),
        name="front",
    )(x, mod, g_mix, w_u, perm)


def _s5_discretise(lam_re, lam_im, log_dt, b_re, b_im):
    dt = jnp.exp(log_dt)[..., None]
    mag = jnp.exp(lam_re * dt)
    a_re, a_im = mag * jnp.cos(lam_im * dt), mag * jnp.sin(lam_im * dt)
    den = lam_re * lam_re + lam_im * lam_im
    q_re = ((a_re - 1) * lam_re + a_im * lam_im) / den
    q_im = (a_im * lam_re - (a_re - 1) * lam_im) / den
    bb_re = q_re[..., None] * b_re - q_im[..., None] * b_im
    bb_im = q_re[..., None] * b_im + q_im[..., None] * b_re
    return a_re, a_im, bb_re, bb_im


def _s5_matrices(lam_re, lam_im, log_dt, b_re, b_im, c_re, c_im, d_skip):
    t = CHUNK_T
    a_re, a_im, bb_re, bb_im = _s5_discretise(lam_re, lam_im, log_dt, b_re, b_im)
    g, p = a_re.shape[1], a_re.shape[2]
    ch = bb_re.shape[-1]
    pw_re, pw_im = [jnp.ones_like(a_re)], [jnp.zeros_like(a_im)]
    for _ in range(t):
        r, i = pw_re[-1], pw_im[-1]
        pw_re.append(r * a_re - i * a_im)
        pw_im.append(r * a_im + i * a_re)
    pw_re, pw_im = jnp.stack(pw_re), jnp.stack(pw_im)
    cr = c_re[None] * pw_re[:, :, :, None, :] - c_im[None] * pw_im[:, :, :, None, :]
    ci = c_re[None] * pw_im[:, :, :, None, :] + c_im[None] * pw_re[:, :, :, None, :]
    kern = (jnp.einsum("dzgcp,zgpe->dzgce", cr, bb_re, precision=HIGHEST)
            - jnp.einsum("dzgcp,zgpe->dzgce", ci, bb_im, precision=HIGHEST))
    s_idx = jnp.arange(t)[:, None]
    t_idx = jnp.arange(t)[None, :]
    dlt = t_idx - s_idx
    kf = kern[jnp.clip(dlt, 0, t - 1), 0]
    kb = kern[jnp.clip(-dlt, 0, t - 1), 1]
    msk_f = (dlt >= 0)[:, :, None, None, None]
    msk_b = (dlt <= 0)[:, :, None, None, None]
    tsum = jnp.where(msk_f, kf, 0.0) + jnp.where(msk_b, kb, 0.0)
    toep = tsum.transpose(2, 0, 4, 1, 3).reshape(g, t * ch, t * ch)
    skip = jnp.tile(d_skip.reshape(g, 1, ch), (1, t, 1)).reshape(g, 1, t * ch)
    toep = toep + jnp.eye(t * ch, dtype=F32)[None] * skip

    pf_re, pf_im = pw_re[t - 1 - jnp.arange(t), 0], pw_im[t - 1 - jnp.arange(t), 0]
    pb_re, pb_im = pw_re[jnp.arange(t), 1], pw_im[jnp.arange(t), 1]

    def in_to_state(q_re, q_im, z):
        bre = bb_re[z].transpose(0, 2, 1)[None]
        bim = bb_im[z].transpose(0, 2, 1)[None]
        s_re = q_re[:, :, None, :] * bre - q_im[:, :, None, :] * bim
        s_im = q_re[:, :, None, :] * bim + q_im[:, :, None, :] * bre
        to_rows = lambda m: m.transpose(1, 0, 2, 3).reshape(g, t * ch, p)
        return to_rows(s_re), to_rows(s_im)

    sf_re, sf_im = in_to_state(pf_re, pf_im, 0)
    sb_re, sb_im = in_to_state(pb_re, pb_im, 1)
    s_all = jnp.stack([sf_re, sf_im, sb_re, sb_im], axis=1)
    s_all = s_all.reshape(g // 2, 2, 4, t * ch, p)
    eye2 = jnp.eye(2, dtype=F32)
    smat = jnp.einsum("ajqrp,jk->ajrqkp", s_all, eye2).reshape(g // 2, 2 * t * ch, 4 * 2 * p)

    def state_to_out(idx, z):
        cre = cr[idx, z].transpose(1, 3, 0, 2).reshape(g, p, t * ch)
        cim = ci[idx, z].transpose(1, 3, 0, 2).reshape(g, p, t * ch)
        return cre, -cim

    cf_re, cf_im = state_to_out(jnp.arange(t) + 1, 0)
    cb_re, cb_im = state_to_out(t - jnp.arange(t), 1)
    c_all = jnp.stack([cf_re, cf_im, cb_re, cb_im], axis=1)
    c_all = c_all.reshape(g // 2, 2, 4, p, t * ch)
    cpow = jnp.einsum("ajqpn,jk->aqjpkn", c_all, eye2).reshape(g // 2, 4 * 2 * p, 2 * t * ch)

    a_t = jnp.stack([pw_re[t, 0], pw_im[t, 0], pw_re[t, 1], pw_im[t, 1]]).reshape(4, g * p)
    return smat.astype(BF16), toep.astype(BF16), cpow.astype(BF16), a_t


def _s5_states_kernel(u_ref, m_ref, fre_ref, fim_ref, bre_ref, bim_ref):
    lhs = jnp.concatenate([u_ref[0], u_ref[1]], axis=1)
    res = _bdot(lhs, m_ref[0])
    fre_ref[...] = res[:, 0 * LANES:1 * LANES]
    fim_ref[...] = res[:, 1 * LANES:2 * LANES]
    bre_ref[...] = res[:, 2 * LANES:3 * LANES]
    bim_ref[...] = res[:, 3 * LANES:4 * LANES]


def _s5_states(ut, smat):
    g, rows, kdim = ut.shape
    tr = min(S5_TR, rows)
    assert rows % tr == 0
    n_state = g * SSM_STATE
    out = jax.ShapeDtypeStruct((rows, n_state), F32)
    ospec = pl.BlockSpec((tr, LANES), lambda a, r: (r, a))
    return pl.pallas_call(
        _s5_states_kernel,
        out_shape=(out, out, out, out),
        grid=(g // 2, rows // tr),
        in_specs=[pl.BlockSpec((2, tr, kdim), lambda a, r: (a, r, 0)),
                  pl.BlockSpec((1, 2 * kdim, 4 * LANES), lambda a, r: (a, 0, 0))],
        out_specs=(ospec, ospec, ospec, ospec),
        compiler_params=_cparams(("arbitrary", "arbitrary")),
        name="s5_states",
    )(ut, smat)


def _s5_scan_kernel(a_ref, cfr_ref, cfi_ref, cbr_ref, cbi_ref,
                    xfr_ref, xfi_ref, xbr_ref, xbi_ref,
                    hfr_ref, hfi_ref, hbr_ref, hbi_ref, *, halves, kc, kx):
    a_fr, a_fi = a_ref[0:1, :], a_ref[1:2, :]
    a_br, a_bi = a_ref[2:3, :], a_ref[3:4, :]

    def rows(half, k, n_chunks, count=1):
        return pl.ds(pl.multiple_of((half * n_chunks + k) * 8, 8 * count), 8 * count)

    def step(h_re, h_im, a_re, a_im, s_re, s_im):
        return (a_re * h_re - a_im * h_im + s_re, a_re * h_im + a_im * h_re + s_im)

    zero = jnp.zeros((8, LANES), F32)

    def ctx_body(i, carry):
        out = []
        for half in range(halves):
            fr, fi, br, bi = carry[half]
            kf, kb = i, kc - 1 - i
            fr, fi = step(fr, fi, a_fr, a_fi, cfr_ref[rows(half, kf, kc), :],
                          cfi_ref[rows(half, kf, kc), :])
            br, bi = step(br, bi, a_br, a_bi, cbr_ref[rows(half, kb, kc), :],
                          cbi_ref[rows(half, kb, kc), :])
            out.append((fr, fi, br, bi))
        return tuple(out)

    carry = lax.fori_loop(0, kc, ctx_body, ((zero,) * 4,) * halves)

    def x_body(i, carry):
        out = []
        for half in range(halves):
            fr, fi, br, bi = carry[half]
            kf = 2 * i
            kb = kx - 2 - 2 * i
            fr1, fi1 = step(fr, fi, a_fr, a_fi, xfr_ref[rows(half, kf, kx), :],
                            xfi_ref[rows(half, kf, kx), :])
            br1, bi1 = step(br, bi, a_br, a_bi, xbr_ref[rows(half, kb + 1, kx), :],
                            xbi_ref[rows(half, kb + 1, kx), :])
            hfr_ref[rows(half, kf, kx, 2), :] = jnp.concatenate([fr, fr1], 0).astype(hfr_ref.dtype)
            hfi_ref[rows(half, kf, kx, 2), :] = jnp.concatenate([fi, fi1], 0).astype(hfi_ref.dtype)
            hbr_ref[rows(half, kb, kx, 2), :] = jnp.concatenate([br1, br], 0).astype(hbr_ref.dtype)
            hbi_ref[rows(half, kb, kx, 2), :] = jnp.concatenate([bi1, bi], 0).astype(hbi_ref.dtype)
            fr, fi = step(fr1, fi1, a_fr, a_fi, xfr_ref[rows(half, kf + 1, kx), :],
                          xfi_ref[rows(half, kf + 1, kx), :])
            br, bi = step(br1, bi1, a_br, a_bi, xbr_ref[rows(half, kb, kx), :],
                          xbi_ref[rows(half, kb, kx), :])
            out.append((fr, fi, br, bi))
        return tuple(out)

    lax.fori_loop(0, kx // 2, x_body, carry)


def _s5_scan(a_t, s_ctx, s_x, batch):
    rows_c, n_state = s_ctx[0].shape
    rows_x = s_x[0].shape[0]
    kc, kx = rows_c // batch, rows_x // batch
    assert kx % 2 == 0
    out = jax.ShapeDtypeStruct((rows_x, n_state), BF16)
    cspec = pl.BlockSpec((rows_c, LANES), lambda j: (0, j))
    xspec = pl.BlockSpec((rows_x, LANES), lambda j: (0, j))
    return pl.pallas_call(
        functools.partial(_s5_scan_kernel, halves=batch // 8, kc=kc, kx=kx),
        out_shape=(out, out, out, out),
        grid=(n_state // LANES,),
        in_specs=[pl.BlockSpec((4, LANES), lambda j: (0, j))] + [cspec] * 4 + [xspec] * 4,
        out_specs=(xspec, xspec, xspec, xspec),
        compiler_params=_cparams(("arbitrary",)),
        name="s5_scan",
    )(a_t, *s_ctx, *s_x)


def _s5_apply_kernel(u_ref, t_ref, c_ref, hfr_ref, hfi_ref, hbr_ref, hbi_ref, y_ref):
    hcat = jnp.concatenate([hfr_ref[...], hfi_ref[...], hbr_ref[...], hbi_ref[...]], axis=1)
    yst = _bdot(hcat, c_ref[0])
    n = u_ref.shape[2]
    y_ref[0] = (_bdot(u_ref[0], t_ref[0]) + yst[:, :n]).astype(y_ref.dtype)
    y_ref[1] = (_bdot(u_ref[1], t_ref[1]) + yst[:, n:]).astype(y_ref.dtype)


def _s5_apply(ut, toep, cpow, h_in):
    g, rows, kdim = ut.shape
    tr = min(S5_TR, rows)
    hspec = pl.BlockSpec((tr, LANES), lambda a, r: (r, a))
    return pl.pallas_call(
        _s5_apply_kernel,
        out_shape=jax.ShapeDtypeStruct((g, rows, kdim), BF16),
        grid=(g // 2, rows // tr),
        in_specs=[pl.BlockSpec((2, tr, kdim), lambda a, r: (a, r, 0)),
                  pl.BlockSpec((2, kdim, kdim), lambda a, r: (a, 0, 0)),
                  pl.BlockSpec((1, 4 * LANES, 2 * kdim), lambda a, r: (a, 0, 0))] + [hspec] * 4,
        out_specs=pl.BlockSpec((2, tr, kdim), lambda a, r: (a, r, 0)),
        compiler_params=_cparams(("arbitrary", "arbitrary")),
        name="s5_apply",
    )(ut, toep, cpow, *h_in)


def _chunk_rows_to_tokens(y_ref, perm_ref, ys):
    nrow = CHUNKS_PER_TILE * 8
    blocks = []
    for sl in range(N_SLAB):
        for th in range(CHUNK_T // GRANULES):
            blocks.append(jnp.concatenate(
                [y_ref[sl * GRANULES + m, :, th * LANES:(th + 1) * LANES] for m in range(GRANULES)],
                axis=1))
    acat = _bdot(jnp.concatenate(blocks, axis=0), perm_ref[...])
    blk = 0
    for sl in range(N_SLAB):
        for th in range(CHUNK_T // GRANULES):
            for kk in range(CHUNKS_PER_TILE):
                for j in range(GRANULES):
                    t = kk * CHUNK_T + th * GRANULES + j
                    ys[sl, pl.ds(t, 8, stride=PITCH), :] = acat[blk * nrow + kk * 8:
                                                               blk * nrow + (kk + 1) * 8,
                                                               j * LANES:(j + 1) * LANES]
            blk += 1
    return jnp.concatenate(
        [jnp.concatenate([ys[sl, pl.ds(b * PITCH, ROW_T), :] for sl in range(N_SLAB)], axis=1)
         for b in range(8)], axis=0)


def _back_kernel(x_ref, y_ref, mod_ref, perm_ref, gmix_ref, win_ref, convw_ref, convb_ref,
                 wglu_ref, wsbr_ref, wcbr_ref, wo_ref, gffn_ref, wrhi_ref, wrlo_ref, br_ref,
                 ltri_ref, x1_ref, h_ref, ids_ref, gates_ref, hist_ref, ys_scr):
    d_model = x_ref.shape[2]
    tm = 8 * ROW_T
    d_ssm = N_SLAB * LANES
    xt = x_ref[...].reshape(tm, d_model)
    xn = _rmsnorm(xt, gmix_ref[...])
    hx = _modulate_tile(xn, mod_ref, 0, 1).astype(BF16)
    p = _bdot(hx, win_ref[...])
    v = p[:, 0:d_ssm]
    gate_b = p[:, d_ssm:2 * d_ssm]
    gate_c = p[:, 2 * d_ssm:3 * d_ssm]
    g_s = p[:, 3 * d_ssm:3 * d_ssm + d_model]
    g_c = p[:, 3 * d_ssm + d_model:]

    z = gate_c * v
    col = lax.broadcasted_iota(jnp.int32, z.shape, 0) % GRID_W
    z_prev = jnp.where(col == 0, 0.0, pltpu.roll(z, 1, 0))
    z_next = jnp.where(col == GRID_W - 1, 0.0, pltpu.roll(z, tm - 1, 0))
    conv = (z_prev * convw_ref[0:1, :] + z * convw_ref[1:2, :]
            + z_next * convw_ref[2:3, :] + convb_ref[...])
    y_conv = gate_b * conv

    ys = jax.nn.gelu(_chunk_rows_to_tokens(y_ref, perm_ref, ys_scr))
    ys = ys * jax.nn.sigmoid(_bdot(ys.astype(BF16), wglu_ref[...]))

    y_s = _bdot(ys.astype(BF16), wsbr_ref[...])
    y_c = _bdot(y_conv.astype(BF16), wcbr_ref[...])
    merged = jax.nn.sigmoid(g_s) * y_s + jax.nn.sigmoid(g_c) * y_c
    mo = _bdot(merged.astype(BF16), wo_ref[...])
    x1 = xt + (mo.reshape(8, ROW_T, d_model) * mod_ref[:, 2:3, :]).reshape(tm, d_model)
    x1_ref[...] = x1.reshape(8, ROW_T, d_model)

    hn = _modulate_tile(_rmsnorm(x1, gffn_ref[...]), mod_ref, 3, 4)
    for b in range(8):
        for s in range(SUBLANES):
            h_ref[b, pl.ds(s, ROW_T, stride=SUBLANES), :] = hn[b * ROW_T:(b + 1) * ROW_T,
                                                              s * LANES:(s + 1) * LANES]

    hn_hi = hn.astype(BF16)
    hn_lo = (hn - hn_hi.astype(F32)).astype(BF16)
    logits = (_bdot(hn_hi, wrhi_ref[...]) + _bdot(hn_lo, wrhi_ref[...])
              + _bdot(hn_hi, wrlo_ref[...]) + br_ref[...])
    lane = lax.broadcasted_iota(jnp.int32, logits.shape, 1)
    neg = jnp.float32(-jnp.inf)
    cur = jnp.where(lane < N_EXPERTS, logits, neg)
    vals, idxs = [], []
    for _ in range(TOP_K):
        mk = jnp.max(cur, axis=-1, keepdims=True)
        ik = jnp.min(jnp.where(cur == mk, lane, LANES), axis=-1, keepdims=True)
        vals.append(mk)
        idxs.append(ik)
        cur = jnp.where(lane == ik, neg, cur)
    exps = [jnp.exp(vk - vals[0]) for vk in vals]
    denom = exps[0] + exps[1] + exps[2] + exps[3]
    ids = jnp.zeros(logits.shape, jnp.int32)
    gates = jnp.zeros(logits.shape, F32)
    onehot = jnp.zeros(logits.shape, F32)
    for k in range(TOP_K):
        onehot = onehot + (lane == idxs[k]).astype(F32)
    before = _bdot(ltri_ref[...], onehot.astype(BF16))
    for k in range(TOP_K):
        rank_k = jnp.sum(jnp.where(lane == idxs[k], before, 0.0), axis=-1, keepdims=True)
        ids = jnp.where(lane == k, idxs[k], ids)
        ids = jnp.where(lane == TOP_K + k, rank_k.astype(jnp.int32), ids)
        gates = jnp.where(lane == k, exps[k] / denom, gates)
    ids_ref[...] = ids.reshape(8, ROW_T, LANES)
    gates_ref[...] = gates.reshape(8, ROW_T, LANES)
    hist_ref[0] = jnp.broadcast_to(jnp.sum(onehot, axis=0, keepdims=True), (8, LANES))


def _back(x, y_chunks, mod, perm, g_mix, w_rest, conv_w, conv_b, w_glu, w_ssm_br,
          w_conv_br, w_o, g_ffn, w_router, b_router):
    b, s, d = x.shape
    g, _, kdim = y_chunks.shape
    tiles = s // ROW_T
    nrow = CHUNKS_PER_TILE * 8
    tok = lambda n: pl.BlockSpec((8, ROW_T, n), lambda i, j: (i, j, 0))

    def const(arr):
        nd = arr.ndim
        return pl.BlockSpec(arr.shape, lambda i, j: (0,) * nd, pipeline_mode=pl.Buffered(1))

    w_r_hi = w_router.astype(BF16)
    w_r_lo = (w_router - w_r_hi.astype(F32)).astype(BF16)
    row = jnp.arange(8 * ROW_T)
    ltri = (row[:, None] > row[None, :]).astype(BF16)
    params = (perm, g_mix, w_rest, conv_w, conv_b, w_glu, w_ssm_br, w_conv_br, w_o, g_ffn,
              w_r_hi, w_r_lo, b_router, ltri)
    return pl.pallas_call(
        _back_kernel,
        out_shape=(jax.ShapeDtypeStruct((b, s, d), F32),
                   jax.ShapeDtypeStruct((b, s * SUBLANES, LANES), F32),
                   jax.ShapeDtypeStruct((b, s, LANES), jnp.int32),
                   jax.ShapeDtypeStruct((b, s, LANES), F32),
                   jax.ShapeDtypeStruct(((b // 8) * tiles, 8, LANES), F32)),
        grid=(b // 8, tiles),
        in_specs=[tok(d),
                  pl.BlockSpec((g, nrow, kdim), lambda i, j: (0, i * tiles + j, 0)),
                  pl.BlockSpec((8, 8, d), lambda i, j: (i, 0, 0))] + [const(a) for a in params],
        out_specs=(tok(d), pl.BlockSpec((8, ROW_T * SUBLANES, LANES), lambda i, j: (i, j, 0)),
                   tok(LANES), tok(LANES),
                   pl.BlockSpec((1, 8, LANES), lambda i, j: (i * tiles + j, 0, 0))),
        scratch_shapes=[pltpu.VMEM((N_SLAB, 8 * PITCH, LANES), F32)],
        compiler_params=_cparams(("arbitrary", "arbitrary")),
        name="back",
    )(x, y_chunks, mod, *params)


def _routing(ids, ranks, tile_hist, tile_of_token):
    tb = MOE_TB
    n_tok = ids.shape[0]
    n_slot = n_tok * TOP_K
    tile_before = jnp.cumsum(tile_hist, axis=0) - tile_hist
    counts = jnp.sum(tile_hist, axis=0)
    padded = (counts + tb - 1) // tb * tb
    pad_end = jnp.cumsum(padded)
    pad_start = pad_end - padded
    start = jnp.cumsum(counts) - counts
    pos = (pad_start[ids] + tile_before.reshape(-1)[tile_of_token[:, None] * N_EXPERTS + ids]
           + ranks).astype(jnp.int32)
    token = jnp.broadcast_to(jnp.arange(n_tok, dtype=jnp.int32)[:, None], (n_tok, TOP_K))
    _, order_token = lax.sort_key_val(pos.reshape(-1), token.reshape(-1))
    n_blocks = n_slot // tb + N_EXPERTS
    cap = n_blocks * tb
    block_first_row = jnp.arange(n_blocks, dtype=jnp.int32) * tb
    block_expert = jnp.minimum(
        jnp.sum(pad_end[None, :] <= block_first_row[:, None], axis=1, dtype=jnp.int32),
        N_EXPERTS - 1)
    row = jnp.arange(cap, dtype=jnp.int32)
    row_expert = jnp.repeat(block_expert, tb)
    within = row - pad_start[row_expert]
    valid = within < counts[row_expert]
    src = jnp.clip(start[row_expert] + within, 0, n_slot - 1)
    buf_token = jnp.where(valid, order_token[src], 0).astype(jnp.int32)
    n_used = (pad_end[-1] // tb).astype(jnp.int32).reshape(1)
    return buf_token, pos, block_expert, n_used


def _to_tiles(ref, val):
    rows = val.shape[0]
    for s in range(SUBLANES):
        ref[pl.ds(s, rows, stride=SUBLANES), :] = val[:, s * LANES:(s + 1) * LANES]


def _tile_piece(ref, first_row, rows, s):
    return ref[pl.ds(first_row * SUBLANES + s, rows, stride=SUBLANES), :]


def _tile_gather_start(idx_ref, src_hbm, dst, sem, n_rows, unrolled, alternate_priority=False):
    def copy(r, t):
        return pltpu.make_async_copy(
            src_hbm.at[pl.ds(pl.multiple_of(t * SUBLANES, SUBLANES), SUBLANES), :],
            dst.at[pl.ds(pl.multiple_of(r * SUBLANES, SUBLANES), SUBLANES), :], sem)

    if unrolled:
        for r in range(n_rows):
            copy(r, idx_ref[0, 0, r]).start(priority=(r % 2) if alternate_priority else 0)
    else:
        def body(r, carry):
            copy(r, idx_ref[0, 0, r]).start()
            return carry
        lax.fori_loop(0, n_rows, body, 0, unroll=8)


def _tile_gather_wait(src_hbm, dst, sem):
    pltpu.make_async_copy(src_hbm.at[pl.ds(0, dst.shape[0]), :], dst, sem).wait()


def _ring_step(i, last, bufs, sem, wait, prefetch, compute):
    n = len(bufs)
    for p in range(n):
        def branch(p=p):
            wait(bufs[p], sem.at[p])
            q = (p + n - 1) % n
            prefetch(bufs[q], sem.at[q])
            compute(bufs[p])

            @pl.when(i == last)
            def _():
                for r in range(1, n):
                    wait(bufs[(p + r) % n], sem.at[(p + r) % n])

        pl.when((i <= last) & (lax.rem(i, n) == p))(branch)


def _moe_kernel(be_ref, nu_ref, tok0_ref, tok1_ref, tok2_ref, h_hbm, wgu_ref, bgu_ref, wd_ref,
                bd_ref, out_ref, xbuf0, xbuf1, xbuf2, sem):
    del be_ref
    i = pl.program_id(0)
    n_used = nu_ref[0]
    tb = xbuf0.shape[0] // SUBLANES
    f = wd_ref.shape[1]

    @pl.when(i == 0)
    def _():
        _tile_gather_start(tok0_ref, h_hbm, xbuf0, sem.at[0], tb, unrolled=False)
        _tile_gather_start(tok1_ref, h_hbm, xbuf1, sem.at[1], tb, unrolled=False)

    def compute(xcur):
        xe = jnp.concatenate([_tile_piece(xcur, 0, tb, s) for s in range(SUBLANES)],
                             axis=1).astype(BF16)
        gu = _bdot(xe, wgu_ref[0]) + bgu_ref[0]
        gt = jnp.minimum(gu[:, :f], SWIGLU_LIMIT)
        up = jnp.clip(gu[:, f:], -SWIGLU_LIMIT, SWIGLU_LIMIT)
        act = gt * jax.nn.sigmoid(SWIGLU_ALPHA * gt) * (up + 1.0)
        _to_tiles(out_ref, _bdot(act.astype(BF16), wd_ref[0]) + bd_ref[0])

    _ring_step(i, n_used - 1, (xbuf0, xbuf1, xbuf2), sem,
               wait=lambda buf, s: _tile_gather_wait(h_hbm, buf, s),
               prefetch=lambda buf, s: _tile_gather_start(tok2_ref, h_hbm, buf, s, tb,
                                                          unrolled=True),
               compute=compute)

    @pl.when(i >= n_used)
    def _():
        out_ref[...] = jnp.zeros(out_ref.shape, out_ref.dtype)


def _moe(h_tiles, buf_token, block_expert, n_used, w_gu, b_gu, w_down, b_down):
    e, d, f2 = w_gu.shape
    f = f2 // 2
    tb = MOE_TB
    cap = buf_token.shape[0]
    n_blocks = cap // tb
    tok3 = buf_token.reshape(n_blocks, 1, tb)
    smem_blk = lambda imap: pl.BlockSpec((1, 1, tb), imap, memory_space=pltpu.SMEM)
    grid_spec = pltpu.PrefetchScalarGridSpec(
        num_scalar_prefetch=2,
        grid=(n_blocks,),
        in_specs=[smem_blk(lambda i, be, nu: (i, 0, 0)),
                  smem_blk(lambda i, be, nu: (jnp.minimum(i + 1, n_blocks - 1), 0, 0)),
                  smem_blk(lambda i, be, nu: (jnp.minimum(i + 2, n_blocks - 1), 0, 0)),
                  pl.BlockSpec(memory_space=pl.ANY),
                  pl.BlockSpec((1, d, f2), lambda i, be, nu: (be[i], 0, 0)),
                  pl.BlockSpec((1, 1, f2), lambda i, be, nu: (be[i], 0, 0)),
                  pl.BlockSpec((1, f, d), lambda i, be, nu: (be[i], 0, 0)),
                  pl.BlockSpec((1, 1, d), lambda i, be, nu: (be[i], 0, 0))],
        out_specs=pl.BlockSpec((tb * SUBLANES, LANES), lambda i, be, nu: (i, 0)),
        scratch_shapes=[pltpu.VMEM((tb * SUBLANES, LANES), F32)] * 3
        + [pltpu.SemaphoreType.DMA((3,))],
    )
    return pl.pallas_call(
        _moe_kernel,
        out_shape=jax.ShapeDtypeStruct((cap * SUBLANES, LANES), F32),
        grid_spec=grid_spec,
        compiler_params=_cparams(("arbitrary",)),
        name="moe",
    )(block_expert, n_used, tok3, tok3, tok3, h_tiles, w_gu, b_gu.reshape(e, 1, f2),
      w_down, b_down.reshape(e, 1, d))


def _combine_kernel(pos0_ref, pos1_ref, pos2_ref, yb_hbm, x1_ref, gates_ref, mod_ref, gfin_ref,
                    out_ref, buf0, buf1, buf2, sem):
    i = pl.program_id(0)
    n = pl.num_programs(0)
    n_rows = buf0.shape[0] // SUBLANES
    tm = n_rows // TOP_K

    @pl.when(i == 0)
    def _():
        _tile_gather_start(pos0_ref, yb_hbm, buf0, sem.at[0], n_rows, unrolled=False)
        _tile_gather_start(pos1_ref, yb_hbm, buf1, sem.at[1], n_rows, unrolled=False)

    def compute(cur):
        g = gates_ref[...]
        gk = [jnp.broadcast_to(g[:, k:k + 1], (tm, LANES)) for k in range(TOP_K)]
        pieces = []
        for s in range(SUBLANES):
            acc = gk[0] * _tile_piece(cur, 0, tm, s)
            for k in range(1, TOP_K):
                acc = acc + gk[k] * _tile_piece(cur, k * tm, tm, s)
            pieces.append(acc)
        x2 = x1_ref[...] + mod_ref[0, 5:6, :] * jnp.concatenate(pieces, axis=1)
        out_ref[...] = _rmsnorm(x2, gfin_ref[...])

    _ring_step(i, n - 1, (buf0, buf1, buf2), sem,
               wait=lambda buf, s: _tile_gather_wait(yb_hbm, buf, s),
               prefetch=lambda buf, s: _tile_gather_start(pos2_ref, yb_hbm, buf, s, n_rows,
                                                          unrolled=True, alternate_priority=True),
               compute=compute)


def _combine(yb, pos, x1_2d, gates_2d, mod, g_final, seq):
    n_tok, d = x1_2d.shape
    tm = min(COMBINE_TM, seq)
    assert seq % tm == 0
    n_tiles = n_tok // tm
    per_batch = seq // tm
    pos3 = pos.reshape(n_tiles, tm, TOP_K).transpose(0, 2, 1).reshape(n_tiles, 1, TOP_K * tm)
    smem_blk = lambda imap: pl.BlockSpec((1, 1, TOP_K * tm), imap, memory_space=pltpu.SMEM)
    buf = pltpu.VMEM((TOP_K * tm * SUBLANES, LANES), F32)
    return pl.pallas_call(
        _combine_kernel,
        out_shape=jax.ShapeDtypeStruct((n_tok, d), F32),
        grid=(n_tiles,),
        in_specs=[smem_blk(lambda i: (i, 0, 0)),
                  smem_blk(lambda i: (jnp.minimum(i + 1, n_tiles - 1), 0, 0)),
                  smem_blk(lambda i: (jnp.minimum(i + 2, n_tiles - 1), 0, 0)),
                  pl.BlockSpec(memory_space=pl.ANY),
                  pl.BlockSpec((tm, d), lambda i: (i, 0)),
                  pl.BlockSpec((tm, LANES), lambda i: (i, 0)),
                  pl.BlockSpec((1, 8, d), lambda i: (i // per_batch, 0, 0)),
                  pl.BlockSpec((1, d), lambda i: (0, 0))],
        out_specs=pl.BlockSpec((tm, d), lambda i: (i, 0)),
        scratch_shapes=[buf, buf, buf, pltpu.SemaphoreType.DMA((3,))],
        compiler_params=_cparams(("arbitrary",)),
        name="combine",
    )(pos3, pos3, pos3, yb, x1_2d, gates_2d, mod, g_final)


def kernel(x, c, ctx, c_ctx, w_mod, b_mod, g_mix, w_in, lam_re, lam_im, log_dt, b_re, b_im,
           c_re, c_im, d_skip, w_glu, conv_w, conv_b, w_ssm_br, w_conv_br, w_o, g_ffn,
           w_router, b_router, w_gu, b_gu, w_down, b_down, g_final):
    depth = w_mod.shape[0]
    assert depth == 1, "single-layer trunk"
    bsz, seq, d = x.shape
    ctx_len = ctx.shape[1]
    d_ssm = d // 2
    assert bsz % 8 == 0 and seq % CHUNK_T == 0 and ctx_len % CHUNK_T == 0 and seq % GRID_W == 0
    assert d == SUBLANES * LANES, "row gathers move one (8,128) f32 tile per token"

    n_cond = -(-(bsz + 1) // 8) * 8
    cond = jnp.zeros((n_cond, d), F32).at[:bsz].set(c).at[bsz].set(c_ctx)
    m = _adaln(cond, w_mod[0], b_mod[0])
    zeros2 = jnp.zeros((n_cond, 2, d), F32)
    mod_all = jnp.concatenate([m.reshape(n_cond, 6, d), zeros2], axis=1)
    mod_x, mod_c = mod_all[:bsz], mod_all[bsz:bsz + 1]

    w_in_bf = w_in[0].astype(BF16)
    w_u, w_rest = w_in_bf[:, :d_ssm], w_in_bf[:, d_ssm:]
    gm = g_mix[0].reshape(1, d)

    perm = _granule_transpose_matrix()
    utx = _front(x, mod_x, gm, w_u, perm)
    utc = _front(ctx, mod_c, gm, w_u, perm)

    smat, toep, cpow, a_t = _s5_matrices(lam_re[0], lam_im[0], log_dt[0], b_re[0], b_im[0],
                                         c_re[0], c_im[0], d_skip[0])
    s_x = _s5_states(utx, smat)
    s_c = _s5_states(utc, smat)
    h_in = _s5_scan(a_t, s_c, s_x, bsz)
    y_chunks = _s5_apply(utx, toep, cpow, h_in)

    pad_r = jnp.zeros((d, LANES - N_EXPERTS), F32)
    w_r = jnp.concatenate([w_router[0], pad_r], axis=1)
    b_r = jnp.concatenate([b_router[0], jnp.zeros((LANES - N_EXPERTS,), F32)]).reshape(1, LANES)
    x1, h, ids, gates, hist = _back(
        x, y_chunks, mod_x, perm, gm, w_rest, conv_w[0], conv_b[0].reshape(1, d_ssm),
        w_glu[0].astype(BF16), w_ssm_br[0].astype(BF16),
        w_conv_br[0].astype(BF16), w_o[0].astype(BF16), g_ffn[0].reshape(1, d), w_r, b_r)

    n_tok = bsz * seq
    ids2 = ids.reshape(n_tok, LANES)
    tiles = seq // ROW_T
    tile_of_token = ((jnp.arange(bsz, dtype=jnp.int32) // 8)[:, None] * tiles
                     + (jnp.arange(seq, dtype=jnp.int32) // ROW_T)[None, :]).reshape(n_tok)
    buf_token, pos, block_expert, n_used = _routing(
        ids2[:, :TOP_K], ids2[:, TOP_K:2 * TOP_K],
        hist[:, 0, :N_EXPERTS].astype(jnp.int32), tile_of_token)
    yb = _moe(h.reshape(n_tok * SUBLANES, LANES), buf_token, block_expert, n_used,
              w_gu[0].astype(BF16), b_gu[0], w_down[0].astype(BF16), b_down[0])
    out = _combine(yb, pos, x1.reshape(n_tok, d), gates.reshape(n_tok, LANES), mod_x,
                   g_final.reshape(1, d), seq)
    return out.reshape(bsz, seq, d)
```

```python
import functools
import math

import jax
import jax.numpy as jnp
from jax import lax
from jax.experimental import pallas as pl
from jax.experimental.pallas import tpu as pltpu

F32 = jnp.float32
BF16 = jnp.bfloat16
HIGHEST = lax.Precision.HIGHEST

RMS_EPS = 1e-6
GRID_W = 64
SSM_GROUP = 16
SSM_STATE = 64
N_EXPERTS = 32
TOP_K = 4
SWIGLU_LIMIT = 7.0
SWIGLU_ALPHA = 1.702

CHUNK_T = 16
LANES = 128
SUBLANES = 8
V7X_VMEM_LIMIT_BYTES = 56 * 1024 * 1024

S5_TR = 1024
MOE_TB = 256
COMBINE_TM = 256


def _cparams(sem):
    return pltpu.CompilerParams(dimension_semantics=sem,
                                vmem_limit_bytes=V7X_VMEM_LIMIT_BYTES)


def _bdot(a, b):
    return jnp.dot(a, b, preferred_element_type=F32)


def _rmsnorm(xt, g):
    ms = jnp.mean(xt * xt, axis=-1, keepdims=True)
    return xt * lax.rsqrt(ms + RMS_EPS) * g


def _adaln_kernel(c_ref, w_ref, b_ref, o_ref):
    s = jax.nn.silu(c_ref[...])
    o_ref[...] = jnp.dot(s, w_ref[...], precision=HIGHEST,
                         preferred_element_type=F32) + b_ref[...]


def _adaln(cond, w_mod, b_mod):
    r, d = cond.shape
    n = w_mod.shape[1]
    tn = n // 4
    return pl.pallas_call(
        _adaln_kernel,
        out_shape=jax.ShapeDtypeStruct((r, n), F32),
        grid=(n // tn,),
        in_specs=[pl.BlockSpec((r, d), lambda j: (0, 0)),
                  pl.BlockSpec((d, tn), lambda j: (0, j)),
                  pl.BlockSpec((1, tn), lambda j: (0, j))],
        out_specs=pl.BlockSpec((r, tn), lambda j: (0, j)),
        compiler_params=_cparams(("arbitrary",)),
        name="adaln",
    )(cond, w_mod, b_mod.reshape(1, n))


ROW_T = GRID_W
PITCH = ROW_T + 8
N_SLAB = 4
CHUNKS_PER_TILE = ROW_T // CHUNK_T
GRANULES = LANES // SSM_GROUP


def _granule_transpose_matrix():
    n = GRANULES * LANES
    idx = jnp.arange(n)
    j, m, c = idx // LANES, (idx % LANES) // SSM_GROUP, idx % SSM_GROUP
    dst = m * LANES + j * SSM_GROUP + c
    return jnp.zeros((n, n), F32).at[idx, dst].set(1.0).astype(BF16)


def _modulate_tile(xn, mod_ref, shift_row, scale_row):
    d = xn.shape[1]
    x3 = xn.reshape(8, ROW_T, d)
    x3 = x3 * (1.0 + mod_ref[:, scale_row:scale_row + 1, :]) + mod_ref[:, shift_row:shift_row + 1, :]
    return x3.reshape(8 * ROW_T, d)


def _front_kernel(x_ref, mod_ref, g_ref, w_ref, perm_ref, u_ref, us):
    d = x_ref.shape[2]
    xn = _rmsnorm(x_ref[...].reshape(8 * ROW_T, d), g_ref[...])
    hx = _modulate_tile(xn, mod_ref, 0, 1)
    u = _bdot(hx.astype(BF16), w_ref[...])
    for sl in range(N_SLAB):
        for b in range(8):
            us[sl, pl.ds(b * PITCH, ROW_T), :] = u[b * ROW_T:(b + 1) * ROW_T,
                                                  sl * LANES:(sl + 1) * LANES]
    blocks = []
    for sl in range(N_SLAB):
        for th in range(CHUNK_T // GRANULES):
            rows = []
            for kk in range(CHUNKS_PER_TILE):
                t0 = kk * CHUNK_T + th * GRANULES
                rows.append(jnp.concatenate(
                    [us[sl, pl.ds(t0 + j, 8, stride=PITCH), :] for j in range(GRANULES)], axis=1))
            blocks.append(jnp.concatenate(rows, axis=0))
    acat = jnp.concatenate(blocks, axis=0).astype(BF16)
    bmat = _bdot(acat, perm_ref[...])
    nrow = CHUNKS_PER_TILE * 8
    blk = 0
    for sl in range(N_SLAB):
        for th in range(CHUNK_T // GRANULES):
            for m in range(GRANULES):
                u_ref[sl * GRANULES + m, :, th * LANES:(th + 1) * LANES] = (
                    bmat[blk * nrow:(blk + 1) * nrow, m * LANES:(m + 1) * LANES].astype(u_ref.dtype))
            blk += 1


def _front(x, mod, g_mix, w_u, perm):
    b, s, d = x.shape
    n = w_u.shape[1]
    g = n // SSM_GROUP
    assert s % ROW_T == 0 and b % 8 == 0 and n == N_SLAB * LANES
    k = s // CHUNK_T
    tiles = s // ROW_T
    nrow = CHUNKS_PER_TILE * 8
    shared = mod.shape[0] == 1
    mod_spec = (pl.BlockSpec((1, 8, d), lambda i, j: (0, 0, 0)) if shared
                else pl.BlockSpec((8, 8, d), lambda i, j: (i, 0, 0)))
    return pl.pallas_call(
        _front_kernel,
        out_shape=jax.ShapeDtypeStruct((g, (b // 8) * k * 8, CHUNK_T * SSM_GROUP), BF16),
        grid=(b // 8, tiles),
        in_specs=[pl.BlockSpec((8, ROW_T, d), lambda i, j: (i, j, 0)),
                  mod_spec,
                  pl.BlockSpec((1, d), lambda i, j: (0, 0)),
                  pl.BlockSpec((d, n), lambda i, j: (0, 0)),
                  pl.BlockSpec(perm.shape, lambda i, j: (0, 0))],
        out_specs=pl.BlockSpec((g, nrow, CHUNK_T * SSM_GROUP), lambda i, j: (0, i * tiles + j, 0)),
        scratch_shapes=[pltpu.VMEM((N_SLAB, 8 * PITCH, LANES), F32)],
        compiler_params=_cparams(("arbitrary", "arbitrary")),
        name="front",
    )(x, mod, g_mix, w_u, perm)


def _s5_discretise(lam_re, lam_im, log_dt, b_re, b_im):
    dt = jnp.exp(log_dt)[..., None]
    mag = jnp.exp(lam_re * dt)
    a_re, a_im = mag * jnp.cos(lam_im * dt), mag * jnp.sin(lam_im * dt)
    den = lam_re * lam_re + lam_im * lam_im
    q_re = ((a_re - 1) * lam_re + a_im * lam_im) / den
    q_im = (a_im * lam_re - (a_re - 1) * lam_im) / den
    bb_re = q_re[..., None] * b_re - q_im[..., None] * b_im
    bb_im = q_re[..., None] * b_im + q_im[..., None] * b_re
    return a_re, a_im, bb_re, bb_im


def _s5_matrices(lam_re, lam_im, log_dt, b_re, b_im, c_re, c_im, d_skip):
    t = CHUNK_T
    a_re, a_im, bb_re, bb_im = _s5_discretise(lam_re, lam_im, log_dt, b_re, b_im)
    g, p = a_re.shape[1], a_re.shape[2]
    ch = bb_re.shape[-1]
    pw_re, pw_im = [jnp.ones_like(a_re)], [jnp.zeros_like(a_im)]
    for _ in range(t):
        r, i = pw_re[-1], pw_im[-1]
        pw_re.append(r * a_re - i * a_im)
        pw_im.append(r * a_im + i * a_re)
    pw_re, pw_im = jnp.stack(pw_re), jnp.stack(pw_im)
    cr = c_re[None] * pw_re[:, :, :, None, :] - c_im[None] * pw_im[:, :, :, None, :]
    ci = c_re[None] * pw_im[:, :, :, None, :] + c_im[None] * pw_re[:, :, :, None, :]
    kern = (jnp.einsum("dzgcp,zgpe->dzgce", cr, bb_re, precision=HIGHEST)
            - jnp.einsum("dzgcp,zgpe->dzgce", ci, bb_im, precision=HIGHEST))
    s_idx = jnp.arange(t)[:, None]
    t_idx = jnp.arange(t)[None, :]
    dlt = t_idx - s_idx
    kf = kern[jnp.clip(dlt, 0, t - 1), 0]
    kb = kern[jnp.clip(-dlt, 0, t - 1), 1]
    msk_f = (dlt >= 0)[:, :, None, None, None]
    msk_b = (dlt <= 0)[:, :, None, None, None]
    tsum = jnp.where(msk_f, kf, 0.0) + jnp.where(msk_b, kb, 0.0)
    toep = tsum.transpose(2, 0, 4, 1, 3).reshape(g, t * ch, t * ch)
    skip = jnp.tile(d_skip.reshape(g, 1, ch), (1, t, 1)).reshape(g, 1, t * ch)
    toep = toep + jnp.eye(t * ch, dtype=F32)[None] * skip

    pf_re, pf_im = pw_re[t - 1 - jnp.arange(t), 0], pw_im[t - 1 - jnp.arange(t), 0]
    pb_re, pb_im = pw_re[jnp.arange(t), 1], pw_im[jnp.arange(t), 1]

    def in_to_state(q_re, q_im, z):
        bre = bb_re[z].transpose(0, 2, 1)[None]
        bim = bb_im[z].transpose(0, 2, 1)[None]
        s_re = q_re[:, :, None, :] * bre - q_im[:, :, None, :] * bim
        s_im = q_re[:, :, None, :] * bim + q_im[:, :, None, :] * bre
        to_rows = lambda m: m.transpose(1, 0, 2, 3).reshape(g, t * ch, p)
        return to_rows(s_re), to_rows(s_im)

    sf_re, sf_im = in_to_state(pf_re, pf_im, 0)
    sb_re, sb_im = in_to_state(pb_re, pb_im, 1)
    s_all = jnp.stack([sf_re, sf_im, sb_re, sb_im], axis=1)
    s_all = s_all.reshape(g // 2, 2, 4, t * ch, p)
    eye2 = jnp.eye(2, dtype=F32)
    smat = jnp.einsum("ajqrp,jk->ajrqkp", s_all, eye2).reshape(g // 2, 2 * t * ch, 4 * 2 * p)

    def state_to_out(idx, z):
        cre = cr[idx, z].transpose(1, 3, 0, 2).reshape(g, p, t * ch)
        cim = ci[idx, z].transpose(1, 3, 0, 2).reshape(g, p, t * ch)
        return cre, -cim

    cf_re, cf_im = state_to_out(jnp.arange(t) + 1, 0)
    cb_re, cb_im = state_to_out(t - jnp.arange(t), 1)
    c_all = jnp.stack([cf_re, cf_im, cb_re, cb_im], axis=1)
    c_all = c_all.reshape(g // 2, 2, 4, p, t * ch)
    cpow = jnp.einsum("ajqpn,jk->aqjpkn", c_all, eye2).reshape(g // 2, 4 * 2 * p, 2 * t * ch)

    a_t = jnp.stack([pw_re[t, 0], pw_im[t, 0], pw_re[t, 1], pw_im[t, 1]]).reshape(4, g * p)
    return smat.astype(BF16), toep.astype(BF16), cpow.astype(BF16), a_t


def _s5_states_kernel(u_ref, m_ref, fre_ref, fim_ref, bre_ref, bim_ref):
    lhs = jnp.concatenate([u_ref[0], u_ref[1]], axis=1)
    res = _bdot(lhs, m_ref[0])
    fre_ref[...] = res[:, 0 * LANES:1 * LANES]
    fim_ref[...] = res[:, 1 * LANES:2 * LANES]
    bre_ref[...] = res[:, 2 * LANES:3 * LANES]
    bim_ref[...] = res[:, 3 * LANES:4 * LANES]


def _s5_states(ut, smat):
    g, rows, kdim = ut.shape
    tr = min(S5_TR, rows)
    assert rows % tr == 0
    n_state = g * SSM_STATE
    out = jax.ShapeDtypeStruct((rows, n_state), F32)
    ospec = pl.BlockSpec((tr, LANES), lambda a, r: (r, a))
    return pl.pallas_call(
        _s5_states_kernel,
        out_shape=(out, out, out, out),
        grid=(g // 2, rows // tr),
        in_specs=[pl.BlockSpec((2, tr, kdim), lambda a, r: (a, r, 0)),
                  pl.BlockSpec((1, 2 * kdim, 4 * LANES), lambda a, r: (a, 0, 0))],
        out_specs=(ospec, ospec, ospec, ospec),
        compiler_params=_cparams(("arbitrary", "arbitrary")),
        name="s5_states",
    )(ut, smat)


def _s5_scan_kernel(a_ref, cfr_ref, cfi_ref, cbr_ref, cbi_ref,
                    xfr_ref, xfi_ref, xbr_ref, xbi_ref,
                    hfr_ref, hfi_ref, hbr_ref, hbi_ref, *, halves, kc, kx):
    a_fr, a_fi = a_ref[0:1, :], a_ref[1:2, :]
    a_br, a_bi = a_ref[2:3, :], a_ref[3:4, :]

    def rows(half, k, n_chunks, count=1):
        return pl.ds(pl.multiple_of((half * n_chunks + k) * 8, 8 * count), 8 * count)

    def step(h_re, h_im, a_re, a_im, s_re, s_im):
        return (a_re * h_re - a_im * h_im + s_re, a_re * h_im + a_im * h_re + s_im)

    zero = jnp.zeros((8, LANES), F32)

    def ctx_body(i, carry):
        out = []
        for half in range(halves):
            fr, fi, br, bi = carry[half]
            kf, kb = i, kc - 1 - i
            fr, fi = step(fr, fi, a_fr, a_fi, cfr_ref[rows(half, kf, kc), :],
                          cfi_ref[rows(half, kf, kc), :])
            br, bi = step(br, bi, a_br, a_bi, cbr_ref[rows(half, kb, kc), :],
                          cbi_ref[rows(half, kb, kc), :])
            out.append((fr, fi, br, bi))
        return tuple(out)

    carry = lax.fori_loop(0, kc, ctx_body, ((zero,) * 4,) * halves)

    def x_body(i, carry):
        out = []
        for half in range(halves):
            fr, fi, br, bi = carry[half]
            kf = 2 * i
            kb = kx - 2 - 2 * i
            fr1, fi1 = step(fr, fi, a_fr, a_fi, xfr_ref[rows(half, kf, kx), :],
                            xfi_ref[rows(half, kf, kx), :])
            br1, bi1 = step(br, bi, a_br, a_bi, xbr_ref[rows(half, kb + 1, kx), :],
                            xbi_ref[rows(half, kb + 1, kx), :])
            hfr_ref[rows(half, kf, kx, 2), :] = jnp.concatenate([fr, fr1], 0).astype(hfr_ref.dtype)
            hfi_ref[rows(half, kf, kx, 2), :] = jnp.concatenate([fi, fi1], 0).astype(hfi_ref.dtype)
            hbr_ref[rows(half, kb, kx, 2), :] = jnp.concatenate([br1, br], 0).astype(hbr_ref.dtype)
            hbi_ref[rows(half, kb, kx, 2), :] = jnp.concatenate([bi1, bi], 0).astype(hbi_ref.dtype)
            fr, fi = step(fr1, fi1, a_fr, a_fi, xfr_ref[rows(half, kf + 1, kx), :],
                          xfi_ref[rows(half, kf + 1, kx), :])
            br, bi = step(br1, bi1, a_br, a_bi, xbr_ref[rows(half, kb, kx), :],
                          xbi_ref[rows(half, kb, kx), :])
            out.append((fr, fi, br, bi))
        return tuple(out)

    lax.fori_loop(0, kx // 2, x_body, carry)


def _s5_scan(a_t, s_ctx, s_x, batch):
    rows_c, n_state = s_ctx[0].shape
    rows_x = s_x[0].shape[0]
    kc, kx = rows_c // batch, rows_x // batch
    assert kx % 2 == 0
    out = jax.ShapeDtypeStruct((rows_x, n_state), BF16)
    cspec = pl.BlockSpec((rows_c, LANES), lambda j: (0, j))
    xspec = pl.BlockSpec((rows_x, LANES), lambda j: (0, j))
    return pl.pallas_call(
        functools.partial(_s5_scan_kernel, halves=batch // 8, kc=kc, kx=kx),
        out_shape=(out, out, out, out),
        grid=(n_state // LANES,),
        in_specs=[pl.BlockSpec((4, LANES), lambda j: (0, j))] + [cspec] * 4 + [xspec] * 4,
        out_specs=(xspec, xspec, xspec, xspec),
        compiler_params=_cparams(("arbitrary",)),
        name="s5_scan",
    )(a_t, *s_ctx, *s_x)


def _s5_apply_kernel(u_ref, t_ref, c_ref, hfr_ref, hfi_ref, hbr_ref, hbi_ref, y_ref):
    hcat = jnp.concatenate([hfr_ref[...], hfi_ref[...], hbr_ref[...], hbi_ref[...]], axis=1)
    yst = _bdot(hcat, c_ref[0])
    n = u_ref.shape[2]
    y_ref[0] = (_bdot(u_ref[0], t_ref[0]) + yst[:, :n]).astype(y_ref.dtype)
    y_ref[1] = (_bdot(u_ref[1], t_ref[1]) + yst[:, n:]).astype(y_ref.dtype)


def _s5_apply(ut, toep, cpow, h_in):
    g, rows, kdim = ut.shape
    tr = min(S5_TR, rows)
    hspec = pl.BlockSpec((tr, LANES), lambda a, r: (r, a))
    return pl.pallas_call(
        _s5_apply_kernel,
        out_shape=jax.ShapeDtypeStruct((g, rows, kdim), BF16),
        grid=(g // 2, rows // tr),
        in_specs=[pl.BlockSpec((2, tr, kdim), lambda a, r: (a, r, 0)),
                  pl.BlockSpec((2, kdim, kdim), lambda a, r: (a, 0, 0)),
                  pl.BlockSpec((1, 4 * LANES, 2 * kdim), lambda a, r: (a, 0, 0))] + [hspec] * 4,
        out_specs=pl.BlockSpec((2, tr, kdim), lambda a, r: (a, r, 0)),
        compiler_params=_cparams(("arbitrary", "arbitrary")),
        name="s5_apply",
    )(ut, toep, cpow, *h_in)


def _chunk_rows_to_tokens(y_ref, perm_ref, ys):
    nrow = CHUNKS_PER_TILE * 8
    blocks = []
    for sl in range(N_SLAB):
        for th in range(CHUNK_T // GRANULES):
            blocks.append(jnp.concatenate(
                [y_ref[sl * GRANULES + m, :, th * LANES:(th + 1) * LANES] for m in range(GRANULES)],
                axis=1))
    acat = _bdot(jnp.concatenate(blocks, axis=0), perm_ref[...])
    blk = 0
    for sl in range(N_SLAB):
        for th in range(CHUNK_T // GRANULES):
            for kk in range(CHUNKS_PER_TILE):
                for j in range(GRANULES):
                    t = kk * CHUNK_T + th * GRANULES + j
                    ys[sl, pl.ds(t, 8, stride=PITCH), :] = acat[blk * nrow + kk * 8:
                                                               blk * nrow + (kk + 1) * 8,
                                                               j * LANES:(j + 1) * LANES]
            blk += 1
    return jnp.concatenate(
        [jnp.concatenate([ys[sl, pl.ds(b * PITCH, ROW_T), :] for sl in range(N_SLAB)], axis=1)
         for b in range(8)], axis=0)


def _back_kernel(x_ref, y_ref, mod_ref, perm_ref, gmix_ref, win_ref, convw_ref, convb_ref,
                 wglu_ref, wsbr_ref, wcbr_ref, wo_ref, gffn_ref, wrhi_ref, wrlo_ref, br_ref,
                 ltri_ref, x1_ref, h_ref, ids_ref, gates_ref, hist_ref, ys_scr):
    d_model = x_ref.shape[2]
    tm = 8 * ROW_T
    d_ssm = N_SLAB * LANES
    xt = x_ref[...].reshape(tm, d_model)
    xn = _rmsnorm(xt, gmix_ref[...])
    hx = _modulate_tile(xn, mod_ref, 0, 1).astype(BF16)
    p = _bdot(hx, win_ref[...])
    v = p[:, 0:d_ssm]
    gate_b = p[:, d_ssm:2 * d_ssm]
    gate_c = p[:, 2 * d_ssm:3 * d_ssm]
    g_s = p[:, 3 * d_ssm:3 * d_ssm + d_model]
    g_c = p[:, 3 * d_ssm + d_model:]

    z = gate_c * v
    col = lax.broadcasted_iota(jnp.int32, z.shape, 0) % GRID_W
    z_prev = jnp.where(col == 0, 0.0, pltpu.roll(z, 1, 0))
    z_next = jnp.where(col == GRID_W - 1, 0.0, pltpu.roll(z, tm - 1, 0))
    conv = (z_prev * convw_ref[0:1, :] + z * convw_ref[1:2, :]
            + z_next * convw_ref[2:3, :] + convb_ref[...])
    y_conv = gate_b * conv

    ys = jax.nn.gelu(_chunk_rows_to_tokens(y_ref, perm_ref, ys_scr))
    ys = ys * jax.nn.sigmoid(_bdot(ys.astype(BF16), wglu_ref[...]))

    y_s = _bdot(ys.astype(BF16), wsbr_ref[...])
    y_c = _bdot(y_conv.astype(BF16), wcbr_ref[...])
    merged = jax.nn.sigmoid(g_s) * y_s + jax.nn.sigmoid(g_c) * y_c
    mo = _bdot(merged.astype(BF16), wo_ref[...])
    x1 = xt + (mo.reshape(8, ROW_T, d_model) * mod_ref[:, 2:3, :]).reshape(tm, d_model)
    x1_ref[...] = x1.reshape(8, ROW_T, d_model)

    hn = _modulate_tile(_rmsnorm(x1, gffn_ref[...]), mod_ref, 3, 4)
    for b in range(8):
        for s in range(SUBLANES):
            h_ref[b, pl.ds(s, ROW_T, stride=SUBLANES), :] = hn[b * ROW_T:(b + 1) * ROW_T,
                                                              s * LANES:(s + 1) * LANES]

    hn_hi = hn.astype(BF16)
    hn_lo = (hn - hn_hi.astype(F32)).astype(BF16)
    logits = (_bdot(hn_hi, wrhi_ref[...]) + _bdot(hn_lo, wrhi_ref[...])
              + _bdot(hn_hi, wrlo_ref[...]) + br_ref[...])
    lane = lax.broadcasted_iota(jnp.int32, logits.shape, 1)
    neg = jnp.float32(-jnp.inf)
    cur = jnp.where(lane < N_EXPERTS, logits, neg)
    vals, idxs = [], []
    for _ in range(TOP_K):
        mk = jnp.max(cur, axis=-1, keepdims=True)
        ik = jnp.min(jnp.where(cur == mk, lane, LANES), axis=-1, keepdims=True)
        vals.append(mk)
        idxs.append(ik)
        cur = jnp.where(lane == ik, neg, cur)
    exps = [jnp.exp(vk - vals[0]) for vk in vals]
    denom = exps[0] + exps[1] + exps[2] + exps[3]
    ids = jnp.zeros(logits.shape, jnp.int32)
    gates = jnp.zeros(logits.shape, F32)
    onehot = jnp.zeros(logits.shape, F32)
    for k in range(TOP_K):
        onehot = onehot + (lane == idxs[k]).astype(F32)
    before = _bdot(ltri_ref[...], onehot.astype(BF16))
    for k in range(TOP_K):
        rank_k = jnp.sum(jnp.where(lane == idxs[k], before, 0.0), axis=-1, keepdims=True)
        ids = jnp.where(lane == k, idxs[k], ids)
        ids = jnp.where(lane == TOP_K + k, rank_k.astype(jnp.int32), ids)
        gates = jnp.where(lane == k, exps[k] / denom, gates)
    ids_ref[...] = ids.reshape(8, ROW_T, LANES)
    gates_ref[...] = gates.reshape(8, ROW_T, LANES)
    hist_ref[0] = jnp.broadcast_to(jnp.sum(onehot, axis=0, keepdims=True), (8, LANES))


def _back(x, y_chunks, mod, perm, g_mix, w_rest, conv_w, conv_b, w_glu, w_ssm_br,
          w_conv_br, w_o, g_ffn, w_router, b_router):
    b, s, d = x.shape
    g, _, kdim = y_chunks.shape
    tiles = s // ROW_T
    nrow = CHUNKS_PER_TILE * 8
    tok = lambda n: pl.BlockSpec((8, ROW_T, n), lambda i, j: (i, j, 0))

    def const(arr):
        nd = arr.ndim
        return pl.BlockSpec(arr.shape, lambda i, j: (0,) * nd, pipeline_mode=pl.Buffered(1))

    w_r_hi = w_router.astype(BF16)
    w_r_lo = (w_router - w_r_hi.astype(F32)).astype(BF16)
    row = jnp.arange(8 * ROW_T)
    ltri = (row[:, None] > row[None, :]).astype(BF16)
    params = (perm, g_mix, w_rest, conv_w, conv_b, w_glu, w_ssm_br, w_conv_br, w_o, g_ffn,
              w_r_hi, w_r_lo, b_router, ltri)
    return pl.pallas_call(
        _back_kernel,
        out_shape=(jax.ShapeDtypeStruct((b, s, d), F32),
                   jax.ShapeDtypeStruct((b, s * SUBLANES, LANES), F32),
                   jax.ShapeDtypeStruct((b, s, LANES), jnp.int32),
                   jax.ShapeDtypeStruct((b, s, LANES), F32),
                   jax.ShapeDtypeStruct(((b // 8) * tiles, 8, LANES), F32)),
        grid=(b // 8, tiles),
        in_specs=[tok(d),
                  pl.BlockSpec((g, nrow, kdim), lambda i, j: (0, i * tiles + j, 0)),
                  pl.BlockSpec((8, 8, d), lambda i, j: (i, 0, 0))] + [const(a) for a in params],
        out_specs=(tok(d), pl.BlockSpec((8, ROW_T * SUBLANES, LANES), lambda i, j: (i, j, 0)),
                   tok(LANES), tok(LANES),
                   pl.BlockSpec((1, 8, LANES), lambda i, j: (i * tiles + j, 0, 0))),
        scratch_shapes=[pltpu.VMEM((N_SLAB, 8 * PITCH, LANES), F32)],
        compiler_params=_cparams(("arbitrary", "arbitrary")),
        name="back",
    )(x, y_chunks, mod, *params)


def _routing(ids, ranks, tile_hist, bsz, seq):
    tb = MOE_TB
    n_tok = ids.shape[0]
    n_slot = n_tok * TOP_K
    tile_before = jnp.cumsum(tile_hist, axis=0) - tile_hist
    counts = jnp.sum(tile_hist, axis=0)
    padded = (counts + tb - 1) // tb * tb
    pad_end = jnp.cumsum(padded)
    pad_start = pad_end - padded
    start = jnp.cumsum(counts) - counts
    halves, tiles = bsz // 8, seq // ROW_T
    base = (pad_start[None, :] + tile_before).reshape(halves, 1, tiles, 1, N_EXPERTS)
    base = jnp.broadcast_to(base, (halves, 8, tiles, ROW_T, N_EXPERTS)).reshape(n_tok, N_EXPERTS)
    onehot = ids[:, :, None] == jnp.arange(N_EXPERTS, dtype=jnp.int32)[None, None, :]
    pos = (jnp.sum(jnp.where(onehot, base[:, None, :], 0), axis=-1) + ranks).astype(jnp.int32)
    token = jnp.broadcast_to(jnp.arange(n_tok, dtype=jnp.int32)[:, None], (n_tok, TOP_K))
    _, order_token = lax.sort_key_val(pos.reshape(-1), token.reshape(-1))
    n_blocks = n_slot // tb + N_EXPERTS
    cap = n_blocks * tb
    block_first_row = jnp.arange(n_blocks, dtype=jnp.int32) * tb
    block_expert = jnp.minimum(
        jnp.sum(pad_end[None, :] <= block_first_row[:, None], axis=1, dtype=jnp.int32),
        N_EXPERTS - 1)
    within = (block_first_row - pad_start[block_expert])[:, None] + jnp.arange(tb, dtype=jnp.int32)
    valid = within < counts[block_expert][:, None]
    src = jnp.clip(start[block_expert][:, None] + within, 0, n_slot - 1)
    buf_token = jnp.where(valid, order_token[src], 0).astype(jnp.int32).reshape(cap)
    n_used = (pad_end[-1] // tb).astype(jnp.int32).reshape(1)
    return buf_token, pos, block_expert, n_used


def _to_tiles(ref, val):
    rows = val.shape[0]
    for s in range(SUBLANES):
        ref[pl.ds(s, rows, stride=SUBLANES), :] = val[:, s * LANES:(s + 1) * LANES]


def _tile_piece(ref, first_row, rows, s):
    return ref[pl.ds(first_row * SUBLANES + s, rows, stride=SUBLANES), :]


def _tile_gather_start(idx_ref, src_hbm, dst, sem, n_rows, unrolled, alternate_priority=False):
    def copy(r, t):
        return pltpu.make_async_copy(
            src_hbm.at[pl.ds(pl.multiple_of(t * SUBLANES, SUBLANES), SUBLANES), :],
            dst.at[pl.ds(pl.multiple_of(r * SUBLANES, SUBLANES), SUBLANES), :], sem)

    if unrolled:
        for r in range(n_rows):
            copy(r, idx_ref[0, 0, r]).start(priority=(r % 2) if alternate_priority else 0)
    else:
        def body(r, carry):
            copy(r, idx_ref[0, 0, r]).start()
            return carry
        lax.fori_loop(0, n_rows, body, 0, unroll=8)


def _tile_gather_wait(src_hbm, dst, sem):
    pltpu.make_async_copy(src_hbm.at[pl.ds(0, dst.shape[0]), :], dst, sem).wait()


def _ring_step(i, last, bufs, sem, wait, prefetch, compute):
    n = len(bufs)
    for p in range(n):
        def branch(p=p):
            wait(bufs[p], sem.at[p])
            q = (p + n - 1) % n
            prefetch(bufs[q], sem.at[q])
            compute(bufs[p])

            @pl.when(i == last)
            def _():
                for r in range(1, n):
                    wait(bufs[(p + r) % n], sem.at[(p + r) % n])

        pl.when((i <= last) & (lax.rem(i, n) == p))(branch)


def _moe_kernel(be_ref, nu_ref, tok0_ref, tok1_ref, tok2_ref, h_hbm, wgu_ref, bgu_ref, wd_ref,
                bd_ref, out_ref, xbuf0, xbuf1, xbuf2, sem):
    del be_ref
    i = pl.program_id(0)
    n_used = nu_ref[0]
    tb = xbuf0.shape[0] // SUBLANES
    f = wd_ref.shape[1]

    @pl.when(i == 0)
    def _():
        _tile_gather_start(tok0_ref, h_hbm, xbuf0, sem.at[0], tb, unrolled=False)
        _tile_gather_start(tok1_ref, h_hbm, xbuf1, sem.at[1], tb, unrolled=False)

    def compute(xcur):
        xe = jnp.concatenate([_tile_piece(xcur, 0, tb, s) for s in range(SUBLANES)],
                             axis=1).astype(BF16)
        gu = _bdot(xe, wgu_ref[0]) + bgu_ref[0]
        gt = jnp.minimum(gu[:, :f], SWIGLU_LIMIT)
        up = jnp.clip(gu[:, f:], -SWIGLU_LIMIT, SWIGLU_LIMIT)
        act = gt * jax.nn.sigmoid(SWIGLU_ALPHA * gt) * (up + 1.0)
        _to_tiles(out_ref, _bdot(act.astype(BF16), wd_ref[0]) + bd_ref[0])

    _ring_step(i, n_used - 1, (xbuf0, xbuf1, xbuf2), sem,
               wait=lambda buf, s: _tile_gather_wait(h_hbm, buf, s),
               prefetch=lambda buf, s: _tile_gather_start(tok2_ref, h_hbm, buf, s, tb,
                                                          unrolled=True),
               compute=compute)

    @pl.when(i >= n_used)
    def _():
        out_ref[...] = jnp.zeros(out_ref.shape, out_ref.dtype)


def _moe(h_tiles, buf_token, block_expert, n_used, w_gu, b_gu, w_down, b_down):
    e, d, f2 = w_gu.shape
    f = f2 // 2
    tb = MOE_TB
    cap = buf_token.shape[0]
    n_blocks = cap // tb
    tok3 = buf_token.reshape(n_blocks, 1, tb)
    smem_blk = lambda imap: pl.BlockSpec((1, 1, tb), imap, memory_space=pltpu.SMEM)
    grid_spec = pltpu.PrefetchScalarGridSpec(
        num_scalar_prefetch=2,
        grid=(n_blocks,),
        in_specs=[smem_blk(lambda i, be, nu: (i, 0, 0)),
                  smem_blk(lambda i, be, nu: (jnp.minimum(i + 1, n_blocks - 1), 0, 0)),
                  smem_blk(lambda i, be, nu: (jnp.minimum(i + 2, n_blocks - 1), 0, 0)),
                  pl.BlockSpec(memory_space=pl.ANY),
                  pl.BlockSpec((1, d, f2), lambda i, be, nu: (be[i], 0, 0)),
                  pl.BlockSpec((1, 1, f2), lambda i, be, nu: (be[i], 0, 0)),
                  pl.BlockSpec((1, f, d), lambda i, be, nu: (be[i], 0, 0)),
                  pl.BlockSpec((1, 1, d), lambda i, be, nu: (be[i], 0, 0))],
        out_specs=pl.BlockSpec((tb * SUBLANES, LANES), lambda i, be, nu: (i, 0)),
        scratch_shapes=[pltpu.VMEM((tb * SUBLANES, LANES), F32)] * 3
        + [pltpu.SemaphoreType.DMA((3,))],
    )
    return pl.pallas_call(
        _moe_kernel,
        out_shape=jax.ShapeDtypeStruct((cap * SUBLANES, LANES), F32),
        grid_spec=grid_spec,
        compiler_params=_cparams(("arbitrary",)),
        name="moe",
    )(block_expert, n_used, tok3, tok3, tok3, h_tiles, w_gu, b_gu.reshape(e, 1, f2),
      w_down, b_down.reshape(e, 1, d))


def _combine_kernel(pos0_ref, pos1_ref, pos2_ref, yb_hbm, x1_ref, gates_ref, mod_ref, gfin_ref,
                    out_ref, buf0, buf1, buf2, sem):
    i = pl.program_id(0)
    n = pl.num_programs(0)
    n_rows = buf0.shape[0] // SUBLANES
    tm = n_rows // TOP_K

    @pl.when(i == 0)
    def _():
        _tile_gather_start(pos0_ref, yb_hbm, buf0, sem.at[0], n_rows, unrolled=False)
        _tile_gather_start(pos1_ref, yb_hbm, buf1, sem.at[1], n_rows, unrolled=False)

    def compute(cur):
        g = gates_ref[...]
        gk = [jnp.broadcast_to(g[:, k:k + 1], (tm, LANES)) for k in range(TOP_K)]
        pieces = []
        for s in range(SUBLANES):
            acc = gk[0] * _tile_piece(cur, 0, tm, s)
            for k in range(1, TOP_K):
                acc = acc + gk[k] * _tile_piece(cur, k * tm, tm, s)
            pieces.append(acc)
        x2 = x1_ref[...] + mod_ref[0, 5:6, :] * jnp.concatenate(pieces, axis=1)
        out_ref[...] = _rmsnorm(x2, gfin_ref[...])

    _ring_step(i, n - 1, (buf0, buf1, buf2), sem,
               wait=lambda buf, s: _tile_gather_wait(yb_hbm, buf, s),
               prefetch=lambda buf, s: _tile_gather_start(pos2_ref, yb_hbm, buf, s, n_rows,
                                                          unrolled=True, alternate_priority=True),
               compute=compute)


def _combine(yb, pos, x1_2d, gates_2d, mod, g_final, seq):
    n_tok, d = x1_2d.shape
    tm = min(COMBINE_TM, seq)
    assert seq % tm == 0
    n_tiles = n_tok // tm
    per_batch = seq // tm
    pos3 = pos.reshape(n_tiles, tm, TOP_K).transpose(0, 2, 1).reshape(n_tiles, 1, TOP_K * tm)
    smem_blk = lambda imap: pl.BlockSpec((1, 1, TOP_K * tm), imap, memory_space=pltpu.SMEM)
    buf = pltpu.VMEM((TOP_K * tm * SUBLANES, LANES), F32)
    return pl.pallas_call(
        _combine_kernel,
        out_shape=jax.ShapeDtypeStruct((n_tok, d), F32),
        grid=(n_tiles,),
        in_specs=[smem_blk(lambda i: (i, 0, 0)),
                  smem_blk(lambda i: (jnp.minimum(i + 1, n_tiles - 1), 0, 0)),
                  smem_blk(lambda i: (jnp.minimum(i + 2, n_tiles - 1), 0, 0)),
                  pl.BlockSpec(memory_space=pl.ANY),
                  pl.BlockSpec((tm, d), lambda i: (i, 0)),
                  pl.BlockSpec((tm, LANES), lambda i: (i, 0)),
                  pl.BlockSpec((1, 8, d), lambda i: (i // per_batch, 0, 0)),
                  pl.BlockSpec((1, d), lambda i: (0, 0))],
        out_specs=pl.BlockSpec((tm, d), lambda i: (i, 0)),
        scratch_shapes=[buf, buf, buf, pltpu.SemaphoreType.DMA((3,))],
        compiler_params=_cparams(("arbitrary",)),
        name="combine",
    )(pos3, pos3, pos3, yb, x1_2d, gates_2d, mod, g_final)


def kernel(x, c, ctx, c_ctx, w_mod, b_mod, g_mix, w_in, lam_re, lam_im, log_dt, b_re, b_im,
           c_re, c_im, d_skip, w_glu, conv_w, conv_b, w_ssm_br, w_conv_br, w_o, g_ffn,
           w_router, b_router, w_gu, b_gu, w_down, b_down, g_final):
    depth = w_mod.shape[0]
    assert depth == 1, "single-layer trunk"
    bsz, seq, d = x.shape
    ctx_len = ctx.shape[1]
    d_ssm = d // 2
    assert bsz % 8 == 0 and seq % CHUNK_T == 0 and ctx_len % CHUNK_T == 0 and seq % GRID_W == 0
    assert d == SUBLANES * LANES, "row gathers move one (8,128) f32 tile per token"

    n_cond = -(-(bsz + 1) // 8) * 8
    cond = jnp.zeros((n_cond, d), F32).at[:bsz].set(c).at[bsz].set(c_ctx)
    m = _adaln(cond, w_mod[0], b_mod[0])
    zeros2 = jnp.zeros((n_cond, 2, d), F32)
    mod_all = jnp.concatenate([m.reshape(n_cond, 6, d), zeros2], axis=1)
    mod_x, mod_c = mod_all[:bsz], mod_all[bsz:bsz + 1]

    w_in_bf = w_in[0].astype(BF16)
    w_u, w_rest = w_in_bf[:, :d_ssm], w_in_bf[:, d_ssm:]
    gm = g_mix[0].reshape(1, d)

    perm = _granule_transpose_matrix()
    utx = _front(x, mod_x, gm, w_u, perm)
    utc = _front(ctx, mod_c, gm, w_u, perm)

    smat, toep, cpow, a_t = _s5_matrices(lam_re[0], lam_im[0], log_dt[0], b_re[0], b_im[0],
                                         c_re[0], c_im[0], d_skip[0])
    s_x = _s5_states(utx, smat)
    s_c = _s5_states(utc, smat)
    h_in = _s5_scan(a_t, s_c, s_x, bsz)
    y_chunks = _s5_apply(utx, toep, cpow, h_in)

    pad_r = jnp.zeros((d, LANES - N_EXPERTS), F32)
    w_r = jnp.concatenate([w_router[0], pad_r], axis=1)
    b_r = jnp.concatenate([b_router[0], jnp.zeros((LANES - N_EXPERTS,), F32)]).reshape(1, LANES)
    x1, h, ids, gates, hist = _back(
        x, y_chunks, mod_x, perm, gm, w_rest, conv_w[0], conv_b[0].reshape(1, d_ssm),
        w_glu[0].astype(BF16), w_ssm_br[0].astype(BF16),
        w_conv_br[0].astype(BF16), w_o[0].astype(BF16), g_ffn[0].reshape(1, d), w_r, b_r)

    n_tok = bsz * seq
    ids2 = ids.reshape(n_tok, LANES)
    buf_token, pos, block_expert, n_used = _routing(
        ids2[:, :TOP_K], ids2[:, TOP_K:2 * TOP_K],
        hist[:, 0, :N_EXPERTS].astype(jnp.int32), bsz, seq)
    yb = _moe(h.reshape(n_tok * SUBLANES, LANES), buf_token, block_expert, n_used,
              w_gu[0].astype(BF16), b_gu[0], w_down[0].astype(BF16), b_down[0])
    out = _combine(yb, pos, x1.reshape(n_tok, d), gates.reshape(n_tok, LANES), mod_x,
                   g_final.reshape(1, d), seq)
    return out.reshape(bsz, seq, d)
```

```python
import functools
import math

import jax
import jax.numpy as jnp
from jax import lax
from jax.experimental import pallas as pl
from jax.experimental.pallas import tpu as pltpu

F32 = jnp.float32
BF16 = jnp.bfloat16
HIGHEST = lax.Precision.HIGHEST

RMS_EPS = 1e-6
GRID_W = 64
SSM_GROUP = 16
SSM_STATE = 64
N_EXPERTS = 32
TOP_K = 4
SWIGLU_LIMIT = 7.0
SWIGLU_ALPHA = 1.702

CHUNK_T = 16
LANES = 128
SUBLANES = 8
V7X_VMEM_LIMIT_BYTES = 56 * 1024 * 1024

S5_TR = 1024
MOE_TB = 512
COMBINE_TM = 256


def _cparams(sem):
    return pltpu.CompilerParams(dimension_semantics=sem,
                                vmem_limit_bytes=V7X_VMEM_LIMIT_BYTES)


def _bdot(a, b):
    return jnp.dot(a, b, preferred_element_type=F32)


def _rmsnorm(xt, g):
    ms = jnp.mean(xt * xt, axis=-1, keepdims=True)
    return xt * lax.rsqrt(ms + RMS_EPS) * g


def _adaln_kernel(c_ref, w_ref, b_ref, o_ref):
    s = jax.nn.silu(c_ref[...])
    o_ref[...] = jnp.dot(s, w_ref[...], precision=HIGHEST,
                         preferred_element_type=F32) + b_ref[...]


def _adaln(cond, w_mod, b_mod):
    r, d = cond.shape
    n = w_mod.shape[1]
    tn = n // 4
    return pl.pallas_call(
        _adaln_kernel,
        out_shape=jax.ShapeDtypeStruct((r, n), F32),
        grid=(n // tn,),
        in_specs=[pl.BlockSpec((r, d), lambda j: (0, 0)),
                  pl.BlockSpec((d, tn), lambda j: (0, j)),
                  pl.BlockSpec((1, tn), lambda j: (0, j))],
        out_specs=pl.BlockSpec((r, tn), lambda j: (0, j)),
        compiler_params=_cparams(("arbitrary",)),
        name="adaln",
    )(cond, w_mod, b_mod.reshape(1, n))


ROW_T = GRID_W
PITCH = ROW_T + 8
N_SLAB = 4
CHUNKS_PER_TILE = ROW_T // CHUNK_T
GRANULES = LANES // SSM_GROUP


def _granule_transpose_matrix():
    n = GRANULES * LANES
    idx = jnp.arange(n)
    j, m, c = idx // LANES, (idx % LANES) // SSM_GROUP, idx % SSM_GROUP
    dst = m * LANES + j * SSM_GROUP + c
    return jnp.zeros((n, n), F32).at[idx, dst].set(1.0).astype(BF16)


def _modulate_tile(xn, mod_ref, shift_row, scale_row):
    d = xn.shape[1]
    x3 = xn.reshape(8, ROW_T, d)
    x3 = x3 * (1.0 + mod_ref[:, scale_row:scale_row + 1, :]) + mod_ref[:, shift_row:shift_row + 1, :]
    return x3.reshape(8 * ROW_T, d)


def _front_kernel(x_ref, mod_ref, g_ref, w_ref, perm_ref, u_ref, us):
    d = x_ref.shape[2]
    xn = _rmsnorm(x_ref[...].reshape(8 * ROW_T, d), g_ref[...])
    hx = _modulate_tile(xn, mod_ref, 0, 1)
    u = _bdot(hx.astype(BF16), w_ref[...])
    for sl in range(N_SLAB):
        for b in range(8):
            us[sl, pl.ds(b * PITCH, ROW_T), :] = u[b * ROW_T:(b + 1) * ROW_T,
                                                  sl * LANES:(sl + 1) * LANES]
    blocks = []
    for sl in range(N_SLAB):
        for th in range(CHUNK_T // GRANULES):
            rows = []
            for kk in range(CHUNKS_PER_TILE):
                t0 = kk * CHUNK_T + th * GRANULES
                rows.append(jnp.concatenate(
                    [us[sl, pl.ds(t0 + j, 8, stride=PITCH), :] for j in range(GRANULES)], axis=1))
            blocks.append(jnp.concatenate(rows, axis=0))
    acat = jnp.concatenate(blocks, axis=0).astype(BF16)
    bmat = _bdot(acat, perm_ref[...])
    nrow = CHUNKS_PER_TILE * 8
    blk = 0
    for sl in range(N_SLAB):
        for th in range(CHUNK_T // GRANULES):
            for m in range(GRANULES):
                u_ref[sl * GRANULES + m, :, th * LANES:(th + 1) * LANES] = (
                    bmat[blk * nrow:(blk + 1) * nrow, m * LANES:(m + 1) * LANES].astype(u_ref.dtype))
            blk += 1


def _front(x, mod, g_mix, w_u, perm):
    b, s, d = x.shape
    n = w_u.shape[1]
    g = n // SSM_GROUP
    assert s % ROW_T == 0 and b % 8 == 0 and n == N_SLAB * LANES
    k = s // CHUNK_T
    tiles = s // ROW_T
    nrow = CHUNKS_PER_TILE * 8
    shared = mod.shape[0] == 1
    mod_spec = (pl.BlockSpec((1, 8, d), lambda i, j: (0, 0, 0)) if shared
                else pl.BlockSpec((8, 8, d), lambda i, j: (i, 0, 0)))
    return pl.pallas_call(
        _front_kernel,
        out_shape=jax.ShapeDtypeStruct((g, (b // 8) * k * 8, CHUNK_T * SSM_GROUP), BF16),
        grid=(b // 8, tiles),
        in_specs=[pl.BlockSpec((8, ROW_T, d), lambda i, j: (i, j, 0)),
                  mod_spec,
                  pl.BlockSpec((1, d), lambda i, j: (0, 0)),
                  pl.BlockSpec((d, n), lambda i, j: (0, 0)),
                  pl.BlockSpec(perm.shape, lambda i, j: (0, 0))],
        out_specs=pl.BlockSpec((g, nrow, CHUNK_T * SSM_GROUP), lambda i, j: (0, i * tiles + j, 0)),
        scratch_shapes=[pltpu.VMEM((N_SLAB, 8 * PITCH, LANES), F32)],
        compiler_params=_cparams(("arbitrary", "arbitrary")),
        name="front",
    )(x, mod, g_mix, w_u, perm)


def _s5_discretise(lam_re, lam_im, log_dt, b_re, b_im):
    dt = jnp.exp(log_dt)[..., None]
    mag = jnp.exp(lam_re * dt)
    a_re, a_im = mag * jnp.cos(lam_im * dt), mag * jnp.sin(lam_im * dt)
    den = lam_re * lam_re + lam_im * lam_im
    q_re = ((a_re - 1) * lam_re + a_im * lam_im) / den
    q_im = (a_im * lam_re - (a_re - 1) * lam_im) / den
    bb_re = q_re[..., None] * b_re - q_im[..., None] * b_im
    bb_im = q_re[..., None] * b_im + q_im[..., None] * b_re
    return a_re, a_im, bb_re, bb_im


def _s5_matrices(lam_re, lam_im, log_dt, b_re, b_im, c_re, c_im, d_skip):
    t = CHUNK_T
    a_re, a_im, bb_re, bb_im = _s5_discretise(lam_re, lam_im, log_dt, b_re, b_im)
    g, p = a_re.shape[1], a_re.shape[2]
    ch = bb_re.shape[-1]
    pw_re, pw_im = [jnp.ones_like(a_re)], [jnp.zeros_like(a_im)]
    for _ in range(t):
        r, i = pw_re[-1], pw_im[-1]
        pw_re.append(r * a_re - i * a_im)
        pw_im.append(r * a_im + i * a_re)
    pw_re, pw_im = jnp.stack(pw_re), jnp.stack(pw_im)
    cr = c_re[None] * pw_re[:, :, :, None, :] - c_im[None] * pw_im[:, :, :, None, :]
    ci = c_re[None] * pw_im[:, :, :, None, :] + c_im[None] * pw_re[:, :, :, None, :]
    kern = (jnp.einsum("dzgcp,zgpe->dzgce", cr, bb_re, precision=HIGHEST)
            - jnp.einsum("dzgcp,zgpe->dzgce", ci, bb_im, precision=HIGHEST))
    s_idx = jnp.arange(t)[:, None]
    t_idx = jnp.arange(t)[None, :]
    dlt = t_idx - s_idx
    kf = kern[jnp.clip(dlt, 0, t - 1), 0]
    kb = kern[jnp.clip(-dlt, 0, t - 1), 1]
    msk_f = (dlt >= 0)[:, :, None, None, None]
    msk_b = (dlt <= 0)[:, :, None, None, None]
    tsum = jnp.where(msk_f, kf, 0.0) + jnp.where(msk_b, kb, 0.0)
    toep = tsum.transpose(2, 0, 4, 1, 3).reshape(g, t * ch, t * ch)
    skip = jnp.tile(d_skip.reshape(g, 1, ch), (1, t, 1)).reshape(g, 1, t * ch)
    toep = toep + jnp.eye(t * ch, dtype=F32)[None] * skip

    pf_re, pf_im = pw_re[t - 1 - jnp.arange(t), 0], pw_im[t - 1 - jnp.arange(t), 0]
    pb_re, pb_im = pw_re[jnp.arange(t), 1], pw_im[jnp.arange(t), 1]

    def in_to_state(q_re, q_im, z):
        bre = bb_re[z].transpose(0, 2, 1)[None]
        bim = bb_im[z].transpose(0, 2, 1)[None]
        s_re = q_re[:, :, None, :] * bre - q_im[:, :, None, :] * bim
        s_im = q_re[:, :, None, :] * bim + q_im[:, :, None, :] * bre
        to_rows = lambda m: m.transpose(1, 0, 2, 3).reshape(g, t * ch, p)
        return to_rows(s_re), to_rows(s_im)

    sf_re, sf_im = in_to_state(pf_re, pf_im, 0)
    sb_re, sb_im = in_to_state(pb_re, pb_im, 1)
    s_all = jnp.stack([sf_re, sf_im, sb_re, sb_im], axis=1)
    s_all = s_all.reshape(g // 2, 2, 4, t * ch, p)
    eye2 = jnp.eye(2, dtype=F32)
    smat = jnp.einsum("ajqrp,jk->ajrqkp", s_all, eye2).reshape(g // 2, 2 * t * ch, 4 * 2 * p)

    def state_to_out(idx, z):
        cre = cr[idx, z].transpose(1, 3, 0, 2).reshape(g, p, t * ch)
        cim = ci[idx, z].transpose(1, 3, 0, 2).reshape(g, p, t * ch)
        return cre, -cim

    cf_re, cf_im = state_to_out(jnp.arange(t) + 1, 0)
    cb_re, cb_im = state_to_out(t - jnp.arange(t), 1)
    c_all = jnp.stack([cf_re, cf_im, cb_re, cb_im], axis=1)
    c_all = c_all.reshape(g // 2, 2, 4, p, t * ch)
    cpow = jnp.einsum("ajqpn,jk->aqjpkn", c_all, eye2).reshape(g // 2, 4 * 2 * p, 2 * t * ch)

    a_t = jnp.stack([pw_re[t, 0], pw_im[t, 0], pw_re[t, 1], pw_im[t, 1]]).reshape(4, g * p)
    return smat.astype(BF16), toep.astype(BF16), cpow.astype(BF16), a_t


def _s5_states_kernel(u_ref, m_ref, fre_ref, fim_ref, bre_ref, bim_ref):
    lhs = jnp.concatenate([u_ref[0], u_ref[1]], axis=1)
    res = _bdot(lhs, m_ref[0])
    fre_ref[...] = res[:, 0 * LANES:1 * LANES]
    fim_ref[...] = res[:, 1 * LANES:2 * LANES]
    bre_ref[...] = res[:, 2 * LANES:3 * LANES]
    bim_ref[...] = res[:, 3 * LANES:4 * LANES]


def _s5_states(ut, smat):
    g, rows, kdim = ut.shape
    tr = min(S5_TR, rows)
    assert rows % tr == 0
    n_state = g * SSM_STATE
    out = jax.ShapeDtypeStruct((rows, n_state), F32)
    ospec = pl.BlockSpec((tr, LANES), lambda a, r: (r, a))
    return pl.pallas_call(
        _s5_states_kernel,
        out_shape=(out, out, out, out),
        grid=(g // 2, rows // tr),
        in_specs=[pl.BlockSpec((2, tr, kdim), lambda a, r: (a, r, 0)),
                  pl.BlockSpec((1, 2 * kdim, 4 * LANES), lambda a, r: (a, 0, 0))],
        out_specs=(ospec, ospec, ospec, ospec),
        compiler_params=_cparams(("arbitrary", "arbitrary")),
        name="s5_states",
    )(ut, smat)


def _s5_scan_kernel(a_ref, cfr_ref, cfi_ref, cbr_ref, cbi_ref,
                    xfr_ref, xfi_ref, xbr_ref, xbi_ref,
                    hfr_ref, hfi_ref, hbr_ref, hbi_ref, *, halves, kc, kx):
    a_fr, a_fi = a_ref[0:1, :], a_ref[1:2, :]
    a_br, a_bi = a_ref[2:3, :], a_ref[3:4, :]

    def rows(half, k, n_chunks, count=1):
        return pl.ds(pl.multiple_of((half * n_chunks + k) * 8, 8 * count), 8 * count)

    def step(h_re, h_im, a_re, a_im, s_re, s_im):
        return (a_re * h_re - a_im * h_im + s_re, a_re * h_im + a_im * h_re + s_im)

    zero = jnp.zeros((8, LANES), F32)

    def ctx_body(i, carry):
        out = []
        for half in range(halves):
            fr, fi, br, bi = carry[half]
            kf, kb = i, kc - 1 - i
            fr, fi = step(fr, fi, a_fr, a_fi, cfr_ref[rows(half, kf, kc), :],
                          cfi_ref[rows(half, kf, kc), :])
            br, bi = step(br, bi, a_br, a_bi, cbr_ref[rows(half, kb, kc), :],
                          cbi_ref[rows(half, kb, kc), :])
            out.append((fr, fi, br, bi))
        return tuple(out)

    carry = lax.fori_loop(0, kc, ctx_body, ((zero,) * 4,) * halves)

    def x_body(i, carry):
        out = []
        for half in range(halves):
            fr, fi, br, bi = carry[half]
            kf = 2 * i
            kb = kx - 2 - 2 * i
            fr1, fi1 = step(fr, fi, a_fr, a_fi, xfr_ref[rows(half, kf, kx), :],
                            xfi_ref[rows(half, kf, kx), :])
            br1, bi1 = step(br, bi, a_br, a_bi, xbr_ref[rows(half, kb + 1, kx), :],
                            xbi_ref[rows(half, kb + 1, kx), :])
            hfr_ref[rows(half, kf, kx, 2), :] = jnp.concatenate([fr, fr1], 0).astype(hfr_ref.dtype)
            hfi_ref[rows(half, kf, kx, 2), :] = jnp.concatenate([fi, fi1], 0).astype(hfi_ref.dtype)
            hbr_ref[rows(half, kb, kx, 2), :] = jnp.concatenate([br1, br], 0).astype(hbr_ref.dtype)
            hbi_ref[rows(half, kb, kx, 2), :] = jnp.concatenate([bi1, bi], 0).astype(hbi_ref.dtype)
            fr, fi = step(fr1, fi1, a_fr, a_fi, xfr_ref[rows(half, kf + 1, kx), :],
                          xfi_ref[rows(half, kf + 1, kx), :])
            br, bi = step(br1, bi1, a_br, a_bi, xbr_ref[rows(half, kb, kx), :],
                          xbi_ref[rows(half, kb, kx), :])
            out.append((fr, fi, br, bi))
        return tuple(out)

    lax.fori_loop(0, kx // 2, x_body, carry)


def _s5_scan(a_t, s_ctx, s_x, batch):
    rows_c, n_state = s_ctx[0].shape
    rows_x = s_x[0].shape[0]
    kc, kx = rows_c // batch, rows_x // batch
    assert kx % 2 == 0
    out = jax.ShapeDtypeStruct((rows_x, n_state), BF16)
    cspec = pl.BlockSpec((rows_c, LANES), lambda j: (0, j))
    xspec = pl.BlockSpec((rows_x, LANES), lambda j: (0, j))
    return pl.pallas_call(
        functools.partial(_s5_scan_kernel, halves=batch // 8, kc=kc, kx=kx),
        out_shape=(out, out, out, out),
        grid=(n_state // LANES,),
        in_specs=[pl.BlockSpec((4, LANES), lambda j: (0, j))] + [cspec] * 4 + [xspec] * 4,
        out_specs=(xspec, xspec, xspec, xspec),
        compiler_params=_cparams(("arbitrary",)),
        name="s5_scan",
    )(a_t, *s_ctx, *s_x)


def _s5_apply_kernel(u_ref, t_ref, c_ref, hfr_ref, hfi_ref, hbr_ref, hbi_ref, y_ref):
    hcat = jnp.concatenate([hfr_ref[...], hfi_ref[...], hbr_ref[...], hbi_ref[...]], axis=1)
    yst = _bdot(hcat, c_ref[0])
    n = u_ref.shape[2]
    y_ref[0] = (_bdot(u_ref[0], t_ref[0]) + yst[:, :n]).astype(y_ref.dtype)
    y_ref[1] = (_bdot(u_ref[1], t_ref[1]) + yst[:, n:]).astype(y_ref.dtype)


def _s5_apply(ut, toep, cpow, h_in):
    g, rows, kdim = ut.shape
    tr = min(S5_TR, rows)
    hspec = pl.BlockSpec((tr, LANES), lambda a, r: (r, a))
    return pl.pallas_call(
        _s5_apply_kernel,
        out_shape=jax.ShapeDtypeStruct((g, rows, kdim), BF16),
        grid=(g // 2, rows // tr),
        in_specs=[pl.BlockSpec((2, tr, kdim), lambda a, r: (a, r, 0)),
                  pl.BlockSpec((2, kdim, kdim), lambda a, r: (a, 0, 0)),
                  pl.BlockSpec((1, 4 * LANES, 2 * kdim), lambda a, r: (a, 0, 0))] + [hspec] * 4,
        out_specs=pl.BlockSpec((2, tr, kdim), lambda a, r: (a, r, 0)),
        compiler_params=_cparams(("arbitrary", "arbitrary")),
        name="s5_apply",
    )(ut, toep, cpow, *h_in)


def _chunk_rows_to_tokens(y_ref, perm_ref, ys):
    nrow = CHUNKS_PER_TILE * 8
    blocks = []
    for sl in range(N_SLAB):
        for th in range(CHUNK_T // GRANULES):
            blocks.append(jnp.concatenate(
                [y_ref[sl * GRANULES + m, :, th * LANES:(th + 1) * LANES] for m in range(GRANULES)],
                axis=1))
    acat = _bdot(jnp.concatenate(blocks, axis=0), perm_ref[...])
    blk = 0
    for sl in range(N_SLAB):
        for th in range(CHUNK_T // GRANULES):
            for kk in range(CHUNKS_PER_TILE):
                for j in range(GRANULES):
                    t = kk * CHUNK_T + th * GRANULES + j
                    ys[sl, pl.ds(t, 8, stride=PITCH), :] = acat[blk * nrow + kk * 8:
                                                               blk * nrow + (kk + 1) * 8,
                                                               j * LANES:(j + 1) * LANES]
            blk += 1
    return jnp.concatenate(
        [jnp.concatenate([ys[sl, pl.ds(b * PITCH, ROW_T), :] for sl in range(N_SLAB)], axis=1)
         for b in range(8)], axis=0)


def _back_kernel(x_ref, y_ref, mod_ref, perm_ref, gmix_ref, win_ref, convw_ref, convb_ref,
                 wglu_ref, wsbr_ref, wcbr_ref, wo_ref, gffn_ref, wrhi_ref, wrlo_ref, br_ref,
                 ltri_ref, x1_ref, h_ref, ids_ref, gates_ref, hist_ref, ys_scr):
    d_model = x_ref.shape[2]
    tm = 8 * ROW_T
    d_ssm = N_SLAB * LANES
    xt = x_ref[...].reshape(tm, d_model)
    xn = _rmsnorm(xt, gmix_ref[...])
    hx = _modulate_tile(xn, mod_ref, 0, 1).astype(BF16)
    p = _bdot(hx, win_ref[...])
    v = p[:, 0:d_ssm]
    gate_b = p[:, d_ssm:2 * d_ssm]
    gate_c = p[:, 2 * d_ssm:3 * d_ssm]
    g_s = p[:, 3 * d_ssm:3 * d_ssm + d_model]
    g_c = p[:, 3 * d_ssm + d_model:]

    z = gate_c * v
    col = lax.broadcasted_iota(jnp.int32, z.shape, 0) % GRID_W
    z_prev = jnp.where(col == 0, 0.0, pltpu.roll(z, 1, 0))
    z_next = jnp.where(col == GRID_W - 1, 0.0, pltpu.roll(z, tm - 1, 0))
    conv = (z_prev * convw_ref[0:1, :] + z * convw_ref[1:2, :]
            + z_next * convw_ref[2:3, :] + convb_ref[...])
    y_conv = gate_b * conv

    ys = jax.nn.gelu(_chunk_rows_to_tokens(y_ref, perm_ref, ys_scr))
    ys = ys * jax.nn.sigmoid(_bdot(ys.astype(BF16), wglu_ref[...]))

    y_s = _bdot(ys.astype(BF16), wsbr_ref[...])
    y_c = _bdot(y_conv.astype(BF16), wcbr_ref[...])
    merged = jax.nn.sigmoid(g_s) * y_s + jax.nn.sigmoid(g_c) * y_c
    mo = _bdot(merged.astype(BF16), wo_ref[...])
    x1 = xt + (mo.reshape(8, ROW_T, d_model) * mod_ref[:, 2:3, :]).reshape(tm, d_model)
    x1_ref[...] = x1.reshape(8, ROW_T, d_model)

    hn = _modulate_tile(_rmsnorm(x1, gffn_ref[...]), mod_ref, 3, 4)
    for b in range(8):
        for s in range(SUBLANES):
            h_ref[b, pl.ds(s, ROW_T, stride=SUBLANES), :] = hn[b * ROW_T:(b + 1) * ROW_T,
                                                              s * LANES:(s + 1) * LANES]

    hn_hi = hn.astype(BF16)
    hn_lo = (hn - hn_hi.astype(F32)).astype(BF16)
    logits = (_bdot(hn_hi, wrhi_ref[...]) + _bdot(hn_lo, wrhi_ref[...])
              + _bdot(hn_hi, wrlo_ref[...]) + br_ref[...])
    lane = lax.broadcasted_iota(jnp.int32, logits.shape, 1)
    neg = jnp.float32(-jnp.inf)
    cur = jnp.where(lane < N_EXPERTS, logits, neg)
    vals, idxs = [], []
    for _ in range(TOP_K):
        mk = jnp.max(cur, axis=-1, keepdims=True)
        ik = jnp.min(jnp.where(cur == mk, lane, LANES), axis=-1, keepdims=True)
        vals.append(mk)
        idxs.append(ik)
        cur = jnp.where(lane == ik, neg, cur)
    exps = [jnp.exp(vk - vals[0]) for vk in vals]
    denom = exps[0] + exps[1] + exps[2] + exps[3]
    ids = jnp.zeros(logits.shape, jnp.int32)
    gates = jnp.zeros(logits.shape, F32)
    onehot = jnp.zeros(logits.shape, F32)
    for k in range(TOP_K):
        onehot = onehot + (lane == idxs[k]).astype(F32)
    before = _bdot(ltri_ref[...], onehot.astype(BF16))
    for k in range(TOP_K):
        rank_k = jnp.sum(jnp.where(lane == idxs[k], before, 0.0), axis=-1, keepdims=True)
        ids = jnp.where(lane == k, idxs[k], ids)
        ids = jnp.where(lane == TOP_K + k, rank_k.astype(jnp.int32), ids)
        gates = jnp.where(lane == k, exps[k] / denom, gates)
    ids_ref[...] = ids.reshape(8, ROW_T, LANES)
    gates_ref[...] = gates.reshape(8, ROW_T, LANES)
    hist_ref[0] = jnp.broadcast_to(jnp.sum(onehot, axis=0, keepdims=True), (8, LANES))


def _back(x, y_chunks, mod, perm, g_mix, w_rest, conv_w, conv_b, w_glu, w_ssm_br,
          w_conv_br, w_o, g_ffn, w_router, b_router):
    b, s, d = x.shape
    g, _, kdim = y_chunks.shape
    tiles = s // ROW_T
    nrow = CHUNKS_PER_TILE * 8
    tok = lambda n: pl.BlockSpec((8, ROW_T, n), lambda i, j: (i, j, 0))

    def const(arr):
        nd = arr.ndim
        return pl.BlockSpec(arr.shape, lambda i, j: (0,) * nd, pipeline_mode=pl.Buffered(1))

    w_r_hi = w_router.astype(BF16)
    w_r_lo = (w_router - w_r_hi.astype(F32)).astype(BF16)
    row = jnp.arange(8 * ROW_T)
    ltri = (row[:, None] > row[None, :]).astype(BF16)
    params = (perm, g_mix, w_rest, conv_w, conv_b, w_glu, w_ssm_br, w_conv_br, w_o, g_ffn,
              w_r_hi, w_r_lo, b_router, ltri)
    return pl.pallas_call(
        _back_kernel,
        out_shape=(jax.ShapeDtypeStruct((b, s, d), F32),
                   jax.ShapeDtypeStruct((b, s * SUBLANES, LANES), F32),
                   jax.ShapeDtypeStruct((b, s, LANES), jnp.int32),
                   jax.ShapeDtypeStruct((b, s, LANES), F32),
                   jax.ShapeDtypeStruct(((b // 8) * tiles, 8, LANES), F32)),
        grid=(b // 8, tiles),
        in_specs=[tok(d),
                  pl.BlockSpec((g, nrow, kdim), lambda i, j: (0, i * tiles + j, 0)),
                  pl.BlockSpec((8, 8, d), lambda i, j: (i, 0, 0))] + [const(a) for a in params],
        out_specs=(tok(d), pl.BlockSpec((8, ROW_T * SUBLANES, LANES), lambda i, j: (i, j, 0)),
                   tok(LANES), tok(LANES),
                   pl.BlockSpec((1, 8, LANES), lambda i, j: (i * tiles + j, 0, 0))),
        scratch_shapes=[pltpu.VMEM((N_SLAB, 8 * PITCH, LANES), F32)],
        compiler_params=_cparams(("arbitrary", "arbitrary")),
        name="back",
    )(x, y_chunks, mod, *params)


def _routing(ids, ranks, tile_hist, bsz, seq):
    tb = MOE_TB
    n_tok = ids.shape[0]
    n_slot = n_tok * TOP_K
    tile_before = jnp.cumsum(tile_hist, axis=0) - tile_hist
    counts = jnp.sum(tile_hist, axis=0)
    padded = (counts + tb - 1) // tb * tb
    pad_end = jnp.cumsum(padded)
    pad_start = pad_end - padded
    start = jnp.cumsum(counts) - counts
    halves, tiles = bsz // 8, seq // ROW_T
    base = (pad_start[None, :] + tile_before).reshape(halves, 1, tiles, 1, N_EXPERTS)
    base = jnp.broadcast_to(base, (halves, 8, tiles, ROW_T, N_EXPERTS)).reshape(n_tok, N_EXPERTS)
    onehot = ids[:, :, None] == jnp.arange(N_EXPERTS, dtype=jnp.int32)[None, None, :]
    pos = (jnp.sum(jnp.where(onehot, base[:, None, :], 0), axis=-1) + ranks).astype(jnp.int32)
    token = jnp.broadcast_to(jnp.arange(n_tok, dtype=jnp.int32)[:, None], (n_tok, TOP_K))
    _, order_token = lax.sort_key_val(pos.reshape(-1), token.reshape(-1))
    n_blocks = n_slot // tb + N_EXPERTS
    cap = n_blocks * tb
    block_first_row = jnp.arange(n_blocks, dtype=jnp.int32) * tb
    block_expert = jnp.minimum(
        jnp.sum(pad_end[None, :] <= block_first_row[:, None], axis=1, dtype=jnp.int32),
        N_EXPERTS - 1)
    within = (block_first_row - pad_start[block_expert])[:, None] + jnp.arange(tb, dtype=jnp.int32)
    valid = within < counts[block_expert][:, None]
    src = jnp.clip(start[block_expert][:, None] + within, 0, n_slot - 1)
    buf_token = jnp.where(valid, order_token[src], 0).astype(jnp.int32).reshape(cap)
    n_used = (pad_end[-1] // tb).astype(jnp.int32).reshape(1)
    return buf_token, pos, block_expert, n_used


def _to_tiles(ref, val):
    rows = val.shape[0]
    for s in range(SUBLANES):
        ref[pl.ds(s, rows, stride=SUBLANES), :] = val[:, s * LANES:(s + 1) * LANES]


def _tile_piece(ref, first_row, rows, s):
    return ref[pl.ds(first_row * SUBLANES + s, rows, stride=SUBLANES), :]


def _tile_gather_start(idx_ref, src_hbm, dst, sem, n_rows, unrolled, alternate_priority=False):
    def copy(r, t):
        return pltpu.make_async_copy(
            src_hbm.at[pl.ds(pl.multiple_of(t * SUBLANES, SUBLANES), SUBLANES), :],
            dst.at[pl.ds(pl.multiple_of(r * SUBLANES, SUBLANES), SUBLANES), :], sem)

    if unrolled:
        for r in range(n_rows):
            copy(r, idx_ref[0, 0, r]).start(priority=(r % 2) if alternate_priority else 0)
    else:
        def body(r, carry):
            copy(r, idx_ref[0, 0, r]).start()
            return carry
        lax.fori_loop(0, n_rows, body, 0, unroll=8)


def _tile_gather_wait(src_hbm, dst, sem):
    pltpu.make_async_copy(src_hbm.at[pl.ds(0, dst.shape[0]), :], dst, sem).wait()


def _ring_step(i, last, bufs, sem, wait, prefetch, compute):
    n = len(bufs)
    for p in range(n):
        def branch(p=p):
            wait(bufs[p], sem.at[p])
            q = (p + n - 1) % n
            prefetch(bufs[q], sem.at[q])
            compute(bufs[p])

            @pl.when(i == last)
            def _():
                for r in range(1, n):
                    wait(bufs[(p + r) % n], sem.at[(p + r) % n])

        pl.when((i <= last) & (lax.rem(i, n) == p))(branch)


def _cast_rows(src_ref, dst_ref, chunk):
    def body(c, carry):
        rows = pl.ds(pl.multiple_of(c * chunk, chunk), chunk)
        dst_ref[rows, :] = src_ref[0, rows, :].astype(dst_ref.dtype)
        return carry
    lax.fori_loop(0, dst_ref.shape[0] // chunk, body, 0)


def _moe_kernel(be_ref, nu_ref, tok0_ref, tok1_ref, tok2_ref, h_hbm, wgu_ref, bgu_ref, wd_ref,
                bd_ref, out_ref, xbuf0, xbuf1, xbuf2, wgu_bf, wd_bf, sem):
    i = pl.program_id(0)
    n_used = nu_ref[0]
    tb = xbuf0.shape[0] // SUBLANES
    f = wd_ref.shape[1]

    @pl.when(i == 0)
    def _():
        _tile_gather_start(tok0_ref, h_hbm, xbuf0, sem.at[0], tb, unrolled=False)
        _tile_gather_start(tok1_ref, h_hbm, xbuf1, sem.at[1], tb, unrolled=False)

    new_expert = (i == 0) | (be_ref[i] != be_ref[jnp.maximum(i - 1, 0)])

    @pl.when(new_expert & (i < n_used))
    def _():
        _cast_rows(wgu_ref, wgu_bf, 128)
        _cast_rows(wd_ref, wd_bf, 128)

    def compute(xcur):
        xe = jnp.concatenate([_tile_piece(xcur, 0, tb, s) for s in range(SUBLANES)],
                             axis=1).astype(BF16)
        gu = _bdot(xe, wgu_bf[...]) + bgu_ref[0]
        gt = jnp.minimum(gu[:, :f], SWIGLU_LIMIT)
        up = jnp.clip(gu[:, f:], -SWIGLU_LIMIT, SWIGLU_LIMIT)
        act = gt * jax.nn.sigmoid(SWIGLU_ALPHA * gt) * (up + 1.0)
        _to_tiles(out_ref, _bdot(act.astype(BF16), wd_bf[...]) + bd_ref[0])

    _ring_step(i, n_used - 1, (xbuf0, xbuf1, xbuf2), sem,
               wait=lambda buf, s: _tile_gather_wait(h_hbm, buf, s),
               prefetch=lambda buf, s: _tile_gather_start(tok2_ref, h_hbm, buf, s, tb,
                                                          unrolled=True),
               compute=compute)

    @pl.when(i >= n_used)
    def _():
        out_ref[...] = jnp.zeros(out_ref.shape, out_ref.dtype)


def _moe(h_tiles, buf_token, block_expert, n_used, w_gu, b_gu, w_down, b_down):
    e, d, f2 = w_gu.shape
    f = f2 // 2
    tb = MOE_TB
    cap = buf_token.shape[0]
    n_blocks = cap // tb
    tok3 = buf_token.reshape(n_blocks, 1, tb)
    smem_blk = lambda imap: pl.BlockSpec((1, 1, tb), imap, memory_space=pltpu.SMEM)
    grid_spec = pltpu.PrefetchScalarGridSpec(
        num_scalar_prefetch=2,
        grid=(n_blocks,),
        in_specs=[smem_blk(lambda i, be, nu: (i, 0, 0)),
                  smem_blk(lambda i, be, nu: (jnp.minimum(i + 1, n_blocks - 1), 0, 0)),
                  smem_blk(lambda i, be, nu: (jnp.minimum(i + 2, n_blocks - 1), 0, 0)),
                  pl.BlockSpec(memory_space=pl.ANY),
                  pl.BlockSpec((1, d, f2), lambda i, be, nu: (be[i], 0, 0)),
                  pl.BlockSpec((1, 1, f2), lambda i, be, nu: (be[i], 0, 0)),
                  pl.BlockSpec((1, f, d), lambda i, be, nu: (be[i], 0, 0)),
                  pl.BlockSpec((1, 1, d), lambda i, be, nu: (be[i], 0, 0))],
        out_specs=pl.BlockSpec((tb * SUBLANES, LANES), lambda i, be, nu: (i, 0)),
        scratch_shapes=[pltpu.VMEM((tb * SUBLANES, LANES), F32)] * 3
        + [pltpu.VMEM((d, f2), BF16), pltpu.VMEM((f, d), BF16), pltpu.SemaphoreType.DMA((3,))],
    )
    return pl.pallas_call(
        _moe_kernel,
        out_shape=jax.ShapeDtypeStruct((cap * SUBLANES, LANES), F32),
        grid_spec=grid_spec,
        compiler_params=_cparams(("arbitrary",)),
        name="moe",
    )(block_expert, n_used, tok3, tok3, tok3, h_tiles, w_gu, b_gu.reshape(e, 1, f2),
      w_down, b_down.reshape(e, 1, d))


def _combine_kernel(pos0_ref, pos1_ref, pos2_ref, yb_hbm, x1_ref, gates_ref, mod_ref, gfin_ref,
                    out_ref, buf0, buf1, buf2, sem):
    i = pl.program_id(0)
    n = pl.num_programs(0)
    n_rows = buf0.shape[0] // SUBLANES
    tm = n_rows // TOP_K

    @pl.when(i == 0)
    def _():
        _tile_gather_start(pos0_ref, yb_hbm, buf0, sem.at[0], n_rows, unrolled=False)
        _tile_gather_start(pos1_ref, yb_hbm, buf1, sem.at[1], n_rows, unrolled=False)

    def compute(cur):
        g = gates_ref[...]
        gk = [jnp.broadcast_to(g[:, k:k + 1], (tm, LANES)) for k in range(TOP_K)]
        pieces = []
        for s in range(SUBLANES):
            acc = gk[0] * _tile_piece(cur, 0, tm, s)
            for k in range(1, TOP_K):
                acc = acc + gk[k] * _tile_piece(cur, k * tm, tm, s)
            pieces.append(acc)
        x2 = x1_ref[...] + mod_ref[0, 5:6, :] * jnp.concatenate(pieces, axis=1)
        out_ref[...] = _rmsnorm(x2, gfin_ref[...])

    _ring_step(i, n - 1, (buf0, buf1, buf2), sem,
               wait=lambda buf, s: _tile_gather_wait(yb_hbm, buf, s),
               prefetch=lambda buf, s: _tile_gather_start(pos2_ref, yb_hbm, buf, s, n_rows,
                                                          unrolled=True, alternate_priority=True),
               compute=compute)


def _combine(yb, pos, x1_2d, gates_2d, mod, g_final, seq):
    n_tok, d = x1_2d.shape
    tm = min(COMBINE_TM, seq)
    assert seq % tm == 0
    n_tiles = n_tok // tm
    per_batch = seq // tm
    pos3 = pos.reshape(n_tiles, tm, TOP_K).transpose(0, 2, 1).reshape(n_tiles, 1, TOP_K * tm)
    smem_blk = lambda imap: pl.BlockSpec((1, 1, TOP_K * tm), imap, memory_space=pltpu.SMEM)
    buf = pltpu.VMEM((TOP_K * tm * SUBLANES, LANES), F32)
    return pl.pallas_call(
        _combine_kernel,
        out_shape=jax.ShapeDtypeStruct((n_tok, d), F32),
        grid=(n_tiles,),
        in_specs=[smem_blk(lambda i: (i, 0, 0)),
                  smem_blk(lambda i: (jnp.minimum(i + 1, n_tiles - 1), 0, 0)),
                  smem_blk(lambda i: (jnp.minimum(i + 2, n_tiles - 1), 0, 0)),
                  pl.BlockSpec(memory_space=pl.ANY),
                  pl.BlockSpec((tm, d), lambda i: (i, 0)),
                  pl.BlockSpec((tm, LANES), lambda i: (i, 0)),
                  pl.BlockSpec((1, 8, d), lambda i: (i // per_batch, 0, 0)),
                  pl.BlockSpec((1, d), lambda i: (0, 0))],
        out_specs=pl.BlockSpec((tm, d), lambda i: (i, 0)),
        scratch_shapes=[buf, buf, buf, pltpu.SemaphoreType.DMA((3,))],
        compiler_params=_cparams(("arbitrary",)),
        name="combine",
    )(pos3, pos3, pos3, yb, x1_2d, gates_2d, mod, g_final)


def kernel(x, c, ctx, c_ctx, w_mod, b_mod, g_mix, w_in, lam_re, lam_im, log_dt, b_re, b_im,
           c_re, c_im, d_skip, w_glu, conv_w, conv_b, w_ssm_br, w_conv_br, w_o, g_ffn,
           w_router, b_router, w_gu, b_gu, w_down, b_down, g_final):
    depth = w_mod.shape[0]
    assert depth == 1, "single-layer trunk"
    bsz, seq, d = x.shape
    ctx_len = ctx.shape[1]
    d_ssm = d // 2
    assert bsz % 8 == 0 and seq % CHUNK_T == 0 and ctx_len % CHUNK_T == 0 and seq % GRID_W == 0
    assert d == SUBLANES * LANES, "row gathers move one (8,128) f32 tile per token"

    n_cond = -(-(bsz + 1) // 8) * 8
    cond = jnp.zeros((n_cond, d), F32).at[:bsz].set(c).at[bsz].set(c_ctx)
    m = _adaln(cond, w_mod[0], b_mod[0])
    zeros2 = jnp.zeros((n_cond, 2, d), F32)
    mod_all = jnp.concatenate([m.reshape(n_cond, 6, d), zeros2], axis=1)
    mod_x, mod_c = mod_all[:bsz], mod_all[bsz:bsz + 1]

    w_in_bf = w_in[0].astype(BF16)
    w_u, w_rest = w_in_bf[:, :d_ssm], w_in_bf[:, d_ssm:]
    gm = g_mix[0].reshape(1, d)

    perm = _granule_transpose_matrix()
    utx = _front(x, mod_x, gm, w_u, perm)
    utc = _front(ctx, mod_c, gm, w_u, perm)

    smat, toep, cpow, a_t = _s5_matrices(lam_re[0], lam_im[0], log_dt[0], b_re[0], b_im[0],
                                         c_re[0], c_im[0], d_skip[0])
    s_x = _s5_states(utx, smat)
    s_c = _s5_states(utc, smat)
    h_in = _s5_scan(a_t, s_c, s_x, bsz)
    y_chunks = _s5_apply(utx, toep, cpow, h_in)

    pad_r = jnp.zeros((d, LANES - N_EXPERTS), F32)
    w_r = jnp.concatenate([w_router[0], pad_r], axis=1)
    b_r = jnp.concatenate([b_router[0], jnp.zeros((LANES - N_EXPERTS,), F32)]).reshape(1, LANES)
    x1, h, ids, gates, hist = _back(
        x, y_chunks, mod_x, perm, gm, w_rest, conv_w[0], conv_b[0].reshape(1, d_ssm),
        w_glu[0].astype(BF16), w_ssm_br[0].astype(BF16),
        w_conv_br[0].astype(BF16), w_o[0].astype(BF16), g_ffn[0].reshape(1, d), w_r, b_r)

    n_tok = bsz * seq
    ids2 = ids.reshape(n_tok, LANES)
    buf_token, pos, block_expert, n_used = _routing(
        ids2[:, :TOP_K], ids2[:, TOP_K:2 * TOP_K],
        hist[:, 0, :N_EXPERTS].astype(jnp.int32), bsz, seq)
    yb = _moe(h.reshape(n_tok * SUBLANES, LANES), buf_token, block_expert, n_used,
              w_gu[0], b_gu[0], w_down[0], b_down[0])
    out = _combine(yb, pos, x1.reshape(n_tok, d), gates.reshape(n_tok, LANES), mod_x,
                   g_final.reshape(1, d), seq)
    return out.reshape(bsz, seq, d)
```

```python
import functools
import math

import jax
import jax.numpy as jnp
from jax import lax
from jax.experimental import pallas as pl
from jax.experimental.pallas import tpu as pltpu

F32 = jnp.float32
BF16 = jnp.bfloat16
HIGHEST = lax.Precision.HIGHEST

RMS_EPS = 1e-6
GRID_W = 64
SSM_GROUP = 16
SSM_STATE = 64
N_EXPERTS = 32
TOP_K = 4
SWIGLU_LIMIT = 7.0
SWIGLU_ALPHA = 1.702

CHUNK_T = 16
LANES = 128
SUBLANES = 8
V7X_VMEM_LIMIT_BYTES = 56 * 1024 * 1024

S5_TR = 1024
MOE_TB = 512
COMBINE_TM = 256


def _cparams(sem):
    return pltpu.CompilerParams(dimension_semantics=sem,
                                vmem_limit_bytes=V7X_VMEM_LIMIT_BYTES)


def _bdot(a, b):
    return jnp.dot(a, b, preferred_element_type=F32)


def _rmsnorm(xt, g):
    ms = jnp.mean(xt * xt, axis=-1, keepdims=True)
    return xt * lax.rsqrt(ms + RMS_EPS) * g


def _adaln_kernel(c_ref, w_ref, b_ref, o_ref):
    s = jax.nn.silu(c_ref[...])
    o_ref[...] = jnp.dot(s, w_ref[...], precision=HIGHEST,
                         preferred_element_type=F32) + b_ref[...]


def _adaln(cond, w_mod, b_mod):
    r, d = cond.shape
    n = w_mod.shape[1]
    tn = n // 4
    return pl.pallas_call(
        _adaln_kernel,
        out_shape=jax.ShapeDtypeStruct((r, n), F32),
        grid=(n // tn,),
        in_specs=[pl.BlockSpec((r, d), lambda j: (0, 0)),
                  pl.BlockSpec((d, tn), lambda j: (0, j)),
                  pl.BlockSpec((1, tn), lambda j: (0, j))],
        out_specs=pl.BlockSpec((r, tn), lambda j: (0, j)),
        compiler_params=_cparams(("arbitrary",)),
        name="adaln",
    )(cond, w_mod, b_mod.reshape(1, n))


ROW_T = GRID_W
PITCH = ROW_T + 8
N_SLAB = 4
CHUNKS_PER_TILE = ROW_T // CHUNK_T
GRANULES = LANES // SSM_GROUP


def _granule_transpose_matrix():
    n = GRANULES * LANES
    idx = jnp.arange(n)
    j, m, c = idx // LANES, (idx % LANES) // SSM_GROUP, idx % SSM_GROUP
    dst = m * LANES + j * SSM_GROUP + c
    return jnp.zeros((n, n), F32).at[idx, dst].set(1.0).astype(BF16)


def _modulate_tile(xn, mod_ref, shift_row, scale_row):
    d = xn.shape[1]
    x3 = xn.reshape(8, ROW_T, d)
    x3 = x3 * (1.0 + mod_ref[:, scale_row:scale_row + 1, :]) + mod_ref[:, shift_row:shift_row + 1, :]
    return x3.reshape(8 * ROW_T, d)


def _front_kernel(x_ref, mod_ref, g_ref, w_ref, perm_ref, u_ref, us):
    d = x_ref.shape[2]
    xn = _rmsnorm(x_ref[...].reshape(8 * ROW_T, d), g_ref[...])
    hx = _modulate_tile(xn, mod_ref, 0, 1)
    u = _bdot(hx.astype(BF16), w_ref[...])
    for sl in range(N_SLAB):
        for b in range(8):
            us[sl, pl.ds(b * PITCH, ROW_T), :] = u[b * ROW_T:(b + 1) * ROW_T,
                                                  sl * LANES:(sl + 1) * LANES]
    blocks = []
    for sl in range(N_SLAB):
        for th in range(CHUNK_T // GRANULES):
            rows = []
            for kk in range(CHUNKS_PER_TILE):
                t0 = kk * CHUNK_T + th * GRANULES
                rows.append(jnp.concatenate(
                    [us[sl, pl.ds(t0 + j, 8, stride=PITCH), :] for j in range(GRANULES)], axis=1))
            blocks.append(jnp.concatenate(rows, axis=0))
    acat = jnp.concatenate(blocks, axis=0).astype(BF16)
    bmat = _bdot(acat, perm_ref[...])
    nrow = CHUNKS_PER_TILE * 8
    blk = 0
    for sl in range(N_SLAB):
        for th in range(CHUNK_T // GRANULES):
            for m in range(GRANULES):
                u_ref[sl * GRANULES + m, :, th * LANES:(th + 1) * LANES] = (
                    bmat[blk * nrow:(blk + 1) * nrow, m * LANES:(m + 1) * LANES].astype(u_ref.dtype))
            blk += 1


def _front(x, mod, g_mix, w_u, perm):
    b, s, d = x.shape
    n = w_u.shape[1]
    g = n // SSM_GROUP
    assert s % ROW_T == 0 and b % 8 == 0 and n == N_SLAB * LANES
    k = s // CHUNK_T
    tiles = s // ROW_T
    nrow = CHUNKS_PER_TILE * 8
    shared = mod.shape[0] == 1
    mod_spec = (pl.BlockSpec((1, 8, d), lambda i, j: (0, 0, 0)) if shared
                else pl.BlockSpec((8, 8, d), lambda i, j: (i, 0, 0)))
    return pl.pallas_call(
        _front_kernel,
        out_shape=jax.ShapeDtypeStruct((g, (b // 8) * k * 8, CHUNK_T * SSM_GROUP), BF16),
        grid=(b // 8, tiles),
        in_specs=[pl.BlockSpec((8, ROW_T, d), lambda i, j: (i, j, 0)),
                  mod_spec,
                  pl.BlockSpec((1, d), lambda i, j: (0, 0)),
                  pl.BlockSpec((d, n), lambda i, j: (0, 0)),
                  pl.BlockSpec(perm.shape, lambda i, j: (0, 0))],
        out_specs=pl.BlockSpec((g, nrow, CHUNK_T * SSM_GROUP), lambda i, j: (0, i * tiles + j, 0)),
        scratch_shapes=[pltpu.VMEM((N_SLAB, 8 * PITCH, LANES), F32)],
        compiler_params=_cparams(("arbitrary", "arbitrary")),
        name="front",
    )(x, mod, g_mix, w_u, perm)


def _s5_discretise(lam_re, lam_im, log_dt, b_re, b_im):
    dt = jnp.exp(log_dt)[..., None]
    mag = jnp.exp(lam_re * dt)
    a_re, a_im = mag * jnp.cos(lam_im * dt), mag * jnp.sin(lam_im * dt)
    den = lam_re * lam_re + lam_im * lam_im
    q_re = ((a_re - 1) * lam_re + a_im * lam_im) / den
    q_im = (a_im * lam_re - (a_re - 1) * lam_im) / den
    bb_re = q_re[..., None] * b_re - q_im[..., None] * b_im
    bb_im = q_re[..., None] * b_im + q_im[..., None] * b_re
    return a_re, a_im, bb_re, bb_im


def _s5_matrices(lam_re, lam_im, log_dt, b_re, b_im, c_re, c_im, d_skip):
    t = CHUNK_T
    a_re, a_im, bb_re, bb_im = _s5_discretise(lam_re, lam_im, log_dt, b_re, b_im)
    g, p = a_re.shape[1], a_re.shape[2]
    ch = bb_re.shape[-1]
    pw_re, pw_im = [jnp.ones_like(a_re)], [jnp.zeros_like(a_im)]
    for _ in range(t):
        r, i = pw_re[-1], pw_im[-1]
        pw_re.append(r * a_re - i * a_im)
        pw_im.append(r * a_im + i * a_re)
    pw_re, pw_im = jnp.stack(pw_re), jnp.stack(pw_im)
    cr = c_re[None] * pw_re[:, :, :, None, :] - c_im[None] * pw_im[:, :, :, None, :]
    ci = c_re[None] * pw_im[:, :, :, None, :] + c_im[None] * pw_re[:, :, :, None, :]
    kern = (jnp.einsum("dzgcp,zgpe->dzgce", cr, bb_re, precision=HIGHEST)
            - jnp.einsum("dzgcp,zgpe->dzgce", ci, bb_im, precision=HIGHEST))
    s_idx = jnp.arange(t)[:, None]
    t_idx = jnp.arange(t)[None, :]
    dlt = t_idx - s_idx
    kf = kern[jnp.clip(dlt, 0, t - 1), 0]
    kb = kern[jnp.clip(-dlt, 0, t - 1), 1]
    msk_f = (dlt >= 0)[:, :, None, None, None]
    msk_b = (dlt <= 0)[:, :, None, None, None]
    tsum = jnp.where(msk_f, kf, 0.0) + jnp.where(msk_b, kb, 0.0)
    toep = tsum.transpose(2, 0, 4, 1, 3).reshape(g, t * ch, t * ch)
    skip = jnp.tile(d_skip.reshape(g, 1, ch), (1, t, 1)).reshape(g, 1, t * ch)
    toep = toep + jnp.eye(t * ch, dtype=F32)[None] * skip

    pf_re, pf_im = pw_re[t - 1 - jnp.arange(t), 0], pw_im[t - 1 - jnp.arange(t), 0]
    pb_re, pb_im = pw_re[jnp.arange(t), 1], pw_im[jnp.arange(t), 1]

    def in_to_state(q_re, q_im, z):
        bre = bb_re[z].transpose(0, 2, 1)[None]
        bim = bb_im[z].transpose(0, 2, 1)[None]
        s_re = q_re[:, :, None, :] * bre - q_im[:, :, None, :] * bim
        s_im = q_re[:, :, None, :] * bim + q_im[:, :, None, :] * bre
        to_rows = lambda m: m.transpose(1, 0, 2, 3).reshape(g, t * ch, p)
        return to_rows(s_re), to_rows(s_im)

    sf_re, sf_im = in_to_state(pf_re, pf_im, 0)
    sb_re, sb_im = in_to_state(pb_re, pb_im, 1)
    s_all = jnp.stack([sf_re, sf_im, sb_re, sb_im], axis=1)
    s_all = s_all.reshape(g // 2, 2, 4, t * ch, p)
    eye2 = jnp.eye(2, dtype=F32)
    smat = jnp.einsum("ajqrp,jk->ajrqkp", s_all, eye2).reshape(g // 2, 2 * t * ch, 4 * 2 * p)

    def state_to_out(idx, z):
        cre = cr[idx, z].transpose(1, 3, 0, 2).reshape(g, p, t * ch)
        cim = ci[idx, z].transpose(1, 3, 0, 2).reshape(g, p, t * ch)
        return cre, -cim

    cf_re, cf_im = state_to_out(jnp.arange(t) + 1, 0)
    cb_re, cb_im = state_to_out(t - jnp.arange(t), 1)
    c_all = jnp.stack([cf_re, cf_im, cb_re, cb_im], axis=1)
    c_all = c_all.reshape(g // 2, 2, 4, p, t * ch)
    cpow = jnp.einsum("ajqpn,jk->aqjpkn", c_all, eye2).reshape(g // 2, 4 * 2 * p, 2 * t * ch)

    a_t = jnp.stack([pw_re[t, 0], pw_im[t, 0], pw_re[t, 1], pw_im[t, 1]]).reshape(4, g * p)
    return smat.astype(BF16), toep.astype(BF16), cpow.astype(BF16), a_t


def _s5_states_kernel(u_ref, m_ref, fre_ref, fim_ref, bre_ref, bim_ref):
    lhs = jnp.concatenate([u_ref[0], u_ref[1]], axis=1)
    res = _bdot(lhs, m_ref[0])
    fre_ref[...] = res[:, 0 * LANES:1 * LANES]
    fim_ref[...] = res[:, 1 * LANES:2 * LANES]
    bre_ref[...] = res[:, 2 * LANES:3 * LANES]
    bim_ref[...] = res[:, 3 * LANES:4 * LANES]


def _s5_states(ut, smat):
    g, rows, kdim = ut.shape
    tr = min(S5_TR, rows)
    assert rows % tr == 0
    n_state = g * SSM_STATE
    out = jax.ShapeDtypeStruct((rows, n_state), F32)
    ospec = pl.BlockSpec((tr, LANES), lambda a, r: (r, a))
    return pl.pallas_call(
        _s5_states_kernel,
        out_shape=(out, out, out, out),
        grid=(g // 2, rows // tr),
        in_specs=[pl.BlockSpec((2, tr, kdim), lambda a, r: (a, r, 0)),
                  pl.BlockSpec((1, 2 * kdim, 4 * LANES), lambda a, r: (a, 0, 0))],
        out_specs=(ospec, ospec, ospec, ospec),
        compiler_params=_cparams(("arbitrary", "arbitrary")),
        name="s5_states",
    )(ut, smat)


def _s5_scan_kernel(a_ref, cfr_ref, cfi_ref, cbr_ref, cbi_ref,
                    xfr_ref, xfi_ref, xbr_ref, xbi_ref,
                    hfr_ref, hfi_ref, hbr_ref, hbi_ref, *, halves, kc, kx):
    a_fr, a_fi = a_ref[0:1, :], a_ref[1:2, :]
    a_br, a_bi = a_ref[2:3, :], a_ref[3:4, :]

    def rows(half, k, n_chunks, count=1):
        return pl.ds(pl.multiple_of((half * n_chunks + k) * 8, 8 * count), 8 * count)

    def step(h_re, h_im, a_re, a_im, s_re, s_im):
        return (a_re * h_re - a_im * h_im + s_re, a_re * h_im + a_im * h_re + s_im)

    zero = jnp.zeros((8, LANES), F32)

    def ctx_body(i, carry):
        out = []
        for half in range(halves):
            fr, fi, br, bi = carry[half]
            kf, kb = i, kc - 1 - i
            fr, fi = step(fr, fi, a_fr, a_fi, cfr_ref[rows(half, kf, kc), :],
                          cfi_ref[rows(half, kf, kc), :])
            br, bi = step(br, bi, a_br, a_bi, cbr_ref[rows(half, kb, kc), :],
                          cbi_ref[rows(half, kb, kc), :])
            out.append((fr, fi, br, bi))
        return tuple(out)

    carry = lax.fori_loop(0, kc, ctx_body, ((zero,) * 4,) * halves)

    def x_body(i, carry):
        out = []
        for half in range(halves):
            fr, fi, br, bi = carry[half]
            kf = 2 * i
            kb = kx - 2 - 2 * i
            fr1, fi1 = step(fr, fi, a_fr, a_fi, xfr_ref[rows(half, kf, kx), :],
                            xfi_ref[rows(half, kf, kx), :])
            br1, bi1 = step(br, bi, a_br, a_bi, xbr_ref[rows(half, kb + 1, kx), :],
                            xbi_ref[rows(half, kb + 1, kx), :])
            hfr_ref[rows(half, kf, kx, 2), :] = jnp.concatenate([fr, fr1], 0).astype(hfr_ref.dtype)
            hfi_ref[rows(half, kf, kx, 2), :] = jnp.concatenate([fi, fi1], 0).astype(hfi_ref.dtype)
            hbr_ref[rows(half, kb, kx, 2), :] = jnp.concatenate([br1, br], 0).astype(hbr_ref.dtype)
            hbi_ref[rows(half, kb, kx, 2), :] = jnp.concatenate([bi1, bi], 0).astype(hbi_ref.dtype)
            fr, fi = step(fr1, fi1, a_fr, a_fi, xfr_ref[rows(half, kf + 1, kx), :],
                          xfi_ref[rows(half, kf + 1, kx), :])
            br, bi = step(br1, bi1, a_br, a_bi, xbr_ref[rows(half, kb, kx), :],
                          xbi_ref[rows(half, kb, kx), :])
            out.append((fr, fi, br, bi))
        return tuple(out)

    lax.fori_loop(0, kx // 2, x_body, carry)


def _s5_scan(a_t, s_ctx, s_x, batch):
    rows_c, n_state = s_ctx[0].shape
    rows_x = s_x[0].shape[0]
    kc, kx = rows_c // batch, rows_x // batch
    assert kx % 2 == 0
    out = jax.ShapeDtypeStruct((rows_x, n_state), BF16)
    cspec = pl.BlockSpec((rows_c, LANES), lambda j: (0, j))
    xspec = pl.BlockSpec((rows_x, LANES), lambda j: (0, j))
    return pl.pallas_call(
        functools.partial(_s5_scan_kernel, halves=batch // 8, kc=kc, kx=kx),
        out_shape=(out, out, out, out),
        grid=(n_state // LANES,),
        in_specs=[pl.BlockSpec((4, LANES), lambda j: (0, j))] + [cspec] * 4 + [xspec] * 4,
        out_specs=(xspec, xspec, xspec, xspec),
        compiler_params=_cparams(("arbitrary",)),
        name="s5_scan",
    )(a_t, *s_ctx, *s_x)


def _s5_apply_kernel(u_ref, t_ref, c_ref, hfr_ref, hfi_ref, hbr_ref, hbi_ref, y_ref):
    hcat = jnp.concatenate([hfr_ref[...], hfi_ref[...], hbr_ref[...], hbi_ref[...]], axis=1)
    yst = _bdot(hcat, c_ref[0])
    n = u_ref.shape[2]
    y_ref[0] = (_bdot(u_ref[0], t_ref[0]) + yst[:, :n]).astype(y_ref.dtype)
    y_ref[1] = (_bdot(u_ref[1], t_ref[1]) + yst[:, n:]).astype(y_ref.dtype)


def _s5_apply(ut, toep, cpow, h_in):
    g, rows, kdim = ut.shape
    tr = min(S5_TR, rows)
    hspec = pl.BlockSpec((tr, LANES), lambda a, r: (r, a))
    return pl.pallas_call(
        _s5_apply_kernel,
        out_shape=jax.ShapeDtypeStruct((g, rows, kdim), BF16),
        grid=(g // 2, rows // tr),
        in_specs=[pl.BlockSpec((2, tr, kdim), lambda a, r: (a, r, 0)),
                  pl.BlockSpec((2, kdim, kdim), lambda a, r: (a, 0, 0)),
                  pl.BlockSpec((1, 4 * LANES, 2 * kdim), lambda a, r: (a, 0, 0))] + [hspec] * 4,
        out_specs=pl.BlockSpec((2, tr, kdim), lambda a, r: (a, r, 0)),
        compiler_params=_cparams(("arbitrary", "arbitrary")),
        name="s5_apply",
    )(ut, toep, cpow, *h_in)


def _chunk_rows_to_tokens(y_ref, perm_ref, ys):
    nrow = CHUNKS_PER_TILE * 8
    blocks = []
    for sl in range(N_SLAB):
        for th in range(CHUNK_T // GRANULES):
            blocks.append(jnp.concatenate(
                [y_ref[sl * GRANULES + m, :, th * LANES:(th + 1) * LANES] for m in range(GRANULES)],
                axis=1))
    acat = _bdot(jnp.concatenate(blocks, axis=0), perm_ref[...])
    blk = 0
    for sl in range(N_SLAB):
        for th in range(CHUNK_T // GRANULES):
            for kk in range(CHUNKS_PER_TILE):
                for j in range(GRANULES):
                    t = kk * CHUNK_T + th * GRANULES + j
                    ys[sl, pl.ds(t, 8, stride=PITCH), :] = acat[blk * nrow + kk * 8:
                                                               blk * nrow + (kk + 1) * 8,
                                                               j * LANES:(j + 1) * LANES]
            blk += 1
    return jnp.concatenate(
        [jnp.concatenate([ys[sl, pl.ds(b * PITCH, ROW_T), :] for sl in range(N_SLAB)], axis=1)
         for b in range(8)], axis=0)


def _back_kernel(x_ref, y_ref, mod_ref, modp_ref, perm_ref, gmix_ref, win_ref, convw_ref,
                 convb_ref, wglu_ref, wsbr_ref, wcbr_ref, wo_ref, gffn_ref, wrhi_ref, wrlo_ref,
                 br_ref, ltri_ref, x1_ref, h_ref, ids_ref, gates_ref, hist_ref,
                 ys_scr, x_carry, merged_carry):
    d_model = x_ref.shape[2]
    tm = 8 * ROW_T

    @pl.when(pl.program_id(0) == 0)
    def _():
        x_carry[...] = jnp.zeros(x_carry.shape, x_carry.dtype)
        merged_carry[...] = jnp.zeros(merged_carry.shape, merged_carry.dtype)

    _back_tail(x_carry[...], merged_carry[...], modp_ref, wo_ref, gffn_ref, wrhi_ref, wrlo_ref,
               br_ref, ltri_ref, x1_ref, h_ref, ids_ref, gates_ref, hist_ref)
    xt = x_ref[...].reshape(tm, d_model)
    x_carry[...] = xt
    merged_carry[...] = _back_mixer(xt, y_ref, mod_ref, perm_ref, gmix_ref, win_ref, convw_ref,
                                    convb_ref, wglu_ref, wsbr_ref, wcbr_ref, ys_scr)


def _back_mixer(xt, y_ref, mod_ref, perm_ref, gmix_ref, win_ref, convw_ref, convb_ref,
                wglu_ref, wsbr_ref, wcbr_ref, ys_scr):
    tm, d_model = xt.shape
    d_ssm = N_SLAB * LANES
    xn = _rmsnorm(xt, gmix_ref[...])
    hx = _modulate_tile(xn, mod_ref, 0, 1).astype(BF16)
    p = _bdot(hx, win_ref[...])
    v = p[:, 0:d_ssm]
    gate_b = p[:, d_ssm:2 * d_ssm]
    gate_c = p[:, 2 * d_ssm:3 * d_ssm]
    g_s = p[:, 3 * d_ssm:3 * d_ssm + d_model]
    g_c = p[:, 3 * d_ssm + d_model:]

    z = gate_c * v
    col = lax.broadcasted_iota(jnp.int32, z.shape, 0) % GRID_W
    z_prev = jnp.where(col == 0, 0.0, pltpu.roll(z, 1, 0))
    z_next = jnp.where(col == GRID_W - 1, 0.0, pltpu.roll(z, tm - 1, 0))
    conv = (z_prev * convw_ref[0:1, :] + z * convw_ref[1:2, :]
            + z_next * convw_ref[2:3, :] + convb_ref[...])
    y_conv = gate_b * conv

    ys = jax.nn.gelu(_chunk_rows_to_tokens(y_ref, perm_ref, ys_scr))
    ys = ys * jax.nn.sigmoid(_bdot(ys.astype(BF16), wglu_ref[...]))

    y_s = _bdot(ys.astype(BF16), wsbr_ref[...])
    y_c = _bdot(y_conv.astype(BF16), wcbr_ref[...])
    return (jax.nn.sigmoid(g_s) * y_s + jax.nn.sigmoid(g_c) * y_c).astype(BF16)


def _back_tail(xt, merged, mod_ref, wo_ref, gffn_ref, wrhi_ref, wrlo_ref, br_ref, ltri_ref,
               x1_ref, h_ref, ids_ref, gates_ref, hist_ref):
    tm, d_model = xt.shape
    mo = _bdot(merged, wo_ref[...])
    x1 = xt + (mo.reshape(8, ROW_T, d_model) * mod_ref[:, 2:3, :]).reshape(tm, d_model)
    x1_ref[...] = x1.reshape(8, ROW_T, d_model)

    hn = _modulate_tile(_rmsnorm(x1, gffn_ref[...]), mod_ref, 3, 4)
    for b in range(8):
        for s in range(SUBLANES):
            h_ref[b, pl.ds(s, ROW_T, stride=SUBLANES), :] = hn[b * ROW_T:(b + 1) * ROW_T,
                                                              s * LANES:(s + 1) * LANES]

    hn_hi = hn.astype(BF16)
    hn_lo = (hn - hn_hi.astype(F32)).astype(BF16)
    logits = (_bdot(hn_hi, wrhi_ref[...]) + _bdot(hn_lo, wrhi_ref[...])
              + _bdot(hn_hi, wrlo_ref[...]) + br_ref[...])
    lane = lax.broadcasted_iota(jnp.int32, logits.shape, 1)
    neg = jnp.float32(-jnp.inf)
    cur = jnp.where(lane < N_EXPERTS, logits, neg)
    vals, idxs = [], []
    for _ in range(TOP_K):
        mk = jnp.max(cur, axis=-1, keepdims=True)
        ik = jnp.min(jnp.where(cur == mk, lane, LANES), axis=-1, keepdims=True)
        vals.append(mk)
        idxs.append(ik)
        cur = jnp.where(lane == ik, neg, cur)
    exps = [jnp.exp(vk - vals[0]) for vk in vals]
    denom = exps[0] + exps[1] + exps[2] + exps[3]
    ids = jnp.zeros(logits.shape, jnp.int32)
    gates = jnp.zeros(logits.shape, F32)
    onehot = jnp.zeros(logits.shape, F32)
    for k in range(TOP_K):
        onehot = onehot + (lane == idxs[k]).astype(F32)
    before = _bdot(ltri_ref[...], onehot.astype(BF16))
    for k in range(TOP_K):
        rank_k = jnp.sum(jnp.where(lane == idxs[k], before, 0.0), axis=-1, keepdims=True)
        ids = jnp.where(lane == k, idxs[k], ids)
        ids = jnp.where(lane == TOP_K + k, rank_k.astype(jnp.int32), ids)
        gates = jnp.where(lane == k, exps[k] / denom, gates)
    ids_ref[...] = ids.reshape(8, ROW_T, LANES)
    gates_ref[...] = gates.reshape(8, ROW_T, LANES)
    hist_ref[0] = jnp.broadcast_to(jnp.sum(onehot, axis=0, keepdims=True), (8, LANES))


def _back(x, y_chunks, mod, perm, g_mix, w_rest, conv_w, conv_b, w_glu, w_ssm_br,
          w_conv_br, w_o, g_ffn, w_router, b_router):
    b, s, d = x.shape
    g, _, kdim = y_chunks.shape
    tiles = s // ROW_T
    n_tiles = (b // 8) * tiles
    nrow = CHUNKS_PER_TILE * 8
    cur = lambda i: jnp.minimum(i, n_tiles - 1)
    prev = lambda i: jnp.maximum(i - 1, 0)
    tok_prev = lambda n, rows=ROW_T: pl.BlockSpec(
        (8, rows, n), lambda i: (prev(i) // tiles, prev(i) % tiles, 0))

    def const(arr):
        nd = arr.ndim
        return pl.BlockSpec(arr.shape, lambda i: (0,) * nd, pipeline_mode=pl.Buffered(1))

    w_r_hi = w_router.astype(BF16)
    w_r_lo = (w_router - w_r_hi.astype(F32)).astype(BF16)
    row = jnp.arange(8 * ROW_T)
    ltri = (row[:, None] > row[None, :]).astype(BF16)
    params = (perm, g_mix, w_rest, conv_w, conv_b, w_glu, w_ssm_br, w_conv_br, w_o, g_ffn,
              w_r_hi, w_r_lo, b_router, ltri)
    return pl.pallas_call(
        _back_kernel,
        out_shape=(jax.ShapeDtypeStruct((b, s, d), F32),
                   jax.ShapeDtypeStruct((b, s * SUBLANES, LANES), F32),
                   jax.ShapeDtypeStruct((b, s, LANES), jnp.int32),
                   jax.ShapeDtypeStruct((b, s, LANES), F32),
                   jax.ShapeDtypeStruct((n_tiles, 8, LANES), F32)),
        grid=(n_tiles + 1,),
        in_specs=[pl.BlockSpec((8, ROW_T, d), lambda i: (cur(i) // tiles, cur(i) % tiles, 0)),
                  pl.BlockSpec((g, nrow, kdim), lambda i: (0, cur(i), 0)),
                  pl.BlockSpec((8, 8, d), lambda i: (cur(i) // tiles, 0, 0)),
                  pl.BlockSpec((8, 8, d), lambda i: (prev(i) // tiles, 0, 0))]
        + [const(a) for a in params],
        out_specs=(tok_prev(d), tok_prev(LANES, ROW_T * SUBLANES), tok_prev(LANES),
                   tok_prev(LANES), pl.BlockSpec((1, 8, LANES), lambda i: (prev(i), 0, 0))),
        scratch_shapes=[pltpu.VMEM((N_SLAB, 8 * PITCH, LANES), F32),
                        pltpu.VMEM((8 * ROW_T, d), F32), pltpu.VMEM((8 * ROW_T, d), BF16)],
        compiler_params=_cparams(("arbitrary",)),
        name="back",
    )(x, y_chunks, mod, mod, *params)


def _routing(ids, ranks, tile_hist, bsz, seq):
    tb = MOE_TB
    n_tok = ids.shape[0]
    n_slot = n_tok * TOP_K
    tile_before = jnp.cumsum(tile_hist, axis=0) - tile_hist
    counts = jnp.sum(tile_hist, axis=0)
    padded = (counts + tb - 1) // tb * tb
    pad_end = jnp.cumsum(padded)
    pad_start = pad_end - padded
    start = jnp.cumsum(counts) - counts
    halves, tiles = bsz // 8, seq // ROW_T
    base = (pad_start[None, :] + tile_before).reshape(halves, 1, tiles, 1, N_EXPERTS)
    base = jnp.broadcast_to(base, (halves, 8, tiles, ROW_T, N_EXPERTS)).reshape(n_tok, N_EXPERTS)
    onehot = ids[:, :, None] == jnp.arange(N_EXPERTS, dtype=jnp.int32)[None, None, :]
    pos = (jnp.sum(jnp.where(onehot, base[:, None, :], 0), axis=-1) + ranks).astype(jnp.int32)
    token = jnp.broadcast_to(jnp.arange(n_tok, dtype=jnp.int32)[:, None], (n_tok, TOP_K))
    _, order_token = lax.sort_key_val(pos.reshape(-1), token.reshape(-1))
    n_blocks = n_slot // tb + N_EXPERTS
    cap = n_blocks * tb
    block_first_row = jnp.arange(n_blocks, dtype=jnp.int32) * tb
    block_expert = jnp.minimum(
        jnp.sum(pad_end[None, :] <= block_first_row[:, None], axis=1, dtype=jnp.int32),
        N_EXPERTS - 1)
    within = (block_first_row - pad_start[block_expert])[:, None] + jnp.arange(tb, dtype=jnp.int32)
    valid = within < counts[block_expert][:, None]
    src = jnp.clip(start[block_expert][:, None] + within, 0, n_slot - 1)
    buf_token = jnp.where(valid, order_token[src], 0).astype(jnp.int32).reshape(cap)
    n_used = (pad_end[-1] // tb).astype(jnp.int32).reshape(1)
    return buf_token, pos, block_expert, n_used


def _to_tiles(ref, val):
    rows = val.shape[0]
    for s in range(SUBLANES):
        ref[pl.ds(s, rows, stride=SUBLANES), :] = val[:, s * LANES:(s + 1) * LANES]


def _tile_piece(ref, first_row, rows, s):
    return ref[pl.ds(first_row * SUBLANES + s, rows, stride=SUBLANES), :]


def _tile_gather_start(idx_ref, src_hbm, dst, sem, n_rows, unrolled, alternate_priority=False,
                       dst_row=lambda r: r):
    def copy(r, t):
        return pltpu.make_async_copy(
            src_hbm.at[pl.ds(pl.multiple_of(t * SUBLANES, SUBLANES), SUBLANES), :],
            dst.at[pl.ds(pl.multiple_of(dst_row(r) * SUBLANES, SUBLANES), SUBLANES), :], sem)

    if unrolled:
        for r in range(n_rows):
            copy(r, idx_ref[0, 0, r]).start(priority=(r % 2) if alternate_priority else 0)
    else:
        def body(r, carry):
            copy(r, idx_ref[0, 0, r]).start()
            return carry
        lax.fori_loop(0, n_rows, body, 0, unroll=8)


def _tile_gather_wait(src_hbm, dst, sem):
    pltpu.make_async_copy(src_hbm.at[pl.ds(0, dst.shape[0]), :], dst, sem).wait()


def _ring_step(i, last, bufs, sem, wait, prefetch, compute):
    n = len(bufs)
    for p in range(n):
        def branch(p=p):
            wait(bufs[p], sem.at[p])
            q = (p + n - 1) % n
            prefetch(bufs[q], sem.at[q])
            compute(bufs[p])

            @pl.when(i == last)
            def _():
                for r in range(1, n):
                    wait(bufs[(p + r) % n], sem.at[(p + r) % n])

        pl.when((i <= last) & (lax.rem(i, n) == p))(branch)


def _cast_rows(src_ref, dst_ref, chunk):
    def body(c, carry):
        rows = pl.ds(pl.multiple_of(c * chunk, chunk), chunk)
        dst_ref[rows, :] = src_ref[0, rows, :].astype(dst_ref.dtype)
        return carry
    lax.fori_loop(0, dst_ref.shape[0] // chunk, body, 0)


def _moe_kernel(be_ref, nu_ref, tok0_ref, tok1_ref, tok2_ref, h_hbm, wgu_ref, bgu_ref, wd_ref,
                bd_ref, out_ref, xbuf0, xbuf1, xbuf2, wgu_bf, wd_bf, sem):
    i = pl.program_id(0)
    n_used = nu_ref[0]
    tb = xbuf0.shape[0] // SUBLANES
    f = wd_ref.shape[1]

    @pl.when(i == 0)
    def _():
        _tile_gather_start(tok0_ref, h_hbm, xbuf0, sem.at[0], tb, unrolled=False)
        _tile_gather_start(tok1_ref, h_hbm, xbuf1, sem.at[1], tb, unrolled=False)

    new_expert = (i == 0) | (be_ref[i] != be_ref[jnp.maximum(i - 1, 0)])

    @pl.when(new_expert & (i < n_used))
    def _():
        _cast_rows(wgu_ref, wgu_bf, 128)
        _cast_rows(wd_ref, wd_bf, 128)

    def compute(xcur):
        xe = jnp.concatenate([_tile_piece(xcur, 0, tb, s) for s in range(SUBLANES)],
                             axis=1).astype(BF16)
        gu = _bdot(xe, wgu_bf[...]) + bgu_ref[0]
        gt = jnp.minimum(gu[:, :f], SWIGLU_LIMIT)
        up = jnp.clip(gu[:, f:], -SWIGLU_LIMIT, SWIGLU_LIMIT)
        act = gt * jax.nn.sigmoid(SWIGLU_ALPHA * gt) * (up + 1.0)
        _to_tiles(out_ref, _bdot(act.astype(BF16), wd_bf[...]) + bd_ref[0])

    _ring_step(i, n_used - 1, (xbuf0, xbuf1, xbuf2), sem,
               wait=lambda buf, s: _tile_gather_wait(h_hbm, buf, s),
               prefetch=lambda buf, s: _tile_gather_start(tok2_ref, h_hbm, buf, s, tb,
                                                          unrolled=True),
               compute=compute)

    @pl.when(i >= n_used)
    def _():
        out_ref[...] = jnp.zeros(out_ref.shape, out_ref.dtype)


def _moe(h_tiles, buf_token, block_expert, n_used, w_gu, b_gu, w_down, b_down):
    e, d, f2 = w_gu.shape
    f = f2 // 2
    tb = MOE_TB
    cap = buf_token.shape[0]
    n_blocks = cap // tb
    tok3 = buf_token.reshape(n_blocks, 1, tb)
    smem_blk = lambda imap: pl.BlockSpec((1, 1, tb), imap, memory_space=pltpu.SMEM)
    grid_spec = pltpu.PrefetchScalarGridSpec(
        num_scalar_prefetch=2,
        grid=(n_blocks,),
        in_specs=[smem_blk(lambda i, be, nu: (i, 0, 0)),
                  smem_blk(lambda i, be, nu: (jnp.minimum(i + 1, n_blocks - 1), 0, 0)),
                  smem_blk(lambda i, be, nu: (jnp.minimum(i + 2, n_blocks - 1), 0, 0)),
                  pl.BlockSpec(memory_space=pl.ANY),
                  pl.BlockSpec((1, d, f2), lambda i, be, nu: (be[i], 0, 0)),
                  pl.BlockSpec((1, 1, f2), lambda i, be, nu: (be[i], 0, 0)),
                  pl.BlockSpec((1, f, d), lambda i, be, nu: (be[i], 0, 0)),
                  pl.BlockSpec((1, 1, d), lambda i, be, nu: (be[i], 0, 0))],
        out_specs=pl.BlockSpec((tb * SUBLANES, LANES), lambda i, be, nu: (i, 0)),
        scratch_shapes=[pltpu.VMEM((tb * SUBLANES, LANES), F32)] * 3
        + [pltpu.VMEM((d, f2), BF16), pltpu.VMEM((f, d), BF16), pltpu.SemaphoreType.DMA((3,))],
    )
    return pl.pallas_call(
        _moe_kernel,
        out_shape=jax.ShapeDtypeStruct((cap * SUBLANES, LANES), F32),
        grid_spec=grid_spec,
        compiler_params=_cparams(("arbitrary",)),
        name="moe",
    )(block_expert, n_used, tok3, tok3, tok3, h_tiles, w_gu, b_gu.reshape(e, 1, f2),
      w_down, b_down.reshape(e, 1, d))


def _combine_kernel(pos0_ref, pos1_ref, pos2_ref, yb_hbm, x1_ref, gates_ref, mod_ref, gfin_ref,
                    out_ref, buf0, buf1, buf2, sem):
    i = pl.program_id(0)
    n = pl.num_programs(0)
    n_rows = buf0.shape[0] // SUBLANES
    tm = n_rows // TOP_K

    slot_major = lambda r: (r % TOP_K) * tm + r // TOP_K

    @pl.when(i == 0)
    def _():
        _tile_gather_start(pos0_ref, yb_hbm, buf0, sem.at[0], n_rows, unrolled=False,
                           dst_row=slot_major)
        _tile_gather_start(pos1_ref, yb_hbm, buf1, sem.at[1], n_rows, unrolled=False,
                           dst_row=slot_major)

    def compute(cur):
        g = gates_ref[...]
        gk = [jnp.broadcast_to(g[:, k:k + 1], (tm, LANES)) for k in range(TOP_K)]
        pieces = []
        for s in range(SUBLANES):
            acc = gk[0] * _tile_piece(cur, 0, tm, s)
            for k in range(1, TOP_K):
                acc = acc + gk[k] * _tile_piece(cur, k * tm, tm, s)
            pieces.append(acc)
        x2 = x1_ref[...] + mod_ref[0, 5:6, :] * jnp.concatenate(pieces, axis=1)
        out_ref[...] = _rmsnorm(x2, gfin_ref[...])

    _ring_step(i, n - 1, (buf0, buf1, buf2), sem,
               wait=lambda buf, s: _tile_gather_wait(yb_hbm, buf, s),
               prefetch=lambda buf, s: _tile_gather_start(pos2_ref, yb_hbm, buf, s, n_rows,
                                                          unrolled=True, alternate_priority=True,
                                                          dst_row=slot_major),
               compute=compute)


def _combine(yb, pos, x1_2d, gates_2d, mod, g_final, seq):
    n_tok, d = x1_2d.shape
    tm = min(COMBINE_TM, seq)
    assert seq % tm == 0
    n_tiles = n_tok // tm
    per_batch = seq // tm
    pos3 = pos.reshape(n_tiles, 1, tm * TOP_K)
    smem_blk = lambda imap: pl.BlockSpec((1, 1, TOP_K * tm), imap, memory_space=pltpu.SMEM)
    buf = pltpu.VMEM((TOP_K * tm * SUBLANES, LANES), F32)
    return pl.pallas_call(
        _combine_kernel,
        out_shape=jax.ShapeDtypeStruct((n_tok, d), F32),
        grid=(n_tiles,),
        in_specs=[smem_blk(lambda i: (i, 0, 0)),
                  smem_blk(lambda i: (jnp.minimum(i + 1, n_tiles - 1), 0, 0)),
                  smem_blk(lambda i: (jnp.minimum(i + 2, n_tiles - 1), 0, 0)),
                  pl.BlockSpec(memory_space=pl.ANY),
                  pl.BlockSpec((tm, d), lambda i: (i, 0)),
                  pl.BlockSpec((tm, LANES), lambda i: (i, 0)),
                  pl.BlockSpec((1, 8, d), lambda i: (i // per_batch, 0, 0)),
                  pl.BlockSpec((1, d), lambda i: (0, 0))],
        out_specs=pl.BlockSpec((tm, d), lambda i: (i, 0)),
        scratch_shapes=[buf, buf, buf, pltpu.SemaphoreType.DMA((3,))],
        compiler_params=_cparams(("arbitrary",)),
        name="combine",
    )(pos3, pos3, pos3, yb, x1_2d, gates_2d, mod, g_final)


def kernel(x, c, ctx, c_ctx, w_mod, b_mod, g_mix, w_in, lam_re, lam_im, log_dt, b_re, b_im,
           c_re, c_im, d_skip, w_glu, conv_w, conv_b, w_ssm_br, w_conv_br, w_o, g_ffn,
           w_router, b_router, w_gu, b_gu, w_down, b_down, g_final):
    depth = w_mod.shape[0]
    assert depth == 1, "single-layer trunk"
    bsz, seq, d = x.shape
    ctx_len = ctx.shape[1]
    d_ssm = d // 2
    assert bsz % 8 == 0 and seq % CHUNK_T == 0 and ctx_len % CHUNK_T == 0 and seq % GRID_W == 0
    assert d == SUBLANES * LANES, "row gathers move one (8,128) f32 tile per token"

    n_cond = -(-(bsz + 1) // 8) * 8
    cond = jnp.zeros((n_cond, d), F32).at[:bsz].set(c).at[bsz].set(c_ctx)
    m = _adaln(cond, w_mod[0], b_mod[0])
    zeros2 = jnp.zeros((n_cond, 2, d), F32)
    mod_all = jnp.concatenate([m.reshape(n_cond, 6, d), zeros2], axis=1)
    mod_x, mod_c = mod_all[:bsz], mod_all[bsz:bsz + 1]

    w_in_bf = w_in[0].astype(BF16)
    w_u, w_rest = w_in_bf[:, :d_ssm], w_in_bf[:, d_ssm:]
    gm = g_mix[0].reshape(1, d)

    perm = _granule_transpose_matrix()
    utx = _front(x, mod_x, gm, w_u, perm)
    utc = _front(ctx, mod_c, gm, w_u, perm)

    smat, toep, cpow, a_t = _s5_matrices(lam_re[0], lam_im[0], log_dt[0], b_re[0], b_im[0],
                                         c_re[0], c_im[0], d_skip[0])
    s_x = _s5_states(utx, smat)
    s_c = _s5_states(utc, smat)
    h_in = _s5_scan(a_t, s_c, s_x, bsz)
    y_chunks = _s5_apply(utx, toep, cpow, h_in)

    pad_r = jnp.zeros((d, LANES - N_EXPERTS), F32)
    w_r = jnp.concatenate([w_router[0], pad_r], axis=1)
    b_r = jnp.concatenate([b_router[0], jnp.zeros((LANES - N_EXPERTS,), F32)]).reshape(1, LANES)
    x1, h, ids, gates, hist = _back(
        x, y_chunks, mod_x, perm, gm, w_rest, conv_w[0], conv_b[0].reshape(1, d_ssm),
        w_glu[0].astype(BF16), w_ssm_br[0].astype(BF16),
        w_conv_br[0].astype(BF16), w_o[0].astype(BF16), g_ffn[0].reshape(1, d), w_r, b_r)

    n_tok = bsz * seq
    ids2 = ids.reshape(n_tok, LANES)
    buf_token, pos, block_expert, n_used = _routing(
        ids2[:, :TOP_K], ids2[:, TOP_K:2 * TOP_K],
        hist[:, 0, :N_EXPERTS].astype(jnp.int32), bsz, seq)
    yb = _moe(h.reshape(n_tok * SUBLANES, LANES), buf_token, block_expert, n_used,
              w_gu[0], b_gu[0], w_down[0], b_down[0])
    out = _combine(yb, pos, x1.reshape(n_tok, d), gates.reshape(n_tok, LANES), mod_x,
                   g_final.reshape(1, d), seq)
    return out.reshape(bsz, seq, d)
```

```python
import functools
import math

import jax
import jax.numpy as jnp
from jax import lax
from jax.experimental import pallas as pl
from jax.experimental.pallas import tpu as pltpu

F32 = jnp.float32
BF16 = jnp.bfloat16
HIGHEST = lax.Precision.HIGHEST

RMS_EPS = 1e-6
GRID_W = 64
SSM_GROUP = 16
SSM_STATE = 64
N_EXPERTS = 32
TOP_K = 4
SWIGLU_LIMIT = 7.0
SWIGLU_ALPHA = 1.702

CHUNK_T = 16
LANES = 128
SUBLANES = 8
V7X_VMEM_LIMIT_BYTES = 56 * 1024 * 1024

S5_TR = 1024
MOE_TB = 512
COMBINE_TM = 256


def _cparams(sem):
    return pltpu.CompilerParams(dimension_semantics=sem,
                                vmem_limit_bytes=V7X_VMEM_LIMIT_BYTES)


def _bdot(a, b):
    return jnp.dot(a, b, preferred_element_type=F32)


def _rmsnorm(xt, g):
    ms = jnp.mean(xt * xt, axis=-1, keepdims=True)
    return xt * lax.rsqrt(ms + RMS_EPS) * g


def _adaln_kernel(c_ref, w_ref, b_ref, o_ref):
    s = jax.nn.silu(c_ref[...])
    o_ref[...] = jnp.dot(s, w_ref[...], precision=HIGHEST,
                         preferred_element_type=F32) + b_ref[...]


def _adaln(cond, w_mod, b_mod):
    r, d = cond.shape
    n = w_mod.shape[1]
    tn = n // 4
    return pl.pallas_call(
        _adaln_kernel,
        out_shape=jax.ShapeDtypeStruct((r, n), F32),
        grid=(n // tn,),
        in_specs=[pl.BlockSpec((r, d), lambda j: (0, 0)),
                  pl.BlockSpec((d, tn), lambda j: (0, j)),
                  pl.BlockSpec((1, tn), lambda j: (0, j))],
        out_specs=pl.BlockSpec((r, tn), lambda j: (0, j)),
        compiler_params=_cparams(("arbitrary",)),
        name="adaln",
    )(cond, w_mod, b_mod.reshape(1, n))


ROW_T = GRID_W
PITCH = ROW_T + 8
N_SLAB = 4
CHUNKS_PER_TILE = ROW_T // CHUNK_T
GRANULES = LANES // SSM_GROUP


def _granule_transpose_matrix():
    n = GRANULES * LANES
    idx = jnp.arange(n)
    j, m, c = idx // LANES, (idx % LANES) // SSM_GROUP, idx % SSM_GROUP
    dst = m * LANES + j * SSM_GROUP + c
    return jnp.zeros((n, n), F32).at[idx, dst].set(1.0).astype(BF16)


def _modulate_tile(xn, mod_ref, shift_row, scale_row):
    d = xn.shape[1]
    x3 = xn.reshape(8, ROW_T, d)
    x3 = x3 * (1.0 + mod_ref[:, scale_row:scale_row + 1, :]) + mod_ref[:, shift_row:shift_row + 1, :]
    return x3.reshape(8 * ROW_T, d)


def _front_kernel(x_ref, mod_ref, g_ref, w_ref, perm_ref, u_ref, us):
    d = x_ref.shape[2]
    xn = _rmsnorm(x_ref[...].reshape(8 * ROW_T, d), g_ref[...])
    hx = _modulate_tile(xn, mod_ref, 0, 1)
    u = _bdot(hx.astype(BF16), w_ref[...])
    for sl in range(N_SLAB):
        for b in range(8):
            us[sl, pl.ds(b * PITCH, ROW_T), :] = u[b * ROW_T:(b + 1) * ROW_T,
                                                  sl * LANES:(sl + 1) * LANES]
    blocks = []
    for sl in range(N_SLAB):
        for th in range(CHUNK_T // GRANULES):
            rows = []
            for kk in range(CHUNKS_PER_TILE):
                t0 = kk * CHUNK_T + th * GRANULES
                rows.append(jnp.concatenate(
                    [us[sl, pl.ds(t0 + j, 8, stride=PITCH), :] for j in range(GRANULES)], axis=1))
            blocks.append(jnp.concatenate(rows, axis=0))
    acat = jnp.concatenate(blocks, axis=0).astype(BF16)
    bmat = _bdot(acat, perm_ref[...])
    nrow = CHUNKS_PER_TILE * 8
    blk = 0
    for sl in range(N_SLAB):
        for th in range(CHUNK_T // GRANULES):
            for m in range(GRANULES):
                u_ref[sl * GRANULES + m, :, th * LANES:(th + 1) * LANES] = (
                    bmat[blk * nrow:(blk + 1) * nrow, m * LANES:(m + 1) * LANES].astype(u_ref.dtype))
            blk += 1


def _front(x, mod, g_mix, w_u, perm):
    b, s, d = x.shape
    n = w_u.shape[1]
    g = n // SSM_GROUP
    assert s % ROW_T == 0 and b % 8 == 0 and n == N_SLAB * LANES
    k = s // CHUNK_T
    tiles = s // ROW_T
    nrow = CHUNKS_PER_TILE * 8
    shared = mod.shape[0] == 1
    mod_spec = (pl.BlockSpec((1, 8, d), lambda i, j: (0, 0, 0)) if shared
                else pl.BlockSpec((8, 8, d), lambda i, j: (i, 0, 0)))
    return pl.pallas_call(
        _front_kernel,
        out_shape=jax.ShapeDtypeStruct((g, (b // 8) * k * 8, CHUNK_T * SSM_GROUP), BF16),
        grid=(b // 8, tiles),
        in_specs=[pl.BlockSpec((8, ROW_T, d), lambda i, j: (i, j, 0)),
                  mod_spec,
                  pl.BlockSpec((1, d), lambda i, j: (0, 0)),
                  pl.BlockSpec((d, n), lambda i, j: (0, 0)),
                  pl.BlockSpec(perm.shape, lambda i, j: (0, 0))],
        out_specs=pl.BlockSpec((g, nrow, CHUNK_T * SSM_GROUP), lambda i, j: (0, i * tiles + j, 0)),
        scratch_shapes=[pltpu.VMEM((N_SLAB, 8 * PITCH, LANES), F32)],
        compiler_params=_cparams(("arbitrary", "arbitrary")),
        name="front",
    )(x, mod, g_mix, w_u, perm)


P_LANES = 3 * LANES


def _s5_params_kernel(lam_ref, bt_ref, ct_ref, dsk_ref, rep_ref,
                      toep_ref, smat_ref, cpow_ref, at_ref):
    t, ch, p = CHUNK_T, SSM_GROUP, SSM_STATE
    width = t * ch
    f32dot = lambda a, b: jnp.dot(a, b, precision=HIGHEST, preferred_element_type=F32)
    eye = (lax.broadcasted_iota(jnp.int32, (p, p), 0) == lax.broadcasted_iota(jnp.int32, (p, p), 1))
    to_col = lambda row: jnp.sum(jnp.where(eye, jnp.broadcast_to(row, (p, p)), 0.0),
                                 axis=1, keepdims=True)
    blk = lax.broadcasted_iota(jnp.int32, (p, P_LANES), 1) // ch
    row16 = lax.broadcasted_iota(jnp.int32, (ch, width), 0)
    lane16 = lax.broadcasted_iota(jnp.int32, (ch, width), 1)

    smats, cpows, ats = [[], []], [[], []], []
    for j in range(2):
        kt, at_dir = [], []
        for z in range(2):
            lr = lam_ref[j, 4 * z:4 * z + 1, :]
            li = lam_ref[j, 4 * z + 1:4 * z + 2, :]
            dt = jnp.exp(lam_ref[j, 4 * z + 2:4 * z + 3, :])
            mag = jnp.exp(lr * dt)
            ar, ai = mag * jnp.cos(li * dt), mag * jnp.sin(li * dt)
            den = lr * lr + li * li
            qr = ((ar - 1.0) * lr + ai * li) / den
            qi = (ai * lr - (ar - 1.0) * li) / den
            bbr = qr * bt_ref[j, 2 * z] - qi * bt_ref[j, 2 * z + 1]
            bbi = qr * bt_ref[j, 2 * z + 1] + qi * bt_ref[j, 2 * z]
            pw = [(jnp.ones_like(ar), jnp.zeros_like(ai))]
            for _ in range(t):
                r, i = pw[-1]
                pw.append((r * ar - i * ai, r * ai + i * ar))
            at_dir.append(pw[t])

            expo = jnp.where(blk <= t, blk if z == 0 else t - blk, 0)
            tr = jnp.ones((p, P_LANES), F32)
            ti = jnp.zeros((p, P_LANES), F32)
            for bit in range(5):
                sr, si = to_col(pw[1 << bit][0]), to_col(pw[1 << bit][1])
                on = ((expo >> bit) & 1) == 1
                fr, fi = jnp.where(on, sr, 1.0), jnp.where(on, si, 0.0)
                tr, ti = tr * fr - ti * fi, tr * fi + ti * fr
            ctr = f32dot(ct_ref[j, 2 * z], rep_ref[...])
            cti = f32dot(ct_ref[j, 2 * z + 1], rep_ref[...])
            crd = ctr * tr - cti * ti
            cid = ctr * ti + cti * tr
            kt.append(f32dot(bbr, crd) - f32dot(bbi, cid))

            off = ch if z == 0 else 0
            cpows[j].append(crd[:, off:off + width])
            cpows[j].append(-cid[:, off:off + width])

            pows = [pw[t - 1 - s] if z == 0 else pw[s] for s in range(t)]
            pr_rows = jnp.concatenate([jnp.broadcast_to(q[0], (ch, p)) for q in pows], axis=0)
            pi_rows = jnp.concatenate([jnp.broadcast_to(q[1], (ch, p)) for q in pows], axis=0)
            br_rows = jnp.concatenate([bbr] * t, axis=0)
            bi_rows = jnp.concatenate([bbi] * t, axis=0)
            smats[j].append(pr_rows * br_rows - pi_rows * bi_rows)
            smats[j].append(pr_rows * bi_rows + pi_rows * br_rows)
        ats.append(at_dir)

        ktf = kt[0][:, :width]
        ktb = kt[1][:, ch:ch + width]
        rows = []
        for s in range(t):
            fwd = ktf if s == 0 else jnp.concatenate(
                [jnp.zeros((ch, ch * s), F32), ktf[:, :width - ch * s]], axis=1)
            sh = ch * (t - 1 - s)
            bwd = ktb if sh == 0 else jnp.concatenate(
                [ktb[:, sh:], jnp.zeros((ch, sh), F32)], axis=1)
            skip = jnp.where(lane16 == ch * s + row16, dsk_ref[j], 0.0)
            rows.append(fwd + bwd + skip)
        toep_ref[j] = jnp.concatenate(rows, axis=0).astype(toep_ref.dtype)

    zs = jnp.zeros((width, p), F32)
    smat_ref[0] = jnp.concatenate(
        [jnp.concatenate([jnp.concatenate([smats[0][q], zs], axis=1) for q in range(4)], axis=1),
         jnp.concatenate([jnp.concatenate([zs, smats[1][q]], axis=1) for q in range(4)], axis=1)],
        axis=0).astype(smat_ref.dtype)
    zc = jnp.zeros((p, width), F32)
    cpow_ref[0] = jnp.concatenate(
        [jnp.concatenate([cpows[j][q], zc] if j == 0 else [zc, cpows[j][q]], axis=1)
         for q in range(4) for j in range(2)], axis=0).astype(cpow_ref.dtype)
    at_ref[...] = jnp.concatenate(
        [jnp.concatenate([ats[0][z][part], ats[1][z][part]], axis=1)
         for z in range(2) for part in range(2)], axis=0)


def _s5_matrices(lam_re, lam_im, log_dt, b_re, b_im, c_re, c_im, d_skip):
    t, ch = CHUNK_T, SSM_GROUP
    _, g, p = lam_re.shape
    assert p == SSM_STATE and b_re.shape[-1] == ch and 2 * p == LANES
    width = t * ch
    zero = jnp.zeros((g, 1, p), F32)
    lam = jnp.concatenate(
        [jnp.stack([lam_re[z], lam_im[z], jnp.broadcast_to(log_dt[z][:, None], (g, p))], axis=1)
         if part == 0 else zero for z in range(2) for part in range(2)], axis=1)
    bt = jnp.stack([b_re[0], b_im[0], b_re[1], b_im[1]], axis=1).transpose(0, 1, 3, 2)
    ct = jnp.stack([c_re[0], c_im[0], c_re[1], c_im[1]], axis=1).transpose(0, 1, 3, 2)
    dsk = jnp.tile(d_skip.reshape(g, 1, ch), (1, 1, t))
    lane = jnp.arange(P_LANES)
    rep = ((lane[None, :] % ch == jnp.arange(ch)[:, None])
           & (lane[None, :] < (t + 1) * ch)).astype(F32)
    pair = lambda *shape: pl.BlockSpec((2,) + shape, lambda a: (a,) + (0,) * len(shape))
    toep, smat, cpow, a_t = pl.pallas_call(
        _s5_params_kernel,
        out_shape=(jax.ShapeDtypeStruct((g, width, width), BF16),
                   jax.ShapeDtypeStruct((g // 2, 2 * width, 4 * LANES), BF16),
                   jax.ShapeDtypeStruct((g // 2, 4 * LANES, 2 * width), BF16),
                   jax.ShapeDtypeStruct((4, g * p), F32)),
        grid=(g // 2,),
        in_specs=[pair(8, p), pair(4, ch, p), pair(4, p, ch), pair(1, width),
                  pl.BlockSpec((ch, P_LANES), lambda a: (0, 0))],
        out_specs=(pair(width, width),
                   pl.BlockSpec((1, 2 * width, 4 * LANES), lambda a: (a, 0, 0)),
                   pl.BlockSpec((1, 4 * LANES, 2 * width), lambda a: (a, 0, 0)),
                   pl.BlockSpec((4, LANES), lambda a: (0, a))),
        compiler_params=_cparams(("arbitrary",)),
        name="s5_params",
    )(lam, bt, ct, dsk, rep)
    return smat, toep, cpow, a_t


def _s5_states_kernel(u_ref, m_ref, fre_ref, fim_ref, bre_ref, bim_ref):
    lhs = jnp.concatenate([u_ref[0], u_ref[1]], axis=1)
    res = _bdot(lhs, m_ref[0])
    fre_ref[...] = res[:, 0 * LANES:1 * LANES]
    fim_ref[...] = res[:, 1 * LANES:2 * LANES]
    bre_ref[...] = res[:, 2 * LANES:3 * LANES]
    bim_ref[...] = res[:, 3 * LANES:4 * LANES]


def _s5_states(ut, smat):
    g, rows, kdim = ut.shape
    tr = min(S5_TR, rows)
    assert rows % tr == 0
    n_state = g * SSM_STATE
    out = jax.ShapeDtypeStruct((rows, n_state), F32)
    ospec = pl.BlockSpec((tr, LANES), lambda a, r: (r, a))
    return pl.pallas_call(
        _s5_states_kernel,
        out_shape=(out, out, out, out),
        grid=(g // 2, rows // tr),
        in_specs=[pl.BlockSpec((2, tr, kdim), lambda a, r: (a, r, 0)),
                  pl.BlockSpec((1, 2 * kdim, 4 * LANES), lambda a, r: (a, 0, 0))],
        out_specs=(ospec, ospec, ospec, ospec),
        compiler_params=_cparams(("arbitrary", "arbitrary")),
        name="s5_states",
    )(ut, smat)


def _s5_scan_kernel(a_ref, cfr_ref, cfi_ref, cbr_ref, cbi_ref,
                    xfr_ref, xfi_ref, xbr_ref, xbi_ref,
                    hfr_ref, hfi_ref, hbr_ref, hbi_ref, *, halves, kc, kx):
    a_fr, a_fi = a_ref[0:1, :], a_ref[1:2, :]
    a_br, a_bi = a_ref[2:3, :], a_ref[3:4, :]

    def rows(half, k, n_chunks, count=1):
        return pl.ds(pl.multiple_of((half * n_chunks + k) * 8, 8 * count), 8 * count)

    def step(h_re, h_im, a_re, a_im, s_re, s_im):
        return (a_re * h_re - a_im * h_im + s_re, a_re * h_im + a_im * h_re + s_im)

    zero = jnp.zeros((8, LANES), F32)

    def ctx_body(i, carry):
        out = []
        for half in range(halves):
            fr, fi, br, bi = carry[half]
            kf, kb = i, kc - 1 - i
            fr, fi = step(fr, fi, a_fr, a_fi, cfr_ref[rows(half, kf, kc), :],
                          cfi_ref[rows(half, kf, kc), :])
            br, bi = step(br, bi, a_br, a_bi, cbr_ref[rows(half, kb, kc), :],
                          cbi_ref[rows(half, kb, kc), :])
            out.append((fr, fi, br, bi))
        return tuple(out)

    carry = lax.fori_loop(0, kc, ctx_body, ((zero,) * 4,) * halves)

    def x_body(i, carry):
        out = []
        for half in range(halves):
            fr, fi, br, bi = carry[half]
            kf = 2 * i
            kb = kx - 2 - 2 * i
            fr1, fi1 = step(fr, fi, a_fr, a_fi, xfr_ref[rows(half, kf, kx), :],
                            xfi_ref[rows(half, kf, kx), :])
            br1, bi1 = step(br, bi, a_br, a_bi, xbr_ref[rows(half, kb + 1, kx), :],
                            xbi_ref[rows(half, kb + 1, kx), :])
            hfr_ref[rows(half, kf, kx, 2), :] = jnp.concatenate([fr, fr1], 0).astype(hfr_ref.dtype)
            hfi_ref[rows(half, kf, kx, 2), :] = jnp.concatenate([fi, fi1], 0).astype(hfi_ref.dtype)
            hbr_ref[rows(half, kb, kx, 2), :] = jnp.concatenate([br1, br], 0).astype(hbr_ref.dtype)
            hbi_ref[rows(half, kb, kx, 2), :] = jnp.concatenate([bi1, bi], 0).astype(hbi_ref.dtype)
            fr, fi = step(fr1, fi1, a_fr, a_fi, xfr_ref[rows(half, kf + 1, kx), :],
                          xfi_ref[rows(half, kf + 1, kx), :])
            br, bi = step(br1, bi1, a_br, a_bi, xbr_ref[rows(half, kb, kx), :],
                          xbi_ref[rows(half, kb, kx), :])
            out.append((fr, fi, br, bi))
        return tuple(out)

    lax.fori_loop(0, kx // 2, x_body, carry)


def _s5_scan(a_t, s_ctx, s_x, batch):
    rows_c, n_state = s_ctx[0].shape
    rows_x = s_x[0].shape[0]
    kc, kx = rows_c // batch, rows_x // batch
    assert kx % 2 == 0
    out = jax.ShapeDtypeStruct((rows_x, n_state), BF16)
    cspec = pl.BlockSpec((rows_c, LANES), lambda j: (0, j))
    xspec = pl.BlockSpec((rows_x, LANES), lambda j: (0, j))
    return pl.pallas_call(
        functools.partial(_s5_scan_kernel, halves=batch // 8, kc=kc, kx=kx),
        out_shape=(out, out, out, out),
        grid=(n_state // LANES,),
        in_specs=[pl.BlockSpec((4, LANES), lambda j: (0, j))] + [cspec] * 4 + [xspec] * 4,
        out_specs=(xspec, xspec, xspec, xspec),
        compiler_params=_cparams(("arbitrary",)),
        name="s5_scan",
    )(a_t, *s_ctx, *s_x)


def _s5_apply_kernel(u_ref, t_ref, c_ref, hfr_ref, hfi_ref, hbr_ref, hbi_ref, y_ref):
    hcat = jnp.concatenate([hfr_ref[...], hfi_ref[...], hbr_ref[...], hbi_ref[...]], axis=1)
    yst = _bdot(hcat, c_ref[0])
    n = u_ref.shape[2]
    y_ref[0] = (_bdot(u_ref[0], t_ref[0]) + yst[:, :n]).astype(y_ref.dtype)
    y_ref[1] = (_bdot(u_ref[1], t_ref[1]) + yst[:, n:]).astype(y_ref.dtype)


def _s5_apply(ut, toep, cpow, h_in):
    g, rows, kdim = ut.shape
    tr = min(S5_TR, rows)
    hspec = pl.BlockSpec((tr, LANES), lambda a, r: (r, a))
    return pl.pallas_call(
        _s5_apply_kernel,
        out_shape=jax.ShapeDtypeStruct((g, rows, kdim), BF16),
        grid=(g // 2, rows // tr),
        in_specs=[pl.BlockSpec((2, tr, kdim), lambda a, r: (a, r, 0)),
                  pl.BlockSpec((2, kdim, kdim), lambda a, r: (a, 0, 0)),
                  pl.BlockSpec((1, 4 * LANES, 2 * kdim), lambda a, r: (a, 0, 0))] + [hspec] * 4,
        out_specs=pl.BlockSpec((2, tr, kdim), lambda a, r: (a, r, 0)),
        compiler_params=_cparams(("arbitrary", "arbitrary")),
        name="s5_apply",
    )(ut, toep, cpow, *h_in)


def _chunk_rows_to_tokens(y_ref, perm_ref, ys):
    nrow = CHUNKS_PER_TILE * 8
    blocks = []
    for sl in range(N_SLAB):
        for th in range(CHUNK_T // GRANULES):
            blocks.append(jnp.concatenate(
                [y_ref[sl * GRANULES + m, :, th * LANES:(th + 1) * LANES] for m in range(GRANULES)],
                axis=1))
    acat = _bdot(jnp.concatenate(blocks, axis=0), perm_ref[...])
    blk = 0
    for sl in range(N_SLAB):
        for th in range(CHUNK_T // GRANULES):
            for kk in range(CHUNKS_PER_TILE):
                for j in range(GRANULES):
                    t = kk * CHUNK_T + th * GRANULES + j
                    ys[sl, pl.ds(t, 8, stride=PITCH), :] = acat[blk * nrow + kk * 8:
                                                               blk * nrow + (kk + 1) * 8,
                                                               j * LANES:(j + 1) * LANES]
            blk += 1
    return jnp.concatenate(
        [jnp.concatenate([ys[sl, pl.ds(b * PITCH, ROW_T), :] for sl in range(N_SLAB)], axis=1)
         for b in range(8)], axis=0)


def _back_kernel(x_ref, y_ref, mod_ref, perm_ref, gmix_ref, win_ref, convw_ref, convb_ref,
                 wglu_ref, wsbr_ref, wcbr_ref, wo_ref, gffn_ref, wrhi_ref, wrlo_ref, br_ref,
                 ltri_ref, x1_ref, h_ref, ids_ref, gates_ref, hist_ref, ys_scr):
    d_model = x_ref.shape[2]
    tm = 8 * ROW_T
    d_ssm = N_SLAB * LANES
    xt = x_ref[...].reshape(tm, d_model)
    xn = _rmsnorm(xt, gmix_ref[...])
    hx = _modulate_tile(xn, mod_ref, 0, 1).astype(BF16)
    p = _bdot(hx, win_ref[...])
    v = p[:, 0:d_ssm]
    gate_b = p[:, d_ssm:2 * d_ssm]
    gate_c = p[:, 2 * d_ssm:3 * d_ssm]
    g_s = p[:, 3 * d_ssm:3 * d_ssm + d_model]
    g_c = p[:, 3 * d_ssm + d_model:]

    z = gate_c * v
    col = lax.broadcasted_iota(jnp.int32, z.shape, 0) % GRID_W
    z_prev = jnp.where(col == 0, 0.0, pltpu.roll(z, 1, 0))
    z_next = jnp.where(col == GRID_W - 1, 0.0, pltpu.roll(z, tm - 1, 0))
    conv = (z_prev * convw_ref[0:1, :] + z * convw_ref[1:2, :]
            + z_next * convw_ref[2:3, :] + convb_ref[...])
    y_conv = gate_b * conv

    ys = jax.nn.gelu(_chunk_rows_to_tokens(y_ref, perm_ref, ys_scr))
    ys = ys * jax.nn.sigmoid(_bdot(ys.astype(BF16), wglu_ref[...]))

    y_s = _bdot(ys.astype(BF16), wsbr_ref[...])
    y_c = _bdot(y_conv.astype(BF16), wcbr_ref[...])
    merged = jax.nn.sigmoid(g_s) * y_s + jax.nn.sigmoid(g_c) * y_c
    mo = _bdot(merged.astype(BF16), wo_ref[...])
    x1 = xt + (mo.reshape(8, ROW_T, d_model) * mod_ref[:, 2:3, :]).reshape(tm, d_model)
    x1_ref[...] = x1.reshape(8, ROW_T, d_model)

    hn = _modulate_tile(_rmsnorm(x1, gffn_ref[...]), mod_ref, 3, 4)
    for b in range(8):
        for s in range(SUBLANES):
            h_ref[b, pl.ds(s, ROW_T, stride=SUBLANES), :] = hn[b * ROW_T:(b + 1) * ROW_T,
                                                              s * LANES:(s + 1) * LANES]

    hn_hi = hn.astype(BF16)
    hn_lo = (hn - hn_hi.astype(F32)).astype(BF16)
    logits = (_bdot(hn_hi, wrhi_ref[...]) + _bdot(hn_lo, wrhi_ref[...])
              + _bdot(hn_hi, wrlo_ref[...]) + br_ref[...])
    lane = lax.broadcasted_iota(jnp.int32, logits.shape, 1)
    neg = jnp.float32(-jnp.inf)
    cur = jnp.where(lane < N_EXPERTS, logits, neg)
    vals, idxs = [], []
    for _ in range(TOP_K):
        mk = jnp.max(cur, axis=-1, keepdims=True)
        ik = jnp.min(jnp.where(cur == mk, lane, LANES), axis=-1, keepdims=True)
        vals.append(mk)
        idxs.append(ik)
        cur = jnp.where(lane == ik, neg, cur)
    exps = [jnp.exp(vk - vals[0]) for vk in vals]
    denom = exps[0] + exps[1] + exps[2] + exps[3]
    ids = jnp.zeros(logits.shape, jnp.int32)
    gates = jnp.zeros(logits.shape, F32)
    onehot = jnp.zeros(logits.shape, F32)
    for k in range(TOP_K):
        onehot = onehot + (lane == idxs[k]).astype(F32)
    before = _bdot(ltri_ref[...], onehot.astype(BF16))
    for k in range(TOP_K):
        rank_k = jnp.sum(jnp.where(lane == idxs[k], before, 0.0), axis=-1, keepdims=True)
        ids = jnp.where(lane == k, idxs[k], ids)
        ids = jnp.where(lane == TOP_K + k, rank_k.astype(jnp.int32), ids)
        gates = jnp.where(lane == k, exps[k] / denom, gates)
    ids_ref[...] = ids.reshape(8, ROW_T, LANES)
    gates_ref[...] = gates.reshape(8, ROW_T, LANES)
    hist_ref[0] = jnp.broadcast_to(jnp.sum(onehot, axis=0, keepdims=True), (8, LANES))


def _back(x, y_chunks, mod, perm, g_mix, w_rest, conv_w, conv_b, w_glu, w_ssm_br,
          w_conv_br, w_o, g_ffn, w_router, b_router):
    b, s, d = x.shape
    g, _, kdim = y_chunks.shape
    tiles = s // ROW_T
    nrow = CHUNKS_PER_TILE * 8
    tok = lambda n: pl.BlockSpec((8, ROW_T, n), lambda i, j: (i, j, 0))

    def const(arr):
        nd = arr.ndim
        return pl.BlockSpec(arr.shape, lambda i, j: (0,) * nd, pipeline_mode=pl.Buffered(1))

    w_r_hi = w_router.astype(BF16)
    w_r_lo = (w_router - w_r_hi.astype(F32)).astype(BF16)
    row = jnp.arange(8 * ROW_T)
    ltri = (row[:, None] > row[None, :]).astype(BF16)
    params = (perm, g_mix, w_rest, conv_w, conv_b, w_glu, w_ssm_br, w_conv_br, w_o, g_ffn,
              w_r_hi, w_r_lo, b_router, ltri)
    return pl.pallas_call(
        _back_kernel,
        out_shape=(jax.ShapeDtypeStruct((b, s, d), F32),
                   jax.ShapeDtypeStruct((b, s * SUBLANES, LANES), F32),
                   jax.ShapeDtypeStruct((b, s, LANES), jnp.int32),
                   jax.ShapeDtypeStruct((b, s, LANES), F32),
                   jax.ShapeDtypeStruct(((b // 8) * tiles, 8, LANES), F32)),
        grid=(b // 8, tiles),
        in_specs=[tok(d),
                  pl.BlockSpec((g, nrow, kdim), lambda i, j: (0, i * tiles + j, 0)),
                  pl.BlockSpec((8, 8, d), lambda i, j: (i, 0, 0))] + [const(a) for a in params],
        out_specs=(tok(d), pl.BlockSpec((8, ROW_T * SUBLANES, LANES), lambda i, j: (i, j, 0)),
                   tok(LANES), tok(LANES),
                   pl.BlockSpec((1, 8, LANES), lambda i, j: (i * tiles + j, 0, 0))),
        scratch_shapes=[pltpu.VMEM((N_SLAB, 8 * PITCH, LANES), F32)],
        compiler_params=_cparams(("arbitrary", "arbitrary")),
        name="back",
    )(x, y_chunks, mod, *params)


def _routing(ids, ranks, tile_hist, bsz, seq):
    tb = MOE_TB
    n_tok = ids.shape[0]
    n_slot = n_tok * TOP_K
    tile_before = jnp.cumsum(tile_hist, axis=0) - tile_hist
    counts = jnp.sum(tile_hist, axis=0)
    padded = (counts + tb - 1) // tb * tb
    pad_end = jnp.cumsum(padded)
    pad_start = pad_end - padded
    start = jnp.cumsum(counts) - counts
    halves, tiles = bsz // 8, seq // ROW_T
    base = (pad_start[None, :] + tile_before).reshape(halves, 1, tiles, 1, N_EXPERTS)
    base = jnp.broadcast_to(base, (halves, 8, tiles, ROW_T, N_EXPERTS)).reshape(n_tok, N_EXPERTS)
    onehot = ids[:, :, None] == jnp.arange(N_EXPERTS, dtype=jnp.int32)[None, None, :]
    pos = (jnp.sum(jnp.where(onehot, base[:, None, :], 0), axis=-1) + ranks).astype(jnp.int32)
    token = jnp.broadcast_to(jnp.arange(n_tok, dtype=jnp.int32)[:, None], (n_tok, TOP_K))
    _, order_token = lax.sort_key_val(pos.reshape(-1), token.reshape(-1))
    n_blocks = n_slot // tb + N_EXPERTS
    cap = n_blocks * tb
    block_first_row = jnp.arange(n_blocks, dtype=jnp.int32) * tb
    block_expert = jnp.minimum(
        jnp.sum(pad_end[None, :] <= block_first_row[:, None], axis=1, dtype=jnp.int32),
        N_EXPERTS - 1)
    within = (block_first_row - pad_start[block_expert])[:, None] + jnp.arange(tb, dtype=jnp.int32)
    valid = within < counts[block_expert][:, None]
    src = jnp.clip(start[block_expert][:, None] + within, 0, n_slot - 1)
    buf_token = jnp.where(valid, order_token[src], 0).astype(jnp.int32).reshape(cap)
    n_used = (pad_end[-1] // tb).astype(jnp.int32).reshape(1)
    return buf_token, pos, block_expert, n_used


def _to_tiles(ref, val):
    rows = val.shape[0]
    for s in range(SUBLANES):
        ref[pl.ds(s, rows, stride=SUBLANES), :] = val[:, s * LANES:(s + 1) * LANES]


def _tile_piece(ref, first_row, rows, s):
    return ref[pl.ds(first_row * SUBLANES + s, rows, stride=SUBLANES), :]


def _tile_gather_start(idx_ref, src_hbm, dst, sem, n_rows, unrolled, alternate_priority=False,
                       dst_row=lambda r: r):
    def copy(r, t):
        return pltpu.make_async_copy(
            src_hbm.at[pl.ds(pl.multiple_of(t * SUBLANES, SUBLANES), SUBLANES), :],
            dst.at[pl.ds(pl.multiple_of(dst_row(r) * SUBLANES, SUBLANES), SUBLANES), :], sem)

    if unrolled:
        for r in range(n_rows):
            copy(r, idx_ref[0, 0, r]).start(priority=(r % 2) if alternate_priority else 0)
    else:
        def body(r, carry):
            copy(r, idx_ref[0, 0, r]).start()
            return carry
        lax.fori_loop(0, n_rows, body, 0, unroll=8)


def _tile_gather_wait(src_hbm, dst, sem):
    pltpu.make_async_copy(src_hbm.at[pl.ds(0, dst.shape[0]), :], dst, sem).wait()


def _ring_step(i, last, bufs, sem, wait, prefetch, compute):
    n = len(bufs)
    for p in range(n):
        def branch(p=p):
            wait(bufs[p], sem.at[p])
            q = (p + n - 1) % n
            prefetch(bufs[q], sem.at[q])
            compute(bufs[p])

            @pl.when(i == last)
            def _():
                for r in range(1, n):
                    wait(bufs[(p + r) % n], sem.at[(p + r) % n])

        pl.when((i <= last) & (lax.rem(i, n) == p))(branch)


def _cast_rows(src_ref, dst_ref, chunk):
    def body(c, carry):
        rows = pl.ds(pl.multiple_of(c * chunk, chunk), chunk)
        dst_ref[rows, :] = src_ref[0, rows, :].astype(dst_ref.dtype)
        return carry
    lax.fori_loop(0, dst_ref.shape[0] // chunk, body, 0)


def _moe_kernel(be_ref, nu_ref, tok0_ref, tok1_ref, tok2_ref, h_hbm, wgu_ref, bgu_ref, wd_ref,
                bd_ref, out_ref, xbuf0, xbuf1, xbuf2, wgu_bf, wd_bf, sem):
    i = pl.program_id(0)
    n_used = nu_ref[0]
    tb = xbuf0.shape[0] // SUBLANES
    f = wd_ref.shape[1]

    @pl.when(i == 0)
    def _():
        _tile_gather_start(tok0_ref, h_hbm, xbuf0, sem.at[0], tb, unrolled=False)
        _tile_gather_start(tok1_ref, h_hbm, xbuf1, sem.at[1], tb, unrolled=False)

    new_expert = (i == 0) | (be_ref[i] != be_ref[jnp.maximum(i - 1, 0)])

    @pl.when(new_expert & (i < n_used))
    def _():
        _cast_rows(wgu_ref, wgu_bf, 128)
        _cast_rows(wd_ref, wd_bf, 128)

    def compute(xcur):
        xe = jnp.concatenate([_tile_piece(xcur, 0, tb, s) for s in range(SUBLANES)],
                             axis=1).astype(BF16)
        gu = _bdot(xe, wgu_bf[...]) + bgu_ref[0]
        gt = jnp.minimum(gu[:, :f], SWIGLU_LIMIT)
        up = jnp.clip(gu[:, f:], -SWIGLU_LIMIT, SWIGLU_LIMIT)
        act = gt * jax.nn.sigmoid(SWIGLU_ALPHA * gt) * (up + 1.0)
        _to_tiles(out_ref, _bdot(act.astype(BF16), wd_bf[...]) + bd_ref[0])

    _ring_step(i, n_used - 1, (xbuf0, xbuf1, xbuf2), sem,
               wait=lambda buf, s: _tile_gather_wait(h_hbm, buf, s),
               prefetch=lambda buf, s: _tile_gather_start(tok2_ref, h_hbm, buf, s, tb,
                                                          unrolled=True),
               compute=compute)

    @pl.when(i >= n_used)
    def _():
        out_ref[...] = jnp.zeros(out_ref.shape, out_ref.dtype)


def _moe(h_tiles, buf_token, block_expert, n_used, w_gu, b_gu, w_down, b_down):
    e, d, f2 = w_gu.shape
    f = f2 // 2
    tb = MOE_TB
    cap = buf_token.shape[0]
    n_blocks = cap // tb
    tok3 = buf_token.reshape(n_blocks, 1, tb)
    smem_blk = lambda imap: pl.BlockSpec((1, 1, tb), imap, memory_space=pltpu.SMEM)
    grid_spec = pltpu.PrefetchScalarGridSpec(
        num_scalar_prefetch=2,
        grid=(n_blocks,),
        in_specs=[smem_blk(lambda i, be, nu: (i, 0, 0)),
                  smem_blk(lambda i, be, nu: (jnp.minimum(i + 1, n_blocks - 1), 0, 0)),
                  smem_blk(lambda i, be, nu: (jnp.minimum(i + 2, n_blocks - 1), 0, 0)),
                  pl.BlockSpec(memory_space=pl.ANY),
                  pl.BlockSpec((1, d, f2), lambda i, be, nu: (be[i], 0, 0)),
                  pl.BlockSpec((1, 1, f2), lambda i, be, nu: (be[i], 0, 0)),
                  pl.BlockSpec((1, f, d), lambda i, be, nu: (be[i], 0, 0)),
                  pl.BlockSpec((1, 1, d), lambda i, be, nu: (be[i], 0, 0))],
        out_specs=pl.BlockSpec((tb * SUBLANES, LANES), lambda i, be, nu: (i, 0)),
        scratch_shapes=[pltpu.VMEM((tb * SUBLANES, LANES), F32)] * 3
        + [pltpu.VMEM((d, f2), BF16), pltpu.VMEM((f, d), BF16), pltpu.SemaphoreType.DMA((3,))],
    )
    return pl.pallas_call(
        _moe_kernel,
        out_shape=jax.ShapeDtypeStruct((cap * SUBLANES, LANES), F32),
        grid_spec=grid_spec,
        compiler_params=_cparams(("arbitrary",)),
        name="moe",
    )(block_expert, n_used, tok3, tok3, tok3, h_tiles, w_gu, b_gu.reshape(e, 1, f2),
      w_down, b_down.reshape(e, 1, d))


def _combine_kernel(pos0_ref, pos1_ref, pos2_ref, yb_hbm, x1_ref, gates_ref, mod_ref, gfin_ref,
                    out_ref, buf0, buf1, buf2, sem):
    i = pl.program_id(0)
    n = pl.num_programs(0)
    n_rows = buf0.shape[0] // SUBLANES
    tm = n_rows // TOP_K

    slot_major = lambda r: (r % TOP_K) * tm + r // TOP_K

    @pl.when(i == 0)
    def _():
        _tile_gather_start(pos0_ref, yb_hbm, buf0, sem.at[0], n_rows, unrolled=False,
                           dst_row=slot_major)
        _tile_gather_start(pos1_ref, yb_hbm, buf1, sem.at[1], n_rows, unrolled=False,
                           dst_row=slot_major)

    def compute(cur):
        g = gates_ref[...]
        gk = [jnp.broadcast_to(g[:, k:k + 1], (tm, LANES)) for k in range(TOP_K)]
        pieces = []
        for s in range(SUBLANES):
            acc = gk[0] * _tile_piece(cur, 0, tm, s)
            for k in range(1, TOP_K):
                acc = acc + gk[k] * _tile_piece(cur, k * tm, tm, s)
            pieces.append(acc)
        x2 = x1_ref[...] + mod_ref[0, 5:6, :] * jnp.concatenate(pieces, axis=1)
        out_ref[...] = _rmsnorm(x2, gfin_ref[...])

    _ring_step(i, n - 1, (buf0, buf1, buf2), sem,
               wait=lambda buf, s: _tile_gather_wait(yb_hbm, buf, s),
               prefetch=lambda buf, s: _tile_gather_start(pos2_ref, yb_hbm, buf, s, n_rows,
                                                          unrolled=True, alternate_priority=True,
                                                          dst_row=slot_major),
               compute=compute)


def _combine(yb, pos, x1_2d, gates_2d, mod, g_final, seq):
    n_tok, d = x1_2d.shape
    tm = min(COMBINE_TM, seq)
    assert seq % tm == 0
    n_tiles = n_tok // tm
    per_batch = seq // tm
    pos3 = pos.reshape(n_tiles, 1, tm * TOP_K)
    smem_blk = lambda imap: pl.BlockSpec((1, 1, TOP_K * tm), imap, memory_space=pltpu.SMEM)
    buf = pltpu.VMEM((TOP_K * tm * SUBLANES, LANES), F32)
    return pl.pallas_call(
        _combine_kernel,
        out_shape=jax.ShapeDtypeStruct((n_tok, d), F32),
        grid=(n_tiles,),
        in_specs=[smem_blk(lambda i: (i, 0, 0)),
                  smem_blk(lambda i: (jnp.minimum(i + 1, n_tiles - 1), 0, 0)),
                  smem_blk(lambda i: (jnp.minimum(i + 2, n_tiles - 1), 0, 0)),
                  pl.BlockSpec(memory_space=pl.ANY),
                  pl.BlockSpec((tm, d), lambda i: (i, 0)),
                  pl.BlockSpec((tm, LANES), lambda i: (i, 0)),
                  pl.BlockSpec((1, 8, d), lambda i: (i // per_batch, 0, 0)),
                  pl.BlockSpec((1, d), lambda i: (0, 0))],
        out_specs=pl.BlockSpec((tm, d), lambda i: (i, 0)),
        scratch_shapes=[buf, buf, buf, pltpu.SemaphoreType.DMA((3,))],
        compiler_params=_cparams(("arbitrary",)),
        name="combine",
    )(pos3, pos3, pos3, yb, x1_2d, gates_2d, mod, g_final)


def kernel(x, c, ctx, c_ctx, w_mod, b_mod, g_mix, w_in, lam_re, lam_im, log_dt, b_re, b_im,
           c_re, c_im, d_skip, w_glu, conv_w, conv_b, w_ssm_br, w_conv_br, w_o, g_ffn,
           w_router, b_router, w_gu, b_gu, w_down, b_down, g_final):
    depth = w_mod.shape[0]
    assert depth == 1, "single-layer trunk"
    bsz, seq, d = x.shape
    ctx_len = ctx.shape[1]
    d_ssm = d // 2
    assert bsz % 8 == 0 and seq % CHUNK_T == 0 and ctx_len % CHUNK_T == 0 and seq % GRID_W == 0
    assert d == SUBLANES * LANES, "row gathers move one (8,128) f32 tile per token"

    n_cond = -(-(bsz + 1) // 8) * 8
    cond = jnp.zeros((n_cond, d), F32).at[:bsz].set(c).at[bsz].set(c_ctx)
    m = _adaln(cond, w_mod[0], b_mod[0])
    zeros2 = jnp.zeros((n_cond, 2, d), F32)
    mod_all = jnp.concatenate([m.reshape(n_cond, 6, d), zeros2], axis=1)
    mod_x, mod_c = mod_all[:bsz], mod_all[bsz:bsz + 1]

    w_in_bf = w_in[0].astype(BF16)
    w_u, w_rest = w_in_bf[:, :d_ssm], w_in_bf[:, d_ssm:]
    gm = g_mix[0].reshape(1, d)

    perm = _granule_transpose_matrix()
    utx = _front(x, mod_x, gm, w_u, perm)
    utc = _front(ctx, mod_c, gm, w_u, perm)

    smat, toep, cpow, a_t = _s5_matrices(lam_re[0], lam_im[0], log_dt[0], b_re[0], b_im[0],
                                         c_re[0], c_im[0], d_skip[0])
    s_x = _s5_states(utx, smat)
    s_c = _s5_states(utc, smat)
    h_in = _s5_scan(a_t, s_c, s_x, bsz)
    y_chunks = _s5_apply(utx, toep, cpow, h_in)

    pad_r = jnp.zeros((d, LANES - N_EXPERTS), F32)
    w_r = jnp.concatenate([w_router[0], pad_r], axis=1)
    b_r = jnp.concatenate([b_router[0], jnp.zeros((LANES - N_EXPERTS,), F32)]).reshape(1, LANES)
    x1, h, ids, gates, hist = _back(
        x, y_chunks, mod_x, perm, gm, w_rest, conv_w[0], conv_b[0].reshape(1, d_ssm),
        w_glu[0].astype(BF16), w_ssm_br[0].astype(BF16),
        w_conv_br[0].astype(BF16), w_o[0].astype(BF16), g_ffn[0].reshape(1, d), w_r, b_r)

    n_tok = bsz * seq
    ids2 = ids.reshape(n_tok, LANES)
    buf_token, pos, block_expert, n_used = _routing(
        ids2[:, :TOP_K], ids2[:, TOP_K:2 * TOP_K],
        hist[:, 0, :N_EXPERTS].astype(jnp.int32), bsz, seq)
    yb = _moe(h.reshape(n_tok * SUBLANES, LANES), buf_token, block_expert, n_used,
              w_gu[0], b_gu[0], w_down[0], b_down[0])
    out = _combine(yb, pos, x1.reshape(n_tok, d), gates.reshape(n_tok, LANES), mod_x,
                   g_final.reshape(1, d), seq)
    return out.reshape(bsz, seq, d)
```

```python
import functools
import math

import jax
import jax.numpy as jnp
from jax import lax
from jax.experimental import pallas as pl
from jax.experimental.pallas import tpu as pltpu

F32 = jnp.float32
BF16 = jnp.bfloat16
HIGHEST = lax.Precision.HIGHEST

RMS_EPS = 1e-6
GRID_W = 64
SSM_GROUP = 16
SSM_STATE = 64
N_EXPERTS = 32
TOP_K = 4
SWIGLU_LIMIT = 7.0
SWIGLU_ALPHA = 1.702

CHUNK_T = 16
LANES = 128
SUBLANES = 8
V7X_VMEM_LIMIT_BYTES = 56 * 1024 * 1024

S5_TR = 1024
MOE_TB = 512
COMBINE_TM = 256


def _cparams(sem):
    return pltpu.CompilerParams(dimension_semantics=sem,
                                vmem_limit_bytes=V7X_VMEM_LIMIT_BYTES)


def _bdot(a, b):
    return jnp.dot(a, b, preferred_element_type=F32)


def _rmsnorm(xt, g):
    ms = jnp.mean(xt * xt, axis=-1, keepdims=True)
    return xt * lax.rsqrt(ms + RMS_EPS) * g


def _adaln_kernel(c_ref, w_ref, b_ref, o_ref):
    s = jax.nn.silu(c_ref[...])
    o_ref[...] = jnp.dot(s, w_ref[...], precision=HIGHEST,
                         preferred_element_type=F32) + b_ref[...]


def _adaln(cond, w_mod, b_mod):
    r, d = cond.shape
    n = w_mod.shape[1]
    tn = n // 4
    return pl.pallas_call(
        _adaln_kernel,
        out_shape=jax.ShapeDtypeStruct((r, n), F32),
        grid=(n // tn,),
        in_specs=[pl.BlockSpec((r, d), lambda j: (0, 0)),
                  pl.BlockSpec((d, tn), lambda j: (0, j)),
                  pl.BlockSpec((1, tn), lambda j: (0, j))],
        out_specs=pl.BlockSpec((r, tn), lambda j: (0, j)),
        compiler_params=_cparams(("arbitrary",)),
        name="adaln",
    )(cond, w_mod, b_mod.reshape(1, n))


ROW_T = GRID_W
PITCH = ROW_T + 8
N_SLAB = 4
CHUNKS_PER_TILE = ROW_T // CHUNK_T
GRANULES = LANES // SSM_GROUP


def _granule_transpose_matrix():
    n = GRANULES * LANES
    idx = jnp.arange(n)
    j, m, c = idx // LANES, (idx % LANES) // SSM_GROUP, idx % SSM_GROUP
    dst = m * LANES + j * SSM_GROUP + c
    return jnp.zeros((n, n), F32).at[idx, dst].set(1.0).astype(BF16)


def _modulate_tile(xn, mod_ref, shift_row, scale_row):
    d = xn.shape[1]
    x3 = xn.reshape(8, ROW_T, d)
    x3 = x3 * (1.0 + mod_ref[:, scale_row:scale_row + 1, :]) + mod_ref[:, shift_row:shift_row + 1, :]
    return x3.reshape(8 * ROW_T, d)


def _front_kernel(x_ref, mod_ref, g_ref, w_ref, perm_ref, u_ref, us):
    d = x_ref.shape[2]
    xn = _rmsnorm(x_ref[...].reshape(8 * ROW_T, d), g_ref[...])
    hx = _modulate_tile(xn, mod_ref, 0, 1)
    u = _bdot(hx.astype(BF16), w_ref[...])
    for sl in range(N_SLAB):
        for b in range(8):
            us[sl, pl.ds(b * PITCH, ROW_T), :] = u[b * ROW_T:(b + 1) * ROW_T,
                                                  sl * LANES:(sl + 1) * LANES]
    blocks = []
    for sl in range(N_SLAB):
        for th in range(CHUNK_T // GRANULES):
            rows = []
            for kk in range(CHUNKS_PER_TILE):
                t0 = kk * CHUNK_T + th * GRANULES
                rows.append(jnp.concatenate(
                    [us[sl, pl.ds(t0 + j, 8, stride=PITCH), :] for j in range(GRANULES)], axis=1))
            blocks.append(jnp.concatenate(rows, axis=0))
    acat = jnp.concatenate(blocks, axis=0).astype(BF16)
    bmat = _bdot(acat, perm_ref[...])
    nrow = CHUNKS_PER_TILE * 8
    blk = 0
    for sl in range(N_SLAB):
        for th in range(CHUNK_T // GRANULES):
            for m in range(GRANULES):
                u_ref[sl * GRANULES + m, :, th * LANES:(th + 1) * LANES] = (
                    bmat[blk * nrow:(blk + 1) * nrow, m * LANES:(m + 1) * LANES].astype(u_ref.dtype))
            blk += 1


def _front(x, mod, g_mix, w_u, perm):
    b, s, d = x.shape
    n = w_u.shape[1]
    g = n // SSM_GROUP
    assert s % ROW_T == 0 and b % 8 == 0 and n == N_SLAB * LANES
    k = s // CHUNK_T
    tiles = s // ROW_T
    nrow = CHUNKS_PER_TILE * 8
    shared = mod.shape[0] == 1
    mod_spec = (pl.BlockSpec((1, 8, d), lambda i, j: (0, 0, 0)) if shared
                else pl.BlockSpec((8, 8, d), lambda i, j: (i, 0, 0)))
    return pl.pallas_call(
        _front_kernel,
        out_shape=jax.ShapeDtypeStruct((g, (b // 8) * k * 8, CHUNK_T * SSM_GROUP), BF16),
        grid=(b // 8, tiles),
        in_specs=[pl.BlockSpec((8, ROW_T, d), lambda i, j: (i, j, 0)),
                  mod_spec,
                  pl.BlockSpec((1, d), lambda i, j: (0, 0)),
                  pl.BlockSpec((d, n), lambda i, j: (0, 0)),
                  pl.BlockSpec(perm.shape, lambda i, j: (0, 0))],
        out_specs=pl.BlockSpec((g, nrow, CHUNK_T * SSM_GROUP), lambda i, j: (0, i * tiles + j, 0)),
        scratch_shapes=[pltpu.VMEM((N_SLAB, 8 * PITCH, LANES), F32)],
        compiler_params=_cparams(("arbitrary", "arbitrary")),
        name="front",
    )(x, mod, g_mix, w_u, perm)


P_LANES = 3 * LANES


def _s5_params_kernel(lam_ref, bt_ref, ct_ref, dsk_ref, rep_ref,
                      toep_ref, smat_ref, cpow_ref, at_ref):
    t, ch, p = CHUNK_T, SSM_GROUP, SSM_STATE
    width = t * ch
    f32dot = lambda a, b: jnp.dot(a, b, precision=HIGHEST, preferred_element_type=F32)
    eye = (lax.broadcasted_iota(jnp.int32, (p, p), 0) == lax.broadcasted_iota(jnp.int32, (p, p), 1))
    to_col = lambda row: jnp.sum(jnp.where(eye, jnp.broadcast_to(row, (p, p)), 0.0),
                                 axis=1, keepdims=True)
    blk = lax.broadcasted_iota(jnp.int32, (p, P_LANES), 1) // ch
    row16 = lax.broadcasted_iota(jnp.int32, (ch, width), 0)
    lane16 = lax.broadcasted_iota(jnp.int32, (ch, width), 1)

    smats, cpows, ats = [[], []], [[], []], []
    for j in range(2):
        kt, at_dir = [], []
        for z in range(2):
            lr = lam_ref[j, 4 * z:4 * z + 1, :]
            li = lam_ref[j, 4 * z + 1:4 * z + 2, :]
            dt = jnp.exp(lam_ref[j, 4 * z + 2:4 * z + 3, :])
            mag = jnp.exp(lr * dt)
            ar, ai = mag * jnp.cos(li * dt), mag * jnp.sin(li * dt)
            den = lr * lr + li * li
            qr = ((ar - 1.0) * lr + ai * li) / den
            qi = (ai * lr - (ar - 1.0) * li) / den
            bbr = qr * bt_ref[j, 2 * z] - qi * bt_ref[j, 2 * z + 1]
            bbi = qr * bt_ref[j, 2 * z + 1] + qi * bt_ref[j, 2 * z]
            pw = [(jnp.ones_like(ar), jnp.zeros_like(ai))]
            for _ in range(t):
                r, i = pw[-1]
                pw.append((r * ar - i * ai, r * ai + i * ar))
            at_dir.append(pw[t])

            expo = jnp.where(blk <= t, blk if z == 0 else t - blk, 0)
            tr = jnp.ones((p, P_LANES), F32)
            ti = jnp.zeros((p, P_LANES), F32)
            for bit in range(5):
                sr, si = to_col(pw[1 << bit][0]), to_col(pw[1 << bit][1])
                on = ((expo >> bit) & 1) == 1
                fr, fi = jnp.where(on, sr, 1.0), jnp.where(on, si, 0.0)
                tr, ti = tr * fr - ti * fi, tr * fi + ti * fr
            ctr = f32dot(ct_ref[j, 2 * z], rep_ref[...])
            cti = f32dot(ct_ref[j, 2 * z + 1], rep_ref[...])
            crd = ctr * tr - cti * ti
            cid = ctr * ti + cti * tr
            kt.append(f32dot(bbr, crd) - f32dot(bbi, cid))

            off = ch if z == 0 else 0
            cpows[j].append(crd[:, off:off + width])
            cpows[j].append(-cid[:, off:off + width])

            pows = [pw[t - 1 - s] if z == 0 else pw[s] for s in range(t)]
            pr_rows = jnp.concatenate([jnp.broadcast_to(q[0], (ch, p)) for q in pows], axis=0)
            pi_rows = jnp.concatenate([jnp.broadcast_to(q[1], (ch, p)) for q in pows], axis=0)
            br_rows = jnp.concatenate([bbr] * t, axis=0)
            bi_rows = jnp.concatenate([bbi] * t, axis=0)
            smats[j].append(pr_rows * br_rows - pi_rows * bi_rows)
            smats[j].append(pr_rows * bi_rows + pi_rows * br_rows)
        ats.append(at_dir)

        ktf = kt[0][:, :width]
        ktb = kt[1][:, ch:ch + width]
        rows = []
        for s in range(t):
            fwd = ktf if s == 0 else jnp.concatenate(
                [jnp.zeros((ch, ch * s), F32), ktf[:, :width - ch * s]], axis=1)
            sh = ch * (t - 1 - s)
            bwd = ktb if sh == 0 else jnp.concatenate(
                [ktb[:, sh:], jnp.zeros((ch, sh), F32)], axis=1)
            skip = jnp.where(lane16 == ch * s + row16, dsk_ref[j], 0.0)
            rows.append(fwd + bwd + skip)
        toep_ref[j] = jnp.concatenate(rows, axis=0).astype(toep_ref.dtype)

    zs = jnp.zeros((width, p), F32)
    smat_ref[0] = jnp.concatenate(
        [jnp.concatenate([jnp.concatenate([smats[0][q], zs], axis=1) for q in range(4)], axis=1),
         jnp.concatenate([jnp.concatenate([zs, smats[1][q]], axis=1) for q in range(4)], axis=1)],
        axis=0).astype(smat_ref.dtype)
    zc = jnp.zeros((p, width), F32)
    cpow_ref[0] = jnp.concatenate(
        [jnp.concatenate([cpows[j][q], zc] if j == 0 else [zc, cpows[j][q]], axis=1)
         for q in range(4) for j in range(2)], axis=0).astype(cpow_ref.dtype)
    at_ref[...] = jnp.concatenate(
        [jnp.concatenate([ats[0][z][part], ats[1][z][part]], axis=1)
         for z in range(2) for part in range(2)], axis=0)


def _s5_matrices(lam_re, lam_im, log_dt, b_re, b_im, c_re, c_im, d_skip):
    t, ch = CHUNK_T, SSM_GROUP
    _, g, p = lam_re.shape
    assert p == SSM_STATE and b_re.shape[-1] == ch and 2 * p == LANES
    width = t * ch
    zero = jnp.zeros((g, 1, p), F32)
    lam = jnp.concatenate(
        [jnp.stack([lam_re[z], lam_im[z], jnp.broadcast_to(log_dt[z][:, None], (g, p))], axis=1)
         if part == 0 else zero for z in range(2) for part in range(2)], axis=1)
    bt = jnp.stack([b_re[0], b_im[0], b_re[1], b_im[1]], axis=1).transpose(0, 1, 3, 2)
    ct = jnp.stack([c_re[0], c_im[0], c_re[1], c_im[1]], axis=1).transpose(0, 1, 3, 2)
    dsk = jnp.tile(d_skip.reshape(g, 1, ch), (1, 1, t))
    lane = jnp.arange(P_LANES)
    rep = ((lane[None, :] % ch == jnp.arange(ch)[:, None])
           & (lane[None, :] < (t + 1) * ch)).astype(F32)
    pair = lambda *shape: pl.BlockSpec((2,) + shape, lambda a: (a,) + (0,) * len(shape))
    toep, smat, cpow, a_t = pl.pallas_call(
        _s5_params_kernel,
        out_shape=(jax.ShapeDtypeStruct((g, width, width), BF16),
                   jax.ShapeDtypeStruct((g // 2, 2 * width, 4 * LANES), BF16),
                   jax.ShapeDtypeStruct((g // 2, 4 * LANES, 2 * width), BF16),
                   jax.ShapeDtypeStruct((4, g * p), F32)),
        grid=(g // 2,),
        in_specs=[pair(8, p), pair(4, ch, p), pair(4, p, ch), pair(1, width),
                  pl.BlockSpec((ch, P_LANES), lambda a: (0, 0))],
        out_specs=(pair(width, width),
                   pl.BlockSpec((1, 2 * width, 4 * LANES), lambda a: (a, 0, 0)),
                   pl.BlockSpec((1, 4 * LANES, 2 * width), lambda a: (a, 0, 0)),
                   pl.BlockSpec((4, LANES), lambda a: (0, a))),
        compiler_params=_cparams(("arbitrary",)),
        name="s5_params",
    )(lam, bt, ct, dsk, rep)
    return smat, toep, cpow, a_t


def _s5_states_kernel(u_ref, m_ref, fre_ref, fim_ref, bre_ref, bim_ref):
    lhs = jnp.concatenate([u_ref[0], u_ref[1]], axis=1)
    res = _bdot(lhs, m_ref[0])
    fre_ref[...] = res[:, 0 * LANES:1 * LANES]
    fim_ref[...] = res[:, 1 * LANES:2 * LANES]
    bre_ref[...] = res[:, 2 * LANES:3 * LANES]
    bim_ref[...] = res[:, 3 * LANES:4 * LANES]


def _s5_states(ut, smat):
    g, rows, kdim = ut.shape
    tr = min(S5_TR, rows)
    assert rows % tr == 0
    n_state = g * SSM_STATE
    out = jax.ShapeDtypeStruct((rows, n_state), F32)
    ospec = pl.BlockSpec((tr, LANES), lambda a, r: (r, a))
    return pl.pallas_call(
        _s5_states_kernel,
        out_shape=(out, out, out, out),
        grid=(g // 2, rows // tr),
        in_specs=[pl.BlockSpec((2, tr, kdim), lambda a, r: (a, r, 0)),
                  pl.BlockSpec((1, 2 * kdim, 4 * LANES), lambda a, r: (a, 0, 0))],
        out_specs=(ospec, ospec, ospec, ospec),
        compiler_params=_cparams(("arbitrary", "arbitrary")),
        name="s5_states",
    )(ut, smat)


def _s5_scan_kernel(a_ref, cfr_ref, cfi_ref, cbr_ref, cbi_ref,
                    xfr_ref, xfi_ref, xbr_ref, xbi_ref,
                    hfr_ref, hfi_ref, hbr_ref, hbi_ref, *, halves, kc, kx):
    a_fr, a_fi = a_ref[0:1, :], a_ref[1:2, :]
    a_br, a_bi = a_ref[2:3, :], a_ref[3:4, :]

    def rows(half, k, n_chunks, count=1):
        return pl.ds(pl.multiple_of((half * n_chunks + k) * 8, 8 * count), 8 * count)

    def step(h_re, h_im, a_re, a_im, s_re, s_im):
        return (a_re * h_re - a_im * h_im + s_re, a_re * h_im + a_im * h_re + s_im)

    zero = jnp.zeros((8, LANES), F32)

    def ctx_body(i, carry):
        out = []
        for half in range(halves):
            fr, fi, br, bi = carry[half]
            kf, kb = i, kc - 1 - i
            fr, fi = step(fr, fi, a_fr, a_fi, cfr_ref[rows(half, kf, kc), :],
                          cfi_ref[rows(half, kf, kc), :])
            br, bi = step(br, bi, a_br, a_bi, cbr_ref[rows(half, kb, kc), :],
                          cbi_ref[rows(half, kb, kc), :])
            out.append((fr, fi, br, bi))
        return tuple(out)

    carry = lax.fori_loop(0, kc, ctx_body, ((zero,) * 4,) * halves)

    def x_body(i, carry):
        out = []
        for half in range(halves):
            fr, fi, br, bi = carry[half]
            kf = 2 * i
            kb = kx - 2 - 2 * i
            fr1, fi1 = step(fr, fi, a_fr, a_fi, xfr_ref[rows(half, kf, kx), :],
                            xfi_ref[rows(half, kf, kx), :])
            br1, bi1 = step(br, bi, a_br, a_bi, xbr_ref[rows(half, kb + 1, kx), :],
                            xbi_ref[rows(half, kb + 1, kx), :])
            hfr_ref[rows(half, kf, kx, 2), :] = jnp.concatenate([fr, fr1], 0).astype(hfr_ref.dtype)
            hfi_ref[rows(half, kf, kx, 2), :] = jnp.concatenate([fi, fi1], 0).astype(hfi_ref.dtype)
            hbr_ref[rows(half, kb, kx, 2), :] = jnp.concatenate([br1, br], 0).astype(hbr_ref.dtype)
            hbi_ref[rows(half, kb, kx, 2), :] = jnp.concatenate([bi1, bi], 0).astype(hbi_ref.dtype)
            fr, fi = step(fr1, fi1, a_fr, a_fi, xfr_ref[rows(half, kf + 1, kx), :],
                          xfi_ref[rows(half, kf + 1, kx), :])
            br, bi = step(br1, bi1, a_br, a_bi, xbr_ref[rows(half, kb, kx), :],
                          xbi_ref[rows(half, kb, kx), :])
            out.append((fr, fi, br, bi))
        return tuple(out)

    lax.fori_loop(0, kx // 2, x_body, carry)


def _s5_scan(a_t, s_ctx, s_x, batch):
    rows_c, n_state = s_ctx[0].shape
    rows_x = s_x[0].shape[0]
    kc, kx = rows_c // batch, rows_x // batch
    assert kx % 2 == 0
    out = jax.ShapeDtypeStruct((rows_x, n_state), BF16)
    cspec = pl.BlockSpec((rows_c, LANES), lambda j: (0, j))
    xspec = pl.BlockSpec((rows_x, LANES), lambda j: (0, j))
    return pl.pallas_call(
        functools.partial(_s5_scan_kernel, halves=batch // 8, kc=kc, kx=kx),
        out_shape=(out, out, out, out),
        grid=(n_state // LANES,),
        in_specs=[pl.BlockSpec((4, LANES), lambda j: (0, j))] + [cspec] * 4 + [xspec] * 4,
        out_specs=(xspec, xspec, xspec, xspec),
        compiler_params=_cparams(("arbitrary",)),
        name="s5_scan",
    )(a_t, *s_ctx, *s_x)


def _s5_apply_kernel(u_ref, t_ref, c_ref, hfr_ref, hfi_ref, hbr_ref, hbi_ref, y_ref):
    hcat = jnp.concatenate([hfr_ref[...], hfi_ref[...], hbr_ref[...], hbi_ref[...]], axis=1)
    yst = _bdot(hcat, c_ref[0])
    n = u_ref.shape[2]
    y_ref[0] = (_bdot(u_ref[0], t_ref[0]) + yst[:, :n]).astype(y_ref.dtype)
    y_ref[1] = (_bdot(u_ref[1], t_ref[1]) + yst[:, n:]).astype(y_ref.dtype)


def _s5_apply(ut, toep, cpow, h_in):
    g, rows, kdim = ut.shape
    tr = min(S5_TR, rows)
    hspec = pl.BlockSpec((tr, LANES), lambda a, r: (r, a))
    return pl.pallas_call(
        _s5_apply_kernel,
        out_shape=jax.ShapeDtypeStruct((g, rows, kdim), BF16),
        grid=(g // 2, rows // tr),
        in_specs=[pl.BlockSpec((2, tr, kdim), lambda a, r: (a, r, 0)),
                  pl.BlockSpec((2, kdim, kdim), lambda a, r: (a, 0, 0)),
                  pl.BlockSpec((1, 4 * LANES, 2 * kdim), lambda a, r: (a, 0, 0))] + [hspec] * 4,
        out_specs=pl.BlockSpec((2, tr, kdim), lambda a, r: (a, r, 0)),
        compiler_params=_cparams(("arbitrary", "arbitrary")),
        name="s5_apply",
    )(ut, toep, cpow, *h_in)


def _chunk_rows_to_tokens(y_ref, perm_ref, ys):
    nrow = CHUNKS_PER_TILE * 8
    blocks = []
    for sl in range(N_SLAB):
        for th in range(CHUNK_T // GRANULES):
            blocks.append(jnp.concatenate(
                [y_ref[sl * GRANULES + m, :, th * LANES:(th + 1) * LANES] for m in range(GRANULES)],
                axis=1))
    acat = _bdot(jnp.concatenate(blocks, axis=0), perm_ref[...])
    blk = 0
    for sl in range(N_SLAB):
        for th in range(CHUNK_T // GRANULES):
            for kk in range(CHUNKS_PER_TILE):
                for j in range(GRANULES):
                    t = kk * CHUNK_T + th * GRANULES + j
                    ys[sl, pl.ds(t, 8, stride=PITCH), :] = acat[blk * nrow + kk * 8:
                                                               blk * nrow + (kk + 1) * 8,
                                                               j * LANES:(j + 1) * LANES]
            blk += 1
    return jnp.concatenate(
        [jnp.concatenate([ys[sl, pl.ds(b * PITCH, ROW_T), :] for sl in range(N_SLAB)], axis=1)
         for b in range(8)], axis=0)


def _back_kernel(x_ref, y_ref, mod_ref, perm_ref, gmix_ref, win_ref, convw_ref, convb_ref,
                 wglu_ref, wsbr_ref, wcbr_ref, wo_ref, gffn_ref, wrhi_ref, wrlo_ref, br_ref,
                 ltri_ref, x1_ref, h_ref, ids_ref, gates_ref, hist_ref, ys_scr):
    d_model = x_ref.shape[2]
    tm = 8 * ROW_T
    d_ssm = N_SLAB * LANES
    xt = x_ref[...].reshape(tm, d_model)
    xn = _rmsnorm(xt, gmix_ref[...])
    hx = _modulate_tile(xn, mod_ref, 0, 1).astype(BF16)
    proj = lambda lo, hi: _bdot(hx, win_ref[:, lo:hi])
    ys_tok = _chunk_rows_to_tokens(y_ref, perm_ref, ys_scr)
    v = proj(0, d_ssm)
    ys = jax.nn.gelu(ys_tok)
    glu = _bdot(ys.astype(BF16), wglu_ref[...])
    gate_c = proj(2 * d_ssm, 3 * d_ssm)
    ys = ys * jax.nn.sigmoid(glu)
    y_s = _bdot(ys.astype(BF16), wsbr_ref[...])
    gate_b = proj(d_ssm, 2 * d_ssm)

    z = gate_c * v
    col = lax.broadcasted_iota(jnp.int32, z.shape, 0) % GRID_W
    z_prev = jnp.where(col == 0, 0.0, pltpu.roll(z, 1, 0))
    z_next = jnp.where(col == GRID_W - 1, 0.0, pltpu.roll(z, tm - 1, 0))
    conv = (z_prev * convw_ref[0:1, :] + z * convw_ref[1:2, :]
            + z_next * convw_ref[2:3, :] + convb_ref[...])
    y_conv = gate_b * conv

    g_s = proj(3 * d_ssm, 3 * d_ssm + d_model)
    y_c = _bdot(y_conv.astype(BF16), wcbr_ref[...])
    merged_s = jax.nn.sigmoid(g_s) * y_s
    g_c = proj(3 * d_ssm + d_model, 3 * d_ssm + 2 * d_model)
    merged = merged_s + jax.nn.sigmoid(g_c) * y_c
    mo = _bdot(merged.astype(BF16), wo_ref[...])
    x1 = xt + (mo.reshape(8, ROW_T, d_model) * mod_ref[:, 2:3, :]).reshape(tm, d_model)
    x1_ref[...] = x1.reshape(8, ROW_T, d_model)

    hn = _modulate_tile(_rmsnorm(x1, gffn_ref[...]), mod_ref, 3, 4)
    for b in range(8):
        for s in range(SUBLANES):
            h_ref[b, pl.ds(s, ROW_T, stride=SUBLANES), :] = hn[b * ROW_T:(b + 1) * ROW_T,
                                                              s * LANES:(s + 1) * LANES]

    hn_hi = hn.astype(BF16)
    hn_lo = (hn - hn_hi.astype(F32)).astype(BF16)
    logits = (_bdot(hn_hi, wrhi_ref[...]) + _bdot(hn_lo, wrhi_ref[...])
              + _bdot(hn_hi, wrlo_ref[...]) + br_ref[...])
    lane = lax.broadcasted_iota(jnp.int32, logits.shape, 1)
    neg = jnp.float32(-jnp.inf)
    cur = jnp.where(lane < N_EXPERTS, logits, neg)
    vals, idxs = [], []
    for _ in range(TOP_K):
        mk = jnp.max(cur, axis=-1, keepdims=True)
        ik = jnp.min(jnp.where(cur == mk, lane, LANES), axis=-1, keepdims=True)
        vals.append(mk)
        idxs.append(ik)
        cur = jnp.where(lane == ik, neg, cur)
    exps = [jnp.exp(vk - vals[0]) for vk in vals]
    denom = exps[0] + exps[1] + exps[2] + exps[3]
    ids = jnp.zeros(logits.shape, jnp.int32)
    gates = jnp.zeros(logits.shape, F32)
    onehot = jnp.zeros(logits.shape, F32)
    for k in range(TOP_K):
        onehot = onehot + (lane == idxs[k]).astype(F32)
    before = _bdot(ltri_ref[...], onehot.astype(BF16))
    for k in range(TOP_K):
        rank_k = jnp.sum(jnp.where(lane == idxs[k], before, 0.0), axis=-1, keepdims=True)
        ids = jnp.where(lane == k, idxs[k], ids)
        ids = jnp.where(lane == TOP_K + k, rank_k.astype(jnp.int32), ids)
        gates = jnp.where(lane == k, exps[k] / denom, gates)
    ids_ref[...] = ids.reshape(8, ROW_T, LANES)
    gates_ref[...] = gates.reshape(8, ROW_T, LANES)
    hist_ref[0] = jnp.broadcast_to(jnp.sum(onehot, axis=0, keepdims=True), (8, LANES))


def _back(x, y_chunks, mod, perm, g_mix, w_rest, conv_w, conv_b, w_glu, w_ssm_br,
          w_conv_br, w_o, g_ffn, w_router, b_router):
    b, s, d = x.shape
    g, _, kdim = y_chunks.shape
    tiles = s // ROW_T
    nrow = CHUNKS_PER_TILE * 8
    tok = lambda n: pl.BlockSpec((8, ROW_T, n), lambda i, j: (i, j, 0))

    def const(arr):
        nd = arr.ndim
        return pl.BlockSpec(arr.shape, lambda i, j: (0,) * nd, pipeline_mode=pl.Buffered(1))

    w_r_hi = w_router.astype(BF16)
    w_r_lo = (w_router - w_r_hi.astype(F32)).astype(BF16)
    row = jnp.arange(8 * ROW_T)
    ltri = (row[:, None] > row[None, :]).astype(BF16)
    params = (perm, g_mix, w_rest, conv_w, conv_b, w_glu, w_ssm_br, w_conv_br, w_o, g_ffn,
              w_r_hi, w_r_lo, b_router, ltri)
    return pl.pallas_call(
        _back_kernel,
        out_shape=(jax.ShapeDtypeStruct((b, s, d), F32),
                   jax.ShapeDtypeStruct((b, s * SUBLANES, LANES), F32),
                   jax.ShapeDtypeStruct((b, s, LANES), jnp.int32),
                   jax.ShapeDtypeStruct((b, s, LANES), F32),
                   jax.ShapeDtypeStruct(((b // 8) * tiles, 8, LANES), F32)),
        grid=(b // 8, tiles),
        in_specs=[tok(d),
                  pl.BlockSpec((g, nrow, kdim), lambda i, j: (0, i * tiles + j, 0)),
                  pl.BlockSpec((8, 8, d), lambda i, j: (i, 0, 0))] + [const(a) for a in params],
        out_specs=(tok(d), pl.BlockSpec((8, ROW_T * SUBLANES, LANES), lambda i, j: (i, j, 0)),
                   tok(LANES), tok(LANES),
                   pl.BlockSpec((1, 8, LANES), lambda i, j: (i * tiles + j, 0, 0))),
        scratch_shapes=[pltpu.VMEM((N_SLAB, 8 * PITCH, LANES), F32)],
        compiler_params=_cparams(("arbitrary", "arbitrary")),
        name="back",
    )(x, y_chunks, mod, *params)


def _routing(ids, ranks, tile_hist, bsz, seq):
    tb = MOE_TB
    n_tok = ids.shape[0]
    n_slot = n_tok * TOP_K
    tile_before = jnp.cumsum(tile_hist, axis=0) - tile_hist
    counts = jnp.sum(tile_hist, axis=0)
    padded = (counts + tb - 1) // tb * tb
    pad_end = jnp.cumsum(padded)
    pad_start = pad_end - padded
    start = jnp.cumsum(counts) - counts
    halves, tiles = bsz // 8, seq // ROW_T
    base = (pad_start[None, :] + tile_before).reshape(halves, 1, tiles, 1, N_EXPERTS)
    base = jnp.broadcast_to(base, (halves, 8, tiles, ROW_T, N_EXPERTS)).reshape(n_tok, N_EXPERTS)
    onehot = ids[:, :, None] == jnp.arange(N_EXPERTS, dtype=jnp.int32)[None, None, :]
    pos = (jnp.sum(jnp.where(onehot, base[:, None, :], 0), axis=-1) + ranks).astype(jnp.int32)
    _, order_slot = lax.sort_key_val(pos.reshape(-1), jnp.arange(n_slot, dtype=jnp.int32))
    order_token = order_slot // TOP_K
    n_blocks = n_slot // tb + N_EXPERTS
    cap = n_blocks * tb
    block_first_row = jnp.arange(n_blocks, dtype=jnp.int32) * tb
    block_expert = jnp.minimum(
        jnp.sum(pad_end[None, :] <= block_first_row[:, None], axis=1, dtype=jnp.int32),
        N_EXPERTS - 1)
    within = (block_first_row - pad_start[block_expert])[:, None] + jnp.arange(tb, dtype=jnp.int32)
    valid = within < counts[block_expert][:, None]
    src = jnp.clip(start[block_expert][:, None] + within, 0, n_slot - 1)
    buf_token = jnp.where(valid, order_token[src], 0).astype(jnp.int32).reshape(cap)
    n_used = (pad_end[-1] // tb).astype(jnp.int32).reshape(1)
    return buf_token, pos, block_expert, n_used


def _to_tiles(ref, val):
    rows = val.shape[0]
    for s in range(SUBLANES):
        ref[pl.ds(s, rows, stride=SUBLANES), :] = val[:, s * LANES:(s + 1) * LANES]


def _tile_piece(ref, first_row, rows, s):
    return ref[pl.ds(first_row * SUBLANES + s, rows, stride=SUBLANES), :]


def _tile_gather_start(idx_ref, src_hbm, dst, sem, n_rows, unrolled, alternate_priority=False,
                       dst_row=lambda r: r):
    def copy(r, t):
        return pltpu.make_async_copy(
            src_hbm.at[pl.ds(pl.multiple_of(t * SUBLANES, SUBLANES), SUBLANES), :],
            dst.at[pl.ds(pl.multiple_of(dst_row(r) * SUBLANES, SUBLANES), SUBLANES), :], sem)

    if unrolled:
        for r in range(n_rows):
            copy(r, idx_ref[0, 0, r]).start(priority=(r % 2) if alternate_priority else 0)
    else:
        def body(r, carry):
            copy(r, idx_ref[0, 0, r]).start()
            return carry
        lax.fori_loop(0, n_rows, body, 0, unroll=8)


def _tile_gather_wait(src_hbm, dst, sem):
    pltpu.make_async_copy(src_hbm.at[pl.ds(0, dst.shape[0]), :], dst, sem).wait()


def _ring_step(i, last, bufs, sem, wait, prefetch, compute):
    n = len(bufs)
    for p in range(n):
        def branch(p=p):
            wait(bufs[p], sem.at[p])
            q = (p + n - 1) % n
            prefetch(bufs[q], sem.at[q])
            compute(bufs[p])

            @pl.when(i == last)
            def _():
                for r in range(1, n):
                    wait(bufs[(p + r) % n], sem.at[(p + r) % n])

        pl.when((i <= last) & (lax.rem(i, n) == p))(branch)


def _cast_rows(src_ref, dst_ref, chunk):
    def body(c, carry):
        rows = pl.ds(pl.multiple_of(c * chunk, chunk), chunk)
        dst_ref[rows, :] = src_ref[0, rows, :].astype(dst_ref.dtype)
        return carry
    lax.fori_loop(0, dst_ref.shape[0] // chunk, body, 0)


def _moe_kernel(be_ref, nu_ref, tok0_ref, tok1_ref, tok2_ref, h_hbm, wgu_ref, bgu_ref, wd_ref,
                bd_ref, out_ref, xbuf0, xbuf1, xbuf2, wgu_bf, wd_bf, sem):
    i = pl.program_id(0)
    n_used = nu_ref[0]
    tb = xbuf0.shape[0] // SUBLANES
    f = wd_ref.shape[1]

    @pl.when(i == 0)
    def _():
        _tile_gather_start(tok0_ref, h_hbm, xbuf0, sem.at[0], tb, unrolled=False)
        _tile_gather_start(tok1_ref, h_hbm, xbuf1, sem.at[1], tb, unrolled=False)

    new_expert = (i == 0) | (be_ref[i] != be_ref[jnp.maximum(i - 1, 0)])

    @pl.when(new_expert & (i < n_used))
    def _():
        _cast_rows(wgu_ref, wgu_bf, 128)
        _cast_rows(wd_ref, wd_bf, 128)

    def compute(xcur):
        xe = jnp.concatenate([_tile_piece(xcur, 0, tb, s) for s in range(SUBLANES)],
                             axis=1).astype(BF16)
        gu = _bdot(xe, wgu_bf[...]) + bgu_ref[0]
        gt = jnp.minimum(gu[:, :f], SWIGLU_LIMIT)
        up = jnp.clip(gu[:, f:], -SWIGLU_LIMIT, SWIGLU_LIMIT)
        act = gt * jax.nn.sigmoid(SWIGLU_ALPHA * gt) * (up + 1.0)
        _to_tiles(out_ref, _bdot(act.astype(BF16), wd_bf[...]) + bd_ref[0])

    _ring_step(i, n_used - 1, (xbuf0, xbuf1, xbuf2), sem,
               wait=lambda buf, s: _tile_gather_wait(h_hbm, buf, s),
               prefetch=lambda buf, s: _tile_gather_start(tok2_ref, h_hbm, buf, s, tb,
                                                          unrolled=True),
               compute=compute)

    @pl.when(i >= n_used)
    def _():
        out_ref[...] = jnp.zeros(out_ref.shape, out_ref.dtype)


def _moe(h_tiles, buf_token, block_expert, n_used, w_gu, b_gu, w_down, b_down):
    e, d, f2 = w_gu.shape
    f = f2 // 2
    tb = MOE_TB
    cap = buf_token.shape[0]
    n_blocks = cap // tb
    tok3 = buf_token.reshape(n_blocks, 1, tb)
    smem_blk = lambda imap: pl.BlockSpec((1, 1, tb), imap, memory_space=pltpu.SMEM)
    grid_spec = pltpu.PrefetchScalarGridSpec(
        num_scalar_prefetch=2,
        grid=(n_blocks,),
        in_specs=[smem_blk(lambda i, be, nu: (i, 0, 0)),
                  smem_blk(lambda i, be, nu: (jnp.minimum(i + 1, n_blocks - 1), 0, 0)),
                  smem_blk(lambda i, be, nu: (jnp.minimum(i + 2, n_blocks - 1), 0, 0)),
                  pl.BlockSpec(memory_space=pl.ANY),
                  pl.BlockSpec((1, d, f2), lambda i, be, nu: (be[i], 0, 0)),
                  pl.BlockSpec((1, 1, f2), lambda i, be, nu: (be[i], 0, 0)),
                  pl.BlockSpec((1, f, d), lambda i, be, nu: (be[i], 0, 0)),
                  pl.BlockSpec((1, 1, d), lambda i, be, nu: (be[i], 0, 0))],
        out_specs=pl.BlockSpec((tb * SUBLANES, LANES), lambda i, be, nu: (i, 0)),
        scratch_shapes=[pltpu.VMEM((tb * SUBLANES, LANES), F32)] * 3
        + [pltpu.VMEM((d, f2), BF16), pltpu.VMEM((f, d), BF16), pltpu.SemaphoreType.DMA((3,))],
    )
    return pl.pallas_call(
        _moe_kernel,
        out_shape=jax.ShapeDtypeStruct((cap * SUBLANES, LANES), F32),
        grid_spec=grid_spec,
        compiler_params=_cparams(("arbitrary",)),
        name="moe",
    )(block_expert, n_used, tok3, tok3, tok3, h_tiles, w_gu, b_gu.reshape(e, 1, f2),
      w_down, b_down.reshape(e, 1, d))


def _combine_kernel(pos0_ref, pos1_ref, pos2_ref, yb_hbm, x1_ref, gates_ref, mod_ref, gfin_ref,
                    out_ref, buf0, buf1, buf2, sem):
    i = pl.program_id(0)
    n = pl.num_programs(0)
    n_rows = buf0.shape[0] // SUBLANES
    tm = n_rows // TOP_K

    slot_major = lambda r: (r % TOP_K) * tm + r // TOP_K

    @pl.when(i == 0)
    def _():
        _tile_gather_start(pos0_ref, yb_hbm, buf0, sem.at[0], n_rows, unrolled=False,
                           dst_row=slot_major)
        _tile_gather_start(pos1_ref, yb_hbm, buf1, sem.at[1], n_rows, unrolled=False,
                           dst_row=slot_major)

    def compute(cur):
        g = gates_ref[...]
        gk = [jnp.broadcast_to(g[:, k:k + 1], (tm, LANES)) for k in range(TOP_K)]
        pieces = []
        for s in range(SUBLANES):
            acc = gk[0] * _tile_piece(cur, 0, tm, s)
            for k in range(1, TOP_K):
                acc = acc + gk[k] * _tile_piece(cur, k * tm, tm, s)
            pieces.append(acc)
        x2 = x1_ref[...] + mod_ref[0, 5:6, :] * jnp.concatenate(pieces, axis=1)
        out_ref[...] = _rmsnorm(x2, gfin_ref[...])

    _ring_step(i, n - 1, (buf0, buf1, buf2), sem,
               wait=lambda buf, s: _tile_gather_wait(yb_hbm, buf, s),
               prefetch=lambda buf, s: _tile_gather_start(pos2_ref, yb_hbm, buf, s, n_rows,
                                                          unrolled=True, alternate_priority=True,
                                                          dst_row=slot_major),
               compute=compute)


def _combine(yb, pos, x1_2d, gates_2d, mod, g_final, seq):
    n_tok, d = x1_2d.shape
    tm = min(COMBINE_TM, seq)
    assert seq % tm == 0
    n_tiles = n_tok // tm
    per_batch = seq // tm
    pos3 = pos.reshape(n_tiles, 1, tm * TOP_K)
    smem_blk = lambda imap: pl.BlockSpec((1, 1, TOP_K * tm), imap, memory_space=pltpu.SMEM)
    buf = pltpu.VMEM((TOP_K * tm * SUBLANES, LANES), F32)
    return pl.pallas_call(
        _combine_kernel,
        out_shape=jax.ShapeDtypeStruct((n_tok, d), F32),
        grid=(n_tiles,),
        in_specs=[smem_blk(lambda i: (i, 0, 0)),
                  smem_blk(lambda i: (jnp.minimum(i + 1, n_tiles - 1), 0, 0)),
                  smem_blk(lambda i: (jnp.minimum(i + 2, n_tiles - 1), 0, 0)),
                  pl.BlockSpec(memory_space=pl.ANY),
                  pl.BlockSpec((tm, d), lambda i: (i, 0)),
                  pl.BlockSpec((tm, LANES), lambda i: (i, 0)),
                  pl.BlockSpec((1, 8, d), lambda i: (i // per_batch, 0, 0)),
                  pl.BlockSpec((1, d), lambda i: (0, 0))],
        out_specs=pl.BlockSpec((tm, d), lambda i: (i, 0)),
        scratch_shapes=[buf, buf, buf, pltpu.SemaphoreType.DMA((3,))],
        compiler_params=_cparams(("arbitrary",)),
        name="combine",
    )(pos3, pos3, pos3, yb, x1_2d, gates_2d, mod, g_final)


def kernel(x, c, ctx, c_ctx, w_mod, b_mod, g_mix, w_in, lam_re, lam_im, log_dt, b_re, b_im,
           c_re, c_im, d_skip, w_glu, conv_w, conv_b, w_ssm_br, w_conv_br, w_o, g_ffn,
           w_router, b_router, w_gu, b_gu, w_down, b_down, g_final):
    depth = w_mod.shape[0]
    assert depth == 1, "single-layer trunk"
    bsz, seq, d = x.shape
    ctx_len = ctx.shape[1]
    d_ssm = d // 2
    assert bsz % 8 == 0 and seq % CHUNK_T == 0 and ctx_len % CHUNK_T == 0 and seq % GRID_W == 0
    assert d == SUBLANES * LANES, "row gathers move one (8,128) f32 tile per token"

    n_cond = -(-(bsz + 1) // 8) * 8
    cond = jnp.zeros((n_cond, d), F32).at[:bsz].set(c).at[bsz].set(c_ctx)
    m = _adaln(cond, w_mod[0], b_mod[0])
    zeros2 = jnp.zeros((n_cond, 2, d), F32)
    mod_all = jnp.concatenate([m.reshape(n_cond, 6, d), zeros2], axis=1)
    mod_x, mod_c = mod_all[:bsz], mod_all[bsz:bsz + 1]

    w_in_bf = w_in[0].astype(BF16)
    w_u, w_rest = w_in_bf[:, :d_ssm], w_in_bf[:, d_ssm:]
    gm = g_mix[0].reshape(1, d)

    perm = _granule_transpose_matrix()
    utx = _front(x, mod_x, gm, w_u, perm)
    utc = _front(ctx, mod_c, gm, w_u, perm)

    smat, toep, cpow, a_t = _s5_matrices(lam_re[0], lam_im[0], log_dt[0], b_re[0], b_im[0],
                                         c_re[0], c_im[0], d_skip[0])
    s_x = _s5_states(utx, smat)
    s_c = _s5_states(utc, smat)
    h_in = _s5_scan(a_t, s_c, s_x, bsz)
    y_chunks = _s5_apply(utx, toep, cpow, h_in)

    pad_r = jnp.zeros((d, LANES - N_EXPERTS), F32)
    w_r = jnp.concatenate([w_router[0], pad_r], axis=1)
    b_r = jnp.concatenate([b_router[0], jnp.zeros((LANES - N_EXPERTS,), F32)]).reshape(1, LANES)
    x1, h, ids, gates, hist = _back(
        x, y_chunks, mod_x, perm, gm, w_rest, conv_w[0], conv_b[0].reshape(1, d_ssm),
        w_glu[0].astype(BF16), w_ssm_br[0].astype(BF16),
        w_conv_br[0].astype(BF16), w_o[0].astype(BF16), g_ffn[0].reshape(1, d), w_r, b_r)

    n_tok = bsz * seq
    ids2 = ids.reshape(n_tok, LANES)
    buf_token, pos, block_expert, n_used = _routing(
        ids2[:, :TOP_K], ids2[:, TOP_K:2 * TOP_K],
        hist[:, 0, :N_EXPERTS].astype(jnp.int32), bsz, seq)
    yb = _moe(h.reshape(n_tok * SUBLANES, LANES), buf_token, block_expert, n_used,
              w_gu[0], b_gu[0], w_down[0], b_down[0])
    out = _combine(yb, pos, x1.reshape(n_tok, d), gates.reshape(n_tok, LANES), mod_x,
                   g_final.reshape(1, d), seq)
    return out.reshape(bsz, seq, d)
```

```python
import functools
import math

import jax
import jax.numpy as jnp
from jax import lax
from jax.experimental import pallas as pl
from jax.experimental.pallas import tpu as pltpu

F32 = jnp.float32
BF16 = jnp.bfloat16
HIGHEST = lax.Precision.HIGHEST

RMS_EPS = 1e-6
GRID_W = 64
SSM_GROUP = 16
SSM_STATE = 64
N_EXPERTS = 32
TOP_K = 4
SWIGLU_LIMIT = 7.0
SWIGLU_ALPHA = 1.702

CHUNK_T = 16
LANES = 128
SUBLANES = 8
V7X_VMEM_LIMIT_BYTES = 56 * 1024 * 1024

S5_TR = 1024
MOE_TB = 512
COMBINE_TM = 256


def _cparams(sem):
    return pltpu.CompilerParams(dimension_semantics=sem,
                                vmem_limit_bytes=V7X_VMEM_LIMIT_BYTES)


def _bdot(a, b):
    return jnp.dot(a, b, preferred_element_type=F32)


def _rmsnorm(xt, g):
    ms = jnp.mean(xt * xt, axis=-1, keepdims=True)
    return xt * lax.rsqrt(ms + RMS_EPS) * g


def _adaln_kernel(c_ref, w_ref, b_ref, o_ref):
    s = jax.nn.silu(c_ref[...])
    o_ref[...] = jnp.dot(s, w_ref[...], precision=HIGHEST,
                         preferred_element_type=F32) + b_ref[...]


def _adaln(cond, w_mod, b_mod):
    r, d = cond.shape
    n = w_mod.shape[1]
    tn = n // 4
    return pl.pallas_call(
        _adaln_kernel,
        out_shape=jax.ShapeDtypeStruct((r, n), F32),
        grid=(n // tn,),
        in_specs=[pl.BlockSpec((r, d), lambda j: (0, 0)),
                  pl.BlockSpec((d, tn), lambda j: (0, j)),
                  pl.BlockSpec((1, tn), lambda j: (0, j))],
        out_specs=pl.BlockSpec((r, tn), lambda j: (0, j)),
        compiler_params=_cparams(("arbitrary",)),
        name="adaln",
    )(cond, w_mod, b_mod.reshape(1, n))


ROW_T = GRID_W
PITCH = ROW_T + 8
N_SLAB = 4
CHUNKS_PER_TILE = ROW_T // CHUNK_T
GRANULES = LANES // SSM_GROUP


def _granule_transpose_matrix():
    n = GRANULES * LANES
    idx = jnp.arange(n)
    j, m, c = idx // LANES, (idx % LANES) // SSM_GROUP, idx % SSM_GROUP
    dst = m * LANES + j * SSM_GROUP + c
    return jnp.zeros((n, n), F32).at[idx, dst].set(1.0).astype(BF16)


def _modulate_tile(xn, mod_ref, shift_row, scale_row):
    d = xn.shape[1]
    x3 = xn.reshape(8, ROW_T, d)
    x3 = x3 * (1.0 + mod_ref[:, scale_row:scale_row + 1, :]) + mod_ref[:, shift_row:shift_row + 1, :]
    return x3.reshape(8 * ROW_T, d)


def _front_kernel(x_ref, mod_ref, g_ref, w_ref, perm_ref, u_ref, us):
    d = x_ref.shape[2]
    xn = _rmsnorm(x_ref[...].reshape(8 * ROW_T, d), g_ref[...])
    hx = _modulate_tile(xn, mod_ref, 0, 1)
    u = _bdot(hx.astype(BF16), w_ref[...])
    for sl in range(N_SLAB):
        for b in range(8):
            us[sl, pl.ds(b * PITCH, ROW_T), :] = u[b * ROW_T:(b + 1) * ROW_T,
                                                  sl * LANES:(sl + 1) * LANES]
    blocks = []
    for sl in range(N_SLAB):
        for th in range(CHUNK_T // GRANULES):
            rows = []
            for kk in range(CHUNKS_PER_TILE):
                t0 = kk * CHUNK_T + th * GRANULES
                rows.append(jnp.concatenate(
                    [us[sl, pl.ds(t0 + j, 8, stride=PITCH), :] for j in range(GRANULES)], axis=1))
            blocks.append(jnp.concatenate(rows, axis=0))
    acat = jnp.concatenate(blocks, axis=0).astype(BF16)
    bmat = _bdot(acat, perm_ref[...])
    nrow = CHUNKS_PER_TILE * 8
    blk = 0
    for sl in range(N_SLAB):
        for th in range(CHUNK_T // GRANULES):
            for m in range(GRANULES):
                u_ref[sl * GRANULES + m, :, th * LANES:(th + 1) * LANES] = (
                    bmat[blk * nrow:(blk + 1) * nrow, m * LANES:(m + 1) * LANES].astype(u_ref.dtype))
            blk += 1


def _front(x, mod, g_mix, w_u, perm):
    b, s, d = x.shape
    n = w_u.shape[1]
    g = n // SSM_GROUP
    assert s % ROW_T == 0 and b % 8 == 0 and n == N_SLAB * LANES
    k = s // CHUNK_T
    tiles = s // ROW_T
    nrow = CHUNKS_PER_TILE * 8
    shared = mod.shape[0] == 1
    mod_spec = (pl.BlockSpec((1, 8, d), lambda i, j: (0, 0, 0)) if shared
                else pl.BlockSpec((8, 8, d), lambda i, j: (i, 0, 0)))
    return pl.pallas_call(
        _front_kernel,
        out_shape=jax.ShapeDtypeStruct((g, (b // 8) * k * 8, CHUNK_T * SSM_GROUP), BF16),
        grid=(b // 8, tiles),
        in_specs=[pl.BlockSpec((8, ROW_T, d), lambda i, j: (i, j, 0)),
                  mod_spec,
                  pl.BlockSpec((1, d), lambda i, j: (0, 0)),
                  pl.BlockSpec((d, n), lambda i, j: (0, 0)),
                  pl.BlockSpec(perm.shape, lambda i, j: (0, 0))],
        out_specs=pl.BlockSpec((g, nrow, CHUNK_T * SSM_GROUP), lambda i, j: (0, i * tiles + j, 0)),
        scratch_shapes=[pltpu.VMEM((N_SLAB, 8 * PITCH, LANES), F32)],
        compiler_params=_cparams(("arbitrary", "arbitrary")),
        name="front",
    )(x, mod, g_mix, w_u, perm)


P_LANES = 3 * LANES


def _s5_params_kernel(lam_ref, bt_ref, ct_ref, dsk_ref, rep_ref,
                      toep_ref, smat_ref, cpow_ref, at_ref):
    t, ch, p = CHUNK_T, SSM_GROUP, SSM_STATE
    width = t * ch
    f32dot = lambda a, b: jnp.dot(a, b, precision=HIGHEST, preferred_element_type=F32)
    eye = (lax.broadcasted_iota(jnp.int32, (p, p), 0) == lax.broadcasted_iota(jnp.int32, (p, p), 1))
    to_col = lambda row: jnp.sum(jnp.where(eye, jnp.broadcast_to(row, (p, p)), 0.0),
                                 axis=1, keepdims=True)
    blk = lax.broadcasted_iota(jnp.int32, (p, P_LANES), 1) // ch
    row16 = lax.broadcasted_iota(jnp.int32, (ch, width), 0)
    lane16 = lax.broadcasted_iota(jnp.int32, (ch, width), 1)

    smats, cpows, ats = [[], []], [[], []], []
    for j in range(2):
        kt, at_dir = [], []
        for z in range(2):
            lr = lam_ref[j, 4 * z:4 * z + 1, :]
            li = lam_ref[j, 4 * z + 1:4 * z + 2, :]
            dt = jnp.exp(lam_ref[j, 4 * z + 2:4 * z + 3, :])
            mag = jnp.exp(lr * dt)
            ar, ai = mag * jnp.cos(li * dt), mag * jnp.sin(li * dt)
            den = lr * lr + li * li
            qr = ((ar - 1.0) * lr + ai * li) / den
            qi = (ai * lr - (ar - 1.0) * li) / den
            bbr = qr * bt_ref[j, 2 * z] - qi * bt_ref[j, 2 * z + 1]
            bbi = qr * bt_ref[j, 2 * z + 1] + qi * bt_ref[j, 2 * z]
            pw = [(jnp.ones_like(ar), jnp.zeros_like(ai))]
            for _ in range(t):
                r, i = pw[-1]
                pw.append((r * ar - i * ai, r * ai + i * ar))
            at_dir.append(pw[t])

            expo = jnp.where(blk <= t, blk if z == 0 else t - blk, 0)
            tr = jnp.ones((p, P_LANES), F32)
            ti = jnp.zeros((p, P_LANES), F32)
            for bit in range(5):
                sr, si = to_col(pw[1 << bit][0]), to_col(pw[1 << bit][1])
                on = ((expo >> bit) & 1) == 1
                fr, fi = jnp.where(on, sr, 1.0), jnp.where(on, si, 0.0)
                tr, ti = tr * fr - ti * fi, tr * fi + ti * fr
            ctr = f32dot(ct_ref[j, 2 * z], rep_ref[...])
            cti = f32dot(ct_ref[j, 2 * z + 1], rep_ref[...])
            crd = ctr * tr - cti * ti
            cid = ctr * ti + cti * tr
            kt.append(f32dot(bbr, crd) - f32dot(bbi, cid))

            off = ch if z == 0 else 0
            cpows[j].append(crd[:, off:off + width])
            cpows[j].append(-cid[:, off:off + width])

            pows = [pw[t - 1 - s] if z == 0 else pw[s] for s in range(t)]
            pr_rows = jnp.concatenate([jnp.broadcast_to(q[0], (ch, p)) for q in pows], axis=0)
            pi_rows = jnp.concatenate([jnp.broadcast_to(q[1], (ch, p)) for q in pows], axis=0)
            br_rows = jnp.concatenate([bbr] * t, axis=0)
            bi_rows = jnp.concatenate([bbi] * t, axis=0)
            smats[j].append(pr_rows * br_rows - pi_rows * bi_rows)
            smats[j].append(pr_rows * bi_rows + pi_rows * br_rows)
        ats.append(at_dir)

        ktf = kt[0][:, :width]
        ktb = kt[1][:, ch:ch + width]
        rows = []
        for s in range(t):
            fwd = ktf if s == 0 else jnp.concatenate(
                [jnp.zeros((ch, ch * s), F32), ktf[:, :width - ch * s]], axis=1)
            sh = ch * (t - 1 - s)
            bwd = ktb if sh == 0 else jnp.concatenate(
                [ktb[:, sh:], jnp.zeros((ch, sh), F32)], axis=1)
            skip = jnp.where(lane16 == ch * s + row16, dsk_ref[j], 0.0)
            rows.append(fwd + bwd + skip)
        toep_ref[j] = jnp.concatenate(rows, axis=0).astype(toep_ref.dtype)

    zs = jnp.zeros((width, p), F32)
    smat_ref[0] = jnp.concatenate(
        [jnp.concatenate([jnp.concatenate([smats[0][q], zs], axis=1) for q in range(4)], axis=1),
         jnp.concatenate([jnp.concatenate([zs, smats[1][q]], axis=1) for q in range(4)], axis=1)],
        axis=0).astype(smat_ref.dtype)
    zc = jnp.zeros((p, width), F32)
    cpow_ref[0] = jnp.concatenate(
        [jnp.concatenate([cpows[j][q], zc] if j == 0 else [zc, cpows[j][q]], axis=1)
         for q in range(4) for j in range(2)], axis=0).astype(cpow_ref.dtype)
    at_ref[...] = jnp.concatenate(
        [jnp.concatenate([ats[0][z][part], ats[1][z][part]], axis=1)
         for z in range(2) for part in range(2)], axis=0)


def _s5_matrices(lam_re, lam_im, log_dt, b_re, b_im, c_re, c_im, d_skip):
    t, ch = CHUNK_T, SSM_GROUP
    _, g, p = lam_re.shape
    assert p == SSM_STATE and b_re.shape[-1] == ch and 2 * p == LANES
    width = t * ch
    zero = jnp.zeros((g, 1, p), F32)
    lam = jnp.concatenate(
        [jnp.stack([lam_re[z], lam_im[z], jnp.broadcast_to(log_dt[z][:, None], (g, p))], axis=1)
         if part == 0 else zero for z in range(2) for part in range(2)], axis=1)
    bt = jnp.stack([b_re[0], b_im[0], b_re[1], b_im[1]], axis=1).transpose(0, 1, 3, 2)
    ct = jnp.stack([c_re[0], c_im[0], c_re[1], c_im[1]], axis=1).transpose(0, 1, 3, 2)
    dsk = jnp.tile(d_skip.reshape(g, 1, ch), (1, 1, t))
    lane = jnp.arange(P_LANES)
    rep = ((lane[None, :] % ch == jnp.arange(ch)[:, None])
           & (lane[None, :] < (t + 1) * ch)).astype(F32)
    pair = lambda *shape: pl.BlockSpec((2,) + shape, lambda a: (a,) + (0,) * len(shape))
    toep, smat, cpow, a_t = pl.pallas_call(
        _s5_params_kernel,
        out_shape=(jax.ShapeDtypeStruct((g, width, width), BF16),
                   jax.ShapeDtypeStruct((g // 2, 2 * width, 4 * LANES), BF16),
                   jax.ShapeDtypeStruct((g // 2, 4 * LANES, 2 * width), BF16),
                   jax.ShapeDtypeStruct((4, g * p), F32)),
        grid=(g // 2,),
        in_specs=[pair(8, p), pair(4, ch, p), pair(4, p, ch), pair(1, width),
                  pl.BlockSpec((ch, P_LANES), lambda a: (0, 0))],
        out_specs=(pair(width, width),
                   pl.BlockSpec((1, 2 * width, 4 * LANES), lambda a: (a, 0, 0)),
                   pl.BlockSpec((1, 4 * LANES, 2 * width), lambda a: (a, 0, 0)),
                   pl.BlockSpec((4, LANES), lambda a: (0, a))),
        compiler_params=_cparams(("arbitrary",)),
        name="s5_params",
    )(lam, bt, ct, dsk, rep)
    return smat, toep, cpow, a_t


def _chunk_states(u_ref, smat, dst_refs, rows):
    tr = min(S5_TR, rows)
    for r0 in range(0, rows, tr):
        lhs = jnp.concatenate([u_ref[0, r0:r0 + tr, :], u_ref[1, r0:r0 + tr, :]], axis=1)
        res = _bdot(lhs, smat)
        for q in range(4):
            dst_refs[q][r0:r0 + tr, :] = res[:, q * LANES:(q + 1) * LANES]


def _scan_chunks(a_ref, c_refs, x_refs, h_refs, halves, kc, kx):
    cfr_ref, cfi_ref, cbr_ref, cbi_ref = c_refs
    xfr_ref, xfi_ref, xbr_ref, xbi_ref = x_refs
    hfr_ref, hfi_ref, hbr_ref, hbi_ref = h_refs
    a_fr, a_fi = a_ref[0:1, :], a_ref[1:2, :]
    a_br, a_bi = a_ref[2:3, :], a_ref[3:4, :]

    def rows(half, k, n_chunks, count=1):
        return pl.ds(pl.multiple_of((half * n_chunks + k) * 8, 8 * count), 8 * count)

    def step(h_re, h_im, a_re, a_im, s_re, s_im):
        return (a_re * h_re - a_im * h_im + s_re, a_re * h_im + a_im * h_re + s_im)

    zero = jnp.zeros((8, LANES), F32)

    def ctx_body(i, carry):
        out = []
        for half in range(halves):
            fr, fi, br, bi = carry[half]
            kf, kb = i, kc - 1 - i
            fr, fi = step(fr, fi, a_fr, a_fi, cfr_ref[rows(half, kf, kc), :],
                          cfi_ref[rows(half, kf, kc), :])
            br, bi = step(br, bi, a_br, a_bi, cbr_ref[rows(half, kb, kc), :],
                          cbi_ref[rows(half, kb, kc), :])
            out.append((fr, fi, br, bi))
        return tuple(out)

    carry = lax.fori_loop(0, kc, ctx_body, ((zero,) * 4,) * halves)

    def x_body(i, carry):
        out = []
        for half in range(halves):
            fr, fi, br, bi = carry[half]
            kf = 2 * i
            kb = kx - 2 - 2 * i
            fr1, fi1 = step(fr, fi, a_fr, a_fi, xfr_ref[rows(half, kf, kx), :],
                            xfi_ref[rows(half, kf, kx), :])
            br1, bi1 = step(br, bi, a_br, a_bi, xbr_ref[rows(half, kb + 1, kx), :],
                            xbi_ref[rows(half, kb + 1, kx), :])
            hfr_ref[rows(half, kf, kx, 2), :] = jnp.concatenate([fr, fr1], 0).astype(hfr_ref.dtype)
            hfi_ref[rows(half, kf, kx, 2), :] = jnp.concatenate([fi, fi1], 0).astype(hfi_ref.dtype)
            hbr_ref[rows(half, kb, kx, 2), :] = jnp.concatenate([br1, br], 0).astype(hbr_ref.dtype)
            hbi_ref[rows(half, kb, kx, 2), :] = jnp.concatenate([bi1, bi], 0).astype(hbi_ref.dtype)
            fr, fi = step(fr1, fi1, a_fr, a_fi, xfr_ref[rows(half, kf + 1, kx), :],
                          xfi_ref[rows(half, kf + 1, kx), :])
            br, bi = step(br1, bi1, a_br, a_bi, xbr_ref[rows(half, kb, kx), :],
                          xbi_ref[rows(half, kb, kx), :])
            out.append((fr, fi, br, bi))
        return tuple(out)

    lax.fori_loop(0, kx // 2, x_body, carry)


def _s5_chunks_kernel(ux_ref, uc_ref, smat_ref, toep_ref, cpow_ref, a_ref, y_ref,
                      *scratch, halves, kc, kx):
    s_x, s_c, h_in = scratch[0:4], scratch[4:8], scratch[8:12]
    rows_x, rows_c = ux_ref.shape[1], uc_ref.shape[1]
    _chunk_states(uc_ref, smat_ref[0], s_c, rows_c)
    _chunk_states(ux_ref, smat_ref[0], s_x, rows_x)
    _scan_chunks(a_ref, s_c, s_x, h_in, halves, kc, kx)
    n = ux_ref.shape[2]
    tr = min(S5_TR, rows_x)
    for r0 in range(0, rows_x, tr):
        hcat = jnp.concatenate([h[r0:r0 + tr, :] for h in h_in], axis=1)
        yst = _bdot(hcat, cpow_ref[0])
        for j in range(2):
            y_ref[j, r0:r0 + tr, :] = (_bdot(ux_ref[j, r0:r0 + tr, :], toep_ref[j])
                                       + yst[:, j * n:(j + 1) * n]).astype(y_ref.dtype)


def _s5_chunks(utx, utc, smat, toep, cpow, a_t, batch):
    g, rows_x, kdim = utx.shape
    rows_c = utc.shape[1]
    kc, kx = rows_c // batch, rows_x // batch
    assert kx % 2 == 0 and rows_x % min(S5_TR, rows_x) == 0
    pair = lambda rows: pl.BlockSpec((2, rows, kdim), lambda a: (a, 0, 0))
    return pl.pallas_call(
        functools.partial(_s5_chunks_kernel, halves=batch // 8, kc=kc, kx=kx),
        out_shape=jax.ShapeDtypeStruct((g, rows_x, kdim), BF16),
        grid=(g // 2,),
        in_specs=[pair(rows_x), pair(rows_c),
                  pl.BlockSpec((1, 2 * kdim, 4 * LANES), lambda a: (a, 0, 0)),
                  pl.BlockSpec((2, kdim, kdim), lambda a: (a, 0, 0)),
                  pl.BlockSpec((1, 4 * LANES, 2 * kdim), lambda a: (a, 0, 0)),
                  pl.BlockSpec((4, LANES), lambda a: (0, a))],
        out_specs=pair(rows_x),
        scratch_shapes=([pltpu.VMEM((rows_x, LANES), F32)] * 4
                        + [pltpu.VMEM((rows_c, LANES), F32)] * 4
                        + [pltpu.VMEM((rows_x, LANES), BF16)] * 4),
        compiler_params=_cparams(("arbitrary",)),
        name="s5_chunks",
    )(utx, utc, smat, toep, cpow, a_t)


def _chunk_rows_to_tokens(y_ref, perm_ref, ys):
    nrow = CHUNKS_PER_TILE * 8
    blocks = []
    for sl in range(N_SLAB):
        for th in range(CHUNK_T // GRANULES):
            blocks.append(jnp.concatenate(
                [y_ref[sl * GRANULES + m, :, th * LANES:(th + 1) * LANES] for m in range(GRANULES)],
                axis=1))
    acat = _bdot(jnp.concatenate(blocks, axis=0), perm_ref[...])
    blk = 0
    for sl in range(N_SLAB):
        for th in range(CHUNK_T // GRANULES):
            for kk in range(CHUNKS_PER_TILE):
                for j in range(GRANULES):
                    t = kk * CHUNK_T + th * GRANULES + j
                    ys[sl, pl.ds(t, 8, stride=PITCH), :] = acat[blk * nrow + kk * 8:
                                                               blk * nrow + (kk + 1) * 8,
                                                               j * LANES:(j + 1) * LANES]
            blk += 1
    return jnp.concatenate(
        [jnp.concatenate([ys[sl, pl.ds(b * PITCH, ROW_T), :] for sl in range(N_SLAB)], axis=1)
         for b in range(8)], axis=0)


def _back_kernel(x_ref, y_ref, mod_ref, perm_ref, gmix_ref, win_ref, convw_ref, convb_ref,
                 wglu_ref, wsbr_ref, wcbr_ref, wo_ref, gffn_ref, wrhi_ref, wrlo_ref, br_ref,
                 ltri_ref, x1_ref, h_ref, ids_ref, gates_ref, hist_ref, ys_scr):
    d_model = x_ref.shape[2]
    tm = 8 * ROW_T
    d_ssm = N_SLAB * LANES
    xt = x_ref[...].reshape(tm, d_model)
    xn = _rmsnorm(xt, gmix_ref[...])
    hx = _modulate_tile(xn, mod_ref, 0, 1).astype(BF16)
    proj = lambda lo, hi: _bdot(hx, win_ref[:, lo:hi])
    ys_tok = _chunk_rows_to_tokens(y_ref, perm_ref, ys_scr)
    v = proj(0, d_ssm)
    ys = jax.nn.gelu(ys_tok)
    glu = _bdot(ys.astype(BF16), wglu_ref[...])
    gate_c = proj(2 * d_ssm, 3 * d_ssm)
    ys = ys * jax.nn.sigmoid(glu)
    y_s = _bdot(ys.astype(BF16), wsbr_ref[...])
    gate_b = proj(d_ssm, 2 * d_ssm)

    z = gate_c * v
    col = lax.broadcasted_iota(jnp.int32, z.shape, 0) % GRID_W
    z_prev = jnp.where(col == 0, 0.0, pltpu.roll(z, 1, 0))
    z_next = jnp.where(col == GRID_W - 1, 0.0, pltpu.roll(z, tm - 1, 0))
    conv = (z_prev * convw_ref[0:1, :] + z * convw_ref[1:2, :]
            + z_next * convw_ref[2:3, :] + convb_ref[...])
    y_conv = gate_b * conv

    g_s = proj(3 * d_ssm, 3 * d_ssm + d_model)
    y_c = _bdot(y_conv.astype(BF16), wcbr_ref[...])
    merged_s = jax.nn.sigmoid(g_s) * y_s
    g_c = proj(3 * d_ssm + d_model, 3 * d_ssm + 2 * d_model)
    merged = merged_s + jax.nn.sigmoid(g_c) * y_c
    mo = _bdot(merged.astype(BF16), wo_ref[...])
    x1 = xt + (mo.reshape(8, ROW_T, d_model) * mod_ref[:, 2:3, :]).reshape(tm, d_model)
    x1_ref[...] = x1.reshape(8, ROW_T, d_model)

    hn = _modulate_tile(_rmsnorm(x1, gffn_ref[...]), mod_ref, 3, 4)
    for b in range(8):
        for s in range(SUBLANES):
            h_ref[b, pl.ds(s, ROW_T, stride=SUBLANES), :] = hn[b * ROW_T:(b + 1) * ROW_T,
                                                              s * LANES:(s + 1) * LANES]

    hn_hi = hn.astype(BF16)
    hn_lo = (hn - hn_hi.astype(F32)).astype(BF16)
    logits = (_bdot(hn_hi, wrhi_ref[...]) + _bdot(hn_lo, wrhi_ref[...])
              + _bdot(hn_hi, wrlo_ref[...]) + br_ref[...])
    lane = lax.broadcasted_iota(jnp.int32, logits.shape, 1)
    neg = jnp.float32(-jnp.inf)
    cur = jnp.where(lane < N_EXPERTS, logits, neg)
    vals, idxs = [], []
    for _ in range(TOP_K):
        mk = jnp.max(cur, axis=-1, keepdims=True)
        ik = jnp.min(jnp.where(cur == mk, lane, LANES), axis=-1, keepdims=True)
        vals.append(mk)
        idxs.append(ik)
        cur = jnp.where(lane == ik, neg, cur)
    exps = [jnp.exp(vk - vals[0]) for vk in vals]
    denom = exps[0] + exps[1] + exps[2] + exps[3]
    ids = jnp.zeros(logits.shape, jnp.int32)
    gates = jnp.zeros(logits.shape, F32)
    onehot = jnp.zeros(logits.shape, F32)
    for k in range(TOP_K):
        onehot = onehot + (lane == idxs[k]).astype(F32)
    before = _bdot(ltri_ref[...], onehot.astype(BF16))
    for k in range(TOP_K):
        rank_k = jnp.sum(jnp.where(lane == idxs[k], before, 0.0), axis=-1, keepdims=True)
        ids = jnp.where(lane == k, idxs[k], ids)
        ids = jnp.where(lane == TOP_K + k, rank_k.astype(jnp.int32), ids)
        gates = jnp.where(lane == k, exps[k] / denom, gates)
    ids_ref[...] = ids.reshape(8, ROW_T, LANES)
    gates_ref[...] = gates.reshape(8, ROW_T, LANES)
    hist_ref[0] = jnp.broadcast_to(jnp.sum(onehot, axis=0, keepdims=True), (8, LANES))


def _back(x, y_chunks, mod, perm, g_mix, w_rest, conv_w, conv_b, w_glu, w_ssm_br,
          w_conv_br, w_o, g_ffn, w_router, b_router):
    b, s, d = x.shape
    g, _, kdim = y_chunks.shape
    tiles = s // ROW_T
    nrow = CHUNKS_PER_TILE * 8
    tok = lambda n: pl.BlockSpec((8, ROW_T, n), lambda i, j: (i, j, 0))

    def const(arr):
        nd = arr.ndim
        return pl.BlockSpec(arr.shape, lambda i, j: (0,) * nd, pipeline_mode=pl.Buffered(1))

    w_r_hi = w_router.astype(BF16)
    w_r_lo = (w_router - w_r_hi.astype(F32)).astype(BF16)
    row = jnp.arange(8 * ROW_T)
    ltri = (row[:, None] > row[None, :]).astype(BF16)
    params = (perm, g_mix, w_rest, conv_w, conv_b, w_glu, w_ssm_br, w_conv_br, w_o, g_ffn,
              w_r_hi, w_r_lo, b_router, ltri)
    return pl.pallas_call(
        _back_kernel,
        out_shape=(jax.ShapeDtypeStruct((b, s, d), F32),
                   jax.ShapeDtypeStruct((b, s * SUBLANES, LANES), F32),
                   jax.ShapeDtypeStruct((b, s, LANES), jnp.int32),
                   jax.ShapeDtypeStruct((b, s, LANES), F32),
                   jax.ShapeDtypeStruct(((b // 8) * tiles, 8, LANES), F32)),
        grid=(b // 8, tiles),
        in_specs=[tok(d),
                  pl.BlockSpec((g, nrow, kdim), lambda i, j: (0, i * tiles + j, 0)),
                  pl.BlockSpec((8, 8, d), lambda i, j: (i, 0, 0))] + [const(a) for a in params],
        out_specs=(tok(d), pl.BlockSpec((8, ROW_T * SUBLANES, LANES), lambda i, j: (i, j, 0)),
                   tok(LANES), tok(LANES),
                   pl.BlockSpec((1, 8, LANES), lambda i, j: (i * tiles + j, 0, 0))),
        scratch_shapes=[pltpu.VMEM((N_SLAB, 8 * PITCH, LANES), F32)],
        compiler_params=_cparams(("arbitrary", "arbitrary")),
        name="back",
    )(x, y_chunks, mod, *params)


def _routing(ids, ranks, tile_hist, bsz, seq):
    tb = MOE_TB
    n_tok = ids.shape[0]
    n_slot = n_tok * TOP_K
    tile_before = jnp.cumsum(tile_hist, axis=0) - tile_hist
    counts = jnp.sum(tile_hist, axis=0)
    padded = (counts + tb - 1) // tb * tb
    pad_end = jnp.cumsum(padded)
    pad_start = pad_end - padded
    start = jnp.cumsum(counts) - counts
    halves, tiles = bsz // 8, seq // ROW_T
    base = (pad_start[None, :] + tile_before).reshape(halves, 1, tiles, 1, N_EXPERTS)
    base = jnp.broadcast_to(base, (halves, 8, tiles, ROW_T, N_EXPERTS)).reshape(n_tok, N_EXPERTS)
    onehot = ids[:, :, None] == jnp.arange(N_EXPERTS, dtype=jnp.int32)[None, None, :]
    pos = (jnp.sum(jnp.where(onehot, base[:, None, :], 0), axis=-1) + ranks).astype(jnp.int32)
    _, order_slot = lax.sort_key_val(pos.reshape(-1), jnp.arange(n_slot, dtype=jnp.int32))
    order_token = order_slot // TOP_K
    n_blocks = n_slot // tb + N_EXPERTS
    cap = n_blocks * tb
    block_first_row = jnp.arange(n_blocks, dtype=jnp.int32) * tb
    block_expert = jnp.minimum(
        jnp.sum(pad_end[None, :] <= block_first_row[:, None], axis=1, dtype=jnp.int32),
        N_EXPERTS - 1)
    within = (block_first_row - pad_start[block_expert])[:, None] + jnp.arange(tb, dtype=jnp.int32)
    valid = within < counts[block_expert][:, None]
    src = jnp.clip(start[block_expert][:, None] + within, 0, n_slot - 1)
    buf_token = jnp.where(valid, order_token[src], 0).astype(jnp.int32).reshape(cap)
    n_used = (pad_end[-1] // tb).astype(jnp.int32).reshape(1)
    return buf_token, pos, block_expert, n_used


def _to_tiles(ref, val):
    rows = val.shape[0]
    for s in range(SUBLANES):
        ref[pl.ds(s, rows, stride=SUBLANES), :] = val[:, s * LANES:(s + 1) * LANES]


def _tile_piece(ref, first_row, rows, s):
    return ref[pl.ds(first_row * SUBLANES + s, rows, stride=SUBLANES), :]


def _tile_gather_start(idx_ref, src_hbm, dst, sem, n_rows, unrolled, alternate_priority=False,
                       dst_row=lambda r: r):
    def copy(r, t):
        return pltpu.make_async_copy(
            src_hbm.at[pl.ds(pl.multiple_of(t * SUBLANES, SUBLANES), SUBLANES), :],
            dst.at[pl.ds(pl.multiple_of(dst_row(r) * SUBLANES, SUBLANES), SUBLANES), :], sem)

    if unrolled:
        for r in range(n_rows):
            copy(r, idx_ref[0, 0, r]).start(priority=(r % 2) if alternate_priority else 0)
    else:
        def body(r, carry):
            copy(r, idx_ref[0, 0, r]).start()
            return carry
        lax.fori_loop(0, n_rows, body, 0, unroll=8)


def _tile_gather_wait(src_hbm, dst, sem):
    pltpu.make_async_copy(src_hbm.at[pl.ds(0, dst.shape[0]), :], dst, sem).wait()


def _ring_step(i, last, bufs, sem, wait, prefetch, compute):
    n = len(bufs)
    for p in range(n):
        def branch(p=p):
            wait(bufs[p], sem.at[p])
            q = (p + n - 1) % n
            prefetch(bufs[q], sem.at[q])
            compute(bufs[p])

            @pl.when(i == last)
            def _():
                for r in range(1, n):
                    wait(bufs[(p + r) % n], sem.at[(p + r) % n])

        pl.when((i <= last) & (lax.rem(i, n) == p))(branch)


def _cast_rows(src_ref, dst_ref, chunk):
    def body(c, carry):
        rows = pl.ds(pl.multiple_of(c * chunk, chunk), chunk)
        dst_ref[rows, :] = src_ref[0, rows, :].astype(dst_ref.dtype)
        return carry
    lax.fori_loop(0, dst_ref.shape[0] // chunk, body, 0)


def _moe_kernel(be_ref, nu_ref, tok0_ref, tok1_ref, tok2_ref, h_hbm, wgu_ref, bgu_ref, wd_ref,
                bd_ref, out_ref, xbuf0, xbuf1, xbuf2, wgu_bf, wd_bf, sem):
    i = pl.program_id(0)
    n_used = nu_ref[0]
    tb = xbuf0.shape[0] // SUBLANES
    f = wd_ref.shape[1]

    @pl.when(i == 0)
    def _():
        _tile_gather_start(tok0_ref, h_hbm, xbuf0, sem.at[0], tb, unrolled=False)
        _tile_gather_start(tok1_ref, h_hbm, xbuf1, sem.at[1], tb, unrolled=False)

    new_expert = (i == 0) | (be_ref[i] != be_ref[jnp.maximum(i - 1, 0)])

    @pl.when(new_expert & (i < n_used))
    def _():
        _cast_rows(wgu_ref, wgu_bf, 128)
        _cast_rows(wd_ref, wd_bf, 128)

    def compute(xcur):
        xe = jnp.concatenate([_tile_piece(xcur, 0, tb, s) for s in range(SUBLANES)],
                             axis=1).astype(BF16)
        gu = _bdot(xe, wgu_bf[...]) + bgu_ref[0]
        gt = jnp.minimum(gu[:, :f], SWIGLU_LIMIT)
        up = jnp.clip(gu[:, f:], -SWIGLU_LIMIT, SWIGLU_LIMIT)
        act = gt * jax.nn.sigmoid(SWIGLU_ALPHA * gt) * (up + 1.0)
        _to_tiles(out_ref, _bdot(act.astype(BF16), wd_bf[...]) + bd_ref[0])

    _ring_step(i, n_used - 1, (xbuf0, xbuf1, xbuf2), sem,
               wait=lambda buf, s: _tile_gather_wait(h_hbm, buf, s),
               prefetch=lambda buf, s: _tile_gather_start(tok2_ref, h_hbm, buf, s, tb,
                                                          unrolled=True),
               compute=compute)

    @pl.when(i >= n_used)
    def _():
        out_ref[...] = jnp.zeros(out_ref.shape, out_ref.dtype)


def _moe(h_tiles, buf_token, block_expert, n_used, w_gu, b_gu, w_down, b_down):
    e, d, f2 = w_gu.shape
    f = f2 // 2
    tb = MOE_TB
    cap = buf_token.shape[0]
    n_blocks = cap // tb
    tok3 = buf_token.reshape(n_blocks, 1, tb)
    smem_blk = lambda imap: pl.BlockSpec((1, 1, tb), imap, memory_space=pltpu.SMEM)
    grid_spec = pltpu.PrefetchScalarGridSpec(
        num_scalar_prefetch=2,
        grid=(n_blocks,),
        in_specs=[smem_blk(lambda i, be, nu: (i, 0, 0)),
                  smem_blk(lambda i, be, nu: (jnp.minimum(i + 1, n_blocks - 1), 0, 0)),
                  smem_blk(lambda i, be, nu: (jnp.minimum(i + 2, n_blocks - 1), 0, 0)),
                  pl.BlockSpec(memory_space=pl.ANY),
                  pl.BlockSpec((1, d, f2), lambda i, be, nu: (be[i], 0, 0)),
                  pl.BlockSpec((1, 1, f2), lambda i, be, nu: (be[i], 0, 0)),
                  pl.BlockSpec((1, f, d), lambda i, be, nu: (be[i], 0, 0)),
                  pl.BlockSpec((1, 1, d), lambda i, be, nu: (be[i], 0, 0))],
        out_specs=pl.BlockSpec((tb * SUBLANES, LANES), lambda i, be, nu: (i, 0)),
        scratch_shapes=[pltpu.VMEM((tb * SUBLANES, LANES), F32)] * 3
        + [pltpu.VMEM((d, f2), BF16), pltpu.VMEM((f, d), BF16), pltpu.SemaphoreType.DMA((3,))],
    )
    return pl.pallas_call(
        _moe_kernel,
        out_shape=jax.ShapeDtypeStruct((cap * SUBLANES, LANES), F32),
        grid_spec=grid_spec,
        compiler_params=_cparams(("arbitrary",)),
        name="moe",
    )(block_expert, n_used, tok3, tok3, tok3, h_tiles, w_gu, b_gu.reshape(e, 1, f2),
      w_down, b_down.reshape(e, 1, d))


def _combine_kernel(pos0_ref, pos1_ref, pos2_ref, yb_hbm, x1_ref, gates_ref, mod_ref, gfin_ref,
                    out_ref, buf0, buf1, buf2, sem):
    i = pl.program_id(0)
    n = pl.num_programs(0)
    n_rows = buf0.shape[0] // SUBLANES
    tm = n_rows // TOP_K

    slot_major = lambda r: (r % TOP_K) * tm + r // TOP_K

    @pl.when(i == 0)
    def _():
        _tile_gather_start(pos0_ref, yb_hbm, buf0, sem.at[0], n_rows, unrolled=False,
                           dst_row=slot_major)
        _tile_gather_start(pos1_ref, yb_hbm, buf1, sem.at[1], n_rows, unrolled=False,
                           dst_row=slot_major)

    def compute(cur):
        g = gates_ref[...]
        gk = [jnp.broadcast_to(g[:, k:k + 1], (tm, LANES)) for k in range(TOP_K)]
        pieces = []
        for s in range(SUBLANES):
            acc = gk[0] * _tile_piece(cur, 0, tm, s)
            for k in range(1, TOP_K):
                acc = acc + gk[k] * _tile_piece(cur, k * tm, tm, s)
            pieces.append(acc)
        x2 = x1_ref[...] + mod_ref[0, 5:6, :] * jnp.concatenate(pieces, axis=1)
        out_ref[...] = _rmsnorm(x2, gfin_ref[...])

    _ring_step(i, n - 1, (buf0, buf1, buf2), sem,
               wait=lambda buf, s: _tile_gather_wait(yb_hbm, buf, s),
               prefetch=lambda buf, s: _tile_gather_start(pos2_ref, yb_hbm, buf, s, n_rows,
                                                          unrolled=True, alternate_priority=True,
                                                          dst_row=slot_major),
               compute=compute)


def _combine(yb, pos, x1_2d, gates_2d, mod, g_final, seq):
    n_tok, d = x1_2d.shape
    tm = min(COMBINE_TM, seq)
    assert seq % tm == 0
    n_tiles = n_tok // tm
    per_batch = seq // tm
    pos3 = pos.reshape(n_tiles, 1, tm * TOP_K)
    smem_blk = lambda imap: pl.BlockSpec((1, 1, TOP_K * tm), imap, memory_space=pltpu.SMEM)
    buf = pltpu.VMEM((TOP_K * tm * SUBLANES, LANES), F32)
    return pl.pallas_call(
        _combine_kernel,
        out_shape=jax.ShapeDtypeStruct((n_tok, d), F32),
        grid=(n_tiles,),
        in_specs=[smem_blk(lambda i: (i, 0, 0)),
                  smem_blk(lambda i: (jnp.minimum(i + 1, n_tiles - 1), 0, 0)),
                  smem_blk(lambda i: (jnp.minimum(i + 2, n_tiles - 1), 0, 0)),
                  pl.BlockSpec(memory_space=pl.ANY),
                  pl.BlockSpec((tm, d), lambda i: (i, 0)),
                  pl.BlockSpec((tm, LANES), lambda i: (i, 0)),
                  pl.BlockSpec((1, 8, d), lambda i: (i // per_batch, 0, 0)),
                  pl.BlockSpec((1, d), lambda i: (0, 0))],
        out_specs=pl.BlockSpec((tm, d), lambda i: (i, 0)),
        scratch_shapes=[buf, buf, buf, pltpu.SemaphoreType.DMA((3,))],
        compiler_params=_cparams(("arbitrary",)),
        name="combine",
    )(pos3, pos3, pos3, yb, x1_2d, gates_2d, mod, g_final)


def kernel(x, c, ctx, c_ctx, w_mod, b_mod, g_mix, w_in, lam_re, lam_im, log_dt, b_re, b_im,
           c_re, c_im, d_skip, w_glu, conv_w, conv_b, w_ssm_br, w_conv_br, w_o, g_ffn,
           w_router, b_router, w_gu, b_gu, w_down, b_down, g_final):
    depth = w_mod.shape[0]
    assert depth == 1, "single-layer trunk"
    bsz, seq, d = x.shape
    ctx_len = ctx.shape[1]
    d_ssm = d // 2
    assert bsz % 8 == 0 and seq % CHUNK_T == 0 and ctx_len % CHUNK_T == 0 and seq % GRID_W == 0
    assert d == SUBLANES * LANES, "row gathers move one (8,128) f32 tile per token"

    n_cond = -(-(bsz + 1) // 8) * 8
    cond = jnp.zeros((n_cond, d), F32).at[:bsz].set(c).at[bsz].set(c_ctx)
    m = _adaln(cond, w_mod[0], b_mod[0])
    zeros2 = jnp.zeros((n_cond, 2, d), F32)
    mod_all = jnp.concatenate([m.reshape(n_cond, 6, d), zeros2], axis=1)
    mod_x, mod_c = mod_all[:bsz], mod_all[bsz:bsz + 1]

    w_in_bf = w_in[0].astype(BF16)
    w_u, w_rest = w_in_bf[:, :d_ssm], w_in_bf[:, d_ssm:]
    gm = g_mix[0].reshape(1, d)

    perm = _granule_transpose_matrix()
    utx = _front(x, mod_x, gm, w_u, perm)
    utc = _front(ctx, mod_c, gm, w_u, perm)

    smat, toep, cpow, a_t = _s5_matrices(lam_re[0], lam_im[0], log_dt[0], b_re[0], b_im[0],
                                         c_re[0], c_im[0], d_skip[0])
    y_chunks = _s5_chunks(utx, utc, smat, toep, cpow, a_t, bsz)

    pad_r = jnp.zeros((d, LANES - N_EXPERTS), F32)
    w_r = jnp.concatenate([w_router[0], pad_r], axis=1)
    b_r = jnp.concatenate([b_router[0], jnp.zeros((LANES - N_EXPERTS,), F32)]).reshape(1, LANES)
    x1, h, ids, gates, hist = _back(
        x, y_chunks, mod_x, perm, gm, w_rest, conv_w[0], conv_b[0].reshape(1, d_ssm),
        w_glu[0].astype(BF16), w_ssm_br[0].astype(BF16),
        w_conv_br[0].astype(BF16), w_o[0].astype(BF16), g_ffn[0].reshape(1, d), w_r, b_r)

    n_tok = bsz * seq
    ids2 = ids.reshape(n_tok, LANES)
    buf_token, pos, block_expert, n_used = _routing(
        ids2[:, :TOP_K], ids2[:, TOP_K:2 * TOP_K],
        hist[:, 0, :N_EXPERTS].astype(jnp.int32), bsz, seq)
    yb = _moe(h.reshape(n_tok * SUBLANES, LANES), buf_token, block_expert, n_used,
              w_gu[0], b_gu[0], w_down[0], b_down[0])
    out = _combine(yb, pos, x1.reshape(n_tok, d), gates.reshape(n_tok, LANES), mod_x,
                   g_final.reshape(1, d), seq)
    return out.reshape(bsz, seq, d)
```

```python
import functools
import math

import jax
import jax.numpy as jnp
from jax import lax
from jax.experimental import pallas as pl
from jax.experimental.pallas import tpu as pltpu

F32 = jnp.float32
BF16 = jnp.bfloat16
HIGHEST = lax.Precision.HIGHEST

RMS_EPS = 1e-6
GRID_W = 64
SSM_GROUP = 16
SSM_STATE = 64
N_EXPERTS = 32
TOP_K = 4
SWIGLU_LIMIT = 7.0
SWIGLU_ALPHA = 1.702

CHUNK_T = 16
LANES = 128
SUBLANES = 8
V7X_VMEM_LIMIT_BYTES = 56 * 1024 * 1024

S5_TR = 1024
MOE_TB = 512


def _cparams(sem):
    return pltpu.CompilerParams(dimension_semantics=sem,
                                vmem_limit_bytes=V7X_VMEM_LIMIT_BYTES)


def _bdot(a, b):
    return jnp.dot(a, b, preferred_element_type=F32)


def _rmsnorm(xt, g):
    ms = jnp.mean(xt * xt, axis=-1, keepdims=True)
    return xt * lax.rsqrt(ms + RMS_EPS) * g


def _adaln_kernel(c_ref, w_ref, b_ref, o_ref):
    s = jax.nn.silu(c_ref[...])
    o_ref[...] = jnp.dot(s, w_ref[...], precision=HIGHEST,
                         preferred_element_type=F32) + b_ref[...]


def _adaln(cond, w_mod, b_mod):
    r, d = cond.shape
    n = w_mod.shape[1]
    tn = n // 4
    return pl.pallas_call(
        _adaln_kernel,
        out_shape=jax.ShapeDtypeStruct((r, n), F32),
        grid=(n // tn,),
        in_specs=[pl.BlockSpec((r, d), lambda j: (0, 0)),
                  pl.BlockSpec((d, tn), lambda j: (0, j)),
                  pl.BlockSpec((1, tn), lambda j: (0, j))],
        out_specs=pl.BlockSpec((r, tn), lambda j: (0, j)),
        compiler_params=_cparams(("arbitrary",)),
        name="adaln",
    )(cond, w_mod, b_mod.reshape(1, n))


ROW_T = GRID_W
PITCH = ROW_T + 8
N_SLAB = 4
CHUNKS_PER_TILE = ROW_T // CHUNK_T
GRANULES = LANES // SSM_GROUP


def _granule_transpose_matrix():
    n = GRANULES * LANES
    idx = jnp.arange(n)
    j, m, c = idx // LANES, (idx % LANES) // SSM_GROUP, idx % SSM_GROUP
    dst = m * LANES + j * SSM_GROUP + c
    return jnp.zeros((n, n), F32).at[idx, dst].set(1.0).astype(BF16)


def _modulate_tile(xn, mod_ref, shift_row, scale_row):
    d = xn.shape[1]
    x3 = xn.reshape(8, ROW_T, d)
    x3 = x3 * (1.0 + mod_ref[:, scale_row:scale_row + 1, :]) + mod_ref[:, shift_row:shift_row + 1, :]
    return x3.reshape(8 * ROW_T, d)


def _front_kernel(x_ref, mod_ref, g_ref, w_ref, perm_ref, u_ref, us):
    d = x_ref.shape[2]
    xn = _rmsnorm(x_ref[...].reshape(8 * ROW_T, d), g_ref[...])
    hx = _modulate_tile(xn, mod_ref, 0, 1)
    u = _bdot(hx.astype(BF16), w_ref[...])
    for sl in range(N_SLAB):
        for b in range(8):
            us[sl, pl.ds(b * PITCH, ROW_T), :] = u[b * ROW_T:(b + 1) * ROW_T,
                                                  sl * LANES:(sl + 1) * LANES]
    blocks = []
    for sl in range(N_SLAB):
        for th in range(CHUNK_T // GRANULES):
            rows = []
            for kk in range(CHUNKS_PER_TILE):
                t0 = kk * CHUNK_T + th * GRANULES
                rows.append(jnp.concatenate(
                    [us[sl, pl.ds(t0 + j, 8, stride=PITCH), :] for j in range(GRANULES)], axis=1))
            blocks.append(jnp.concatenate(rows, axis=0))
    acat = jnp.concatenate(blocks, axis=0).astype(BF16)
    bmat = _bdot(acat, perm_ref[...])
    nrow = CHUNKS_PER_TILE * 8
    blk = 0
    for sl in range(N_SLAB):
        for th in range(CHUNK_T // GRANULES):
            for m in range(GRANULES):
                u_ref[sl * GRANULES + m, :, th * LANES:(th + 1) * LANES] = (
                    bmat[blk * nrow:(blk + 1) * nrow, m * LANES:(m + 1) * LANES].astype(u_ref.dtype))
            blk += 1


def _front(x, mod, g_mix, w_u, perm):
    b, s, d = x.shape
    n = w_u.shape[1]
    g = n // SSM_GROUP
    assert s % ROW_T == 0 and b % 8 == 0 and n == N_SLAB * LANES
    k = s // CHUNK_T
    tiles = s // ROW_T
    nrow = CHUNKS_PER_TILE * 8
    shared = mod.shape[0] == 1
    mod_spec = (pl.BlockSpec((1, 8, d), lambda i, j: (0, 0, 0)) if shared
                else pl.BlockSpec((8, 8, d), lambda i, j: (i, 0, 0)))
    return pl.pallas_call(
        _front_kernel,
        out_shape=jax.ShapeDtypeStruct((g, (b // 8) * k * 8, CHUNK_T * SSM_GROUP), BF16),
        grid=(b // 8, tiles),
        in_specs=[pl.BlockSpec((8, ROW_T, d), lambda i, j: (i, j, 0)),
                  mod_spec,
                  pl.BlockSpec((1, d), lambda i, j: (0, 0)),
                  pl.BlockSpec((d, n), lambda i, j: (0, 0)),
                  pl.BlockSpec(perm.shape, lambda i, j: (0, 0))],
        out_specs=pl.BlockSpec((g, nrow, CHUNK_T * SSM_GROUP), lambda i, j: (0, i * tiles + j, 0)),
        scratch_shapes=[pltpu.VMEM((N_SLAB, 8 * PITCH, LANES), F32)],
        compiler_params=_cparams(("arbitrary", "arbitrary")),
        name="front",
    )(x, mod, g_mix, w_u, perm)


P_LANES = 3 * LANES


def _s5_params_kernel(lam_ref, bt_ref, ct_ref, dsk_ref, rep_ref,
                      toep_ref, smat_ref, cpow_ref, at_ref):
    t, ch, p = CHUNK_T, SSM_GROUP, SSM_STATE
    width = t * ch
    f32dot = lambda a, b: jnp.dot(a, b, precision=HIGHEST, preferred_element_type=F32)
    eye = (lax.broadcasted_iota(jnp.int32, (p, p), 0) == lax.broadcasted_iota(jnp.int32, (p, p), 1))
    to_col = lambda row: jnp.sum(jnp.where(eye, jnp.broadcast_to(row, (p, p)), 0.0),
                                 axis=1, keepdims=True)
    blk = lax.broadcasted_iota(jnp.int32, (p, P_LANES), 1) // ch
    row16 = lax.broadcasted_iota(jnp.int32, (ch, width), 0)
    lane16 = lax.broadcasted_iota(jnp.int32, (ch, width), 1)

    smats, cpows, ats = [[], []], [[], []], []
    for j in range(2):
        kt, at_dir = [], []
        for z in range(2):
            lr = lam_ref[j, 4 * z:4 * z + 1, :]
            li = lam_ref[j, 4 * z + 1:4 * z + 2, :]
            dt = jnp.exp(lam_ref[j, 4 * z + 2:4 * z + 3, :])
            mag = jnp.exp(lr * dt)
            ar, ai = mag * jnp.cos(li * dt), mag * jnp.sin(li * dt)
            den = lr * lr + li * li
            qr = ((ar - 1.0) * lr + ai * li) / den
            qi = (ai * lr - (ar - 1.0) * li) / den
            bbr = qr * bt_ref[j, 2 * z] - qi * bt_ref[j, 2 * z + 1]
            bbi = qr * bt_ref[j, 2 * z + 1] + qi * bt_ref[j, 2 * z]
            pw = [(jnp.ones_like(ar), jnp.zeros_like(ai))]
            for _ in range(t):
                r, i = pw[-1]
                pw.append((r * ar - i * ai, r * ai + i * ar))
            at_dir.append(pw[t])

            expo = jnp.where(blk <= t, blk if z == 0 else t - blk, 0)
            tr = jnp.ones((p, P_LANES), F32)
            ti = jnp.zeros((p, P_LANES), F32)
            for bit in range(5):
                sr, si = to_col(pw[1 << bit][0]), to_col(pw[1 << bit][1])
                on = ((expo >> bit) & 1) == 1
                fr, fi = jnp.where(on, sr, 1.0), jnp.where(on, si, 0.0)
                tr, ti = tr * fr - ti * fi, tr * fi + ti * fr
            ctr = f32dot(ct_ref[j, 2 * z], rep_ref[...])
            cti = f32dot(ct_ref[j, 2 * z + 1], rep_ref[...])
            crd = ctr * tr - cti * ti
            cid = ctr * ti + cti * tr
            kt.append(f32dot(bbr, crd) - f32dot(bbi, cid))

            off = ch if z == 0 else 0
            cpows[j].append(crd[:, off:off + width])
            cpows[j].append(-cid[:, off:off + width])

            pows = [pw[t - 1 - s] if z == 0 else pw[s] for s in range(t)]
            pr_rows = jnp.concatenate([jnp.broadcast_to(q[0], (ch, p)) for q in pows], axis=0)
            pi_rows = jnp.concatenate([jnp.broadcast_to(q[1], (ch, p)) for q in pows], axis=0)
            br_rows = jnp.concatenate([bbr] * t, axis=0)
            bi_rows = jnp.concatenate([bbi] * t, axis=0)
            smats[j].append(pr_rows * br_rows - pi_rows * bi_rows)
            smats[j].append(pr_rows * bi_rows + pi_rows * br_rows)
        ats.append(at_dir)

        ktf = kt[0][:, :width]
        ktb = kt[1][:, ch:ch + width]
        rows = []
        for s in range(t):
            fwd = ktf if s == 0 else jnp.concatenate(
                [jnp.zeros((ch, ch * s), F32), ktf[:, :width - ch * s]], axis=1)
            sh = ch * (t - 1 - s)
            bwd = ktb if sh == 0 else jnp.concatenate(
                [ktb[:, sh:], jnp.zeros((ch, sh), F32)], axis=1)
            skip = jnp.where(lane16 == ch * s + row16, dsk_ref[j], 0.0)
            rows.append(fwd + bwd + skip)
        toep_ref[j] = jnp.concatenate(rows, axis=0).astype(toep_ref.dtype)

    zs = jnp.zeros((width, p), F32)
    smat_ref[0] = jnp.concatenate(
        [jnp.concatenate([jnp.concatenate([smats[0][q], zs], axis=1) for q in range(4)], axis=1),
         jnp.concatenate([jnp.concatenate([zs, smats[1][q]], axis=1) for q in range(4)], axis=1)],
        axis=0).astype(smat_ref.dtype)
    zc = jnp.zeros((p, width), F32)
    cpow_ref[0] = jnp.concatenate(
        [jnp.concatenate([cpows[j][q], zc] if j == 0 else [zc, cpows[j][q]], axis=1)
         for q in range(4) for j in range(2)], axis=0).astype(cpow_ref.dtype)
    at_ref[...] = jnp.concatenate(
        [jnp.concatenate([ats[0][z][part], ats[1][z][part]], axis=1)
         for z in range(2) for part in range(2)], axis=0)


def _s5_matrices(lam_re, lam_im, log_dt, b_re, b_im, c_re, c_im, d_skip):
    t, ch = CHUNK_T, SSM_GROUP
    _, g, p = lam_re.shape
    assert p == SSM_STATE and b_re.shape[-1] == ch and 2 * p == LANES
    width = t * ch
    zero = jnp.zeros((g, 1, p), F32)
    lam = jnp.concatenate(
        [jnp.stack([lam_re[z], lam_im[z], jnp.broadcast_to(log_dt[z][:, None], (g, p))], axis=1)
         if part == 0 else zero for z in range(2) for part in range(2)], axis=1)
    bt = jnp.stack([b_re[0], b_im[0], b_re[1], b_im[1]], axis=1).transpose(0, 1, 3, 2)
    ct = jnp.stack([c_re[0], c_im[0], c_re[1], c_im[1]], axis=1).transpose(0, 1, 3, 2)
    dsk = jnp.tile(d_skip.reshape(g, 1, ch), (1, 1, t))
    lane = jnp.arange(P_LANES)
    rep = ((lane[None, :] % ch == jnp.arange(ch)[:, None])
           & (lane[None, :] < (t + 1) * ch)).astype(F32)
    pair = lambda *shape: pl.BlockSpec((2,) + shape, lambda a: (a,) + (0,) * len(shape))
    toep, smat, cpow, a_t = pl.pallas_call(
        _s5_params_kernel,
        out_shape=(jax.ShapeDtypeStruct((g, width, width), BF16),
                   jax.ShapeDtypeStruct((g // 2, 2 * width, 4 * LANES), BF16),
                   jax.ShapeDtypeStruct((g // 2, 4 * LANES, 2 * width), BF16),
                   jax.ShapeDtypeStruct((4, g * p), F32)),
        grid=(g // 2,),
        in_specs=[pair(8, p), pair(4, ch, p), pair(4, p, ch), pair(1, width),
                  pl.BlockSpec((ch, P_LANES), lambda a: (0, 0))],
        out_specs=(pair(width, width),
                   pl.BlockSpec((1, 2 * width, 4 * LANES), lambda a: (a, 0, 0)),
                   pl.BlockSpec((1, 4 * LANES, 2 * width), lambda a: (a, 0, 0)),
                   pl.BlockSpec((4, LANES), lambda a: (0, a))),
        compiler_params=_cparams(("arbitrary",)),
        name="s5_params",
    )(lam, bt, ct, dsk, rep)
    return smat, toep, cpow, a_t


def _chunk_states(u_ref, smat, dst_refs, rows):
    tr = min(S5_TR, rows)
    for r0 in range(0, rows, tr):
        lhs = jnp.concatenate([u_ref[0, r0:r0 + tr, :], u_ref[1, r0:r0 + tr, :]], axis=1)
        res = _bdot(lhs, smat)
        for q in range(4):
            dst_refs[q][r0:r0 + tr, :] = res[:, q * LANES:(q + 1) * LANES]


def _scan_chunks(a_ref, c_refs, x_refs, h_refs, halves, kc, kx):
    cfr_ref, cfi_ref, cbr_ref, cbi_ref = c_refs
    xfr_ref, xfi_ref, xbr_ref, xbi_ref = x_refs
    hfr_ref, hfi_ref, hbr_ref, hbi_ref = h_refs
    a_fr, a_fi = a_ref[0:1, :], a_ref[1:2, :]
    a_br, a_bi = a_ref[2:3, :], a_ref[3:4, :]

    def rows(half, k, n_chunks, count=1):
        return pl.ds(pl.multiple_of((half * n_chunks + k) * 8, 8 * count), 8 * count)

    def step(h_re, h_im, a_re, a_im, s_re, s_im):
        return (a_re * h_re - a_im * h_im + s_re, a_re * h_im + a_im * h_re + s_im)

    zero = jnp.zeros((8, LANES), F32)

    def ctx_body(i, carry):
        out = []
        for half in range(halves):
            fr, fi, br, bi = carry[half]
            kf, kb = i, kc - 1 - i
            fr, fi = step(fr, fi, a_fr, a_fi, cfr_ref[rows(half, kf, kc), :],
                          cfi_ref[rows(half, kf, kc), :])
            br, bi = step(br, bi, a_br, a_bi, cbr_ref[rows(half, kb, kc), :],
                          cbi_ref[rows(half, kb, kc), :])
            out.append((fr, fi, br, bi))
        return tuple(out)

    carry = lax.fori_loop(0, kc, ctx_body, ((zero,) * 4,) * halves)

    def x_body(i, carry):
        out = []
        for half in range(halves):
            fr, fi, br, bi = carry[half]
            kf = 2 * i
            kb = kx - 2 - 2 * i
            fr1, fi1 = step(fr, fi, a_fr, a_fi, xfr_ref[rows(half, kf, kx), :],
                            xfi_ref[rows(half, kf, kx), :])
            br1, bi1 = step(br, bi, a_br, a_bi, xbr_ref[rows(half, kb + 1, kx), :],
                            xbi_ref[rows(half, kb + 1, kx), :])
            hfr_ref[rows(half, kf, kx, 2), :] = jnp.concatenate([fr, fr1], 0).astype(hfr_ref.dtype)
            hfi_ref[rows(half, kf, kx, 2), :] = jnp.concatenate([fi, fi1], 0).astype(hfi_ref.dtype)
            hbr_ref[rows(half, kb, kx, 2), :] = jnp.concatenate([br1, br], 0).astype(hbr_ref.dtype)
            hbi_ref[rows(half, kb, kx, 2), :] = jnp.concatenate([bi1, bi], 0).astype(hbi_ref.dtype)
            fr, fi = step(fr1, fi1, a_fr, a_fi, xfr_ref[rows(half, kf + 1, kx), :],
                          xfi_ref[rows(half, kf + 1, kx), :])
            br, bi = step(br1, bi1, a_br, a_bi, xbr_ref[rows(half, kb, kx), :],
                          xbi_ref[rows(half, kb, kx), :])
            out.append((fr, fi, br, bi))
        return tuple(out)

    lax.fori_loop(0, kx // 2, x_body, carry)


def _s5_chunks_kernel(ux_ref, uc_ref, smat_ref, toep_ref, cpow_ref, a_ref, y_ref,
                      *scratch, halves, kc, kx):
    s_x, s_c, h_in = scratch[0:4], scratch[4:8], scratch[8:12]
    rows_x, rows_c = ux_ref.shape[1], uc_ref.shape[1]
    _chunk_states(uc_ref, smat_ref[0], s_c, rows_c)
    _chunk_states(ux_ref, smat_ref[0], s_x, rows_x)
    _scan_chunks(a_ref, s_c, s_x, h_in, halves, kc, kx)
    n = ux_ref.shape[2]
    tr = min(S5_TR, rows_x)
    for r0 in range(0, rows_x, tr):
        hcat = jnp.concatenate([h[r0:r0 + tr, :] for h in h_in], axis=1)
        yst = _bdot(hcat, cpow_ref[0])
        for j in range(2):
            y_ref[j, r0:r0 + tr, :] = (_bdot(ux_ref[j, r0:r0 + tr, :], toep_ref[j])
                                       + yst[:, j * n:(j + 1) * n]).astype(y_ref.dtype)


def _s5_chunks(utx, utc, smat, toep, cpow, a_t, batch):
    g, rows_x, kdim = utx.shape
    rows_c = utc.shape[1]
    kc, kx = rows_c // batch, rows_x // batch
    assert kx % 2 == 0 and rows_x % min(S5_TR, rows_x) == 0
    pair = lambda rows: pl.BlockSpec((2, rows, kdim), lambda a: (a, 0, 0))
    return pl.pallas_call(
        functools.partial(_s5_chunks_kernel, halves=batch // 8, kc=kc, kx=kx),
        out_shape=jax.ShapeDtypeStruct((g, rows_x, kdim), BF16),
        grid=(g // 2,),
        in_specs=[pair(rows_x), pair(rows_c),
                  pl.BlockSpec((1, 2 * kdim, 4 * LANES), lambda a: (a, 0, 0)),
                  pl.BlockSpec((2, kdim, kdim), lambda a: (a, 0, 0)),
                  pl.BlockSpec((1, 4 * LANES, 2 * kdim), lambda a: (a, 0, 0)),
                  pl.BlockSpec((4, LANES), lambda a: (0, a))],
        out_specs=pair(rows_x),
        scratch_shapes=([pltpu.VMEM((rows_x, LANES), F32)] * 4
                        + [pltpu.VMEM((rows_c, LANES), F32)] * 4
                        + [pltpu.VMEM((rows_x, LANES), BF16)] * 4),
        compiler_params=_cparams(("arbitrary",)),
        name="s5_chunks",
    )(utx, utc, smat, toep, cpow, a_t)


def _chunk_rows_to_tokens(y_ref, perm_ref, ys):
    nrow = CHUNKS_PER_TILE * 8
    blocks = []
    for sl in range(N_SLAB):
        for th in range(CHUNK_T // GRANULES):
            blocks.append(jnp.concatenate(
                [y_ref[sl * GRANULES + m, :, th * LANES:(th + 1) * LANES] for m in range(GRANULES)],
                axis=1))
    acat = _bdot(jnp.concatenate(blocks, axis=0), perm_ref[...])
    blk = 0
    for sl in range(N_SLAB):
        for th in range(CHUNK_T // GRANULES):
            for kk in range(CHUNKS_PER_TILE):
                for j in range(GRANULES):
                    t = kk * CHUNK_T + th * GRANULES + j
                    ys[sl, pl.ds(t, 8, stride=PITCH), :] = acat[blk * nrow + kk * 8:
                                                               blk * nrow + (kk + 1) * 8,
                                                               j * LANES:(j + 1) * LANES]
            blk += 1
    return jnp.concatenate(
        [jnp.concatenate([ys[sl, pl.ds(b * PITCH, ROW_T), :] for sl in range(N_SLAB)], axis=1)
         for b in range(8)], axis=0)


def _back_kernel(x_ref, y_ref, mod_ref, perm_ref, gmix_ref, win_ref, convw_ref, convb_ref,
                 wglu_ref, wsbr_ref, wcbr_ref, wo_ref, gffn_ref, wrhi_ref, wrlo_ref, br_ref,
                 ltri_ref, x1_ref, h_ref, ids_ref, gates_ref, hist_ref, ys_scr):
    d_model = x_ref.shape[2]
    tm = 8 * ROW_T
    d_ssm = N_SLAB * LANES
    xt = x_ref[...].reshape(tm, d_model)
    xn = _rmsnorm(xt, gmix_ref[...])
    hx = _modulate_tile(xn, mod_ref, 0, 1).astype(BF16)
    proj = lambda lo, hi: _bdot(hx, win_ref[:, lo:hi])
    ys_tok = _chunk_rows_to_tokens(y_ref, perm_ref, ys_scr)
    v = proj(0, d_ssm)
    ys = jax.nn.gelu(ys_tok)
    glu = _bdot(ys.astype(BF16), wglu_ref[...])
    gate_c = proj(2 * d_ssm, 3 * d_ssm)
    ys = ys * jax.nn.sigmoid(glu)
    y_s = _bdot(ys.astype(BF16), wsbr_ref[...])
    gate_b = proj(d_ssm, 2 * d_ssm)

    z = gate_c * v
    col = lax.broadcasted_iota(jnp.int32, z.shape, 0) % GRID_W
    z_prev = jnp.where(col == 0, 0.0, pltpu.roll(z, 1, 0))
    z_next = jnp.where(col == GRID_W - 1, 0.0, pltpu.roll(z, tm - 1, 0))
    conv = (z_prev * convw_ref[0:1, :] + z * convw_ref[1:2, :]
            + z_next * convw_ref[2:3, :] + convb_ref[...])
    y_conv = gate_b * conv

    g_s = proj(3 * d_ssm, 3 * d_ssm + d_model)
    y_c = _bdot(y_conv.astype(BF16), wcbr_ref[...])
    merged_s = jax.nn.sigmoid(g_s) * y_s
    g_c = proj(3 * d_ssm + d_model, 3 * d_ssm + 2 * d_model)
    merged = merged_s + jax.nn.sigmoid(g_c) * y_c
    mo = _bdot(merged.astype(BF16), wo_ref[...])
    x1 = xt + (mo.reshape(8, ROW_T, d_model) * mod_ref[:, 2:3, :]).reshape(tm, d_model)
    x1_ref[...] = x1.reshape(8, ROW_T, d_model)

    hn = _modulate_tile(_rmsnorm(x1, gffn_ref[...]), mod_ref, 3, 4)
    for b in range(8):
        for s in range(SUBLANES):
            h_ref[b, pl.ds(s, ROW_T, stride=SUBLANES), :] = hn[b * ROW_T:(b + 1) * ROW_T,
                                                              s * LANES:(s + 1) * LANES]

    hn_hi = hn.astype(BF16)
    hn_lo = (hn - hn_hi.astype(F32)).astype(BF16)
    logits = (_bdot(hn_hi, wrhi_ref[...]) + _bdot(hn_lo, wrhi_ref[...])
              + _bdot(hn_hi, wrlo_ref[...]) + br_ref[...])
    lane = lax.broadcasted_iota(jnp.int32, logits.shape, 1)
    neg = jnp.float32(-jnp.inf)
    cur = jnp.where(lane < N_EXPERTS, logits, neg)
    vals, idxs = [], []
    for _ in range(TOP_K):
        mk = jnp.max(cur, axis=-1, keepdims=True)
        ik = jnp.min(jnp.where(cur == mk, lane, LANES), axis=-1, keepdims=True)
        vals.append(mk)
        idxs.append(ik)
        cur = jnp.where(lane == ik, neg, cur)
    exps = [jnp.exp(vk - vals[0]) for vk in vals]
    denom = exps[0] + exps[1] + exps[2] + exps[3]
    ids = jnp.zeros(logits.shape, jnp.int32)
    gates = jnp.zeros(logits.shape, F32)
    onehot = jnp.zeros(logits.shape, F32)
    for k in range(TOP_K):
        onehot = onehot + (lane == idxs[k]).astype(F32)
    before = _bdot(ltri_ref[...], onehot.astype(BF16))
    for k in range(TOP_K):
        rank_k = jnp.sum(jnp.where(lane == idxs[k], before, 0.0), axis=-1, keepdims=True)
        ids = jnp.where(lane == k, idxs[k], ids)
        ids = jnp.where(lane == TOP_K + k, rank_k.astype(jnp.int32), ids)
        gates = jnp.where(lane == k, exps[k] / denom, gates)
    ids_ref[0] = jnp.transpose(ids)[0:2 * TOP_K, :]
    gates_ref[...] = gates.reshape(8, ROW_T, LANES)
    hist_ref[0] = jnp.broadcast_to(jnp.sum(onehot, axis=0, keepdims=True), (8, LANES))


def _back(x, y_chunks, mod, perm, g_mix, w_rest, conv_w, conv_b, w_glu, w_ssm_br,
          w_conv_br, w_o, g_ffn, w_router, b_router):
    b, s, d = x.shape
    g, _, kdim = y_chunks.shape
    tiles = s // ROW_T
    nrow = CHUNKS_PER_TILE * 8
    tok = lambda n: pl.BlockSpec((8, ROW_T, n), lambda i, j: (i, j, 0))

    def const(arr):
        nd = arr.ndim
        return pl.BlockSpec(arr.shape, lambda i, j: (0,) * nd, pipeline_mode=pl.Buffered(1))

    w_r_hi = w_router.astype(BF16)
    w_r_lo = (w_router - w_r_hi.astype(F32)).astype(BF16)
    row = jnp.arange(8 * ROW_T)
    ltri = (row[:, None] > row[None, :]).astype(BF16)
    params = (perm, g_mix, w_rest, conv_w, conv_b, w_glu, w_ssm_br, w_conv_br, w_o, g_ffn,
              w_r_hi, w_r_lo, b_router, ltri)
    return pl.pallas_call(
        _back_kernel,
        out_shape=(jax.ShapeDtypeStruct((b, s, d), F32),
                   jax.ShapeDtypeStruct((b, s * SUBLANES, LANES), F32),
                   jax.ShapeDtypeStruct(((b // 8) * tiles, 2 * TOP_K, 8 * ROW_T), jnp.int32),
                   jax.ShapeDtypeStruct((b, s, LANES), F32),
                   jax.ShapeDtypeStruct(((b // 8) * tiles, 8, LANES), F32)),
        grid=(b // 8, tiles),
        in_specs=[tok(d),
                  pl.BlockSpec((g, nrow, kdim), lambda i, j: (0, i * tiles + j, 0)),
                  pl.BlockSpec((8, 8, d), lambda i, j: (i, 0, 0))] + [const(a) for a in params],
        out_specs=(tok(d), pl.BlockSpec((8, ROW_T * SUBLANES, LANES), lambda i, j: (i, j, 0)),
                   pl.BlockSpec((1, 2 * TOP_K, 8 * ROW_T), lambda i, j: (i * tiles + j, 0, 0)),
                   tok(LANES),
                   pl.BlockSpec((1, 8, LANES), lambda i, j: (i * tiles + j, 0, 0))),
        scratch_shapes=[pltpu.VMEM((N_SLAB, 8 * PITCH, LANES), F32)],
        compiler_params=_cparams(("arbitrary", "arbitrary")),
        name="back",
    )(x, y_chunks, mod, *params)


def _routing(ids_ranks, tile_hist, bsz, seq):
    tb = MOE_TB
    n_tiles, _, tm = ids_ranks.shape
    ids, ranks = ids_ranks[:, :TOP_K, :], ids_ranks[:, TOP_K:, :]
    n_slot = n_tiles * tm * TOP_K
    tile_before = jnp.cumsum(tile_hist, axis=0) - tile_hist
    counts = jnp.sum(tile_hist, axis=0)
    padded = (counts + tb - 1) // tb * tb
    pad_end = jnp.cumsum(padded)
    pad_start = pad_end - padded
    start = jnp.cumsum(counts) - counts
    base = pad_start[None, :] + tile_before
    pos = ranks
    for e in range(N_EXPERTS):
        pos = pos + jnp.where(ids == e, base[:, e][:, None, None], 0)
    pos = pos.astype(jnp.int32)
    tiles = seq // ROW_T
    tile, row = jnp.arange(n_tiles, dtype=jnp.int32), jnp.arange(tm, dtype=jnp.int32)
    token = (((tile // tiles)[:, None] * 8 + (row // ROW_T)[None, :]) * seq
             + (tile % tiles)[:, None] * ROW_T + (row % ROW_T)[None, :])
    slot_token = jnp.broadcast_to(token[:, None, :], pos.shape)
    _, order_token = lax.sort_key_val(pos.reshape(-1), slot_token.reshape(-1))
    n_blocks = n_slot // tb + N_EXPERTS
    cap = n_blocks * tb
    block_first_row = jnp.arange(n_blocks, dtype=jnp.int32) * tb
    block_expert = jnp.minimum(
        jnp.sum(pad_end[None, :] <= block_first_row[:, None], axis=1, dtype=jnp.int32),
        N_EXPERTS - 1)
    within = (block_first_row - pad_start[block_expert])[:, None] + jnp.arange(tb, dtype=jnp.int32)
    valid = within < counts[block_expert][:, None]
    src = jnp.clip(start[block_expert][:, None] + within, 0, n_slot - 1)
    buf_token = jnp.where(valid, order_token[src], 0).astype(jnp.int32).reshape(cap)
    n_used = (pad_end[-1] // tb).astype(jnp.int32).reshape(1)
    return buf_token, pos.reshape(n_tiles, TOP_K * tm), block_expert, n_used


def _to_tiles(ref, val):
    rows = val.shape[0]
    for s in range(SUBLANES):
        ref[pl.ds(s, rows, stride=SUBLANES), :] = val[:, s * LANES:(s + 1) * LANES]


def _tile_piece(ref, first_row, rows, s):
    return ref[pl.ds(first_row * SUBLANES + s, rows, stride=SUBLANES), :]


def _tile_gather_start(idx_ref, src_hbm, dst, sem, n_rows, unrolled, alternate_priority=False):
    def copy(r, t):
        return pltpu.make_async_copy(
            src_hbm.at[pl.ds(pl.multiple_of(t * SUBLANES, SUBLANES), SUBLANES), :],
            dst.at[pl.ds(pl.multiple_of(r * SUBLANES, SUBLANES), SUBLANES), :], sem)

    if unrolled:
        for r in range(n_rows):
            copy(r, idx_ref[0, 0, r]).start(priority=(r % 2) if alternate_priority else 0)
    else:
        def body(r, carry):
            copy(r, idx_ref[0, 0, r]).start()
            return carry
        lax.fori_loop(0, n_rows, body, 0, unroll=8)


def _tile_gather_wait(src_hbm, dst, sem):
    pltpu.make_async_copy(src_hbm.at[pl.ds(0, dst.shape[0]), :], dst, sem).wait()


def _ring_step(i, last, bufs, sem, wait, prefetch, compute):
    n = len(bufs)
    for p in range(n):
        def branch(p=p):
            wait(bufs[p], sem.at[p])
            q = (p + n - 1) % n
            prefetch(bufs[q], sem.at[q])
            compute(bufs[p])

            @pl.when(i == last)
            def _():
                for r in range(1, n):
                    wait(bufs[(p + r) % n], sem.at[(p + r) % n])

        pl.when((i <= last) & (lax.rem(i, n) == p))(branch)


def _cast_rows(src_ref, dst_ref, chunk):
    def body(c, carry):
        rows = pl.ds(pl.multiple_of(c * chunk, chunk), chunk)
        dst_ref[rows, :] = src_ref[0, rows, :].astype(dst_ref.dtype)
        return carry
    lax.fori_loop(0, dst_ref.shape[0] // chunk, body, 0)


def _moe_kernel(be_ref, nu_ref, tok0_ref, tok1_ref, tok2_ref, h_hbm, wgu_ref, bgu_ref, wd_ref,
                bd_ref, out_ref, xbuf0, xbuf1, xbuf2, wgu_bf, wd_bf, sem):
    i = pl.program_id(0)
    n_used = nu_ref[0]
    tb = xbuf0.shape[0] // SUBLANES
    f = wd_ref.shape[1]

    @pl.when(i == 0)
    def _():
        _tile_gather_start(tok0_ref, h_hbm, xbuf0, sem.at[0], tb, unrolled=False)
        _tile_gather_start(tok1_ref, h_hbm, xbuf1, sem.at[1], tb, unrolled=False)

    new_expert = (i == 0) | (be_ref[i] != be_ref[jnp.maximum(i - 1, 0)])

    @pl.when(new_expert & (i < n_used))
    def _():
        _cast_rows(wgu_ref, wgu_bf, 128)
        _cast_rows(wd_ref, wd_bf, 128)

    def compute(xcur):
        xe = jnp.concatenate([_tile_piece(xcur, 0, tb, s) for s in range(SUBLANES)],
                             axis=1).astype(BF16)
        gu = _bdot(xe, wgu_bf[...]) + bgu_ref[0]
        gt = jnp.minimum(gu[:, :f], SWIGLU_LIMIT)
        up = jnp.clip(gu[:, f:], -SWIGLU_LIMIT, SWIGLU_LIMIT)
        act = gt * jax.nn.sigmoid(SWIGLU_ALPHA * gt) * (up + 1.0)
        _to_tiles(out_ref, _bdot(act.astype(BF16), wd_bf[...]) + bd_ref[0])

    _ring_step(i, n_used - 1, (xbuf0, xbuf1, xbuf2), sem,
               wait=lambda buf, s: _tile_gather_wait(h_hbm, buf, s),
               prefetch=lambda buf, s: _tile_gather_start(tok2_ref, h_hbm, buf, s, tb,
                                                          unrolled=True),
               compute=compute)

    @pl.when(i >= n_used)
    def _():
        out_ref[...] = jnp.zeros(out_ref.shape, out_ref.dtype)


def _moe(h_tiles, buf_token, block_expert, n_used, w_gu, b_gu, w_down, b_down):
    e, d, f2 = w_gu.shape
    f = f2 // 2
    tb = MOE_TB
    cap = buf_token.shape[0]
    n_blocks = cap // tb
    tok3 = buf_token.reshape(n_blocks, 1, tb)
    smem_blk = lambda imap: pl.BlockSpec((1, 1, tb), imap, memory_space=pltpu.SMEM)
    grid_spec = pltpu.PrefetchScalarGridSpec(
        num_scalar_prefetch=2,
        grid=(n_blocks,),
        in_specs=[smem_blk(lambda i, be, nu: (i, 0, 0)),
                  smem_blk(lambda i, be, nu: (jnp.minimum(i + 1, n_blocks - 1), 0, 0)),
                  smem_blk(lambda i, be, nu: (jnp.minimum(i + 2, n_blocks - 1), 0, 0)),
                  pl.BlockSpec(memory_space=pl.ANY),
                  pl.BlockSpec((1, d, f2), lambda i, be, nu: (be[i], 0, 0)),
                  pl.BlockSpec((1, 1, f2), lambda i, be, nu: (be[i], 0, 0)),
                  pl.BlockSpec((1, f, d), lambda i, be, nu: (be[i], 0, 0)),
                  pl.BlockSpec((1, 1, d), lambda i, be, nu: (be[i], 0, 0))],
        out_specs=pl.BlockSpec((tb * SUBLANES, LANES), lambda i, be, nu: (i, 0)),
        scratch_shapes=[pltpu.VMEM((tb * SUBLANES, LANES), F32)] * 3
        + [pltpu.VMEM((d, f2), BF16), pltpu.VMEM((f, d), BF16), pltpu.SemaphoreType.DMA((3,))],
    )
    return pl.pallas_call(
        _moe_kernel,
        out_shape=jax.ShapeDtypeStruct((cap * SUBLANES, LANES), F32),
        grid_spec=grid_spec,
        compiler_params=_cparams(("arbitrary",)),
        name="moe",
    )(block_expert, n_used, tok3, tok3, tok3, h_tiles, w_gu, b_gu.reshape(e, 1, f2),
      w_down, b_down.reshape(e, 1, d))


def _combine_kernel(pos0_ref, pos1_ref, pos2_ref, yb_hbm, x1_ref, gates_ref, mod_ref, gfin_ref,
                    out_ref, buf0, buf1, buf2, sem):
    i = pl.program_id(0)
    n = pl.num_programs(0)
    n_rows = buf0.shape[0] // SUBLANES
    tm = n_rows // TOP_K
    d = x1_ref.shape[2]

    @pl.when(i == 0)
    def _():
        _tile_gather_start(pos0_ref, yb_hbm, buf0, sem.at[0], n_rows, unrolled=False)
        _tile_gather_start(pos1_ref, yb_hbm, buf1, sem.at[1], n_rows, unrolled=False)

    def compute(cur):
        g = gates_ref[...].reshape(tm, LANES)
        gk = [jnp.broadcast_to(g[:, k:k + 1], (tm, LANES)) for k in range(TOP_K)]
        pieces = []
        for s in range(SUBLANES):
            acc = gk[0] * _tile_piece(cur, 0, tm, s)
            for k in range(1, TOP_K):
                acc = acc + gk[k] * _tile_piece(cur, k * tm, tm, s)
            pieces.append(acc)
        moe = jnp.concatenate(pieces, axis=1).reshape(8, ROW_T, d)
        x2 = (x1_ref[...] + mod_ref[:, 5:6, :] * moe).reshape(tm, d)
        out_ref[...] = _rmsnorm(x2, gfin_ref[...]).reshape(8, ROW_T, d)

    _ring_step(i, n - 1, (buf0, buf1, buf2), sem,
               wait=lambda buf, s: _tile_gather_wait(yb_hbm, buf, s),
               prefetch=lambda buf, s: _tile_gather_start(pos2_ref, yb_hbm, buf, s, n_rows,
                                                          unrolled=True, alternate_priority=True),
               compute=compute)


def _combine(yb, pos, x1, gates, mod, g_final):
    b, s, d = x1.shape
    tm = 8 * ROW_T
    tiles = s // ROW_T
    n_tiles = (b // 8) * tiles
    pos3 = pos.reshape(n_tiles, 1, TOP_K * tm)
    smem_blk = lambda imap: pl.BlockSpec((1, 1, TOP_K * tm), imap, memory_space=pltpu.SMEM)
    tok = lambda n: pl.BlockSpec((8, ROW_T, n), lambda i: (i // tiles, i % tiles, 0))
    buf = pltpu.VMEM((TOP_K * tm * SUBLANES, LANES), F32)
    return pl.pallas_call(
        _combine_kernel,
        out_shape=jax.ShapeDtypeStruct((b, s, d), F32),
        grid=(n_tiles,),
        in_specs=[smem_blk(lambda i: (i, 0, 0)),
                  smem_blk(lambda i: (jnp.minimum(i + 1, n_tiles - 1), 0, 0)),
                  smem_blk(lambda i: (jnp.minimum(i + 2, n_tiles - 1), 0, 0)),
                  pl.BlockSpec(memory_space=pl.ANY),
                  tok(d), tok(LANES),
                  pl.BlockSpec((8, 8, d), lambda i: (i // tiles, 0, 0)),
                  pl.BlockSpec((1, d), lambda i: (0, 0))],
        out_specs=tok(d),
        scratch_shapes=[buf, buf, buf, pltpu.SemaphoreType.DMA((3,))],
        compiler_params=_cparams(("arbitrary",)),
        name="combine",
    )(pos3, pos3, pos3, yb, x1, gates, mod, g_final)


def kernel(x, c, ctx, c_ctx, w_mod, b_mod, g_mix, w_in, lam_re, lam_im, log_dt, b_re, b_im,
           c_re, c_im, d_skip, w_glu, conv_w, conv_b, w_ssm_br, w_conv_br, w_o, g_ffn,
           w_router, b_router, w_gu, b_gu, w_down, b_down, g_final):
    depth = w_mod.shape[0]
    assert depth == 1, "single-layer trunk"
    bsz, seq, d = x.shape
    ctx_len = ctx.shape[1]
    d_ssm = d // 2
    assert bsz % 8 == 0 and seq % CHUNK_T == 0 and ctx_len % CHUNK_T == 0 and seq % GRID_W == 0
    assert d == SUBLANES * LANES, "row gathers move one (8,128) f32 tile per token"

    n_cond = -(-(bsz + 1) // 8) * 8
    cond = jnp.zeros((n_cond, d), F32).at[:bsz].set(c).at[bsz].set(c_ctx)
    m = _adaln(cond, w_mod[0], b_mod[0])
    zeros2 = jnp.zeros((n_cond, 2, d), F32)
    mod_all = jnp.concatenate([m.reshape(n_cond, 6, d), zeros2], axis=1)
    mod_x, mod_c = mod_all[:bsz], mod_all[bsz:bsz + 1]

    w_in_bf = w_in[0].astype(BF16)
    w_u, w_rest = w_in_bf[:, :d_ssm], w_in_bf[:, d_ssm:]
    gm = g_mix[0].reshape(1, d)

    perm = _granule_transpose_matrix()
    utx = _front(x, mod_x, gm, w_u, perm)
    utc = _front(ctx, mod_c, gm, w_u, perm)

    smat, toep, cpow, a_t = _s5_matrices(lam_re[0], lam_im[0], log_dt[0], b_re[0], b_im[0],
                                         c_re[0], c_im[0], d_skip[0])
    y_chunks = _s5_chunks(utx, utc, smat, toep, cpow, a_t, bsz)

    pad_r = jnp.zeros((d, LANES - N_EXPERTS), F32)
    w_r = jnp.concatenate([w_router[0], pad_r], axis=1)
    b_r = jnp.concatenate([b_router[0], jnp.zeros((LANES - N_EXPERTS,), F32)]).reshape(1, LANES)
    x1, h, ids, gates, hist = _back(
        x, y_chunks, mod_x, perm, gm, w_rest, conv_w[0], conv_b[0].reshape(1, d_ssm),
        w_glu[0].astype(BF16), w_ssm_br[0].astype(BF16),
        w_conv_br[0].astype(BF16), w_o[0].astype(BF16), g_ffn[0].reshape(1, d), w_r, b_r)

    n_tok = bsz * seq
    buf_token, pos, block_expert, n_used = _routing(
        ids, hist[:, 0, :N_EXPERTS].astype(jnp.int32), bsz, seq)
    yb = _moe(h.reshape(n_tok * SUBLANES, LANES), buf_token, block_expert, n_used,
              w_gu[0], b_gu[0], w_down[0], b_down[0])
    return _combine(yb, pos, x1, gates, mod_x, g_final.reshape(1, d))
```

```python
import functools
import math

import jax
import jax.numpy as jnp
from jax import lax
from jax.experimental import pallas as pl
from jax.experimental.pallas import tpu as pltpu

F32 = jnp.float32
BF16 = jnp.bfloat16
HIGHEST = lax.Precision.HIGHEST

RMS_EPS = 1e-6
GRID_W = 64
SSM_GROUP = 16
SSM_STATE = 64
N_EXPERTS = 32
TOP_K = 4
SWIGLU_LIMIT = 7.0
SWIGLU_ALPHA = 1.702

CHUNK_T = 16
LANES = 128
SUBLANES = 8
V7X_VMEM_LIMIT_BYTES = 56 * 1024 * 1024

S5_TR = 1024
MOE_TB = 512


def _cparams(sem):
    return pltpu.CompilerParams(dimension_semantics=sem,
                                vmem_limit_bytes=V7X_VMEM_LIMIT_BYTES)


def _bdot(a, b):
    return jnp.dot(a, b, preferred_element_type=F32)


def _rmsnorm(xt, g):
    ms = jnp.mean(xt * xt, axis=-1, keepdims=True)
    return xt * lax.rsqrt(ms + RMS_EPS) * g


def _adaln_kernel(c_ref, w_ref, b_ref, o_ref):
    s = jax.nn.silu(c_ref[...])
    o_ref[...] = jnp.dot(s, w_ref[...], precision=HIGHEST,
                         preferred_element_type=F32) + b_ref[...]


def _adaln(cond, w_mod, b_mod):
    r, d = cond.shape
    n = w_mod.shape[1]
    tn = n // 4
    return pl.pallas_call(
        _adaln_kernel,
        out_shape=jax.ShapeDtypeStruct((r, n), F32),
        grid=(n // tn,),
        in_specs=[pl.BlockSpec((r, d), lambda j: (0, 0)),
                  pl.BlockSpec((d, tn), lambda j: (0, j)),
                  pl.BlockSpec((1, tn), lambda j: (0, j))],
        out_specs=pl.BlockSpec((r, tn), lambda j: (0, j)),
        compiler_params=_cparams(("arbitrary",)),
        name="adaln",
    )(cond, w_mod, b_mod.reshape(1, n))


ROW_T = GRID_W
PITCH = ROW_T + 8
N_SLAB = 4
CHUNKS_PER_TILE = ROW_T // CHUNK_T
GRANULES = LANES // SSM_GROUP


def _granule_transpose_matrix():
    n = GRANULES * LANES
    idx = jnp.arange(n)
    j, m, c = idx // LANES, (idx % LANES) // SSM_GROUP, idx % SSM_GROUP
    dst = m * LANES + j * SSM_GROUP + c
    return jnp.zeros((n, n), F32).at[idx, dst].set(1.0).astype(BF16)


def _modulate_tile(xn, mod_ref, shift_row, scale_row):
    d = xn.shape[1]
    x3 = xn.reshape(8, ROW_T, d)
    x3 = x3 * (1.0 + mod_ref[:, scale_row:scale_row + 1, :]) + mod_ref[:, shift_row:shift_row + 1, :]
    return x3.reshape(8 * ROW_T, d)


def _front_kernel(x_ref, mod_ref, g_ref, w_ref, perm_ref, u_ref, us):
    d = x_ref.shape[2]
    xn = _rmsnorm(x_ref[...].reshape(8 * ROW_T, d), g_ref[...])
    hx = _modulate_tile(xn, mod_ref, 0, 1)
    u = _bdot(hx.astype(BF16), w_ref[...])
    for sl in range(N_SLAB):
        for b in range(8):
            us[sl, pl.ds(b * PITCH, ROW_T), :] = u[b * ROW_T:(b + 1) * ROW_T,
                                                  sl * LANES:(sl + 1) * LANES]
    blocks = []
    for sl in range(N_SLAB):
        for th in range(CHUNK_T // GRANULES):
            rows = []
            for kk in range(CHUNKS_PER_TILE):
                t0 = kk * CHUNK_T + th * GRANULES
                rows.append(jnp.concatenate(
                    [us[sl, pl.ds(t0 + j, 8, stride=PITCH), :] for j in range(GRANULES)], axis=1))
            blocks.append(jnp.concatenate(rows, axis=0))
    acat = jnp.concatenate(blocks, axis=0).astype(BF16)
    bmat = _bdot(acat, perm_ref[...])
    nrow = CHUNKS_PER_TILE * 8
    blk = 0
    for sl in range(N_SLAB):
        for th in range(CHUNK_T // GRANULES):
            for m in range(GRANULES):
                u_ref[sl * GRANULES + m, :, th * LANES:(th + 1) * LANES] = (
                    bmat[blk * nrow:(blk + 1) * nrow, m * LANES:(m + 1) * LANES].astype(u_ref.dtype))
            blk += 1


def _front(x, mod, g_mix, w_u, perm):
    b, s, d = x.shape
    n = w_u.shape[1]
    g = n // SSM_GROUP
    assert s % ROW_T == 0 and b % 8 == 0 and n == N_SLAB * LANES
    k = s // CHUNK_T
    tiles = s // ROW_T
    nrow = CHUNKS_PER_TILE * 8
    shared = mod.shape[0] == 1
    mod_spec = (pl.BlockSpec((1, 8, d), lambda i, j: (0, 0, 0)) if shared
                else pl.BlockSpec((8, 8, d), lambda i, j: (i, 0, 0)))
    return pl.pallas_call(
        _front_kernel,
        out_shape=jax.ShapeDtypeStruct((g, (b // 8) * k * 8, CHUNK_T * SSM_GROUP), BF16),
        grid=(b // 8, tiles),
        in_specs=[pl.BlockSpec((8, ROW_T, d), lambda i, j: (i, j, 0)),
                  mod_spec,
                  pl.BlockSpec((1, d), lambda i, j: (0, 0)),
                  pl.BlockSpec((d, n), lambda i, j: (0, 0)),
                  pl.BlockSpec(perm.shape, lambda i, j: (0, 0))],
        out_specs=pl.BlockSpec((g, nrow, CHUNK_T * SSM_GROUP), lambda i, j: (0, i * tiles + j, 0)),
        scratch_shapes=[pltpu.VMEM((N_SLAB, 8 * PITCH, LANES), F32)],
        compiler_params=_cparams(("arbitrary", "arbitrary")),
        name="front",
    )(x, mod, g_mix, w_u, perm)


P_LANES = 3 * LANES


def _s5_params_kernel(lam_ref, bt_ref, ct_ref, dsk_ref, rep_ref,
                      toep_ref, smat_ref, cpow_ref, at_ref):
    t, ch, p = CHUNK_T, SSM_GROUP, SSM_STATE
    width = t * ch
    f32dot = lambda a, b: jnp.dot(a, b, precision=HIGHEST, preferred_element_type=F32)
    eye = (lax.broadcasted_iota(jnp.int32, (p, p), 0) == lax.broadcasted_iota(jnp.int32, (p, p), 1))
    to_col = lambda row: jnp.sum(jnp.where(eye, jnp.broadcast_to(row, (p, p)), 0.0),
                                 axis=1, keepdims=True)
    blk = lax.broadcasted_iota(jnp.int32, (p, P_LANES), 1) // ch
    row16 = lax.broadcasted_iota(jnp.int32, (ch, width), 0)
    lane16 = lax.broadcasted_iota(jnp.int32, (ch, width), 1)

    smats, cpows, ats = [[], []], [[], []], []
    for j in range(2):
        kt, at_dir = [], []
        for z in range(2):
            lr = lam_ref[j, 4 * z:4 * z + 1, :]
            li = lam_ref[j, 4 * z + 1:4 * z + 2, :]
            dt = jnp.exp(lam_ref[j, 4 * z + 2:4 * z + 3, :])
            mag = jnp.exp(lr * dt)
            ar, ai = mag * jnp.cos(li * dt), mag * jnp.sin(li * dt)
            den = lr * lr + li * li
            qr = ((ar - 1.0) * lr + ai * li) / den
            qi = (ai * lr - (ar - 1.0) * li) / den
            bbr = qr * bt_ref[j, 2 * z] - qi * bt_ref[j, 2 * z + 1]
            bbi = qr * bt_ref[j, 2 * z + 1] + qi * bt_ref[j, 2 * z]
            pw = [(jnp.ones_like(ar), jnp.zeros_like(ai))]
            for _ in range(t):
                r, i = pw[-1]
                pw.append((r * ar - i * ai, r * ai + i * ar))
            at_dir.append(pw[t])

            expo = jnp.where(blk <= t, blk if z == 0 else t - blk, 0)
            tr = jnp.ones((p, P_LANES), F32)
            ti = jnp.zeros((p, P_LANES), F32)
            for bit in range(5):
                sr, si = to_col(pw[1 << bit][0]), to_col(pw[1 << bit][1])
                on = ((expo >> bit) & 1) == 1
                fr, fi = jnp.where(on, sr, 1.0), jnp.where(on, si, 0.0)
                tr, ti = tr * fr - ti * fi, tr * fi + ti * fr
            ctr = f32dot(ct_ref[j, 2 * z], rep_ref[...])
            cti = f32dot(ct_ref[j, 2 * z + 1], rep_ref[...])
            crd = ctr * tr - cti * ti
            cid = ctr * ti + cti * tr
            kt.append(f32dot(bbr, crd) - f32dot(bbi, cid))

            off = ch if z == 0 else 0
            cpows[j].append(crd[:, off:off + width])
            cpows[j].append(-cid[:, off:off + width])

            pows = [pw[t - 1 - s] if z == 0 else pw[s] for s in range(t)]
            pr_rows = jnp.concatenate([jnp.broadcast_to(q[0], (ch, p)) for q in pows], axis=0)
            pi_rows = jnp.concatenate([jnp.broadcast_to(q[1], (ch, p)) for q in pows], axis=0)
            br_rows = jnp.concatenate([bbr] * t, axis=0)
            bi_rows = jnp.concatenate([bbi] * t, axis=0)
            smats[j].append(pr_rows * br_rows - pi_rows * bi_rows)
            smats[j].append(pr_rows * bi_rows + pi_rows * br_rows)
        ats.append(at_dir)

        ktf = kt[0][:, :width]
        ktb = kt[1][:, ch:ch + width]
        rows = []
        for s in range(t):
            fwd = ktf if s == 0 else jnp.concatenate(
                [jnp.zeros((ch, ch * s), F32), ktf[:, :width - ch * s]], axis=1)
            sh = ch * (t - 1 - s)
            bwd = ktb if sh == 0 else jnp.concatenate(
                [ktb[:, sh:], jnp.zeros((ch, sh), F32)], axis=1)
            skip = jnp.where(lane16 == ch * s + row16, dsk_ref[j], 0.0)
            rows.append(fwd + bwd + skip)
        toep_ref[j] = jnp.concatenate(rows, axis=0).astype(toep_ref.dtype)

    zs = jnp.zeros((width, p), F32)
    smat_ref[0] = jnp.concatenate(
        [jnp.concatenate([jnp.concatenate([smats[0][q], zs], axis=1) for q in range(4)], axis=1),
         jnp.concatenate([jnp.concatenate([zs, smats[1][q]], axis=1) for q in range(4)], axis=1)],
        axis=0).astype(smat_ref.dtype)
    zc = jnp.zeros((p, width), F32)
    cpow_ref[0] = jnp.concatenate(
        [jnp.concatenate([cpows[j][q], zc] if j == 0 else [zc, cpows[j][q]], axis=1)
         for q in range(4) for j in range(2)], axis=0).astype(cpow_ref.dtype)
    at_ref[...] = jnp.concatenate(
        [jnp.concatenate([ats[0][z][part], ats[1][z][part]], axis=1)
         for z in range(2) for part in range(2)], axis=0)


def _s5_matrices(lam_re, lam_im, log_dt, b_re, b_im, c_re, c_im, d_skip):
    t, ch = CHUNK_T, SSM_GROUP
    _, g, p = lam_re.shape
    assert p == SSM_STATE and b_re.shape[-1] == ch and 2 * p == LANES
    width = t * ch
    zero = jnp.zeros((g, 1, p), F32)
    lam = jnp.concatenate(
        [jnp.stack([lam_re[z], lam_im[z], jnp.broadcast_to(log_dt[z][:, None], (g, p))], axis=1)
         if part == 0 else zero for z in range(2) for part in range(2)], axis=1)
    bt = jnp.stack([b_re[0], b_im[0], b_re[1], b_im[1]], axis=1).transpose(0, 1, 3, 2)
    ct = jnp.stack([c_re[0], c_im[0], c_re[1], c_im[1]], axis=1).transpose(0, 1, 3, 2)
    dsk = jnp.tile(d_skip.reshape(g, 1, ch), (1, 1, t))
    lane = jnp.arange(P_LANES)
    rep = ((lane[None, :] % ch == jnp.arange(ch)[:, None])
           & (lane[None, :] < (t + 1) * ch)).astype(F32)
    pair = lambda *shape: pl.BlockSpec((2,) + shape, lambda a: (a,) + (0,) * len(shape))
    toep, smat, cpow, a_t = pl.pallas_call(
        _s5_params_kernel,
        out_shape=(jax.ShapeDtypeStruct((g, width, width), BF16),
                   jax.ShapeDtypeStruct((g // 2, 2 * width, 4 * LANES), BF16),
                   jax.ShapeDtypeStruct((g // 2, 4 * LANES, 2 * width), BF16),
                   jax.ShapeDtypeStruct((4, g * p), F32)),
        grid=(g // 2,),
        in_specs=[pair(8, p), pair(4, ch, p), pair(4, p, ch), pair(1, width),
                  pl.BlockSpec((ch, P_LANES), lambda a: (0, 0))],
        out_specs=(pair(width, width),
                   pl.BlockSpec((1, 2 * width, 4 * LANES), lambda a: (a, 0, 0)),
                   pl.BlockSpec((1, 4 * LANES, 2 * width), lambda a: (a, 0, 0)),
                   pl.BlockSpec((4, LANES), lambda a: (0, a))),
        compiler_params=_cparams(("arbitrary",)),
        name="s5_params",
    )(lam, bt, ct, dsk, rep)
    return smat, toep, cpow, a_t


def _chunk_states(u_ref, smat, dst_refs, rows):
    tr = min(S5_TR, rows)
    for r0 in range(0, rows, tr):
        lhs = jnp.concatenate([u_ref[0, r0:r0 + tr, :], u_ref[1, r0:r0 + tr, :]], axis=1)
        res = _bdot(lhs, smat)
        for q in range(4):
            dst_refs[q][r0:r0 + tr, :] = res[:, q * LANES:(q + 1) * LANES]


def _scan_chunks(a_ref, c_refs, x_refs, h_refs, halves, kc, kx):
    cfr_ref, cfi_ref, cbr_ref, cbi_ref = c_refs
    xfr_ref, xfi_ref, xbr_ref, xbi_ref = x_refs
    hfr_ref, hfi_ref, hbr_ref, hbi_ref = h_refs
    a_fr, a_fi = a_ref[0:1, :], a_ref[1:2, :]
    a_br, a_bi = a_ref[2:3, :], a_ref[3:4, :]

    def rows(half, k, n_chunks, count=1):
        return pl.ds(pl.multiple_of((half * n_chunks + k) * 8, 8 * count), 8 * count)

    def step(h_re, h_im, a_re, a_im, s_re, s_im):
        return (a_re * h_re - a_im * h_im + s_re, a_re * h_im + a_im * h_re + s_im)

    zero = jnp.zeros((8, LANES), F32)

    def ctx_body(i, carry):
        out = []
        for half in range(halves):
            fr, fi, br, bi = carry[half]
            kf, kb = i, kc - 1 - i
            fr, fi = step(fr, fi, a_fr, a_fi, cfr_ref[rows(half, kf, kc), :],
                          cfi_ref[rows(half, kf, kc), :])
            br, bi = step(br, bi, a_br, a_bi, cbr_ref[rows(half, kb, kc), :],
                          cbi_ref[rows(half, kb, kc), :])
            out.append((fr, fi, br, bi))
        return tuple(out)

    carry = lax.fori_loop(0, kc, ctx_body, ((zero,) * 4,) * halves)

    def x_body(i, carry):
        out = []
        for half in range(halves):
            fr, fi, br, bi = carry[half]
            kf = 2 * i
            kb = kx - 2 - 2 * i
            fr1, fi1 = step(fr, fi, a_fr, a_fi, xfr_ref[rows(half, kf, kx), :],
                            xfi_ref[rows(half, kf, kx), :])
            br1, bi1 = step(br, bi, a_br, a_bi, xbr_ref[rows(half, kb + 1, kx), :],
                            xbi_ref[rows(half, kb + 1, kx), :])
            hfr_ref[rows(half, kf, kx, 2), :] = jnp.concatenate([fr, fr1], 0).astype(hfr_ref.dtype)
            hfi_ref[rows(half, kf, kx, 2), :] = jnp.concatenate([fi, fi1], 0).astype(hfi_ref.dtype)
            hbr_ref[rows(half, kb, kx, 2), :] = jnp.concatenate([br1, br], 0).astype(hbr_ref.dtype)
            hbi_ref[rows(half, kb, kx, 2), :] = jnp.concatenate([bi1, bi], 0).astype(hbi_ref.dtype)
            fr, fi = step(fr1, fi1, a_fr, a_fi, xfr_ref[rows(half, kf + 1, kx), :],
                          xfi_ref[rows(half, kf + 1, kx), :])
            br, bi = step(br1, bi1, a_br, a_bi, xbr_ref[rows(half, kb, kx), :],
                          xbi_ref[rows(half, kb, kx), :])
            out.append((fr, fi, br, bi))
        return tuple(out)

    lax.fori_loop(0, kx // 2, x_body, carry)


def _s5_chunks_kernel(ux_ref, uc_ref, smat_ref, toep_ref, cpow_ref, a_ref, y_ref,
                      *scratch, halves, kc, kx):
    s_x, s_c, h_in = scratch[0:4], scratch[4:8], scratch[8:12]
    rows_x, rows_c = ux_ref.shape[1], uc_ref.shape[1]
    _chunk_states(uc_ref, smat_ref[0], s_c, rows_c)
    _chunk_states(ux_ref, smat_ref[0], s_x, rows_x)
    _scan_chunks(a_ref, s_c, s_x, h_in, halves, kc, kx)
    n = ux_ref.shape[2]
    tr = min(S5_TR, rows_x)
    for r0 in range(0, rows_x, tr):
        hcat = jnp.concatenate([h[r0:r0 + tr, :] for h in h_in], axis=1)
        yst = _bdot(hcat, cpow_ref[0])
        for j in range(2):
            y_ref[j, r0:r0 + tr, :] = (_bdot(ux_ref[j, r0:r0 + tr, :], toep_ref[j])
                                       + yst[:, j * n:(j + 1) * n]).astype(y_ref.dtype)


def _s5_chunks(utx, utc, smat, toep, cpow, a_t, batch):
    g, rows_x, kdim = utx.shape
    rows_c = utc.shape[1]
    kc, kx = rows_c // batch, rows_x // batch
    assert kx % 2 == 0 and rows_x % min(S5_TR, rows_x) == 0
    pair = lambda rows: pl.BlockSpec((2, rows, kdim), lambda a: (a, 0, 0))
    return pl.pallas_call(
        functools.partial(_s5_chunks_kernel, halves=batch // 8, kc=kc, kx=kx),
        out_shape=jax.ShapeDtypeStruct((g, rows_x, kdim), BF16),
        grid=(g // 2,),
        in_specs=[pair(rows_x), pair(rows_c),
                  pl.BlockSpec((1, 2 * kdim, 4 * LANES), lambda a: (a, 0, 0)),
                  pl.BlockSpec((2, kdim, kdim), lambda a: (a, 0, 0)),
                  pl.BlockSpec((1, 4 * LANES, 2 * kdim), lambda a: (a, 0, 0)),
                  pl.BlockSpec((4, LANES), lambda a: (0, a))],
        out_specs=pair(rows_x),
        scratch_shapes=([pltpu.VMEM((rows_x, LANES), F32)] * 4
                        + [pltpu.VMEM((rows_c, LANES), F32)] * 4
                        + [pltpu.VMEM((rows_x, LANES), BF16)] * 4),
        compiler_params=_cparams(("arbitrary",)),
        name="s5_chunks",
    )(utx, utc, smat, toep, cpow, a_t)


def _chunk_rows_to_tokens(y_ref, perm_ref, ys):
    nrow = CHUNKS_PER_TILE * 8
    blocks = []
    for sl in range(N_SLAB):
        for th in range(CHUNK_T // GRANULES):
            blocks.append(jnp.concatenate(
                [y_ref[sl * GRANULES + m, :, th * LANES:(th + 1) * LANES] for m in range(GRANULES)],
                axis=1))
    acat = _bdot(jnp.concatenate(blocks, axis=0), perm_ref[...])
    blk = 0
    for sl in range(N_SLAB):
        for th in range(CHUNK_T // GRANULES):
            for kk in range(CHUNKS_PER_TILE):
                for j in range(GRANULES):
                    t = kk * CHUNK_T + th * GRANULES + j
                    ys[sl, pl.ds(t, 8, stride=PITCH), :] = acat[blk * nrow + kk * 8:
                                                               blk * nrow + (kk + 1) * 8,
                                                               j * LANES:(j + 1) * LANES]
            blk += 1
    return jnp.concatenate(
        [jnp.concatenate([ys[sl, pl.ds(b * PITCH, ROW_T), :] for sl in range(N_SLAB)], axis=1)
         for b in range(8)], axis=0)


def _back_kernel(x_ref, y_ref, mod_ref, perm_ref, gmix_ref, win_ref, convw_ref, convb_ref,
                 wglu_ref, wsbr_ref, wcbr_ref, wo_ref, gffn_ref, wrhi_ref, wrlo_ref, br_ref,
                 ltri_ref, x1_ref, h_ref, ids_ref, gates_ref, hist_ref, ys_scr):
    d_model = x_ref.shape[2]
    tm = 8 * ROW_T
    d_ssm = N_SLAB * LANES
    xt = x_ref[...].reshape(tm, d_model)
    xn = _rmsnorm(xt, gmix_ref[...])
    hx = _modulate_tile(xn, mod_ref, 0, 1).astype(BF16)
    proj = lambda lo, hi: _bdot(hx, win_ref[:, lo:hi])
    ys_tok = _chunk_rows_to_tokens(y_ref, perm_ref, ys_scr)
    v = proj(0, d_ssm)
    ys = jax.nn.gelu(ys_tok)
    glu = _bdot(ys.astype(BF16), wglu_ref[...])
    gate_c = proj(2 * d_ssm, 3 * d_ssm)
    ys = ys * jax.nn.sigmoid(glu)
    y_s = _bdot(ys.astype(BF16), wsbr_ref[...])
    gate_b = proj(d_ssm, 2 * d_ssm)

    z = gate_c * v
    col = lax.broadcasted_iota(jnp.int32, z.shape, 0) % GRID_W
    z_prev = jnp.where(col == 0, 0.0, pltpu.roll(z, 1, 0))
    z_next = jnp.where(col == GRID_W - 1, 0.0, pltpu.roll(z, tm - 1, 0))
    conv = (z_prev * convw_ref[0:1, :] + z * convw_ref[1:2, :]
            + z_next * convw_ref[2:3, :] + convb_ref[...])
    y_conv = gate_b * conv

    g_s = proj(3 * d_ssm, 3 * d_ssm + d_model)
    y_c = _bdot(y_conv.astype(BF16), wcbr_ref[...])
    merged_s = jax.nn.sigmoid(g_s) * y_s
    g_c = proj(3 * d_ssm + d_model, 3 * d_ssm + 2 * d_model)
    merged = merged_s + jax.nn.sigmoid(g_c) * y_c
    mo = _bdot(merged.astype(BF16), wo_ref[...])
    x1 = xt + (mo.reshape(8, ROW_T, d_model) * mod_ref[:, 2:3, :]).reshape(tm, d_model)
    x1_ref[...] = x1.reshape(8, ROW_T, d_model)

    hn = _modulate_tile(_rmsnorm(x1, gffn_ref[...]), mod_ref, 3, 4)
    for b in range(8):
        for s in range(SUBLANES):
            h_ref[b, pl.ds(s, ROW_T, stride=SUBLANES), :] = hn[b * ROW_T:(b + 1) * ROW_T,
                                                              s * LANES:(s + 1) * LANES]

    hn_hi = hn.astype(BF16)
    hn_lo = (hn - hn_hi.astype(F32)).astype(BF16)
    logits = (_bdot(hn_hi, wrhi_ref[...]) + _bdot(hn_lo, wrhi_ref[...])
              + _bdot(hn_hi, wrlo_ref[...]) + br_ref[...])
    lane = lax.broadcasted_iota(jnp.int32, logits.shape, 1)
    neg = jnp.float32(-jnp.inf)
    cur = jnp.where(lane < N_EXPERTS, logits, neg)
    vals, idxs = [], []
    for _ in range(TOP_K):
        mk = jnp.max(cur, axis=-1, keepdims=True)
        ik = jnp.min(jnp.where(cur == mk, lane, LANES), axis=-1, keepdims=True)
        vals.append(mk)
        idxs.append(ik)
        cur = jnp.where(lane == ik, neg, cur)
    exps = [jnp.exp(vk - vals[0]) for vk in vals]
    denom = exps[0] + exps[1] + exps[2] + exps[3]
    ids = jnp.zeros(logits.shape, jnp.int32)
    gates = jnp.zeros(logits.shape, F32)
    onehot = jnp.zeros(logits.shape, F32)
    for k in range(TOP_K):
        onehot = onehot + (lane == idxs[k]).astype(F32)
    before = _bdot(ltri_ref[...], onehot.astype(BF16))
    for k in range(TOP_K):
        rank_k = jnp.sum(jnp.where(lane == idxs[k], before, 0.0), axis=-1, keepdims=True)
        ids = jnp.where(lane == k, idxs[k], ids)
        ids = jnp.where(lane == TOP_K + k, rank_k.astype(jnp.int32), ids)
        gates = jnp.where(lane == k, exps[k] / denom, gates)
    ids_ref[0] = jnp.transpose(ids)[0:2 * TOP_K, :]
    gates_ref[...] = gates.reshape(8, ROW_T, LANES)
    hist_ref[0] = jnp.broadcast_to(jnp.sum(onehot, axis=0, keepdims=True), (8, LANES))


def _back(x, y_chunks, mod, perm, g_mix, w_rest, conv_w, conv_b, w_glu, w_ssm_br,
          w_conv_br, w_o, g_ffn, w_router, b_router):
    b, s, d = x.shape
    g, _, kdim = y_chunks.shape
    tiles = s // ROW_T
    nrow = CHUNKS_PER_TILE * 8
    tok = lambda n: pl.BlockSpec((8, ROW_T, n), lambda i, j: (i, j, 0))

    def const(arr):
        nd = arr.ndim
        return pl.BlockSpec(arr.shape, lambda i, j: (0,) * nd, pipeline_mode=pl.Buffered(1))

    w_r_hi = w_router.astype(BF16)
    w_r_lo = (w_router - w_r_hi.astype(F32)).astype(BF16)
    row = jnp.arange(8 * ROW_T)
    ltri = (row[:, None] > row[None, :]).astype(BF16)
    params = (perm, g_mix, w_rest, conv_w, conv_b, w_glu, w_ssm_br, w_conv_br, w_o, g_ffn,
              w_r_hi, w_r_lo, b_router, ltri)
    return pl.pallas_call(
        _back_kernel,
        out_shape=(jax.ShapeDtypeStruct((b, s, d), F32),
                   jax.ShapeDtypeStruct((b, s * SUBLANES, LANES), F32),
                   jax.ShapeDtypeStruct(((b // 8) * tiles, 2 * TOP_K, 8 * ROW_T), jnp.int32),
                   jax.ShapeDtypeStruct((b, s, LANES), F32),
                   jax.ShapeDtypeStruct(((b // 8) * tiles, 8, LANES), F32)),
        grid=(b // 8, tiles),
        in_specs=[tok(d),
                  pl.BlockSpec((g, nrow, kdim), lambda i, j: (0, i * tiles + j, 0)),
                  pl.BlockSpec((8, 8, d), lambda i, j: (i, 0, 0))] + [const(a) for a in params],
        out_specs=(tok(d), pl.BlockSpec((8, ROW_T * SUBLANES, LANES), lambda i, j: (i, j, 0)),
                   pl.BlockSpec((1, 2 * TOP_K, 8 * ROW_T), lambda i, j: (i * tiles + j, 0, 0)),
                   tok(LANES),
                   pl.BlockSpec((1, 8, LANES), lambda i, j: (i * tiles + j, 0, 0))),
        scratch_shapes=[pltpu.VMEM((N_SLAB, 8 * PITCH, LANES), F32)],
        compiler_params=_cparams(("arbitrary", "arbitrary")),
        name="back",
    )(x, y_chunks, mod, *params)


def _routing(ids_ranks, tile_hist, bsz, seq):
    tb = MOE_TB
    n_tiles, _, tm = ids_ranks.shape
    ids, ranks = ids_ranks[:, :TOP_K, :], ids_ranks[:, TOP_K:, :]
    n_slot = n_tiles * tm * TOP_K
    tile_before = jnp.cumsum(tile_hist, axis=0) - tile_hist
    counts = jnp.sum(tile_hist, axis=0)
    padded = (counts + tb - 1) // tb * tb
    pad_end = jnp.cumsum(padded)
    pad_start = pad_end - padded
    start = jnp.cumsum(counts) - counts
    base = pad_start[None, :] + tile_before
    pos = ranks
    for e in range(N_EXPERTS):
        pos = pos + jnp.where(ids == e, base[:, e][:, None, None], 0)
    pos = pos.astype(jnp.int32)
    tiles = seq // ROW_T
    tile, row = jnp.arange(n_tiles, dtype=jnp.int32), jnp.arange(tm, dtype=jnp.int32)
    token = (((tile // tiles)[:, None] * 8 + (row // ROW_T)[None, :]) * seq
             + (tile % tiles)[:, None] * ROW_T + (row % ROW_T)[None, :])
    slot_token = jnp.broadcast_to(token[:, None, :], pos.shape)
    _, order_token = lax.sort_key_val(pos.reshape(-1), slot_token.reshape(-1))
    n_blocks = n_slot // tb + N_EXPERTS
    cap = n_blocks * tb
    block_first_row = jnp.arange(n_blocks, dtype=jnp.int32) * tb
    block_expert = jnp.minimum(
        jnp.sum(pad_end[None, :] <= block_first_row[:, None], axis=1, dtype=jnp.int32),
        N_EXPERTS - 1)
    within = (block_first_row - pad_start[block_expert])[:, None] + jnp.arange(tb, dtype=jnp.int32)
    valid = within < counts[block_expert][:, None]
    src = jnp.clip(start[block_expert][:, None] + within, 0, n_slot - 1)
    buf_token = jnp.where(valid, order_token[src], 0).astype(jnp.int32).reshape(cap)
    n_used = (pad_end[-1] // tb).astype(jnp.int32).reshape(1)
    return buf_token, pos.reshape(n_tiles, TOP_K * tm), block_expert, n_used


def _to_tiles(ref, val):
    rows = val.shape[0]
    for s in range(SUBLANES):
        ref[pl.ds(s, rows, stride=SUBLANES), :] = val[:, s * LANES:(s + 1) * LANES]


def _tile_piece(ref, first_row, rows, s):
    return ref[pl.ds(first_row * SUBLANES + s, rows, stride=SUBLANES), :]


def _tile_gather_start(idx_ref, src_hbm, dst, sem, n_rows, unrolled, alternate_priority=False):
    def copy(r, t):
        return pltpu.make_async_copy(
            src_hbm.at[pl.ds(pl.multiple_of(t * SUBLANES, SUBLANES), SUBLANES), :],
            dst.at[pl.ds(pl.multiple_of(r * SUBLANES, SUBLANES), SUBLANES), :], sem)

    if unrolled:
        for r in range(n_rows):
            copy(r, idx_ref[0, 0, r]).start(priority=(r % 2) if alternate_priority else 0)
    else:
        def body(r, carry):
            copy(r, idx_ref[0, 0, r]).start()
            return carry
        lax.fori_loop(0, n_rows, body, 0, unroll=8)


def _tile_gather_wait(src_hbm, dst, sem):
    pltpu.make_async_copy(src_hbm.at[pl.ds(0, dst.shape[0]), :], dst, sem).wait()


def _ring_step(i, last, bufs, sem, wait, prefetch, compute):
    n = len(bufs)
    for p in range(n):
        def branch(p=p):
            wait(bufs[p], sem.at[p])
            q = (p + n - 1) % n
            prefetch(bufs[q], sem.at[q])
            compute(bufs[p])

            @pl.when(i == last)
            def _():
                for r in range(1, n):
                    wait(bufs[(p + r) % n], sem.at[(p + r) % n])

        pl.when((i <= last) & (lax.rem(i, n) == p))(branch)


def _cast_rows(src_ref, dst_ref, chunk):
    def body(c, carry):
        rows = pl.ds(pl.multiple_of(c * chunk, chunk), chunk)
        dst_ref[rows, :] = src_ref[0, rows, :].astype(dst_ref.dtype)
        return carry
    lax.fori_loop(0, dst_ref.shape[0] // chunk, body, 0)


def _moe_kernel(be_ref, nu_ref, tok0_ref, tok1_ref, tok2_ref, h_hbm, wgu_ref, bgu_ref, wd_ref,
                bd_ref, out_ref, xbuf0, xbuf1, xbuf2, wgu_bf, wd_bf, sem):
    i = pl.program_id(0)
    n_used = nu_ref[0]
    tb = xbuf0.shape[0] // SUBLANES
    f = wd_ref.shape[1]

    @pl.when(i == 0)
    def _():
        _tile_gather_start(tok0_ref, h_hbm, xbuf0, sem.at[0], tb, unrolled=False)
        _tile_gather_start(tok1_ref, h_hbm, xbuf1, sem.at[1], tb, unrolled=False)

    new_expert = (i == 0) | (be_ref[i] != be_ref[jnp.maximum(i - 1, 0)])

    @pl.when(new_expert & (i < n_used))
    def _():
        _cast_rows(wgu_ref, wgu_bf, 128)
        _cast_rows(wd_ref, wd_bf, 128)

    def compute(xcur):
        xe = jnp.concatenate([_tile_piece(xcur, 0, tb, s) for s in range(SUBLANES)],
                             axis=1).astype(BF16)
        gu = _bdot(xe, wgu_bf[...]) + bgu_ref[0]
        gt = jnp.minimum(gu[:, :f], SWIGLU_LIMIT)
        up = jnp.clip(gu[:, f:], -SWIGLU_LIMIT, SWIGLU_LIMIT)
        act = gt * jax.nn.sigmoid(SWIGLU_ALPHA * gt) * (up + 1.0)
        _to_tiles(out_ref, _bdot(act.astype(BF16), wd_bf[...]) + bd_ref[0])

    _ring_step(i, n_used - 1, (xbuf0, xbuf1, xbuf2), sem,
               wait=lambda buf, s: _tile_gather_wait(h_hbm, buf, s),
               prefetch=lambda buf, s: _tile_gather_start(tok2_ref, h_hbm, buf, s, tb,
                                                          unrolled=True),
               compute=compute)

    @pl.when(i >= n_used)
    def _():
        out_ref[...] = jnp.zeros(out_ref.shape, out_ref.dtype)


def _moe(h_tiles, buf_token, block_expert, n_used, w_gu, b_gu, w_down, b_down):
    e, d, f2 = w_gu.shape
    f = f2 // 2
    tb = MOE_TB
    cap = buf_token.shape[0]
    n_blocks = cap // tb
    tok3 = buf_token.reshape(n_blocks, 1, tb)
    smem_blk = lambda imap: pl.BlockSpec((1, 1, tb), imap, memory_space=pltpu.SMEM)
    grid_spec = pltpu.PrefetchScalarGridSpec(
        num_scalar_prefetch=2,
        grid=(n_blocks,),
        in_specs=[smem_blk(lambda i, be, nu: (i, 0, 0)),
                  smem_blk(lambda i, be, nu: (jnp.minimum(i + 1, n_blocks - 1), 0, 0)),
                  smem_blk(lambda i, be, nu: (jnp.minimum(i + 2, n_blocks - 1), 0, 0)),
                  pl.BlockSpec(memory_space=pl.ANY),
                  pl.BlockSpec((1, d, f2), lambda i, be, nu: (be[i], 0, 0)),
                  pl.BlockSpec((1, 1, f2), lambda i, be, nu: (be[i], 0, 0)),
                  pl.BlockSpec((1, f, d), lambda i, be, nu: (be[i], 0, 0)),
                  pl.BlockSpec((1, 1, d), lambda i, be, nu: (be[i], 0, 0))],
        out_specs=pl.BlockSpec((tb * SUBLANES, LANES), lambda i, be, nu: (i, 0)),
        scratch_shapes=[pltpu.VMEM((tb * SUBLANES, LANES), F32)] * 3
        + [pltpu.VMEM((d, f2), BF16), pltpu.VMEM((f, d), BF16), pltpu.SemaphoreType.DMA((3,))],
    )
    return pl.pallas_call(
        _moe_kernel,
        out_shape=jax.ShapeDtypeStruct((cap * SUBLANES, LANES), F32),
        grid_spec=grid_spec,
        compiler_params=_cparams(("arbitrary",)),
        name="moe",
    )(block_expert, n_used, tok3, tok3, tok3, h_tiles, w_gu, b_gu.reshape(e, 1, f2),
      w_down, b_down.reshape(e, 1, d))


def _combine_kernel(pos0_ref, pos1_ref, pos2_ref, yb_hbm, x1_ref, gates_ref, mod_ref, gfin_ref,
                    out_ref, buf0, buf1, buf2, sem):
    i = pl.program_id(0)
    n = pl.num_programs(0)
    n_rows = buf0.shape[0] // SUBLANES
    tm = n_rows // TOP_K
    d = x1_ref.shape[2]

    @pl.when(i == 0)
    def _():
        _tile_gather_start(pos0_ref, yb_hbm, buf0, sem.at[0], n_rows, unrolled=False)
        _tile_gather_start(pos1_ref, yb_hbm, buf1, sem.at[1], n_rows, unrolled=False)

    def compute(cur):
        g = gates_ref[...].reshape(tm, LANES)
        gk = [jnp.broadcast_to(g[:, k:k + 1], (tm, LANES)) for k in range(TOP_K)]
        pieces = []
        for s in range(SUBLANES):
            acc = gk[0] * _tile_piece(cur, 0, tm, s)
            for k in range(1, TOP_K):
                acc = acc + gk[k] * _tile_piece(cur, k * tm, tm, s)
            pieces.append(acc)
        nb = x1_ref.shape[0]
        moe = jnp.concatenate(pieces, axis=1).reshape(nb, ROW_T, d)
        x2 = (x1_ref[...] + mod_ref[:, 5:6, :] * moe).reshape(tm, d)
        out_ref[...] = _rmsnorm(x2, gfin_ref[...]).reshape(nb, ROW_T, d)

    _ring_step(i, n - 1, (buf0, buf1, buf2), sem,
               wait=lambda buf, s: _tile_gather_wait(yb_hbm, buf, s),
               prefetch=lambda buf, s: _tile_gather_start(pos2_ref, yb_hbm, buf, s, n_rows,
                                                          unrolled=True, alternate_priority=True),
               compute=compute)


def _combine(yb, pos, x1, gates, mod, g_final):
    b, s, d = x1.shape
    nb = 4
    tm = nb * ROW_T
    tiles = s // ROW_T
    n_tiles = (b // nb) * tiles
    split = 8 // nb
    pos3 = pos.reshape(-1, TOP_K, split, tm).transpose(0, 2, 1, 3).reshape(n_tiles, 1, TOP_K * tm)
    smem_blk = lambda imap: pl.BlockSpec((1, 1, TOP_K * tm), imap, memory_space=pltpu.SMEM)
    bidx = lambda i: (i // (split * tiles)) * split + i % split
    tok = lambda n: pl.BlockSpec((nb, ROW_T, n), lambda i: (bidx(i), (i // split) % tiles, 0))
    buf = pltpu.VMEM((TOP_K * tm * SUBLANES, LANES), F32)
    return pl.pallas_call(
        _combine_kernel,
        out_shape=jax.ShapeDtypeStruct((b, s, d), F32),
        grid=(n_tiles,),
        in_specs=[smem_blk(lambda i: (i, 0, 0)),
                  smem_blk(lambda i: (jnp.minimum(i + 1, n_tiles - 1), 0, 0)),
                  smem_blk(lambda i: (jnp.minimum(i + 2, n_tiles - 1), 0, 0)),
                  pl.BlockSpec(memory_space=pl.ANY),
                  tok(d), tok(LANES),
                  pl.BlockSpec((nb, 8, d), lambda i: (bidx(i), 0, 0)),
                  pl.BlockSpec((1, d), lambda i: (0, 0))],
        out_specs=tok(d),
        scratch_shapes=[buf, buf, buf, pltpu.SemaphoreType.DMA((3,))],
        compiler_params=_cparams(("arbitrary",)),
        name="combine",
    )(pos3, pos3, pos3, yb, x1, gates, mod, g_final)


def kernel(x, c, ctx, c_ctx, w_mod, b_mod, g_mix, w_in, lam_re, lam_im, log_dt, b_re, b_im,
           c_re, c_im, d_skip, w_glu, conv_w, conv_b, w_ssm_br, w_conv_br, w_o, g_ffn,
           w_router, b_router, w_gu, b_gu, w_down, b_down, g_final):
    depth = w_mod.shape[0]
    assert depth == 1, "single-layer trunk"
    bsz, seq, d = x.shape
    ctx_len = ctx.shape[1]
    d_ssm = d // 2
    assert bsz % 8 == 0 and seq % CHUNK_T == 0 and ctx_len % CHUNK_T == 0 and seq % GRID_W == 0
    assert d == SUBLANES * LANES, "row gathers move one (8,128) f32 tile per token"

    n_cond = -(-(bsz + 1) // 8) * 8
    cond = jnp.zeros((n_cond, d), F32).at[:bsz].set(c).at[bsz].set(c_ctx)
    m = _adaln(cond, w_mod[0], b_mod[0])
    zeros2 = jnp.zeros((n_cond, 2, d), F32)
    mod_all = jnp.concatenate([m.reshape(n_cond, 6, d), zeros2], axis=1)
    mod_x, mod_c = mod_all[:bsz], mod_all[bsz:bsz + 1]

    w_in_bf = w_in[0].astype(BF16)
    w_u, w_rest = w_in_bf[:, :d_ssm], w_in_bf[:, d_ssm:]
    gm = g_mix[0].reshape(1, d)

    perm = _granule_transpose_matrix()
    utx = _front(x, mod_x, gm, w_u, perm)
    utc = _front(ctx, mod_c, gm, w_u, perm)

    smat, toep, cpow, a_t = _s5_matrices(lam_re[0], lam_im[0], log_dt[0], b_re[0], b_im[0],
                                         c_re[0], c_im[0], d_skip[0])
    y_chunks = _s5_chunks(utx, utc, smat, toep, cpow, a_t, bsz)

    pad_r = jnp.zeros((d, LANES - N_EXPERTS), F32)
    w_r = jnp.concatenate([w_router[0], pad_r], axis=1)
    b_r = jnp.concatenate([b_router[0], jnp.zeros((LANES - N_EXPERTS,), F32)]).reshape(1, LANES)
    x1, h, ids, gates, hist = _back(
        x, y_chunks, mod_x, perm, gm, w_rest, conv_w[0], conv_b[0].reshape(1, d_ssm),
        w_glu[0].astype(BF16), w_ssm_br[0].astype(BF16),
        w_conv_br[0].astype(BF16), w_o[0].astype(BF16), g_ffn[0].reshape(1, d), w_r, b_r)

    n_tok = bsz * seq
    buf_token, pos, block_expert, n_used = _routing(
        ids, hist[:, 0, :N_EXPERTS].astype(jnp.int32), bsz, seq)
    yb = _moe(h.reshape(n_tok * SUBLANES, LANES), buf_token, block_expert, n_used,
              w_gu[0], b_gu[0], w_down[0], b_down[0])
    return _combine(yb, pos, x1, gates, mod_x, g_final.reshape(1, d))
```

```python
import functools

import jax
import jax.numpy as jnp
from jax import lax
from jax.experimental import pallas as pl
from jax.experimental.pallas import tpu as pltpu

F32 = jnp.float32
BF16 = jnp.bfloat16
HIGHEST = lax.Precision.HIGHEST

RMS_EPS = 1e-6
GRID_W = 64
SSM_GROUP = 16
SSM_STATE = 64
N_EXPERTS = 32
TOP_K = 4
SWIGLU_LIMIT = 7.0
SWIGLU_ALPHA = 1.702

CHUNK_T = 16
LANES = 128
SUBLANES = 8
V7X_VMEM_LIMIT_BYTES = 56 * 1024 * 1024

S5_TR = 1024
MOE_TB = 512


def _cparams(sem):
    return pltpu.CompilerParams(dimension_semantics=sem,
                                vmem_limit_bytes=V7X_VMEM_LIMIT_BYTES)


def _bdot(a, b):
    return jnp.dot(a, b, preferred_element_type=F32)


def _rmsnorm(xt, g):
    ms = jnp.mean(xt * xt, axis=-1, keepdims=True)
    return xt * lax.rsqrt(ms + RMS_EPS) * g


def _adaln_kernel(c_ref, w_ref, b_ref, o_ref):
    s = jax.nn.silu(c_ref[...])
    o_ref[...] = jnp.dot(s, w_ref[...], precision=HIGHEST,
                         preferred_element_type=F32) + b_ref[...]


def _adaln(cond, w_mod, b_mod):
    r, d = cond.shape
    n = w_mod.shape[1]
    tn = n // 4
    return pl.pallas_call(
        _adaln_kernel,
        out_shape=jax.ShapeDtypeStruct((r, n), F32),
        grid=(n // tn,),
        in_specs=[pl.BlockSpec((r, d), lambda j: (0, 0)),
                  pl.BlockSpec((d, tn), lambda j: (0, j)),
                  pl.BlockSpec((1, tn), lambda j: (0, j))],
        out_specs=pl.BlockSpec((r, tn), lambda j: (0, j)),
        compiler_params=_cparams(("arbitrary",)),
        name="adaln",
    )(cond, w_mod, b_mod.reshape(1, n))


ROW_T = GRID_W
PITCH = ROW_T + 8
N_SLAB = 4
CHUNKS_PER_TILE = ROW_T // CHUNK_T
GRANULES = LANES // SSM_GROUP


def _granule_transpose_matrix():
    n = GRANULES * LANES
    idx = jnp.arange(n)
    j, m, c = idx // LANES, (idx % LANES) // SSM_GROUP, idx % SSM_GROUP
    dst = m * LANES + j * SSM_GROUP + c
    return jnp.zeros((n, n), F32).at[idx, dst].set(1.0).astype(BF16)


def _modulate_tile(xn, mod_ref, shift_row, scale_row):
    d = xn.shape[1]
    x3 = xn.reshape(8, ROW_T, d)
    x3 = x3 * (1.0 + mod_ref[:, scale_row:scale_row + 1, :]) + mod_ref[:, shift_row:shift_row + 1, :]
    return x3.reshape(8 * ROW_T, d)


def _front_kernel(x_ref, mod_ref, g_ref, w_ref, perm_ref, u_ref, us):
    d = x_ref.shape[2]
    xn = _rmsnorm(x_ref[...].reshape(8 * ROW_T, d), g_ref[...])
    hx = _modulate_tile(xn, mod_ref, 0, 1)
    u = _bdot(hx.astype(BF16), w_ref[...])
    for sl in range(N_SLAB):
        for b in range(8):
            us[sl, pl.ds(b * PITCH, ROW_T), :] = u[b * ROW_T:(b + 1) * ROW_T,
                                                  sl * LANES:(sl + 1) * LANES]
    blocks = []
    for sl in range(N_SLAB):
        for th in range(CHUNK_T // GRANULES):
            rows = []
            for kk in range(CHUNKS_PER_TILE):
                t0 = kk * CHUNK_T + th * GRANULES
                rows.append(jnp.concatenate(
                    [us[sl, pl.ds(t0 + j, 8, stride=PITCH), :] for j in range(GRANULES)], axis=1))
            blocks.append(jnp.concatenate(rows, axis=0))
    acat = jnp.concatenate(blocks, axis=0).astype(BF16)
    bmat = _bdot(acat, perm_ref[...])
    nrow = CHUNKS_PER_TILE * 8
    blk = 0
    for sl in range(N_SLAB):
        for th in range(CHUNK_T // GRANULES):
            for m in range(GRANULES):
                u_ref[sl * GRANULES + m, :, th * LANES:(th + 1) * LANES] = (
                    bmat[blk * nrow:(blk + 1) * nrow, m * LANES:(m + 1) * LANES].astype(u_ref.dtype))
            blk += 1


def _front(x, mod, g_mix, w_u, perm):
    b, s, d = x.shape
    n = w_u.shape[1]
    g = n // SSM_GROUP
    assert s % ROW_T == 0 and b % 8 == 0 and n == N_SLAB * LANES
    k = s // CHUNK_T
    tiles = s // ROW_T
    nrow = CHUNKS_PER_TILE * 8
    shared = mod.shape[0] == 1
    mod_spec = (pl.BlockSpec((1, 8, d), lambda i, j: (0, 0, 0)) if shared
                else pl.BlockSpec((8, 8, d), lambda i, j: (i, 0, 0)))
    return pl.pallas_call(
        _front_kernel,
        out_shape=jax.ShapeDtypeStruct((g, (b // 8) * k * 8, CHUNK_T * SSM_GROUP), BF16),
        grid=(b // 8, tiles),
        in_specs=[pl.BlockSpec((8, ROW_T, d), lambda i, j: (i, j, 0)),
                  mod_spec,
                  pl.BlockSpec((1, d), lambda i, j: (0, 0)),
                  pl.BlockSpec((d, n), lambda i, j: (0, 0)),
                  pl.BlockSpec(perm.shape, lambda i, j: (0, 0))],
        out_specs=pl.BlockSpec((g, nrow, CHUNK_T * SSM_GROUP), lambda i, j: (0, i * tiles + j, 0)),
        scratch_shapes=[pltpu.VMEM((N_SLAB, 8 * PITCH, LANES), F32)],
        compiler_params=_cparams(("arbitrary", "arbitrary")),
        name="front",
    )(x, mod, g_mix, w_u, perm)


P_LANES = 3 * LANES


def _s5_params_kernel(lam_ref, bt_ref, ct_ref, dsk_ref, rep_ref,
                      toep_ref, smat_ref, cpow_ref, at_ref):
    t, ch, p = CHUNK_T, SSM_GROUP, SSM_STATE
    width = t * ch
    f32dot = lambda a, b: jnp.dot(a, b, precision=HIGHEST, preferred_element_type=F32)
    eye = (lax.broadcasted_iota(jnp.int32, (p, p), 0) == lax.broadcasted_iota(jnp.int32, (p, p), 1))
    to_col = lambda row: jnp.sum(jnp.where(eye, jnp.broadcast_to(row, (p, p)), 0.0),
                                 axis=1, keepdims=True)
    blk = lax.broadcasted_iota(jnp.int32, (p, P_LANES), 1) // ch
    row16 = lax.broadcasted_iota(jnp.int32, (ch, width), 0)
    lane16 = lax.broadcasted_iota(jnp.int32, (ch, width), 1)

    smats, cpows, ats = [[], []], [[], []], []
    for j in range(2):
        kt, at_dir = [], []
        for z in range(2):
            lr = lam_ref[j, 4 * z:4 * z + 1, :]
            li = lam_ref[j, 4 * z + 1:4 * z + 2, :]
            dt = jnp.exp(lam_ref[j, 4 * z + 2:4 * z + 3, :])
            mag = jnp.exp(lr * dt)
            ar, ai = mag * jnp.cos(li * dt), mag * jnp.sin(li * dt)
            den = lr * lr + li * li
            qr = ((ar - 1.0) * lr + ai * li) / den
            qi = (ai * lr - (ar - 1.0) * li) / den
            bbr = qr * bt_ref[j, 2 * z] - qi * bt_ref[j, 2 * z + 1]
            bbi = qr * bt_ref[j, 2 * z + 1] + qi * bt_ref[j, 2 * z]
            pw = [(jnp.ones_like(ar), jnp.zeros_like(ai))]
            for _ in range(t):
                r, i = pw[-1]
                pw.append((r * ar - i * ai, r * ai + i * ar))
            at_dir.append(pw[t])

            expo = jnp.where(blk <= t, blk if z == 0 else t - blk, 0)
            tr = jnp.ones((p, P_LANES), F32)
            ti = jnp.zeros((p, P_LANES), F32)
            for bit in range(5):
                sr, si = to_col(pw[1 << bit][0]), to_col(pw[1 << bit][1])
                on = ((expo >> bit) & 1) == 1
                fr, fi = jnp.where(on, sr, 1.0), jnp.where(on, si, 0.0)
                tr, ti = tr * fr - ti * fi, tr * fi + ti * fr
            ctr = f32dot(ct_ref[j, 2 * z], rep_ref[...])
            cti = f32dot(ct_ref[j, 2 * z + 1], rep_ref[...])
            crd = ctr * tr - cti * ti
            cid = ctr * ti + cti * tr
            kt.append(f32dot(bbr, crd) - f32dot(bbi, cid))

            off = ch if z == 0 else 0
            cpows[j].append(crd[:, off:off + width])
            cpows[j].append(-cid[:, off:off + width])

            pows = [pw[t - 1 - s] if z == 0 else pw[s] for s in range(t)]
            pr_rows = jnp.concatenate([jnp.broadcast_to(q[0], (ch, p)) for q in pows], axis=0)
            pi_rows = jnp.concatenate([jnp.broadcast_to(q[1], (ch, p)) for q in pows], axis=0)
            br_rows = jnp.concatenate([bbr] * t, axis=0)
            bi_rows = jnp.concatenate([bbi] * t, axis=0)
            smats[j].append(pr_rows * br_rows - pi_rows * bi_rows)
            smats[j].append(pr_rows * bi_rows + pi_rows * br_rows)
        ats.append(at_dir)

        ktf = kt[0][:, :width]
        ktb = kt[1][:, ch:ch + width]
        rows = []
        for s in range(t):
            fwd = ktf if s == 0 else jnp.concatenate(
                [jnp.zeros((ch, ch * s), F32), ktf[:, :width - ch * s]], axis=1)
            sh = ch * (t - 1 - s)
            bwd = ktb if sh == 0 else jnp.concatenate(
                [ktb[:, sh:], jnp.zeros((ch, sh), F32)], axis=1)
            skip = jnp.where(lane16 == ch * s + row16, dsk_ref[j], 0.0)
            rows.append(fwd + bwd + skip)
        toep_ref[j] = jnp.concatenate(rows, axis=0).astype(toep_ref.dtype)

    zs = jnp.zeros((width, p), F32)
    smat_ref[0] = jnp.concatenate(
        [jnp.concatenate([jnp.concatenate([smats[0][q], zs], axis=1) for q in range(4)], axis=1),
         jnp.concatenate([jnp.concatenate([zs, smats[1][q]], axis=1) for q in range(4)], axis=1)],
        axis=0).astype(smat_ref.dtype)
    zc = jnp.zeros((p, width), F32)
    cpow_ref[0] = jnp.concatenate(
        [jnp.concatenate([cpows[j][q], zc] if j == 0 else [zc, cpows[j][q]], axis=1)
         for q in range(4) for j in range(2)], axis=0).astype(cpow_ref.dtype)
    at_ref[...] = jnp.concatenate(
        [jnp.concatenate([ats[0][z][part], ats[1][z][part]], axis=1)
         for z in range(2) for part in range(2)], axis=0)


def _s5_matrices(lam_re, lam_im, log_dt, b_re, b_im, c_re, c_im, d_skip):
    t, ch = CHUNK_T, SSM_GROUP
    _, g, p = lam_re.shape
    assert p == SSM_STATE and b_re.shape[-1] == ch and 2 * p == LANES
    width = t * ch
    zero = jnp.zeros((g, 1, p), F32)
    lam = jnp.concatenate(
        [jnp.stack([lam_re[z], lam_im[z], jnp.broadcast_to(log_dt[z][:, None], (g, p))], axis=1)
         if part == 0 else zero for z in range(2) for part in range(2)], axis=1)
    bt = jnp.stack([b_re[0], b_im[0], b_re[1], b_im[1]], axis=1).transpose(0, 1, 3, 2)
    ct = jnp.stack([c_re[0], c_im[0], c_re[1], c_im[1]], axis=1).transpose(0, 1, 3, 2)
    dsk = jnp.tile(d_skip.reshape(g, 1, ch), (1, 1, t))
    lane = jnp.arange(P_LANES)
    rep = ((lane[None, :] % ch == jnp.arange(ch)[:, None])
           & (lane[None, :] < (t + 1) * ch)).astype(F32)
    pair = lambda *shape: pl.BlockSpec((2,) + shape, lambda a: (a,) + (0,) * len(shape))
    toep, smat, cpow, a_t = pl.pallas_call(
        _s5_params_kernel,
        out_shape=(jax.ShapeDtypeStruct((g, width, width), BF16),
                   jax.ShapeDtypeStruct((g // 2, 2 * width, 4 * LANES), BF16),
                   jax.ShapeDtypeStruct((g // 2, 4 * LANES, 2 * width), BF16),
                   jax.ShapeDtypeStruct((4, g * p), F32)),
        grid=(g // 2,),
        in_specs=[pair(8, p), pair(4, ch, p), pair(4, p, ch), pair(1, width),
                  pl.BlockSpec((ch, P_LANES), lambda a: (0, 0))],
        out_specs=(pair(width, width),
                   pl.BlockSpec((1, 2 * width, 4 * LANES), lambda a: (a, 0, 0)),
                   pl.BlockSpec((1, 4 * LANES, 2 * width), lambda a: (a, 0, 0)),
                   pl.BlockSpec((4, LANES), lambda a: (0, a))),
        compiler_params=_cparams(("arbitrary",)),
        name="s5_params",
    )(lam, bt, ct, dsk, rep)
    return smat, toep, cpow, a_t


def _chunk_states(u_ref, smat, dst_refs, rows):
    tr = min(S5_TR, rows)
    for r0 in range(0, rows, tr):
        lhs = jnp.concatenate([u_ref[0, r0:r0 + tr, :], u_ref[1, r0:r0 + tr, :]], axis=1)
        res = _bdot(lhs, smat)
        for q in range(4):
            dst_refs[q][r0:r0 + tr, :] = res[:, q * LANES:(q + 1) * LANES]


def _scan_chunks(a_ref, c_refs, x_refs, h_refs, halves, kc, kx):
    cfr_ref, cfi_ref, cbr_ref, cbi_ref = c_refs
    xfr_ref, xfi_ref, xbr_ref, xbi_ref = x_refs
    hfr_ref, hfi_ref, hbr_ref, hbi_ref = h_refs
    a_fr, a_fi = a_ref[0:1, :], a_ref[1:2, :]
    a_br, a_bi = a_ref[2:3, :], a_ref[3:4, :]

    def rows(half, k, n_chunks, count=1):
        return pl.ds(pl.multiple_of((half * n_chunks + k) * 8, 8 * count), 8 * count)

    def step(h_re, h_im, a_re, a_im, s_re, s_im):
        return (a_re * h_re - a_im * h_im + s_re, a_re * h_im + a_im * h_re + s_im)

    zero = jnp.zeros((8, LANES), F32)

    def ctx_body(i, carry):
        out = []
        for half in range(halves):
            fr, fi, br, bi = carry[half]
            kf, kb = i, kc - 1 - i
            fr, fi = step(fr, fi, a_fr, a_fi, cfr_ref[rows(half, kf, kc), :],
                          cfi_ref[rows(half, kf, kc), :])
            br, bi = step(br, bi, a_br, a_bi, cbr_ref[rows(half, kb, kc), :],
                          cbi_ref[rows(half, kb, kc), :])
            out.append((fr, fi, br, bi))
        return tuple(out)

    carry = lax.fori_loop(0, kc, ctx_body, ((zero,) * 4,) * halves)

    def x_body(i, carry):
        out = []
        for half in range(halves):
            fr, fi, br, bi = carry[half]
            kf = 2 * i
            kb = kx - 2 - 2 * i
            fr1, fi1 = step(fr, fi, a_fr, a_fi, xfr_ref[rows(half, kf, kx), :],
                            xfi_ref[rows(half, kf, kx), :])
            br1, bi1 = step(br, bi, a_br, a_bi, xbr_ref[rows(half, kb + 1, kx), :],
                            xbi_ref[rows(half, kb + 1, kx), :])
            hfr_ref[rows(half, kf, kx, 2), :] = jnp.concatenate([fr, fr1], 0).astype(hfr_ref.dtype)
            hfi_ref[rows(half, kf, kx, 2), :] = jnp.concatenate([fi, fi1], 0).astype(hfi_ref.dtype)
            hbr_ref[rows(half, kb, kx, 2), :] = jnp.concatenate([br1, br], 0).astype(hbr_ref.dtype)
            hbi_ref[rows(half, kb, kx, 2), :] = jnp.concatenate([bi1, bi], 0).astype(hbi_ref.dtype)
            fr, fi = step(fr1, fi1, a_fr, a_fi, xfr_ref[rows(half, kf + 1, kx), :],
                          xfi_ref[rows(half, kf + 1, kx), :])
            br, bi = step(br1, bi1, a_br, a_bi, xbr_ref[rows(half, kb, kx), :],
                          xbi_ref[rows(half, kb, kx), :])
            out.append((fr, fi, br, bi))
        return tuple(out)

    lax.fori_loop(0, kx // 2, x_body, carry)


def _s5_chunks_kernel(ux_ref, uc_ref, smat_ref, toep_ref, cpow_ref, a_ref, y_ref,
                      *scratch, halves, kc, kx):
    s_x, s_c, h_in = scratch[0:4], scratch[4:8], scratch[8:12]
    rows_x, rows_c = ux_ref.shape[1], uc_ref.shape[1]
    _chunk_states(uc_ref, smat_ref[0], s_c, rows_c)
    _chunk_states(ux_ref, smat_ref[0], s_x, rows_x)
    _scan_chunks(a_ref, s_c, s_x, h_in, halves, kc, kx)
    n = ux_ref.shape[2]
    tr = min(S5_TR, rows_x)
    for r0 in range(0, rows_x, tr):
        hcat = jnp.concatenate([h[r0:r0 + tr, :] for h in h_in], axis=1)
        yst = _bdot(hcat, cpow_ref[0])
        for j in range(2):
            y_ref[j, r0:r0 + tr, :] = (_bdot(ux_ref[j, r0:r0 + tr, :], toep_ref[j])
                                       + yst[:, j * n:(j + 1) * n]).astype(y_ref.dtype)


def _s5_chunks(utx, utc, smat, toep, cpow, a_t, batch):
    g, rows_x, kdim = utx.shape
    rows_c = utc.shape[1]
    kc, kx = rows_c // batch, rows_x // batch
    assert kx % 2 == 0 and rows_x % min(S5_TR, rows_x) == 0
    pair = lambda rows: pl.BlockSpec((2, rows, kdim), lambda a: (a, 0, 0))
    return pl.pallas_call(
        functools.partial(_s5_chunks_kernel, halves=batch // 8, kc=kc, kx=kx),
        out_shape=jax.ShapeDtypeStruct((g, rows_x, kdim), BF16),
        grid=(g // 2,),
        in_specs=[pair(rows_x), pair(rows_c),
                  pl.BlockSpec((1, 2 * kdim, 4 * LANES), lambda a: (a, 0, 0)),
                  pl.BlockSpec((2, kdim, kdim), lambda a: (a, 0, 0)),
                  pl.BlockSpec((1, 4 * LANES, 2 * kdim), lambda a: (a, 0, 0)),
                  pl.BlockSpec((4, LANES), lambda a: (0, a))],
        out_specs=pair(rows_x),
        scratch_shapes=([pltpu.VMEM((rows_x, LANES), F32)] * 4
                        + [pltpu.VMEM((rows_c, LANES), F32)] * 4
                        + [pltpu.VMEM((rows_x, LANES), BF16)] * 4),
        compiler_params=_cparams(("arbitrary",)),
        name="s5_chunks",
    )(utx, utc, smat, toep, cpow, a_t)


def _chunk_rows_to_tokens(y_ref, perm_ref, ys):
    nrow = CHUNKS_PER_TILE * 8
    blocks = []
    for sl in range(N_SLAB):
        for th in range(CHUNK_T // GRANULES):
            blocks.append(jnp.concatenate(
                [y_ref[sl * GRANULES + m, :, th * LANES:(th + 1) * LANES] for m in range(GRANULES)],
                axis=1))
    acat = _bdot(jnp.concatenate(blocks, axis=0), perm_ref[...])
    blk = 0
    for sl in range(N_SLAB):
        for th in range(CHUNK_T // GRANULES):
            for kk in range(CHUNKS_PER_TILE):
                for j in range(GRANULES):
                    t = kk * CHUNK_T + th * GRANULES + j
                    ys[sl, pl.ds(t, 8, stride=PITCH), :] = acat[blk * nrow + kk * 8:
                                                               blk * nrow + (kk + 1) * 8,
                                                               j * LANES:(j + 1) * LANES]
            blk += 1
    return jnp.concatenate(
        [jnp.concatenate([ys[sl, pl.ds(b * PITCH, ROW_T), :] for sl in range(N_SLAB)], axis=1)
         for b in range(8)], axis=0)


def _back_kernel(x_ref, y_ref, mod_ref, perm_ref, gmix_ref, win_ref, convw_ref, convb_ref,
                 wglu_ref, wsbr_ref, wcbr_ref, wo_ref, gffn_ref, wrhi_ref, wrlo_ref, br_ref,
                 ltri_ref, x1_ref, h_ref, ids_ref, gates_ref, hist_ref, ys_scr):
    d_model = x_ref.shape[2]
    tm = 8 * ROW_T
    d_ssm = N_SLAB * LANES
    xt = x_ref[...].reshape(tm, d_model)
    xn = _rmsnorm(xt, gmix_ref[...])
    hx = _modulate_tile(xn, mod_ref, 0, 1).astype(BF16)
    proj = lambda lo, hi: _bdot(hx, win_ref[:, lo:hi])
    ys_tok = _chunk_rows_to_tokens(y_ref, perm_ref, ys_scr)
    v = proj(0, d_ssm)
    ys = jax.nn.gelu(ys_tok)
    glu = _bdot(ys.astype(BF16), wglu_ref[...])
    gate_c = proj(2 * d_ssm, 3 * d_ssm)
    ys = ys * jax.nn.sigmoid(glu)
    y_s = _bdot(ys.astype(BF16), wsbr_ref[...])
    gate_b = proj(d_ssm, 2 * d_ssm)

    z = gate_c * v
    col = lax.broadcasted_iota(jnp.int32, z.shape, 0) % GRID_W
    z_prev = jnp.where(col == 0, 0.0, pltpu.roll(z, 1, 0))
    z_next = jnp.where(col == GRID_W - 1, 0.0, pltpu.roll(z, tm - 1, 0))
    conv = (z_prev * convw_ref[0:1, :] + z * convw_ref[1:2, :]
            + z_next * convw_ref[2:3, :] + convb_ref[...])
    y_conv = gate_b * conv

    g_s = proj(3 * d_ssm, 3 * d_ssm + d_model)
    y_c = _bdot(y_conv.astype(BF16), wcbr_ref[...])
    merged_s = jax.nn.sigmoid(g_s) * y_s
    g_c = proj(3 * d_ssm + d_model, 3 * d_ssm + 2 * d_model)
    merged = merged_s + jax.nn.sigmoid(g_c) * y_c
    mo = _bdot(merged.astype(BF16), wo_ref[...])
    x1 = xt + (mo.reshape(8, ROW_T, d_model) * mod_ref[:, 2:3, :]).reshape(tm, d_model)
    x1_ref[...] = x1.reshape(8, ROW_T, d_model)

    hn = _modulate_tile(_rmsnorm(x1, gffn_ref[...]), mod_ref, 3, 4)
    for b in range(8):
        for s in range(SUBLANES):
            h_ref[b, pl.ds(s, ROW_T, stride=SUBLANES), :] = hn[b * ROW_T:(b + 1) * ROW_T,
                                                              s * LANES:(s + 1) * LANES]

    hn_hi = hn.astype(BF16)
    hn_lo = (hn - hn_hi.astype(F32)).astype(BF16)
    logits = (_bdot(hn_hi, wrhi_ref[...]) + _bdot(hn_lo, wrhi_ref[...])
              + _bdot(hn_hi, wrlo_ref[...]) + br_ref[...])
    lane = lax.broadcasted_iota(jnp.int32, logits.shape, 1)
    neg = jnp.float32(-jnp.inf)
    cur = jnp.where(lane < N_EXPERTS, logits, neg)
    vals, idxs = [], []
    for _ in range(TOP_K):
        mk = jnp.max(cur, axis=-1, keepdims=True)
        ik = jnp.min(jnp.where(cur == mk, lane, LANES), axis=-1, keepdims=True)
        vals.append(mk)
        idxs.append(ik)
        cur = jnp.where(lane == ik, neg, cur)
    exps = [jnp.exp(vk - vals[0]) for vk in vals]
    denom = exps[0] + exps[1] + exps[2] + exps[3]
    ids = jnp.zeros(logits.shape, jnp.int32)
    gates = jnp.zeros(logits.shape, F32)
    onehot = jnp.zeros(logits.shape, F32)
    for k in range(TOP_K):
        onehot = onehot + (lane == idxs[k]).astype(F32)
    before = _bdot(ltri_ref[...], onehot.astype(BF16))
    for k in range(TOP_K):
        rank_k = jnp.sum(jnp.where(lane == idxs[k], before, 0.0), axis=-1, keepdims=True)
        ids = jnp.where(lane == k, idxs[k], ids)
        ids = jnp.where(lane == TOP_K + k, rank_k.astype(jnp.int32), ids)
        gates = jnp.where(lane == k, exps[k] / denom, gates)
    ids_ref[0] = jnp.transpose(ids)[0:2 * TOP_K, :]
    gates_ref[...] = gates.reshape(8, ROW_T, LANES)
    hist_ref[0] = jnp.broadcast_to(jnp.sum(onehot, axis=0, keepdims=True), (8, LANES))


def _back(x, y_chunks, mod, perm, g_mix, w_rest, conv_w, conv_b, w_glu, w_ssm_br,
          w_conv_br, w_o, g_ffn, w_router, b_router):
    b, s, d = x.shape
    g, _, kdim = y_chunks.shape
    tiles = s // ROW_T
    nrow = CHUNKS_PER_TILE * 8
    tok = lambda n: pl.BlockSpec((8, ROW_T, n), lambda i, j: (i, j, 0))

    def const(arr):
        nd = arr.ndim
        return pl.BlockSpec(arr.shape, lambda i, j: (0,) * nd, pipeline_mode=pl.Buffered(1))

    w_r_hi = w_router.astype(BF16)
    w_r_lo = (w_router - w_r_hi.astype(F32)).astype(BF16)
    row = jnp.arange(8 * ROW_T)
    ltri = (row[:, None] > row[None, :]).astype(BF16)
    params = (perm, g_mix, w_rest, conv_w, conv_b, w_glu, w_ssm_br, w_conv_br, w_o, g_ffn,
              w_r_hi, w_r_lo, b_router, ltri)
    return pl.pallas_call(
        _back_kernel,
        out_shape=(jax.ShapeDtypeStruct((b, s, d), F32),
                   jax.ShapeDtypeStruct((b, s * SUBLANES, LANES), F32),
                   jax.ShapeDtypeStruct(((b // 8) * tiles, 2 * TOP_K, 8 * ROW_T), jnp.int32),
                   jax.ShapeDtypeStruct((b, s, LANES), F32),
                   jax.ShapeDtypeStruct(((b // 8) * tiles, 8, LANES), F32)),
        grid=(b // 8, tiles),
        in_specs=[tok(d),
                  pl.BlockSpec((g, nrow, kdim), lambda i, j: (0, i * tiles + j, 0)),
                  pl.BlockSpec((8, 8, d), lambda i, j: (i, 0, 0))] + [const(a) for a in params],
        out_specs=(tok(d), pl.BlockSpec((8, ROW_T * SUBLANES, LANES), lambda i, j: (i, j, 0)),
                   pl.BlockSpec((1, 2 * TOP_K, 8 * ROW_T), lambda i, j: (i * tiles + j, 0, 0)),
                   tok(LANES),
                   pl.BlockSpec((1, 8, LANES), lambda i, j: (i * tiles + j, 0, 0))),
        scratch_shapes=[pltpu.VMEM((N_SLAB, 8 * PITCH, LANES), F32)],
        compiler_params=_cparams(("arbitrary", "arbitrary")),
        name="back",
    )(x, y_chunks, mod, *params)


def _routing(ids_ranks, tile_hist, bsz, seq):
    tb = MOE_TB
    n_tiles, _, tm = ids_ranks.shape
    ids, ranks = ids_ranks[:, :TOP_K, :], ids_ranks[:, TOP_K:, :]
    n_slot = n_tiles * tm * TOP_K
    tile_before = jnp.cumsum(tile_hist, axis=0) - tile_hist
    counts = jnp.sum(tile_hist, axis=0)
    padded = (counts + tb - 1) // tb * tb
    pad_end = jnp.cumsum(padded)
    pad_start = pad_end - padded
    start = jnp.cumsum(counts) - counts
    base = pad_start[None, :] + tile_before
    pos = ranks
    for e in range(N_EXPERTS):
        pos = pos + jnp.where(ids == e, base[:, e][:, None, None], 0)
    pos = pos.astype(jnp.int32)
    _, order_slot = lax.sort_key_val(pos.reshape(-1), jnp.arange(n_slot, dtype=jnp.int32))

    def token_of_slot(slot):
        tiles = seq // ROW_T
        tile, row = slot // (TOP_K * tm), slot % tm
        return ((tile // tiles) * 8 + row // ROW_T) * seq + (tile % tiles) * ROW_T + row % ROW_T
    n_blocks = n_slot // tb + N_EXPERTS
    cap = n_blocks * tb
    block_first_row = jnp.arange(n_blocks, dtype=jnp.int32) * tb
    block_expert = jnp.minimum(
        jnp.sum(pad_end[None, :] <= block_first_row[:, None], axis=1, dtype=jnp.int32),
        N_EXPERTS - 1)
    within = (block_first_row - pad_start[block_expert])[:, None] + jnp.arange(tb, dtype=jnp.int32)
    valid = within < counts[block_expert][:, None]
    src = jnp.clip(start[block_expert][:, None] + within, 0, n_slot - 1)
    buf_token = jnp.where(valid, token_of_slot(order_slot[src]), 0).astype(jnp.int32).reshape(cap)
    n_used = (pad_end[-1] // tb).astype(jnp.int32).reshape(1)
    return buf_token, pos.reshape(n_tiles, TOP_K * tm), block_expert, n_used


def _to_tiles(ref, val):
    rows = val.shape[0]
    for s in range(SUBLANES):
        ref[pl.ds(s, rows, stride=SUBLANES), :] = val[:, s * LANES:(s + 1) * LANES]


def _tile_piece(ref, first_row, rows, s):
    return ref[pl.ds(first_row * SUBLANES + s, rows, stride=SUBLANES), :]


def _tile_gather_start(idx_ref, src_hbm, dst, sem, n_rows, unrolled, alternate_priority=False):
    def copy(r, t):
        return pltpu.make_async_copy(
            src_hbm.at[pl.ds(pl.multiple_of(t * SUBLANES, SUBLANES), SUBLANES), :],
            dst.at[pl.ds(pl.multiple_of(r * SUBLANES, SUBLANES), SUBLANES), :], sem)

    if unrolled:
        for r in range(n_rows):
            copy(r, idx_ref[0, 0, r]).start(priority=(r % 2) if alternate_priority else 0)
    else:
        def body(r, carry):
            copy(r, idx_ref[0, 0, r]).start()
            return carry
        lax.fori_loop(0, n_rows, body, 0, unroll=8)


def _tile_gather_wait(src_hbm, dst, sem):
    pltpu.make_async_copy(src_hbm.at[pl.ds(0, dst.shape[0]), :], dst, sem).wait()


def _ring_step(i, last, bufs, sem, wait, prefetch, compute):
    n = len(bufs)
    for p in range(n):
        def branch(p=p):
            wait(bufs[p], sem.at[p])
            q = (p + n - 1) % n
            prefetch(bufs[q], sem.at[q])
            compute(bufs[p])

            @pl.when(i == last)
            def _():
                for r in range(1, n):
                    wait(bufs[(p + r) % n], sem.at[(p + r) % n])

        pl.when((i <= last) & (lax.rem(i, n) == p))(branch)


def _cast_rows(src_ref, dst_ref, chunk):
    def body(c, carry):
        rows = pl.ds(pl.multiple_of(c * chunk, chunk), chunk)
        dst_ref[rows, :] = src_ref[0, rows, :].astype(dst_ref.dtype)
        return carry
    lax.fori_loop(0, dst_ref.shape[0] // chunk, body, 0)


def _moe_kernel(be_ref, nu_ref, tok0_ref, tok1_ref, tok2_ref, h_hbm, wgu_ref, bgu_ref, wd_ref,
                bd_ref, out_ref, xbuf0, xbuf1, xbuf2, wgu_bf, wd_bf, sem):
    i = pl.program_id(0)
    n_used = nu_ref[0]
    tb = xbuf0.shape[0] // SUBLANES
    f = wd_ref.shape[1]

    @pl.when(i == 0)
    def _():
        _tile_gather_start(tok0_ref, h_hbm, xbuf0, sem.at[0], tb, unrolled=False)
        _tile_gather_start(tok1_ref, h_hbm, xbuf1, sem.at[1], tb, unrolled=False)

    new_expert = (i == 0) | (be_ref[i] != be_ref[jnp.maximum(i - 1, 0)])

    @pl.when(new_expert & (i < n_used))
    def _():
        _cast_rows(wgu_ref, wgu_bf, 128)
        _cast_rows(wd_ref, wd_bf, 128)

    def compute(xcur):
        xe = jnp.concatenate([_tile_piece(xcur, 0, tb, s) for s in range(SUBLANES)],
                             axis=1).astype(BF16)
        gu = _bdot(xe, wgu_bf[...]) + bgu_ref[0]
        gt = jnp.minimum(gu[:, :f], SWIGLU_LIMIT)
        up = jnp.clip(gu[:, f:], -SWIGLU_LIMIT, SWIGLU_LIMIT)
        act = gt * jax.nn.sigmoid(SWIGLU_ALPHA * gt) * (up + 1.0)
        _to_tiles(out_ref, _bdot(act.astype(BF16), wd_bf[...]) + bd_ref[0])

    _ring_step(i, n_used - 1, (xbuf0, xbuf1, xbuf2), sem,
               wait=lambda buf, s: _tile_gather_wait(h_hbm, buf, s),
               prefetch=lambda buf, s: _tile_gather_start(tok2_ref, h_hbm, buf, s, tb,
                                                          unrolled=True),
               compute=compute)

    @pl.when(i >= n_used)
    def _():
        out_ref[...] = jnp.zeros(out_ref.shape, out_ref.dtype)


def _moe(h_tiles, buf_token, block_expert, n_used, w_gu, b_gu, w_down, b_down):
    e, d, f2 = w_gu.shape
    f = f2 // 2
    tb = MOE_TB
    cap = buf_token.shape[0]
    n_blocks = cap // tb
    tok3 = buf_token.reshape(n_blocks, 1, tb)
    smem_blk = lambda imap: pl.BlockSpec((1, 1, tb), imap, memory_space=pltpu.SMEM)
    grid_spec = pltpu.PrefetchScalarGridSpec(
        num_scalar_prefetch=2,
        grid=(n_blocks,),
        in_specs=[smem_blk(lambda i, be, nu: (i, 0, 0)),
                  smem_blk(lambda i, be, nu: (jnp.minimum(i + 1, n_blocks - 1), 0, 0)),
                  smem_blk(lambda i, be, nu: (jnp.minimum(i + 2, n_blocks - 1), 0, 0)),
                  pl.BlockSpec(memory_space=pl.ANY),
                  pl.BlockSpec((1, d, f2), lambda i, be, nu: (be[i], 0, 0)),
                  pl.BlockSpec((1, 1, f2), lambda i, be, nu: (be[i], 0, 0)),
                  pl.BlockSpec((1, f, d), lambda i, be, nu: (be[i], 0, 0)),
                  pl.BlockSpec((1, 1, d), lambda i, be, nu: (be[i], 0, 0))],
        out_specs=pl.BlockSpec((tb * SUBLANES, LANES), lambda i, be, nu: (i, 0)),
        scratch_shapes=[pltpu.VMEM((tb * SUBLANES, LANES), F32)] * 3
        + [pltpu.VMEM((d, f2), BF16), pltpu.VMEM((f, d), BF16), pltpu.SemaphoreType.DMA((3,))],
    )
    return pl.pallas_call(
        _moe_kernel,
        out_shape=jax.ShapeDtypeStruct((cap * SUBLANES, LANES), F32),
        grid_spec=grid_spec,
        compiler_params=_cparams(("arbitrary",)),
        name="moe",
    )(block_expert, n_used, tok3, tok3, tok3, h_tiles, w_gu, b_gu.reshape(e, 1, f2),
      w_down, b_down.reshape(e, 1, d))


def _combine_kernel(pos0_ref, pos1_ref, pos2_ref, yb_hbm, x1_ref, gates_ref, mod_ref, gfin_ref,
                    out_ref, buf0, buf1, buf2, sem):
    i = pl.program_id(0)
    n = pl.num_programs(0)
    n_rows = buf0.shape[0] // SUBLANES
    tm = n_rows // TOP_K
    d = x1_ref.shape[2]

    @pl.when(i == 0)
    def _():
        _tile_gather_start(pos0_ref, yb_hbm, buf0, sem.at[0], n_rows, unrolled=False)
        _tile_gather_start(pos1_ref, yb_hbm, buf1, sem.at[1], n_rows, unrolled=False)

    def compute(cur):
        g = gates_ref[...].reshape(tm, LANES)
        gk = [jnp.broadcast_to(g[:, k:k + 1], (tm, LANES)) for k in range(TOP_K)]
        pieces = []
        for s in range(SUBLANES):
            acc = gk[0] * _tile_piece(cur, 0, tm, s)
            for k in range(1, TOP_K):
                acc = acc + gk[k] * _tile_piece(cur, k * tm, tm, s)
            pieces.append(acc)
        nb = x1_ref.shape[0]
        moe = jnp.concatenate(pieces, axis=1).reshape(nb, ROW_T, d)
        x2 = (x1_ref[...] + mod_ref[:, 5:6, :] * moe).reshape(tm, d)
        out_ref[...] = _rmsnorm(x2, gfin_ref[...]).reshape(nb, ROW_T, d)

    _ring_step(i, n - 1, (buf0, buf1, buf2), sem,
               wait=lambda buf, s: _tile_gather_wait(yb_hbm, buf, s),
               prefetch=lambda buf, s: _tile_gather_start(pos2_ref, yb_hbm, buf, s, n_rows,
                                                          unrolled=True, alternate_priority=True),
               compute=compute)


def _combine(yb, pos, x1, gates, mod, g_final):
    b, s, d = x1.shape
    nb = 4
    tm = nb * ROW_T
    tiles = s // ROW_T
    n_tiles = (b // nb) * tiles
    split = 8 // nb
    pos3 = pos.reshape(-1, TOP_K, split, tm).transpose(0, 2, 1, 3).reshape(n_tiles, 1, TOP_K * tm)
    smem_blk = lambda imap: pl.BlockSpec((1, 1, TOP_K * tm), imap, memory_space=pltpu.SMEM)
    bidx = lambda i: (i // (split * tiles)) * split + i % split
    tok = lambda n: pl.BlockSpec((nb, ROW_T, n), lambda i: (bidx(i), (i // split) % tiles, 0))
    buf = pltpu.VMEM((TOP_K * tm * SUBLANES, LANES), F32)
    return pl.pallas_call(
        _combine_kernel,
        out_shape=jax.ShapeDtypeStruct((b, s, d), F32),
        grid=(n_tiles,),
        in_specs=[smem_blk(lambda i: (i, 0, 0)),
                  smem_blk(lambda i: (jnp.minimum(i + 1, n_tiles - 1), 0, 0)),
                  smem_blk(lambda i: (jnp.minimum(i + 2, n_tiles - 1), 0, 0)),
                  pl.BlockSpec(memory_space=pl.ANY),
                  tok(d), tok(LANES),
                  pl.BlockSpec((nb, 8, d), lambda i: (bidx(i), 0, 0)),
                  pl.BlockSpec((1, d), lambda i: (0, 0))],
        out_specs=tok(d),
        scratch_shapes=[buf, buf, buf, pltpu.SemaphoreType.DMA((3,))],
        compiler_params=_cparams(("arbitrary",)),
        name="combine",
    )(pos3, pos3, pos3, yb, x1, gates, mod, g_final)


def kernel(x, c, ctx, c_ctx, w_mod, b_mod, g_mix, w_in, lam_re, lam_im, log_dt, b_re, b_im,
           c_re, c_im, d_skip, w_glu, conv_w, conv_b, w_ssm_br, w_conv_br, w_o, g_ffn,
           w_router, b_router, w_gu, b_gu, w_down, b_down, g_final):
    depth = w_mod.shape[0]
    assert depth == 1, "single-layer trunk"
    bsz, seq, d = x.shape
    ctx_len = ctx.shape[1]
    d_ssm = d // 2
    assert bsz % 8 == 0 and seq % CHUNK_T == 0 and ctx_len % CHUNK_T == 0 and seq % GRID_W == 0
    assert d == SUBLANES * LANES, "row gathers move one (8,128) f32 tile per token"

    n_cond = -(-(bsz + 1) // 8) * 8
    cond = jnp.zeros((n_cond, d), F32).at[:bsz].set(c).at[bsz].set(c_ctx)
    m = _adaln(cond, w_mod[0], b_mod[0])
    zeros2 = jnp.zeros((n_cond, 2, d), F32)
    mod_all = jnp.concatenate([m.reshape(n_cond, 6, d), zeros2], axis=1)
    mod_x, mod_c = mod_all[:bsz], mod_all[bsz:bsz + 1]

    w_in_bf = w_in[0].astype(BF16)
    w_u, w_rest = w_in_bf[:, :d_ssm], w_in_bf[:, d_ssm:]
    gm = g_mix[0].reshape(1, d)

    perm = _granule_transpose_matrix()
    utx = _front(x, mod_x, gm, w_u, perm)
    utc = _front(ctx, mod_c, gm, w_u, perm)

    smat, toep, cpow, a_t = _s5_matrices(lam_re[0], lam_im[0], log_dt[0], b_re[0], b_im[0],
                                         c_re[0], c_im[0], d_skip[0])
    y_chunks = _s5_chunks(utx, utc, smat, toep, cpow, a_t, bsz)

    pad_r = jnp.zeros((d, LANES - N_EXPERTS), F32)
    w_r = jnp.concatenate([w_router[0], pad_r], axis=1)
    b_r = jnp.concatenate([b_router[0], jnp.zeros((LANES - N_EXPERTS,), F32)]).reshape(1, LANES)
    x1, h, ids, gates, hist = _back(
        x, y_chunks, mod_x, perm, gm, w_rest, conv_w[0], conv_b[0].reshape(1, d_ssm),
        w_glu[0].astype(BF16), w_ssm_br[0].astype(BF16),
        w_conv_br[0].astype(BF16), w_o[0].astype(BF16), g_ffn[0].reshape(1, d), w_r, b_r)

    n_tok = bsz * seq
    buf_token, pos, block_expert, n_used = _routing(
        ids, hist[:, 0, :N_EXPERTS].astype(jnp.int32), bsz, seq)
    yb = _moe(h.reshape(n_tok * SUBLANES, LANES), buf_token, block_expert, n_used,
              w_gu[0], b_gu[0], w_down[0], b_down[0])
    return _combine(yb, pos, x1, gates, mod_x, g_final.reshape(1, d))
```

```python
import functools

import jax
import jax.numpy as jnp
from jax import lax
from jax.experimental import pallas as pl
from jax.experimental.pallas import tpu as pltpu

F32 = jnp.float32
BF16 = jnp.bfloat16
HIGHEST = lax.Precision.HIGHEST

RMS_EPS = 1e-6
GRID_W = 64
SSM_GROUP = 16
SSM_STATE = 64
N_EXPERTS = 32
TOP_K = 4
SWIGLU_LIMIT = 7.0
SWIGLU_ALPHA = 1.702

CHUNK_T = 16
LANES = 128
SUBLANES = 8
V7X_VMEM_LIMIT_BYTES = 56 * 1024 * 1024

S5_TR = 1024
MOE_TB = 512


def _cparams(sem):
    return pltpu.CompilerParams(dimension_semantics=sem,
                                vmem_limit_bytes=V7X_VMEM_LIMIT_BYTES)


def _bdot(a, b):
    return jnp.dot(a, b, preferred_element_type=F32)


def _rmsnorm(xt, g):
    ms = jnp.mean(xt * xt, axis=-1, keepdims=True)
    return xt * lax.rsqrt(ms + RMS_EPS) * g


def _adaln_kernel(c_ref, w_ref, b_ref, o_ref):
    s = jax.nn.silu(c_ref[...])
    o_ref[...] = jnp.dot(s, w_ref[...], precision=HIGHEST,
                         preferred_element_type=F32) + b_ref[...]


def _adaln(cond, w_mod, b_mod):
    r, d = cond.shape
    n = w_mod.shape[1]
    tn = n // 4
    return pl.pallas_call(
        _adaln_kernel,
        out_shape=jax.ShapeDtypeStruct((r, n), F32),
        grid=(n // tn,),
        in_specs=[pl.BlockSpec((r, d), lambda j: (0, 0)),
                  pl.BlockSpec((d, tn), lambda j: (0, j)),
                  pl.BlockSpec((1, tn), lambda j: (0, j))],
        out_specs=pl.BlockSpec((r, tn), lambda j: (0, j)),
        compiler_params=_cparams(("arbitrary",)),
        name="adaln",
    )(cond, w_mod, b_mod.reshape(1, n))


ROW_T = GRID_W
PITCH = ROW_T + 8
N_SLAB = 4
CHUNKS_PER_TILE = ROW_T // CHUNK_T
GRANULES = LANES // SSM_GROUP


def _granule_transpose_matrix():
    n = GRANULES * LANES
    idx = jnp.arange(n)
    j, m, c = idx // LANES, (idx % LANES) // SSM_GROUP, idx % SSM_GROUP
    dst = m * LANES + j * SSM_GROUP + c
    return jnp.zeros((n, n), F32).at[idx, dst].set(1.0).astype(BF16)


def _modulate_tile(xn, mod_ref, shift_row, scale_row):
    d = xn.shape[1]
    x3 = xn.reshape(8, ROW_T, d)
    x3 = x3 * (1.0 + mod_ref[:, scale_row:scale_row + 1, :]) + mod_ref[:, shift_row:shift_row + 1, :]
    return x3.reshape(8 * ROW_T, d)


def _front_kernel(x_ref, mod_ref, g_ref, w_ref, perm_ref, u_ref, us):
    d = x_ref.shape[2]
    xn = _rmsnorm(x_ref[...].reshape(8 * ROW_T, d), g_ref[...])
    hx = _modulate_tile(xn, mod_ref, 0, 1)
    u = _bdot(hx.astype(BF16), w_ref[...])
    for sl in range(N_SLAB):
        for b in range(8):
            us[sl, pl.ds(b * PITCH, ROW_T), :] = u[b * ROW_T:(b + 1) * ROW_T,
                                                  sl * LANES:(sl + 1) * LANES]
    blocks = []
    for sl in range(N_SLAB):
        for th in range(CHUNK_T // GRANULES):
            rows = []
            for kk in range(CHUNKS_PER_TILE):
                t0 = kk * CHUNK_T + th * GRANULES
                rows.append(jnp.concatenate(
                    [us[sl, pl.ds(t0 + j, 8, stride=PITCH), :] for j in range(GRANULES)], axis=1))
            blocks.append(jnp.concatenate(rows, axis=0))
    acat = jnp.concatenate(blocks, axis=0).astype(BF16)
    bmat = _bdot(acat, perm_ref[...])
    nrow = CHUNKS_PER_TILE * 8
    blk = 0
    for sl in range(N_SLAB):
        for th in range(CHUNK_T // GRANULES):
            for m in range(GRANULES):
                u_ref[sl * GRANULES + m, :, th * LANES:(th + 1) * LANES] = (
                    bmat[blk * nrow:(blk + 1) * nrow, m * LANES:(m + 1) * LANES].astype(u_ref.dtype))
            blk += 1


def _front(x, mod, g_mix, w_u, perm):
    b, s, d = x.shape
    n = w_u.shape[1]
    g = n // SSM_GROUP
    assert s % ROW_T == 0 and b % 8 == 0 and n == N_SLAB * LANES
    k = s // CHUNK_T
    tiles = s // ROW_T
    nrow = CHUNKS_PER_TILE * 8
    shared = mod.shape[0] == 1
    mod_spec = (pl.BlockSpec((1, 8, d), lambda i, j: (0, 0, 0)) if shared
                else pl.BlockSpec((8, 8, d), lambda i, j: (i, 0, 0)))
    return pl.pallas_call(
        _front_kernel,
        out_shape=jax.ShapeDtypeStruct((g, (b // 8) * k * 8, CHUNK_T * SSM_GROUP), BF16),
        grid=(b // 8, tiles),
        in_specs=[pl.BlockSpec((8, ROW_T, d), lambda i, j: (i, j, 0)),
                  mod_spec,
                  pl.BlockSpec((1, d), lambda i, j: (0, 0)),
                  pl.BlockSpec((d, n), lambda i, j: (0, 0)),
                  pl.BlockSpec(perm.shape, lambda i, j: (0, 0))],
        out_specs=pl.BlockSpec((g, nrow, CHUNK_T * SSM_GROUP), lambda i, j: (0, i * tiles + j, 0)),
        scratch_shapes=[pltpu.VMEM((N_SLAB, 8 * PITCH, LANES), F32)],
        compiler_params=_cparams(("arbitrary", "arbitrary")),
        name="front",
    )(x, mod, g_mix, w_u, perm)


P_LANES = 3 * LANES


def _s5_params_kernel(lam_ref, bt_ref, ct_ref, dsk_ref, rep_ref,
                      toep_ref, smat_ref, cpow_ref, at_ref):
    t, ch, p = CHUNK_T, SSM_GROUP, SSM_STATE
    width = t * ch
    f32dot = lambda a, b: jnp.dot(a, b, precision=HIGHEST, preferred_element_type=F32)
    eye = (lax.broadcasted_iota(jnp.int32, (p, p), 0) == lax.broadcasted_iota(jnp.int32, (p, p), 1))
    to_col = lambda row: jnp.sum(jnp.where(eye, jnp.broadcast_to(row, (p, p)), 0.0),
                                 axis=1, keepdims=True)
    blk = lax.broadcasted_iota(jnp.int32, (p, P_LANES), 1) // ch
    row16 = lax.broadcasted_iota(jnp.int32, (ch, width), 0)
    lane16 = lax.broadcasted_iota(jnp.int32, (ch, width), 1)

    smats, cpows, ats = [[], []], [[], []], []
    for j in range(2):
        kt, at_dir = [], []
        for z in range(2):
            lr = lam_ref[j, 4 * z:4 * z + 1, :]
            li = lam_ref[j, 4 * z + 1:4 * z + 2, :]
            dt = jnp.exp(lam_ref[j, 4 * z + 2:4 * z + 3, :])
            mag = jnp.exp(lr * dt)
            ar, ai = mag * jnp.cos(li * dt), mag * jnp.sin(li * dt)
            den = lr * lr + li * li
            qr = ((ar - 1.0) * lr + ai * li) / den
            qi = (ai * lr - (ar - 1.0) * li) / den
            bbr = qr * bt_ref[j, 2 * z] - qi * bt_ref[j, 2 * z + 1]
            bbi = qr * bt_ref[j, 2 * z + 1] + qi * bt_ref[j, 2 * z]
            pw = [(jnp.ones_like(ar), jnp.zeros_like(ai))]
            for _ in range(t):
                r, i = pw[-1]
                pw.append((r * ar - i * ai, r * ai + i * ar))
            at_dir.append(pw[t])

            expo = jnp.where(blk <= t, blk if z == 0 else t - blk, 0)
            tr = jnp.ones((p, P_LANES), F32)
            ti = jnp.zeros((p, P_LANES), F32)
            for bit in range(5):
                sr, si = to_col(pw[1 << bit][0]), to_col(pw[1 << bit][1])
                on = ((expo >> bit) & 1) == 1
                fr, fi = jnp.where(on, sr, 1.0), jnp.where(on, si, 0.0)
                tr, ti = tr * fr - ti * fi, tr * fi + ti * fr
            ctr = f32dot(ct_ref[j, 2 * z], rep_ref[...])
            cti = f32dot(ct_ref[j, 2 * z + 1], rep_ref[...])
            crd = ctr * tr - cti * ti
            cid = ctr * ti + cti * tr
            kt.append(f32dot(bbr, crd) - f32dot(bbi, cid))

            off = ch if z == 0 else 0
            cpows[j].append(crd[:, off:off + width])
            cpows[j].append(-cid[:, off:off + width])

            pows = [pw[t - 1 - s] if z == 0 else pw[s] for s in range(t)]
            pr_rows = jnp.concatenate([jnp.broadcast_to(q[0], (ch, p)) for q in pows], axis=0)
            pi_rows = jnp.concatenate([jnp.broadcast_to(q[1], (ch, p)) for q in pows], axis=0)
            br_rows = jnp.concatenate([bbr] * t, axis=0)
            bi_rows = jnp.concatenate([bbi] * t, axis=0)
            smats[j].append(pr_rows * br_rows - pi_rows * bi_rows)
            smats[j].append(pr_rows * bi_rows + pi_rows * br_rows)
        ats.append(at_dir)

        ktf = kt[0][:, :width]
        ktb = kt[1][:, ch:ch + width]
        rows = []
        for s in range(t):
            fwd = ktf if s == 0 else jnp.concatenate(
                [jnp.zeros((ch, ch * s), F32), ktf[:, :width - ch * s]], axis=1)
            sh = ch * (t - 1 - s)
            bwd = ktb if sh == 0 else jnp.concatenate(
                [ktb[:, sh:], jnp.zeros((ch, sh), F32)], axis=1)
            skip = jnp.where(lane16 == ch * s + row16, dsk_ref[j], 0.0)
            rows.append(fwd + bwd + skip)
        toep_ref[j] = jnp.concatenate(rows, axis=0).astype(toep_ref.dtype)

    zs = jnp.zeros((width, p), F32)
    smat_ref[0] = jnp.concatenate(
        [jnp.concatenate([jnp.concatenate([smats[0][q], zs], axis=1) for q in range(4)], axis=1),
         jnp.concatenate([jnp.concatenate([zs, smats[1][q]], axis=1) for q in range(4)], axis=1)],
        axis=0).astype(smat_ref.dtype)
    zc = jnp.zeros((p, width), F32)
    cpow_ref[0] = jnp.concatenate(
        [jnp.concatenate([cpows[j][q], zc] if j == 0 else [zc, cpows[j][q]], axis=1)
         for q in range(4) for j in range(2)], axis=0).astype(cpow_ref.dtype)
    at_ref[...] = jnp.concatenate(
        [jnp.concatenate([ats[0][z][part], ats[1][z][part]], axis=1)
         for z in range(2) for part in range(2)], axis=0)


def _s5_matrices(lam_re, lam_im, log_dt, b_re, b_im, c_re, c_im, d_skip):
    t, ch = CHUNK_T, SSM_GROUP
    _, g, p = lam_re.shape
    assert p == SSM_STATE and b_re.shape[-1] == ch and 2 * p == LANES
    width = t * ch
    zero = jnp.zeros((g, 1, p), F32)
    lam = jnp.concatenate(
        [jnp.stack([lam_re[z], lam_im[z], jnp.broadcast_to(log_dt[z][:, None], (g, p))], axis=1)
         if part == 0 else zero for z in range(2) for part in range(2)], axis=1)
    bt = jnp.stack([b_re[0], b_im[0], b_re[1], b_im[1]], axis=1).transpose(0, 1, 3, 2)
    ct = jnp.stack([c_re[0], c_im[0], c_re[1], c_im[1]], axis=1).transpose(0, 1, 3, 2)
    dsk = jnp.tile(d_skip.reshape(g, 1, ch), (1, 1, t))
    lane = jnp.arange(P_LANES)
    rep = ((lane[None, :] % ch == jnp.arange(ch)[:, None])
           & (lane[None, :] < (t + 1) * ch)).astype(F32)
    pair = lambda *shape: pl.BlockSpec((2,) + shape, lambda a: (a,) + (0,) * len(shape))
    toep, smat, cpow, a_t = pl.pallas_call(
        _s5_params_kernel,
        out_shape=(jax.ShapeDtypeStruct((g, width, width), BF16),
                   jax.ShapeDtypeStruct((g // 2, 2 * width, 4 * LANES), BF16),
                   jax.ShapeDtypeStruct((g // 2, 4 * LANES, 2 * width), BF16),
                   jax.ShapeDtypeStruct((4, g * p), F32)),
        grid=(g // 2,),
        in_specs=[pair(8, p), pair(4, ch, p), pair(4, p, ch), pair(1, width),
                  pl.BlockSpec((ch, P_LANES), lambda a: (0, 0))],
        out_specs=(pair(width, width),
                   pl.BlockSpec((1, 2 * width, 4 * LANES), lambda a: (a, 0, 0)),
                   pl.BlockSpec((1, 4 * LANES, 2 * width), lambda a: (a, 0, 0)),
                   pl.BlockSpec((4, LANES), lambda a: (0, a))),
        compiler_params=_cparams(("arbitrary",)),
        name="s5_params",
    )(lam, bt, ct, dsk, rep)
    return smat, toep, cpow, a_t


def _chunk_states(u_ref, smat, dst_refs, rows):
    tr = min(S5_TR, rows)
    for r0 in range(0, rows, tr):
        lhs = jnp.concatenate([u_ref[0, r0:r0 + tr, :], u_ref[1, r0:r0 + tr, :]], axis=1)
        res = _bdot(lhs, smat)
        for q in range(4):
            dst_refs[q][r0:r0 + tr, :] = res[:, q * LANES:(q + 1) * LANES]


def _scan_chunks(a_ref, c_refs, x_refs, h_refs, halves, kc, kx):
    cfr_ref, cfi_ref, cbr_ref, cbi_ref = c_refs
    xfr_ref, xfi_ref, xbr_ref, xbi_ref = x_refs
    hfr_ref, hfi_ref, hbr_ref, hbi_ref = h_refs
    a_fr, a_fi = a_ref[0:1, :], a_ref[1:2, :]
    a_br, a_bi = a_ref[2:3, :], a_ref[3:4, :]

    def rows(half, k, n_chunks, count=1):
        return pl.ds(pl.multiple_of((half * n_chunks + k) * 8, 8 * count), 8 * count)

    def step(h_re, h_im, a_re, a_im, s_re, s_im):
        return (a_re * h_re - a_im * h_im + s_re, a_re * h_im + a_im * h_re + s_im)

    zero = jnp.zeros((8, LANES), F32)

    def ctx_body(i, carry):
        out = []
        for half in range(halves):
            fr, fi, br, bi = carry[half]
            kf, kb = i, kc - 1 - i
            fr, fi = step(fr, fi, a_fr, a_fi, cfr_ref[rows(half, kf, kc), :],
                          cfi_ref[rows(half, kf, kc), :])
            br, bi = step(br, bi, a_br, a_bi, cbr_ref[rows(half, kb, kc), :],
                          cbi_ref[rows(half, kb, kc), :])
            out.append((fr, fi, br, bi))
        return tuple(out)

    carry = lax.fori_loop(0, kc, ctx_body, ((zero,) * 4,) * halves)

    def x_body(i, carry):
        out = []
        for half in range(halves):
            fr, fi, br, bi = carry[half]
            kf = 2 * i
            kb = kx - 2 - 2 * i
            fr1, fi1 = step(fr, fi, a_fr, a_fi, xfr_ref[rows(half, kf, kx), :],
                            xfi_ref[rows(half, kf, kx), :])
            br1, bi1 = step(br, bi, a_br, a_bi, xbr_ref[rows(half, kb + 1, kx), :],
                            xbi_ref[rows(half, kb + 1, kx), :])
            hfr_ref[rows(half, kf, kx, 2), :] = jnp.concatenate([fr, fr1], 0).astype(hfr_ref.dtype)
            hfi_ref[rows(half, kf, kx, 2), :] = jnp.concatenate([fi, fi1], 0).astype(hfi_ref.dtype)
            hbr_ref[rows(half, kb, kx, 2), :] = jnp.concatenate([br1, br], 0).astype(hbr_ref.dtype)
            hbi_ref[rows(half, kb, kx, 2), :] = jnp.concatenate([bi1, bi], 0).astype(hbi_ref.dtype)
            fr, fi = step(fr1, fi1, a_fr, a_fi, xfr_ref[rows(half, kf + 1, kx), :],
                          xfi_ref[rows(half, kf + 1, kx), :])
            br, bi = step(br1, bi1, a_br, a_bi, xbr_ref[rows(half, kb, kx), :],
                          xbi_ref[rows(half, kb, kx), :])
            out.append((fr, fi, br, bi))
        return tuple(out)

    lax.fori_loop(0, kx // 2, x_body, carry)


def _s5_chunks_kernel(ux_ref, uc_ref, smat_ref, toep_ref, cpow_ref, a_ref, y_ref,
                      *scratch, halves, kc, kx):
    s_x, s_c, h_in = scratch[0:4], scratch[4:8], scratch[8:12]
    rows_x, rows_c = ux_ref.shape[1], uc_ref.shape[1]
    _chunk_states(uc_ref, smat_ref[0], s_c, rows_c)
    _chunk_states(ux_ref, smat_ref[0], s_x, rows_x)
    _scan_chunks(a_ref, s_c, s_x, h_in, halves, kc, kx)
    n = ux_ref.shape[2]
    tr = min(S5_TR, rows_x)
    for r0 in range(0, rows_x, tr):
        hcat = jnp.concatenate([h[r0:r0 + tr, :] for h in h_in], axis=1)
        yst = _bdot(hcat, cpow_ref[0])
        for j in range(2):
            y_ref[j, r0:r0 + tr, :] = (_bdot(ux_ref[j, r0:r0 + tr, :], toep_ref[j])
                                       + yst[:, j * n:(j + 1) * n]).astype(y_ref.dtype)


def _s5_chunks(utx, utc, smat, toep, cpow, a_t, batch):
    g, rows_x, kdim = utx.shape
    rows_c = utc.shape[1]
    kc, kx = rows_c // batch, rows_x // batch
    assert kx % 2 == 0 and rows_x % min(S5_TR, rows_x) == 0
    pair = lambda rows: pl.BlockSpec((2, rows, kdim), lambda a: (a, 0, 0))
    return pl.pallas_call(
        functools.partial(_s5_chunks_kernel, halves=batch // 8, kc=kc, kx=kx),
        out_shape=jax.ShapeDtypeStruct((g, rows_x, kdim), BF16),
        grid=(g // 2,),
        in_specs=[pair(rows_x), pair(rows_c),
                  pl.BlockSpec((1, 2 * kdim, 4 * LANES), lambda a: (a, 0, 0)),
                  pl.BlockSpec((2, kdim, kdim), lambda a: (a, 0, 0)),
                  pl.BlockSpec((1, 4 * LANES, 2 * kdim), lambda a: (a, 0, 0)),
                  pl.BlockSpec((4, LANES), lambda a: (0, a))],
        out_specs=pair(rows_x),
        scratch_shapes=([pltpu.VMEM((rows_x, LANES), F32)] * 4
                        + [pltpu.VMEM((rows_c, LANES), F32)] * 4
                        + [pltpu.VMEM((rows_x, LANES), BF16)] * 4),
        compiler_params=_cparams(("arbitrary",)),
        name="s5_chunks",
    )(utx, utc, smat, toep, cpow, a_t)


def _chunk_rows_to_tokens(y_ref, perm_ref, ys):
    nrow = CHUNKS_PER_TILE * 8
    blocks = []
    for sl in range(N_SLAB):
        for th in range(CHUNK_T // GRANULES):
            blocks.append(jnp.concatenate(
                [y_ref[sl * GRANULES + m, :, th * LANES:(th + 1) * LANES] for m in range(GRANULES)],
                axis=1))
    acat = _bdot(jnp.concatenate(blocks, axis=0), perm_ref[...])
    blk = 0
    for sl in range(N_SLAB):
        for th in range(CHUNK_T // GRANULES):
            for kk in range(CHUNKS_PER_TILE):
                for j in range(GRANULES):
                    t = kk * CHUNK_T + th * GRANULES + j
                    ys[sl, pl.ds(t, 8, stride=PITCH), :] = acat[blk * nrow + kk * 8:
                                                               blk * nrow + (kk + 1) * 8,
                                                               j * LANES:(j + 1) * LANES]
            blk += 1
    return jnp.concatenate(
        [jnp.concatenate([ys[sl, pl.ds(b * PITCH, ROW_T), :] for sl in range(N_SLAB)], axis=1)
         for b in range(8)], axis=0)


def _back_kernel(x_ref, y_ref, mod_ref, perm_ref, gmix_ref, win_ref, convw_ref, convb_ref,
                 wglu_ref, wsbr_ref, wcbr_ref, wo_ref, gffn_ref, wrhi_ref, wrlo_ref, br_ref,
                 ltri_ref, x1_ref, h_ref, ids_ref, gates_ref, hist_ref, ys_scr):
    d_model = x_ref.shape[2]
    tm = 8 * ROW_T
    d_ssm = N_SLAB * LANES
    xt = x_ref[...].reshape(tm, d_model)
    xn = _rmsnorm(xt, gmix_ref[...])
    hx = _modulate_tile(xn, mod_ref, 0, 1).astype(BF16)
    proj = lambda lo, hi: _bdot(hx, win_ref[:, lo:hi])
    ys_tok = _chunk_rows_to_tokens(y_ref, perm_ref, ys_scr)
    v = proj(0, d_ssm)
    ys = jax.nn.gelu(ys_tok)
    glu = _bdot(ys.astype(BF16), wglu_ref[...])
    gate_c = proj(2 * d_ssm, 3 * d_ssm)
    ys = ys * jax.nn.sigmoid(glu)
    y_s = _bdot(ys.astype(BF16), wsbr_ref[...])
    gate_b = proj(d_ssm, 2 * d_ssm)

    z = gate_c * v
    col = lax.broadcasted_iota(jnp.int32, z.shape, 0) % GRID_W
    z_prev = jnp.where(col == 0, 0.0, pltpu.roll(z, 1, 0))
    z_next = jnp.where(col == GRID_W - 1, 0.0, pltpu.roll(z, tm - 1, 0))
    conv = (z_prev * convw_ref[0:1, :] + z * convw_ref[1:2, :]
            + z_next * convw_ref[2:3, :] + convb_ref[...])
    y_conv = gate_b * conv

    g_s = proj(3 * d_ssm, 3 * d_ssm + d_model)
    y_c = _bdot(y_conv.astype(BF16), wcbr_ref[...])
    merged_s = jax.nn.sigmoid(g_s) * y_s
    g_c = proj(3 * d_ssm + d_model, 3 * d_ssm + 2 * d_model)
    merged = merged_s + jax.nn.sigmoid(g_c) * y_c
    mo = _bdot(merged.astype(BF16), wo_ref[...])
    x1 = xt + (mo.reshape(8, ROW_T, d_model) * mod_ref[:, 2:3, :]).reshape(tm, d_model)
    x1_ref[...] = x1.reshape(8, ROW_T, d_model)

    hn = _modulate_tile(_rmsnorm(x1, gffn_ref[...]), mod_ref, 3, 4)
    for b in range(8):
        for s in range(SUBLANES):
            h_ref[b, pl.ds(s, ROW_T, stride=SUBLANES), :] = hn[b * ROW_T:(b + 1) * ROW_T,
                                                              s * LANES:(s + 1) * LANES]

    hn_hi = hn.astype(BF16)
    hn_lo = (hn - hn_hi.astype(F32)).astype(BF16)
    logits = (_bdot(hn_hi, wrhi_ref[...]) + _bdot(hn_lo, wrhi_ref[...])
              + _bdot(hn_hi, wrlo_ref[...]) + br_ref[...])
    lane = lax.broadcasted_iota(jnp.int32, logits.shape, 1)
    neg = jnp.float32(-jnp.inf)
    cur = jnp.where(lane < N_EXPERTS, logits, neg)
    vals, idxs = [], []
    for _ in range(TOP_K):
        mk = jnp.max(cur, axis=-1, keepdims=True)
        ik = jnp.min(jnp.where(cur == mk, lane, LANES), axis=-1, keepdims=True)
        vals.append(mk)
        idxs.append(ik)
        cur = jnp.where(lane == ik, neg, cur)
    exps = [jnp.exp(vk - vals[0]) for vk in vals]
    denom = exps[0] + exps[1] + exps[2] + exps[3]
    ids = jnp.zeros(logits.shape, jnp.int32)
    gates = jnp.zeros(logits.shape, F32)
    onehot = jnp.zeros(logits.shape, F32)
    for k in range(TOP_K):
        onehot = onehot + (lane == idxs[k]).astype(F32)
    before = _bdot(ltri_ref[...], onehot.astype(BF16))
    for k in range(TOP_K):
        rank_k = jnp.sum(jnp.where(lane == idxs[k], before, 0.0), axis=-1, keepdims=True)
        ids = jnp.where(lane == k, idxs[k], ids)
        ids = jnp.where(lane == TOP_K + k, rank_k.astype(jnp.int32), ids)
        gates = jnp.where(lane == k, exps[k] / denom, gates)
    ids_ref[0] = jnp.transpose(ids)[0:2 * TOP_K, :]
    gates_ref[...] = gates.reshape(8, ROW_T, LANES)
    hist_ref[0] = jnp.broadcast_to(jnp.sum(onehot, axis=0, keepdims=True), (8, LANES))


def _back(x, y_chunks, mod, perm, g_mix, w_rest, conv_w, conv_b, w_glu, w_ssm_br,
          w_conv_br, w_o, g_ffn, w_router, b_router):
    b, s, d = x.shape
    g, _, kdim = y_chunks.shape
    tiles = s // ROW_T
    nrow = CHUNKS_PER_TILE * 8
    tok = lambda n: pl.BlockSpec((8, ROW_T, n), lambda i, j: (i, j, 0))

    def const(arr):
        nd = arr.ndim
        return pl.BlockSpec(arr.shape, lambda i, j: (0,) * nd, pipeline_mode=pl.Buffered(1))

    w_r_hi = w_router.astype(BF16)
    w_r_lo = (w_router - w_r_hi.astype(F32)).astype(BF16)
    row = jnp.arange(8 * ROW_T)
    ltri = (row[:, None] > row[None, :]).astype(BF16)
    params = (perm, g_mix, w_rest, conv_w, conv_b, w_glu, w_ssm_br, w_conv_br, w_o, g_ffn,
              w_r_hi, w_r_lo, b_router, ltri)
    return pl.pallas_call(
        _back_kernel,
        out_shape=(jax.ShapeDtypeStruct((b, s, d), F32),
                   jax.ShapeDtypeStruct((b, s * SUBLANES, LANES), F32),
                   jax.ShapeDtypeStruct(((b // 8) * tiles, 2 * TOP_K, 8 * ROW_T), jnp.int32),
                   jax.ShapeDtypeStruct((b, s, LANES), F32),
                   jax.ShapeDtypeStruct(((b // 8) * tiles, 8, LANES), F32)),
        grid=(b // 8, tiles),
        in_specs=[tok(d),
                  pl.BlockSpec((g, nrow, kdim), lambda i, j: (0, i * tiles + j, 0)),
                  pl.BlockSpec((8, 8, d), lambda i, j: (i, 0, 0))] + [const(a) for a in params],
        out_specs=(tok(d), pl.BlockSpec((8, ROW_T * SUBLANES, LANES), lambda i, j: (i, j, 0)),
                   pl.BlockSpec((1, 2 * TOP_K, 8 * ROW_T), lambda i, j: (i * tiles + j, 0, 0)),
                   tok(LANES),
                   pl.BlockSpec((1, 8, LANES), lambda i, j: (i * tiles + j, 0, 0))),
        scratch_shapes=[pltpu.VMEM((N_SLAB, 8 * PITCH, LANES), F32)],
        compiler_params=_cparams(("arbitrary", "arbitrary")),
        name="back",
    )(x, y_chunks, mod, *params)


def _routing(ids_ranks, tile_hist, bsz, seq):
    tb = MOE_TB
    n_tiles, _, tm = ids_ranks.shape
    ids, ranks = ids_ranks[:, :TOP_K, :], ids_ranks[:, TOP_K:, :]
    n_slot = n_tiles * tm * TOP_K
    tile_before = jnp.cumsum(tile_hist, axis=0) - tile_hist
    counts = jnp.sum(tile_hist, axis=0)
    padded = (counts + tb - 1) // tb * tb
    pad_end = jnp.cumsum(padded)
    pad_start = pad_end - padded
    start = jnp.cumsum(counts) - counts
    base = pad_start[None, :] + tile_before
    pos = ranks
    for e in range(N_EXPERTS):
        pos = pos + jnp.where(ids == e, base[:, e][:, None, None], 0)
    pos = pos.astype(jnp.int32)
    _, order_slot = lax.sort_key_val(pos.reshape(-1), jnp.arange(n_slot, dtype=jnp.int32))

    def token_of_slot(slot):
        tiles = seq // ROW_T
        tile, row = slot // (TOP_K * tm), slot % tm
        return ((tile // tiles) * 8 + row // ROW_T) * seq + (tile % tiles) * ROW_T + row % ROW_T
    n_blocks = n_slot // tb + N_EXPERTS
    cap = n_blocks * tb
    block_first_row = jnp.arange(n_blocks, dtype=jnp.int32) * tb
    block_expert = jnp.minimum(
        jnp.sum(pad_end[None, :] <= block_first_row[:, None], axis=1, dtype=jnp.int32),
        N_EXPERTS - 1)
    within = (block_first_row - pad_start[block_expert])[:, None] + jnp.arange(tb, dtype=jnp.int32)
    valid = within < counts[block_expert][:, None]
    src = jnp.clip(start[block_expert][:, None] + within, 0, n_slot - 1)
    buf_token = jnp.where(valid, token_of_slot(order_slot[src]), 0).astype(jnp.int32).reshape(cap)
    n_used = (pad_end[-1] // tb).astype(jnp.int32).reshape(1)
    return buf_token, pos.reshape(n_tiles, TOP_K * tm), block_expert, n_used


def _to_tiles(ref, val):
    rows = val.shape[0]
    for s in range(SUBLANES):
        ref[pl.ds(s, rows, stride=SUBLANES), :] = val[:, s * LANES:(s + 1) * LANES]


def _tile_piece(ref, first_row, rows, s):
    return ref[pl.ds(first_row * SUBLANES + s, rows, stride=SUBLANES), :]


def _tile_gather_start(idx_ref, src_hbm, dst, sem, n_rows, unrolled, alternate_priority=False,
                       priority=0):
    def copy(r, t):
        return pltpu.make_async_copy(
            src_hbm.at[pl.ds(pl.multiple_of(t * SUBLANES, SUBLANES), SUBLANES), :],
            dst.at[pl.ds(pl.multiple_of(r * SUBLANES, SUBLANES), SUBLANES), :], sem)

    if unrolled:
        for r in range(n_rows):
            copy(r, idx_ref[0, 0, r]).start(priority=(r % 2) if alternate_priority else priority)
    else:
        def body(r, carry):
            copy(r, idx_ref[0, 0, r]).start()
            return carry
        lax.fori_loop(0, n_rows, body, 0, unroll=8)


def _tile_gather_wait(src_hbm, dst, sem):
    pltpu.make_async_copy(src_hbm.at[pl.ds(0, dst.shape[0]), :], dst, sem).wait()


def _ring_step(i, last, bufs, sem, wait, prefetch, compute):
    n = len(bufs)
    for p in range(n):
        def branch(p=p):
            wait(bufs[p], sem.at[p])
            q = (p + n - 1) % n
            prefetch(bufs[q], sem.at[q])
            compute(bufs[p])

            @pl.when(i == last)
            def _():
                for r in range(1, n):
                    wait(bufs[(p + r) % n], sem.at[(p + r) % n])

        pl.when((i <= last) & (lax.rem(i, n) == p))(branch)


def _cast_rows(src_ref, dst_ref, chunk):
    def body(c, carry):
        rows = pl.ds(pl.multiple_of(c * chunk, chunk), chunk)
        dst_ref[rows, :] = src_ref[0, rows, :].astype(dst_ref.dtype)
        return carry
    lax.fori_loop(0, dst_ref.shape[0] // chunk, body, 0)


def _moe_kernel(be_ref, nu_ref, tok0_ref, tok1_ref, tok2_ref, h_hbm, wgu_ref, bgu_ref, wd_ref,
                bd_ref, out_ref, xbuf0, xbuf1, xbuf2, wgu_bf, wd_bf, sem):
    i = pl.program_id(0)
    n_used = nu_ref[0]
    tb = xbuf0.shape[0] // SUBLANES
    f = wd_ref.shape[1]

    @pl.when(i == 0)
    def _():
        _tile_gather_start(tok0_ref, h_hbm, xbuf0, sem.at[0], tb, unrolled=False)
        _tile_gather_start(tok1_ref, h_hbm, xbuf1, sem.at[1], tb, unrolled=False)

    new_expert = (i == 0) | (be_ref[i] != be_ref[jnp.maximum(i - 1, 0)])

    @pl.when(new_expert & (i < n_used))
    def _():
        _cast_rows(wgu_ref, wgu_bf, 128)
        _cast_rows(wd_ref, wd_bf, 128)

    def compute(xcur):
        xe = jnp.concatenate([_tile_piece(xcur, 0, tb, s) for s in range(SUBLANES)],
                             axis=1).astype(BF16)
        gu = _bdot(xe, wgu_bf[...]) + bgu_ref[0]
        gt = jnp.minimum(gu[:, :f], SWIGLU_LIMIT)
        up = jnp.clip(gu[:, f:], -SWIGLU_LIMIT, SWIGLU_LIMIT)
        act = gt * jax.nn.sigmoid(SWIGLU_ALPHA * gt) * (up + 1.0)
        _to_tiles(out_ref, _bdot(act.astype(BF16), wd_bf[...]) + bd_ref[0])

    _ring_step(i, n_used - 1, (xbuf0, xbuf1, xbuf2), sem,
               wait=lambda buf, s: _tile_gather_wait(h_hbm, buf, s),
               prefetch=lambda buf, s: _tile_gather_start(tok2_ref, h_hbm, buf, s, tb,
                                                          unrolled=True, priority=1),
               compute=compute)

    @pl.when(i >= n_used)
    def _():
        out_ref[...] = jnp.zeros(out_ref.shape, out_ref.dtype)


def _moe(h_tiles, buf_token, block_expert, n_used, w_gu, b_gu, w_down, b_down):
    e, d, f2 = w_gu.shape
    f = f2 // 2
    tb = MOE_TB
    cap = buf_token.shape[0]
    n_blocks = cap // tb
    tok3 = buf_token.reshape(n_blocks, 1, tb)
    smem_blk = lambda imap: pl.BlockSpec((1, 1, tb), imap, memory_space=pltpu.SMEM)
    grid_spec = pltpu.PrefetchScalarGridSpec(
        num_scalar_prefetch=2,
        grid=(n_blocks,),
        in_specs=[smem_blk(lambda i, be, nu: (i, 0, 0)),
                  smem_blk(lambda i, be, nu: (jnp.minimum(i + 1, n_blocks - 1), 0, 0)),
                  smem_blk(lambda i, be, nu: (jnp.minimum(i + 2, n_blocks - 1), 0, 0)),
                  pl.BlockSpec(memory_space=pl.ANY),
                  pl.BlockSpec((1, d, f2), lambda i, be, nu: (be[i], 0, 0)),
                  pl.BlockSpec((1, 1, f2), lambda i, be, nu: (be[i], 0, 0)),
                  pl.BlockSpec((1, f, d), lambda i, be, nu: (be[i], 0, 0)),
                  pl.BlockSpec((1, 1, d), lambda i, be, nu: (be[i], 0, 0))],
        out_specs=pl.BlockSpec((tb * SUBLANES, LANES), lambda i, be, nu: (i, 0)),
        scratch_shapes=[pltpu.VMEM((tb * SUBLANES, LANES), F32)] * 3
        + [pltpu.VMEM((d, f2), BF16), pltpu.VMEM((f, d), BF16), pltpu.SemaphoreType.DMA((3,))],
    )
    return pl.pallas_call(
        _moe_kernel,
        out_shape=jax.ShapeDtypeStruct((cap * SUBLANES, LANES), F32),
        grid_spec=grid_spec,
        compiler_params=_cparams(("arbitrary",)),
        name="moe",
    )(block_expert, n_used, tok3, tok3, tok3, h_tiles, w_gu, b_gu.reshape(e, 1, f2),
      w_down, b_down.reshape(e, 1, d))


def _combine_kernel(pos0_ref, pos1_ref, pos2_ref, yb_hbm, x1_ref, gates_ref, mod_ref, gfin_ref,
                    out_ref, buf0, buf1, buf2, sem):
    i = pl.program_id(0)
    n = pl.num_programs(0)
    n_rows = buf0.shape[0] // SUBLANES
    tm = n_rows // TOP_K
    d = x1_ref.shape[2]

    @pl.when(i == 0)
    def _():
        _tile_gather_start(pos0_ref, yb_hbm, buf0, sem.at[0], n_rows, unrolled=False)
        _tile_gather_start(pos1_ref, yb_hbm, buf1, sem.at[1], n_rows, unrolled=False)

    def compute(cur):
        g = gates_ref[...].reshape(tm, LANES)
        gk = [jnp.broadcast_to(g[:, k:k + 1], (tm, LANES)) for k in range(TOP_K)]
        pieces = []
        for s in range(SUBLANES):
            acc = gk[0] * _tile_piece(cur, 0, tm, s)
            for k in range(1, TOP_K):
                acc = acc + gk[k] * _tile_piece(cur, k * tm, tm, s)
            pieces.append(acc)
        nb = x1_ref.shape[0]
        moe = jnp.concatenate(pieces, axis=1).reshape(nb, ROW_T, d)
        x2 = (x1_ref[...] + mod_ref[:, 5:6, :] * moe).reshape(tm, d)
        out_ref[...] = _rmsnorm(x2, gfin_ref[...]).reshape(nb, ROW_T, d)

    _ring_step(i, n - 1, (buf0, buf1, buf2), sem,
               wait=lambda buf, s: _tile_gather_wait(yb_hbm, buf, s),
               prefetch=lambda buf, s: _tile_gather_start(pos2_ref, yb_hbm, buf, s, n_rows,
                                                          unrolled=True, alternate_priority=True),
               compute=compute)


def _combine(yb, pos, x1, gates, mod, g_final):
    b, s, d = x1.shape
    nb = 4
    tm = nb * ROW_T
    tiles = s // ROW_T
    n_tiles = (b // nb) * tiles
    split = 8 // nb
    pos3 = pos.reshape(-1, TOP_K, split, tm).transpose(0, 2, 1, 3).reshape(n_tiles, 1, TOP_K * tm)
    smem_blk = lambda imap: pl.BlockSpec((1, 1, TOP_K * tm), imap, memory_space=pltpu.SMEM)
    bidx = lambda i: (i // (split * tiles)) * split + i % split
    tok = lambda n: pl.BlockSpec((nb, ROW_T, n), lambda i: (bidx(i), (i // split) % tiles, 0))
    buf = pltpu.VMEM((TOP_K * tm * SUBLANES, LANES), F32)
    return pl.pallas_call(
        _combine_kernel,
        out_shape=jax.ShapeDtypeStruct((b, s, d), F32),
        grid=(n_tiles,),
        in_specs=[smem_blk(lambda i: (i, 0, 0)),
                  smem_blk(lambda i: (jnp.minimum(i + 1, n_tiles - 1), 0, 0)),
                  smem_blk(lambda i: (jnp.minimum(i + 2, n_tiles - 1), 0, 0)),
                  pl.BlockSpec(memory_space=pl.ANY),
                  tok(d), tok(LANES),
                  pl.BlockSpec((nb, 8, d), lambda i: (bidx(i), 0, 0)),
                  pl.BlockSpec((1, d), lambda i: (0, 0))],
        out_specs=tok(d),
        scratch_shapes=[buf, buf, buf, pltpu.SemaphoreType.DMA((3,))],
        compiler_params=_cparams(("arbitrary",)),
        name="combine",
    )(pos3, pos3, pos3, yb, x1, gates, mod, g_final)


def kernel(x, c, ctx, c_ctx, w_mod, b_mod, g_mix, w_in, lam_re, lam_im, log_dt, b_re, b_im,
           c_re, c_im, d_skip, w_glu, conv_w, conv_b, w_ssm_br, w_conv_br, w_o, g_ffn,
           w_router, b_router, w_gu, b_gu, w_down, b_down, g_final):
    depth = w_mod.shape[0]
    assert depth == 1, "single-layer trunk"
    bsz, seq, d = x.shape
    ctx_len = ctx.shape[1]
    d_ssm = d // 2
    assert bsz % 8 == 0 and seq % CHUNK_T == 0 and ctx_len % CHUNK_T == 0 and seq % GRID_W == 0
    assert d == SUBLANES * LANES, "row gathers move one (8,128) f32 tile per token"

    n_cond = -(-(bsz + 1) // 8) * 8
    cond = jnp.zeros((n_cond, d), F32).at[:bsz].set(c).at[bsz].set(c_ctx)
    m = _adaln(cond, w_mod[0], b_mod[0])
    zeros2 = jnp.zeros((n_cond, 2, d), F32)
    mod_all = jnp.concatenate([m.reshape(n_cond, 6, d), zeros2], axis=1)
    mod_x, mod_c = mod_all[:bsz], mod_all[bsz:bsz + 1]

    w_in_bf = w_in[0].astype(BF16)
    w_u, w_rest = w_in_bf[:, :d_ssm], w_in_bf[:, d_ssm:]
    gm = g_mix[0].reshape(1, d)

    perm = _granule_transpose_matrix()
    utx = _front(x, mod_x, gm, w_u, perm)
    utc = _front(ctx, mod_c, gm, w_u, perm)

    smat, toep, cpow, a_t = _s5_matrices(lam_re[0], lam_im[0], log_dt[0], b_re[0], b_im[0],
                                         c_re[0], c_im[0], d_skip[0])
    y_chunks = _s5_chunks(utx, utc, smat, toep, cpow, a_t, bsz)

    pad_r = jnp.zeros((d, LANES - N_EXPERTS), F32)
    w_r = jnp.concatenate([w_router[0], pad_r], axis=1)
    b_r = jnp.concatenate([b_router[0], jnp.zeros((LANES - N_EXPERTS,), F32)]).reshape(1, LANES)
    x1, h, ids, gates, hist = _back(
        x, y_chunks, mod_x, perm, gm, w_rest, conv_w[0], conv_b[0].reshape(1, d_ssm),
        w_glu[0].astype(BF16), w_ssm_br[0].astype(BF16),
        w_conv_br[0].astype(BF16), w_o[0].astype(BF16), g_ffn[0].reshape(1, d), w_r, b_r)

    n_tok = bsz * seq
    buf_token, pos, block_expert, n_used = _routing(
        ids, hist[:, 0, :N_EXPERTS].astype(jnp.int32), bsz, seq)
    yb = _moe(h.reshape(n_tok * SUBLANES, LANES), buf_token, block_expert, n_used,
              w_gu[0], b_gu[0], w_down[0], b_down[0])
    return _combine(yb, pos, x1, gates, mod_x, g_final.reshape(1, d))
```

```python
import functools

import jax
import jax.numpy as jnp
from jax import lax
from jax.experimental import pallas as pl
from jax.experimental.pallas import tpu as pltpu

F32 = jnp.float32
BF16 = jnp.bfloat16
HIGHEST = lax.Precision.HIGHEST

RMS_EPS = 1e-6
GRID_W = 64
SSM_GROUP = 16
SSM_STATE = 64
N_EXPERTS = 32
TOP_K = 4
SWIGLU_LIMIT = 7.0
SWIGLU_ALPHA = 1.702

CHUNK_T = 16
LANES = 128
SUBLANES = 8
V7X_VMEM_LIMIT_BYTES = 56 * 1024 * 1024

S5_TR = 1024
MOE_TB = 512
MOE_FC = 512


def _cparams(sem):
    return pltpu.CompilerParams(dimension_semantics=sem,
                                vmem_limit_bytes=V7X_VMEM_LIMIT_BYTES)


def _bdot(a, b):
    return jnp.dot(a, b, preferred_element_type=F32)


def _rmsnorm(xt, g):
    ms = jnp.mean(xt * xt, axis=-1, keepdims=True)
    return xt * lax.rsqrt(ms + RMS_EPS) * g


def _adaln_kernel(c_ref, w_ref, b_ref, o_ref):
    s = jax.nn.silu(c_ref[...])
    o_ref[...] = jnp.dot(s, w_ref[...], precision=HIGHEST,
                         preferred_element_type=F32) + b_ref[...]


def _adaln(cond, w_mod, b_mod):
    r, d = cond.shape
    n = w_mod.shape[1]
    tn = n // 4
    return pl.pallas_call(
        _adaln_kernel,
        out_shape=jax.ShapeDtypeStruct((r, n), F32),
        grid=(n // tn,),
        in_specs=[pl.BlockSpec((r, d), lambda j: (0, 0)),
                  pl.BlockSpec((d, tn), lambda j: (0, j)),
                  pl.BlockSpec((1, tn), lambda j: (0, j))],
        out_specs=pl.BlockSpec((r, tn), lambda j: (0, j)),
        compiler_params=_cparams(("arbitrary",)),
        name="adaln",
    )(cond, w_mod, b_mod.reshape(1, n))


ROW_T = GRID_W
PITCH = ROW_T + 8
N_SLAB = 4
CHUNKS_PER_TILE = ROW_T // CHUNK_T
GRANULES = LANES // SSM_GROUP


def _granule_transpose_matrix():
    n = GRANULES * LANES
    idx = jnp.arange(n)
    j, m, c = idx // LANES, (idx % LANES) // SSM_GROUP, idx % SSM_GROUP
    dst = m * LANES + j * SSM_GROUP + c
    return jnp.zeros((n, n), F32).at[idx, dst].set(1.0).astype(BF16)


def _modulate_tile(xn, mod_ref, shift_row, scale_row):
    d = xn.shape[1]
    x3 = xn.reshape(8, ROW_T, d)
    x3 = x3 * (1.0 + mod_ref[:, scale_row:scale_row + 1, :]) + mod_ref[:, shift_row:shift_row + 1, :]
    return x3.reshape(8 * ROW_T, d)


def _front_kernel(x_ref, mod_ref, g_ref, w_ref, perm_ref, u_ref, us):
    d = x_ref.shape[2]
    xn = _rmsnorm(x_ref[...].reshape(8 * ROW_T, d), g_ref[...])
    hx = _modulate_tile(xn, mod_ref, 0, 1)
    u = _bdot(hx.astype(BF16), w_ref[...])
    for sl in range(N_SLAB):
        for b in range(8):
            us[sl, pl.ds(b * PITCH, ROW_T), :] = u[b * ROW_T:(b + 1) * ROW_T,
                                                  sl * LANES:(sl + 1) * LANES]
    blocks = []
    for sl in range(N_SLAB):
        for th in range(CHUNK_T // GRANULES):
            rows = []
            for kk in range(CHUNKS_PER_TILE):
                t0 = kk * CHUNK_T + th * GRANULES
                rows.append(jnp.concatenate(
                    [us[sl, pl.ds(t0 + j, 8, stride=PITCH), :] for j in range(GRANULES)], axis=1))
            blocks.append(jnp.concatenate(rows, axis=0))
    acat = jnp.concatenate(blocks, axis=0).astype(BF16)
    bmat = _bdot(acat, perm_ref[...])
    nrow = CHUNKS_PER_TILE * 8
    blk = 0
    for sl in range(N_SLAB):
        for th in range(CHUNK_T // GRANULES):
            for m in range(GRANULES):
                u_ref[sl * GRANULES + m, :, th * LANES:(th + 1) * LANES] = (
                    bmat[blk * nrow:(blk + 1) * nrow, m * LANES:(m + 1) * LANES].astype(u_ref.dtype))
            blk += 1


def _front(x, mod, g_mix, w_u, perm):
    b, s, d = x.shape
    n = w_u.shape[1]
    g = n // SSM_GROUP
    assert s % ROW_T == 0 and b % 8 == 0 and n == N_SLAB * LANES
    k = s // CHUNK_T
    tiles = s // ROW_T
    nrow = CHUNKS_PER_TILE * 8
    shared = mod.shape[0] == 1
    mod_spec = (pl.BlockSpec((1, 8, d), lambda i, j: (0, 0, 0)) if shared
                else pl.BlockSpec((8, 8, d), lambda i, j: (i, 0, 0)))
    return pl.pallas_call(
        _front_kernel,
        out_shape=jax.ShapeDtypeStruct((g, (b // 8) * k * 8, CHUNK_T * SSM_GROUP), BF16),
        grid=(b // 8, tiles),
        in_specs=[pl.BlockSpec((8, ROW_T, d), lambda i, j: (i, j, 0)),
                  mod_spec,
                  pl.BlockSpec((1, d), lambda i, j: (0, 0)),
                  pl.BlockSpec((d, n), lambda i, j: (0, 0)),
                  pl.BlockSpec(perm.shape, lambda i, j: (0, 0))],
        out_specs=pl.BlockSpec((g, nrow, CHUNK_T * SSM_GROUP), lambda i, j: (0, i * tiles + j, 0)),
        scratch_shapes=[pltpu.VMEM((N_SLAB, 8 * PITCH, LANES), F32)],
        compiler_params=_cparams(("arbitrary", "arbitrary")),
        name="front",
    )(x, mod, g_mix, w_u, perm)


P_LANES = 3 * LANES


def _s5_params_kernel(lam_ref, bt_ref, ct_ref, dsk_ref, rep_ref,
                      toep_ref, smat_ref, cpow_ref, at_ref):
    t, ch, p = CHUNK_T, SSM_GROUP, SSM_STATE
    width = t * ch
    f32dot = lambda a, b: jnp.dot(a, b, precision=HIGHEST, preferred_element_type=F32)
    eye = (lax.broadcasted_iota(jnp.int32, (p, p), 0) == lax.broadcasted_iota(jnp.int32, (p, p), 1))
    to_col = lambda row: jnp.sum(jnp.where(eye, jnp.broadcast_to(row, (p, p)), 0.0),
                                 axis=1, keepdims=True)
    blk = lax.broadcasted_iota(jnp.int32, (p, P_LANES), 1) // ch
    row16 = lax.broadcasted_iota(jnp.int32, (ch, width), 0)
    lane16 = lax.broadcasted_iota(jnp.int32, (ch, width), 1)

    smats, cpows, ats = [[], []], [[], []], []
    for j in range(2):
        kt, at_dir = [], []
        for z in range(2):
            lr = lam_ref[j, 4 * z:4 * z + 1, :]
            li = lam_ref[j, 4 * z + 1:4 * z + 2, :]
            dt = jnp.exp(lam_ref[j, 4 * z + 2:4 * z + 3, :])
            mag = jnp.exp(lr * dt)
            ar, ai = mag * jnp.cos(li * dt), mag * jnp.sin(li * dt)
            den = lr * lr + li * li
            qr = ((ar - 1.0) * lr + ai * li) / den
            qi = (ai * lr - (ar - 1.0) * li) / den
            bbr = qr * bt_ref[j, 2 * z] - qi * bt_ref[j, 2 * z + 1]
            bbi = qr * bt_ref[j, 2 * z + 1] + qi * bt_ref[j, 2 * z]
            pw = [(jnp.ones_like(ar), jnp.zeros_like(ai))]
            for _ in range(t):
                r, i = pw[-1]
                pw.append((r * ar - i * ai, r * ai + i * ar))
            at_dir.append(pw[t])

            expo = jnp.where(blk <= t, blk if z == 0 else t - blk, 0)
            tr = jnp.ones((p, P_LANES), F32)
            ti = jnp.zeros((p, P_LANES), F32)
            for bit in range(5):
                sr, si = to_col(pw[1 << bit][0]), to_col(pw[1 << bit][1])
                on = ((expo >> bit) & 1) == 1
                fr, fi = jnp.where(on, sr, 1.0), jnp.where(on, si, 0.0)
                tr, ti = tr * fr - ti * fi, tr * fi + ti * fr
            ctr = f32dot(ct_ref[j, 2 * z], rep_ref[...])
            cti = f32dot(ct_ref[j, 2 * z + 1], rep_ref[...])
            crd = ctr * tr - cti * ti
            cid = ctr * ti + cti * tr
            kt.append(f32dot(bbr, crd) - f32dot(bbi, cid))

            off = ch if z == 0 else 0
            cpows[j].append(crd[:, off:off + width])
            cpows[j].append(-cid[:, off:off + width])

            pows = [pw[t - 1 - s] if z == 0 else pw[s] for s in range(t)]
            pr_rows = jnp.concatenate([jnp.broadcast_to(q[0], (ch, p)) for q in pows], axis=0)
            pi_rows = jnp.concatenate([jnp.broadcast_to(q[1], (ch, p)) for q in pows], axis=0)
            br_rows = jnp.concatenate([bbr] * t, axis=0)
            bi_rows = jnp.concatenate([bbi] * t, axis=0)
            smats[j].append(pr_rows * br_rows - pi_rows * bi_rows)
            smats[j].append(pr_rows * bi_rows + pi_rows * br_rows)
        ats.append(at_dir)

        ktf = kt[0][:, :width]
        ktb = kt[1][:, ch:ch + width]
        rows = []
        for s in range(t):
            fwd = ktf if s == 0 else jnp.concatenate(
                [jnp.zeros((ch, ch * s), F32), ktf[:, :width - ch * s]], axis=1)
            sh = ch * (t - 1 - s)
            bwd = ktb if sh == 0 else jnp.concatenate(
                [ktb[:, sh:], jnp.zeros((ch, sh), F32)], axis=1)
            skip = jnp.where(lane16 == ch * s + row16, dsk_ref[j], 0.0)
            rows.append(fwd + bwd + skip)
        toep_ref[j] = jnp.concatenate(rows, axis=0).astype(toep_ref.dtype)

    zs = jnp.zeros((width, p), F32)
    smat_ref[0] = jnp.concatenate(
        [jnp.concatenate([jnp.concatenate([smats[0][q], zs], axis=1) for q in range(4)], axis=1),
         jnp.concatenate([jnp.concatenate([zs, smats[1][q]], axis=1) for q in range(4)], axis=1)],
        axis=0).astype(smat_ref.dtype)
    zc = jnp.zeros((p, width), F32)
    cpow_ref[0] = jnp.concatenate(
        [jnp.concatenate([cpows[j][q], zc] if j == 0 else [zc, cpows[j][q]], axis=1)
         for q in range(4) for j in range(2)], axis=0).astype(cpow_ref.dtype)
    at_ref[...] = jnp.concatenate(
        [jnp.concatenate([ats[0][z][part], ats[1][z][part]], axis=1)
         for z in range(2) for part in range(2)], axis=0)


def _s5_matrices(lam_re, lam_im, log_dt, b_re, b_im, c_re, c_im, d_skip):
    t, ch = CHUNK_T, SSM_GROUP
    _, g, p = lam_re.shape
    assert p == SSM_STATE and b_re.shape[-1] == ch and 2 * p == LANES
    width = t * ch
    zero = jnp.zeros((g, 1, p), F32)
    lam = jnp.concatenate(
        [jnp.stack([lam_re[z], lam_im[z], jnp.broadcast_to(log_dt[z][:, None], (g, p))], axis=1)
         if part == 0 else zero for z in range(2) for part in range(2)], axis=1)
    bt = jnp.stack([b_re[0], b_im[0], b_re[1], b_im[1]], axis=1).transpose(0, 1, 3, 2)
    ct = jnp.stack([c_re[0], c_im[0], c_re[1], c_im[1]], axis=1).transpose(0, 1, 3, 2)
    dsk = jnp.tile(d_skip.reshape(g, 1, ch), (1, 1, t))
    lane = jnp.arange(P_LANES)
    rep = ((lane[None, :] % ch == jnp.arange(ch)[:, None])
           & (lane[None, :] < (t + 1) * ch)).astype(F32)
    pair = lambda *shape: pl.BlockSpec((2,) + shape, lambda a: (a,) + (0,) * len(shape))
    toep, smat, cpow, a_t = pl.pallas_call(
        _s5_params_kernel,
        out_shape=(jax.ShapeDtypeStruct((g, width, width), BF16),
                   jax.ShapeDtypeStruct((g // 2, 2 * width, 4 * LANES), BF16),
                   jax.ShapeDtypeStruct((g // 2, 4 * LANES, 2 * width), BF16),
                   jax.ShapeDtypeStruct((4, g * p), F32)),
        grid=(g // 2,),
        in_specs=[pair(8, p), pair(4, ch, p), pair(4, p, ch), pair(1, width),
                  pl.BlockSpec((ch, P_LANES), lambda a: (0, 0))],
        out_specs=(pair(width, width),
                   pl.BlockSpec((1, 2 * width, 4 * LANES), lambda a: (a, 0, 0)),
                   pl.BlockSpec((1, 4 * LANES, 2 * width), lambda a: (a, 0, 0)),
                   pl.BlockSpec((4, LANES), lambda a: (0, a))),
        compiler_params=_cparams(("arbitrary",)),
        name="s5_params",
    )(lam, bt, ct, dsk, rep)
    return smat, toep, cpow, a_t


def _chunk_states(u_ref, smat, dst_refs, rows):
    tr = min(S5_TR, rows)
    for r0 in range(0, rows, tr):
        lhs = jnp.concatenate([u_ref[0, r0:r0 + tr, :], u_ref[1, r0:r0 + tr, :]], axis=1)
        res = _bdot(lhs, smat)
        for q in range(4):
            dst_refs[q][r0:r0 + tr, :] = res[:, q * LANES:(q + 1) * LANES]


def _scan_chunks(a_ref, c_refs, x_refs, h_refs, halves, kc, kx):
    cfr_ref, cfi_ref, cbr_ref, cbi_ref = c_refs
    xfr_ref, xfi_ref, xbr_ref, xbi_ref = x_refs
    hfr_ref, hfi_ref, hbr_ref, hbi_ref = h_refs
    a_fr, a_fi = a_ref[0:1, :], a_ref[1:2, :]
    a_br, a_bi = a_ref[2:3, :], a_ref[3:4, :]

    def rows(half, k, n_chunks, count=1):
        return pl.ds(pl.multiple_of((half * n_chunks + k) * 8, 8 * count), 8 * count)

    def step(h_re, h_im, a_re, a_im, s_re, s_im):
        return (a_re * h_re - a_im * h_im + s_re, a_re * h_im + a_im * h_re + s_im)

    zero = jnp.zeros((8, LANES), F32)

    def ctx_body(i, carry):
        out = []
        for half in range(halves):
            fr, fi, br, bi = carry[half]
            kf, kb = i, kc - 1 - i
            fr, fi = step(fr, fi, a_fr, a_fi, cfr_ref[rows(half, kf, kc), :],
                          cfi_ref[rows(half, kf, kc), :])
            br, bi = step(br, bi, a_br, a_bi, cbr_ref[rows(half, kb, kc), :],
                          cbi_ref[rows(half, kb, kc), :])
            out.append((fr, fi, br, bi))
        return tuple(out)

    carry = lax.fori_loop(0, kc, ctx_body, ((zero,) * 4,) * halves)

    def x_body(i, carry):
        out = []
        for half in range(halves):
            fr, fi, br, bi = carry[half]
            kf = 2 * i
            kb = kx - 2 - 2 * i
            fr1, fi1 = step(fr, fi, a_fr, a_fi, xfr_ref[rows(half, kf, kx), :],
                            xfi_ref[rows(half, kf, kx), :])
            br1, bi1 = step(br, bi, a_br, a_bi, xbr_ref[rows(half, kb + 1, kx), :],
                            xbi_ref[rows(half, kb + 1, kx), :])
            hfr_ref[rows(half, kf, kx, 2), :] = jnp.concatenate([fr, fr1], 0).astype(hfr_ref.dtype)
            hfi_ref[rows(half, kf, kx, 2), :] = jnp.concatenate([fi, fi1], 0).astype(hfi_ref.dtype)
            hbr_ref[rows(half, kb, kx, 2), :] = jnp.concatenate([br1, br], 0).astype(hbr_ref.dtype)
            hbi_ref[rows(half, kb, kx, 2), :] = jnp.concatenate([bi1, bi], 0).astype(hbi_ref.dtype)
            fr, fi = step(fr1, fi1, a_fr, a_fi, xfr_ref[rows(half, kf + 1, kx), :],
                          xfi_ref[rows(half, kf + 1, kx), :])
            br, bi = step(br1, bi1, a_br, a_bi, xbr_ref[rows(half, kb, kx), :],
                          xbi_ref[rows(half, kb, kx), :])
            out.append((fr, fi, br, bi))
        return tuple(out)

    lax.fori_loop(0, kx // 2, x_body, carry)


def _s5_chunks_kernel(ux_ref, uc_ref, smat_ref, toep_ref, cpow_ref, a_ref, y_ref,
                      *scratch, halves, kc, kx):
    s_x, s_c, h_in = scratch[0:4], scratch[4:8], scratch[8:12]
    rows_x, rows_c = ux_ref.shape[1], uc_ref.shape[1]
    _chunk_states(uc_ref, smat_ref[0], s_c, rows_c)
    _chunk_states(ux_ref, smat_ref[0], s_x, rows_x)
    _scan_chunks(a_ref, s_c, s_x, h_in, halves, kc, kx)
    n = ux_ref.shape[2]
    tr = min(S5_TR, rows_x)
    for r0 in range(0, rows_x, tr):
        hcat = jnp.concatenate([h[r0:r0 + tr, :] for h in h_in], axis=1)
        yst = _bdot(hcat, cpow_ref[0])
        for j in range(2):
            y_ref[j, r0:r0 + tr, :] = (_bdot(ux_ref[j, r0:r0 + tr, :], toep_ref[j])
                                       + yst[:, j * n:(j + 1) * n]).astype(y_ref.dtype)


def _s5_chunks(utx, utc, smat, toep, cpow, a_t, batch):
    g, rows_x, kdim = utx.shape
    rows_c = utc.shape[1]
    kc, kx = rows_c // batch, rows_x // batch
    assert kx % 2 == 0 and rows_x % min(S5_TR, rows_x) == 0
    pair = lambda rows: pl.BlockSpec((2, rows, kdim), lambda a: (a, 0, 0))
    return pl.pallas_call(
        functools.partial(_s5_chunks_kernel, halves=batch // 8, kc=kc, kx=kx),
        out_shape=jax.ShapeDtypeStruct((g, rows_x, kdim), BF16),
        grid=(g // 2,),
        in_specs=[pair(rows_x), pair(rows_c),
                  pl.BlockSpec((1, 2 * kdim, 4 * LANES), lambda a: (a, 0, 0)),
                  pl.BlockSpec((2, kdim, kdim), lambda a: (a, 0, 0)),
                  pl.BlockSpec((1, 4 * LANES, 2 * kdim), lambda a: (a, 0, 0)),
                  pl.BlockSpec((4, LANES), lambda a: (0, a))],
        out_specs=pair(rows_x),
        scratch_shapes=([pltpu.VMEM((rows_x, LANES), F32)] * 4
                        + [pltpu.VMEM((rows_c, LANES), F32)] * 4
                        + [pltpu.VMEM((rows_x, LANES), BF16)] * 4),
        compiler_params=_cparams(("arbitrary",)),
        name="s5_chunks",
    )(utx, utc, smat, toep, cpow, a_t)


def _chunk_rows_to_tokens(y_ref, perm_ref, ys):
    nrow = CHUNKS_PER_TILE * 8
    blocks = []
    for sl in range(N_SLAB):
        for th in range(CHUNK_T // GRANULES):
            blocks.append(jnp.concatenate(
                [y_ref[sl * GRANULES + m, :, th * LANES:(th + 1) * LANES] for m in range(GRANULES)],
                axis=1))
    acat = _bdot(jnp.concatenate(blocks, axis=0), perm_ref[...])
    blk = 0
    for sl in range(N_SLAB):
        for th in range(CHUNK_T // GRANULES):
            for kk in range(CHUNKS_PER_TILE):
                for j in range(GRANULES):
                    t = kk * CHUNK_T + th * GRANULES + j
                    ys[sl, pl.ds(t, 8, stride=PITCH), :] = acat[blk * nrow + kk * 8:
                                                               blk * nrow + (kk + 1) * 8,
                                                               j * LANES:(j + 1) * LANES]
            blk += 1
    return jnp.concatenate(
        [jnp.concatenate([ys[sl, pl.ds(b * PITCH, ROW_T), :] for sl in range(N_SLAB)], axis=1)
         for b in range(8)], axis=0)


def _back_kernel(x_ref, y_ref, mod_ref, perm_ref, gmix_ref, win_ref, convw_ref, convb_ref,
                 wglu_ref, wsbr_ref, wcbr_ref, wo_ref, gffn_ref, wrhi_ref, wrlo_ref, br_ref,
                 ltri_ref, x1_ref, h_ref, ids_ref, gates_ref, hist_ref, ys_scr):
    d_model = x_ref.shape[2]
    tm = 8 * ROW_T
    d_ssm = N_SLAB * LANES
    xt = x_ref[...].reshape(tm, d_model)
    xn = _rmsnorm(xt, gmix_ref[...])
    hx = _modulate_tile(xn, mod_ref, 0, 1).astype(BF16)
    proj = lambda lo, hi: _bdot(hx, win_ref[:, lo:hi])
    ys_tok = _chunk_rows_to_tokens(y_ref, perm_ref, ys_scr)
    v = proj(0, d_ssm)
    ys = jax.nn.gelu(ys_tok)
    glu = _bdot(ys.astype(BF16), wglu_ref[...])
    gate_c = proj(2 * d_ssm, 3 * d_ssm)
    ys = ys * jax.nn.sigmoid(glu)
    y_s = _bdot(ys.astype(BF16), wsbr_ref[...])
    gate_b = proj(d_ssm, 2 * d_ssm)

    z = gate_c * v
    col = lax.broadcasted_iota(jnp.int32, z.shape, 0) % GRID_W
    z_prev = jnp.where(col == 0, 0.0, pltpu.roll(z, 1, 0))
    z_next = jnp.where(col == GRID_W - 1, 0.0, pltpu.roll(z, tm - 1, 0))
    conv = (z_prev * convw_ref[0:1, :] + z * convw_ref[1:2, :]
            + z_next * convw_ref[2:3, :] + convb_ref[...])
    y_conv = gate_b * conv

    g_s = proj(3 * d_ssm, 3 * d_ssm + d_model)
    y_c = _bdot(y_conv.astype(BF16), wcbr_ref[...])
    merged_s = jax.nn.sigmoid(g_s) * y_s
    g_c = proj(3 * d_ssm + d_model, 3 * d_ssm + 2 * d_model)
    merged = merged_s + jax.nn.sigmoid(g_c) * y_c
    mo = _bdot(merged.astype(BF16), wo_ref[...])
    x1 = xt + (mo.reshape(8, ROW_T, d_model) * mod_ref[:, 2:3, :]).reshape(tm, d_model)
    x1_ref[...] = x1.reshape(8, ROW_T, d_model)

    hn = _modulate_tile(_rmsnorm(x1, gffn_ref[...]), mod_ref, 3, 4)
    for b in range(8):
        for s in range(SUBLANES):
            h_ref[b, pl.ds(s, ROW_T, stride=SUBLANES), :] = hn[b * ROW_T:(b + 1) * ROW_T,
                                                              s * LANES:(s + 1) * LANES]

    hn_hi = hn.astype(BF16)
    hn_lo = (hn - hn_hi.astype(F32)).astype(BF16)
    logits = (_bdot(hn_hi, wrhi_ref[...]) + _bdot(hn_lo, wrhi_ref[...])
              + _bdot(hn_hi, wrlo_ref[...]) + br_ref[...])
    lane = lax.broadcasted_iota(jnp.int32, logits.shape, 1)
    neg = jnp.float32(-jnp.inf)
    cur = jnp.where(lane < N_EXPERTS, logits, neg)
    vals, idxs = [], []
    for _ in range(TOP_K):
        mk = jnp.max(cur, axis=-1, keepdims=True)
        ik = jnp.min(jnp.where(cur == mk, lane, LANES), axis=-1, keepdims=True)
        vals.append(mk)
        idxs.append(ik)
        cur = jnp.where(lane == ik, neg, cur)
    exps = [jnp.exp(vk - vals[0]) for vk in vals]
    denom = exps[0] + exps[1] + exps[2] + exps[3]
    ids = jnp.zeros(logits.shape, jnp.int32)
    gates = jnp.zeros(logits.shape, F32)
    onehot = jnp.zeros(logits.shape, F32)
    for k in range(TOP_K):
        onehot = onehot + (lane == idxs[k]).astype(F32)
    before = _bdot(ltri_ref[...], onehot.astype(BF16))
    for k in range(TOP_K):
        rank_k = jnp.sum(jnp.where(lane == idxs[k], before, 0.0), axis=-1, keepdims=True)
        ids = jnp.where(lane == k, idxs[k], ids)
        ids = jnp.where(lane == TOP_K + k, rank_k.astype(jnp.int32), ids)
        gates = jnp.where(lane == k, exps[k] / denom, gates)
    ids_ref[0] = jnp.transpose(ids)[0:2 * TOP_K, :]
    gates_ref[...] = gates.reshape(8, ROW_T, LANES)
    hist_ref[0] = jnp.broadcast_to(jnp.sum(onehot, axis=0, keepdims=True), (8, LANES))


def _back(x, y_chunks, mod, perm, g_mix, w_rest, conv_w, conv_b, w_glu, w_ssm_br,
          w_conv_br, w_o, g_ffn, w_router, b_router):
    b, s, d = x.shape
    g, _, kdim = y_chunks.shape
    tiles = s // ROW_T
    nrow = CHUNKS_PER_TILE * 8
    tok = lambda n: pl.BlockSpec((8, ROW_T, n), lambda i, j: (i, j, 0))

    def const(arr):
        nd = arr.ndim
        return pl.BlockSpec(arr.shape, lambda i, j: (0,) * nd, pipeline_mode=pl.Buffered(1))

    w_r_hi = w_router.astype(BF16)
    w_r_lo = (w_router - w_r_hi.astype(F32)).astype(BF16)
    row = jnp.arange(8 * ROW_T)
    ltri = (row[:, None] > row[None, :]).astype(BF16)
    params = (perm, g_mix, w_rest, conv_w, conv_b, w_glu, w_ssm_br, w_conv_br, w_o, g_ffn,
              w_r_hi, w_r_lo, b_router, ltri)
    return pl.pallas_call(
        _back_kernel,
        out_shape=(jax.ShapeDtypeStruct((b, s, d), F32),
                   jax.ShapeDtypeStruct((b, s * SUBLANES, LANES), F32),
                   jax.ShapeDtypeStruct(((b // 8) * tiles, 2 * TOP_K, 8 * ROW_T), jnp.int32),
                   jax.ShapeDtypeStruct((b, s, LANES), F32),
                   jax.ShapeDtypeStruct(((b // 8) * tiles, 8, LANES), F32)),
        grid=(b // 8, tiles),
        in_specs=[tok(d),
                  pl.BlockSpec((g, nrow, kdim), lambda i, j: (0, i * tiles + j, 0)),
                  pl.BlockSpec((8, 8, d), lambda i, j: (i, 0, 0))] + [const(a) for a in params],
        out_specs=(tok(d), pl.BlockSpec((8, ROW_T * SUBLANES, LANES), lambda i, j: (i, j, 0)),
                   pl.BlockSpec((1, 2 * TOP_K, 8 * ROW_T), lambda i, j: (i * tiles + j, 0, 0)),
                   tok(LANES),
                   pl.BlockSpec((1, 8, LANES), lambda i, j: (i * tiles + j, 0, 0))),
        scratch_shapes=[pltpu.VMEM((N_SLAB, 8 * PITCH, LANES), F32)],
        compiler_params=_cparams(("arbitrary", "arbitrary")),
        name="back",
    )(x, y_chunks, mod, *params)


def _routing(ids_ranks, tile_hist, bsz, seq):
    tb = MOE_TB
    n_tiles, _, tm = ids_ranks.shape
    ids, ranks = ids_ranks[:, :TOP_K, :], ids_ranks[:, TOP_K:, :]
    n_slot = n_tiles * tm * TOP_K
    tile_before = jnp.cumsum(tile_hist, axis=0) - tile_hist
    counts = jnp.sum(tile_hist, axis=0)
    padded = (counts + tb - 1) // tb * tb
    pad_end = jnp.cumsum(padded)
    pad_start = pad_end - padded
    start = jnp.cumsum(counts) - counts
    base = pad_start[None, :] + tile_before
    pos = ranks
    for e in range(N_EXPERTS):
        pos = pos + jnp.where(ids == e, base[:, e][:, None, None], 0)
    pos = pos.astype(jnp.int32)
    _, order_slot = lax.sort_key_val(pos.reshape(-1), jnp.arange(n_slot, dtype=jnp.int32))

    def token_of_slot(slot):
        tiles = seq // ROW_T
        tile, row = slot // (TOP_K * tm), slot % tm
        return ((tile // tiles) * 8 + row // ROW_T) * seq + (tile % tiles) * ROW_T + row % ROW_T
    n_blocks = n_slot // tb + N_EXPERTS
    cap = n_blocks * tb
    block_first_row = jnp.arange(n_blocks, dtype=jnp.int32) * tb
    block_expert = jnp.minimum(
        jnp.sum(pad_end[None, :] <= block_first_row[:, None], axis=1, dtype=jnp.int32),
        N_EXPERTS - 1)
    within = (block_first_row - pad_start[block_expert])[:, None] + jnp.arange(tb, dtype=jnp.int32)
    valid = within < counts[block_expert][:, None]
    src = jnp.clip(start[block_expert][:, None] + within, 0, n_slot - 1)
    buf_token = jnp.where(valid, token_of_slot(order_slot[src]), 0).astype(jnp.int32).reshape(cap)
    n_used = (pad_end[-1] // tb).astype(jnp.int32).reshape(1)
    return buf_token, pos.reshape(n_tiles, TOP_K * tm), block_expert, n_used


def _to_tiles(ref, val):
    rows = val.shape[0]
    for s in range(SUBLANES):
        ref[pl.ds(s, rows, stride=SUBLANES), :] = val[:, s * LANES:(s + 1) * LANES]


def _tile_piece(ref, first_row, rows, s):
    return ref[pl.ds(first_row * SUBLANES + s, rows, stride=SUBLANES), :]


def _tile_gather_start(idx_ref, src_hbm, dst, sem, n_rows, unrolled, alternate_priority=False):
    def copy(r, t):
        return pltpu.make_async_copy(
            src_hbm.at[pl.ds(pl.multiple_of(t * SUBLANES, SUBLANES), SUBLANES), :],
            dst.at[pl.ds(pl.multiple_of(r * SUBLANES, SUBLANES), SUBLANES), :], sem)

    if unrolled:
        for r in range(n_rows):
            copy(r, idx_ref[0, 0, r]).start(priority=(r % 2) if alternate_priority else 0)
    else:
        def body(r, carry):
            copy(r, idx_ref[0, 0, r]).start()
            return carry
        lax.fori_loop(0, n_rows, body, 0, unroll=8)


def _tile_gather_wait(src_hbm, dst, sem):
    pltpu.make_async_copy(src_hbm.at[pl.ds(0, dst.shape[0]), :], dst, sem).wait()


def _ring_step(i, last, bufs, sem, wait, prefetch, compute):
    n = len(bufs)
    for p in range(n):
        def branch(p=p):
            wait(bufs[p], sem.at[p])
            q = (p + n - 1) % n
            prefetch(bufs[q], sem.at[q])
            compute(bufs[p])

            @pl.when(i == last)
            def _():
                for r in range(1, n):
                    wait(bufs[(p + r) % n], sem.at[(p + r) % n])

        pl.when((i <= last) & (lax.rem(i, n) == p))(branch)


def _cast_rows(src_ref, dst_ref, chunk):
    def body(c, carry):
        rows = pl.ds(pl.multiple_of(c * chunk, chunk), chunk)
        dst_ref[rows, :] = src_ref[0, rows, :].astype(dst_ref.dtype)
        return carry
    lax.fori_loop(0, dst_ref.shape[0] // chunk, body, 0)


def _moe_kernel(be_ref, nu_ref, tok0_ref, tok1_ref, tok2_ref, h_hbm, wgu_ref, bgu_ref, wd_ref,
                bd_ref, out_ref, xbuf0, xbuf1, xbuf2, wgu_bf, wd_bf, sem):
    i = pl.program_id(0)
    n_used = nu_ref[0]
    tb = xbuf0.shape[0] // SUBLANES
    f = wd_ref.shape[1]

    @pl.when(i == 0)
    def _():
        _tile_gather_start(tok0_ref, h_hbm, xbuf0, sem.at[0], tb, unrolled=False)
        _tile_gather_start(tok1_ref, h_hbm, xbuf1, sem.at[1], tb, unrolled=False)

    new_expert = (i == 0) | (be_ref[i] != be_ref[jnp.maximum(i - 1, 0)])

    @pl.when(new_expert & (i < n_used))
    def _():
        _cast_rows(wgu_ref, wgu_bf, 128)
        _cast_rows(wd_ref, wd_bf, 128)

    def compute(xcur):
        xe = jnp.concatenate([_tile_piece(xcur, 0, tb, s) for s in range(SUBLANES)],
                             axis=1).astype(BF16)
        y = bd_ref[0]
        for c in range(0, f, MOE_FC):
            gt = jnp.minimum(_bdot(xe, wgu_bf[:, c:c + MOE_FC]) + bgu_ref[0, :, c:c + MOE_FC],
                             SWIGLU_LIMIT)
            up = jnp.clip(_bdot(xe, wgu_bf[:, f + c:f + c + MOE_FC])
                          + bgu_ref[0, :, f + c:f + c + MOE_FC], -SWIGLU_LIMIT, SWIGLU_LIMIT)
            act = gt * jax.nn.sigmoid(SWIGLU_ALPHA * gt) * (up + 1.0)
            y = y + _bdot(act.astype(BF16), wd_bf[c:c + MOE_FC, :])
        _to_tiles(out_ref, y)

    _ring_step(i, n_used - 1, (xbuf0, xbuf1, xbuf2), sem,
               wait=lambda buf, s: _tile_gather_wait(h_hbm, buf, s),
               prefetch=lambda buf, s: _tile_gather_start(tok2_ref, h_hbm, buf, s, tb,
                                                          unrolled=True),
               compute=compute)

    @pl.when(i >= n_used)
    def _():
        out_ref[...] = jnp.zeros(out_ref.shape, out_ref.dtype)


def _moe(h_tiles, buf_token, block_expert, n_used, w_gu, b_gu, w_down, b_down):
    e, d, f2 = w_gu.shape
    f = f2 // 2
    tb = MOE_TB
    cap = buf_token.shape[0]
    n_blocks = cap // tb
    tok3 = buf_token.reshape(n_blocks, 1, tb)
    smem_blk = lambda imap: pl.BlockSpec((1, 1, tb), imap, memory_space=pltpu.SMEM)
    grid_spec = pltpu.PrefetchScalarGridSpec(
        num_scalar_prefetch=2,
        grid=(n_blocks,),
        in_specs=[smem_blk(lambda i, be, nu: (i, 0, 0)),
                  smem_blk(lambda i, be, nu: (jnp.minimum(i + 1, n_blocks - 1), 0, 0)),
                  smem_blk(lambda i, be, nu: (jnp.minimum(i + 2, n_blocks - 1), 0, 0)),
                  pl.BlockSpec(memory_space=pl.ANY),
                  pl.BlockSpec((1, d, f2), lambda i, be, nu: (be[i], 0, 0)),
                  pl.BlockSpec((1, 1, f2), lambda i, be, nu: (be[i], 0, 0)),
                  pl.BlockSpec((1, f, d), lambda i, be, nu: (be[i], 0, 0)),
                  pl.BlockSpec((1, 1, d), lambda i, be, nu: (be[i], 0, 0))],
        out_specs=pl.BlockSpec((tb * SUBLANES, LANES), lambda i, be, nu: (i, 0)),
        scratch_shapes=[pltpu.VMEM((tb * SUBLANES, LANES), F32)] * 3
        + [pltpu.VMEM((d, f2), BF16), pltpu.VMEM((f, d), BF16), pltpu.SemaphoreType.DMA((3,))],
    )
    return pl.pallas_call(
        _moe_kernel,
        out_shape=jax.ShapeDtypeStruct((cap * SUBLANES, LANES), F32),
        grid_spec=grid_spec,
        compiler_params=_cparams(("arbitrary",)),
        name="moe",
    )(block_expert, n_used, tok3, tok3, tok3, h_tiles, w_gu, b_gu.reshape(e, 1, f2),
      w_down, b_down.reshape(e, 1, d))


def _combine_kernel(pos0_ref, pos1_ref, pos2_ref, yb_hbm, x1_ref, gates_ref, mod_ref, gfin_ref,
                    out_ref, buf0, buf1, buf2, sem):
    i = pl.program_id(0)
    n = pl.num_programs(0)
    n_rows = buf0.shape[0] // SUBLANES
    tm = n_rows // TOP_K
    d = x1_ref.shape[2]

    @pl.when(i == 0)
    def _():
        _tile_gather_start(pos0_ref, yb_hbm, buf0, sem.at[0], n_rows, unrolled=False)
        _tile_gather_start(pos1_ref, yb_hbm, buf1, sem.at[1], n_rows, unrolled=False)

    def compute(cur):
        g = gates_ref[...].reshape(tm, LANES)
        gk = [jnp.broadcast_to(g[:, k:k + 1], (tm, LANES)) for k in range(TOP_K)]
        pieces = []
        for s in range(SUBLANES):
            acc = gk[0] * _tile_piece(cur, 0, tm, s)
            for k in range(1, TOP_K):
                acc = acc + gk[k] * _tile_piece(cur, k * tm, tm, s)
            pieces.append(acc)
        nb = x1_ref.shape[0]
        moe = jnp.concatenate(pieces, axis=1).reshape(nb, ROW_T, d)
        x2 = (x1_ref[...] + mod_ref[:, 5:6, :] * moe).reshape(tm, d)
        out_ref[...] = _rmsnorm(x2, gfin_ref[...]).reshape(nb, ROW_T, d)

    _ring_step(i, n - 1, (buf0, buf1, buf2), sem,
               wait=lambda buf, s: _tile_gather_wait(yb_hbm, buf, s),
               prefetch=lambda buf, s: _tile_gather_start(pos2_ref, yb_hbm, buf, s, n_rows,
                                                          unrolled=True, alternate_priority=True),
               compute=compute)


def _combine(yb, pos, x1, gates, mod, g_final):
    b, s, d = x1.shape
    nb = 4
    tm = nb * ROW_T
    tiles = s // ROW_T
    n_tiles = (b // nb) * tiles
    split = 8 // nb
    pos3 = pos.reshape(-1, TOP_K, split, tm).transpose(0, 2, 1, 3).reshape(n_tiles, 1, TOP_K * tm)
    smem_blk = lambda imap: pl.BlockSpec((1, 1, TOP_K * tm), imap, memory_space=pltpu.SMEM)
    bidx = lambda i: (i // (split * tiles)) * split + i % split
    tok = lambda n: pl.BlockSpec((nb, ROW_T, n), lambda i: (bidx(i), (i // split) % tiles, 0))
    buf = pltpu.VMEM((TOP_K * tm * SUBLANES, LANES), F32)
    return pl.pallas_call(
        _combine_kernel,
        out_shape=jax.ShapeDtypeStruct((b, s, d), F32),
        grid=(n_tiles,),
        in_specs=[smem_blk(lambda i: (i, 0, 0)),
                  smem_blk(lambda i: (jnp.minimum(i + 1, n_tiles - 1), 0, 0)),
                  smem_blk(lambda i: (jnp.minimum(i + 2, n_tiles - 1), 0, 0)),
                  pl.BlockSpec(memory_space=pl.ANY),
                  tok(d), tok(LANES),
                  pl.BlockSpec((nb, 8, d), lambda i: (bidx(i), 0, 0)),
                  pl.BlockSpec((1, d), lambda i: (0, 0))],
        out_specs=tok(d),
        scratch_shapes=[buf, buf, buf, pltpu.SemaphoreType.DMA((3,))],
        compiler_params=_cparams(("arbitrary",)),
        name="combine",
    )(pos3, pos3, pos3, yb, x1, gates, mod, g_final)


def kernel(x, c, ctx, c_ctx, w_mod, b_mod, g_mix, w_in, lam_re, lam_im, log_dt, b_re, b_im,
           c_re, c_im, d_skip, w_glu, conv_w, conv_b, w_ssm_br, w_conv_br, w_o, g_ffn,
           w_router, b_router, w_gu, b_gu, w_down, b_down, g_final):
    depth = w_mod.shape[0]
    assert depth == 1, "single-layer trunk"
    bsz, seq, d = x.shape
    ctx_len = ctx.shape[1]
    d_ssm = d // 2
    assert bsz % 8 == 0 and seq % CHUNK_T == 0 and ctx_len % CHUNK_T == 0 and seq % GRID_W == 0
    assert d == SUBLANES * LANES, "row gathers move one (8,128) f32 tile per token"

    n_cond = -(-(bsz + 1) // 8) * 8
    cond = jnp.zeros((n_cond, d), F32).at[:bsz].set(c).at[bsz].set(c_ctx)
    m = _adaln(cond, w_mod[0], b_mod[0])
    zeros2 = jnp.zeros((n_cond, 2, d), F32)
    mod_all = jnp.concatenate([m.reshape(n_cond, 6, d), zeros2], axis=1)
    mod_x, mod_c = mod_all[:bsz], mod_all[bsz:bsz + 1]

    w_in_bf = w_in[0].astype(BF16)
    w_u, w_rest = w_in_bf[:, :d_ssm], w_in_bf[:, d_ssm:]
    gm = g_mix[0].reshape(1, d)

    perm = _granule_transpose_matrix()
    utx = _front(x, mod_x, gm, w_u, perm)
    utc = _front(ctx, mod_c, gm, w_u, perm)

    smat, toep, cpow, a_t = _s5_matrices(lam_re[0], lam_im[0], log_dt[0], b_re[0], b_im[0],
                                         c_re[0], c_im[0], d_skip[0])
    y_chunks = _s5_chunks(utx, utc, smat, toep, cpow, a_t, bsz)

    pad_r = jnp.zeros((d, LANES - N_EXPERTS), F32)
    w_r = jnp.concatenate([w_router[0], pad_r], axis=1)
    b_r = jnp.concatenate([b_router[0], jnp.zeros((LANES - N_EXPERTS,), F32)]).reshape(1, LANES)
    x1, h, ids, gates, hist = _back(
        x, y_chunks, mod_x, perm, gm, w_rest, conv_w[0], conv_b[0].reshape(1, d_ssm),
        w_glu[0].astype(BF16), w_ssm_br[0].astype(BF16),
        w_conv_br[0].astype(BF16), w_o[0].astype(BF16), g_ffn[0].reshape(1, d), w_r, b_r)

    n_tok = bsz * seq
    buf_token, pos, block_expert, n_used = _routing(
        ids, hist[:, 0, :N_EXPERTS].astype(jnp.int32), bsz, seq)
    yb = _moe(h.reshape(n_tok * SUBLANES, LANES), buf_token, block_expert, n_used,
              w_gu[0], b_gu[0], w_down[0], b_down[0])
    return _combine(yb, pos, x1, gates, mod_x, g_final.reshape(1, d))
```
